```python
import math
import jax, jax.numpy as jnp
from jax import lax
import numpy as np

D_MODEL = 2048
BATCH = 2
SEQ = 4096
DEPTH = 2

GRID_W = 64
CTX_LEN = 256
HEAD_DIM = 128
N_Q_HEADS = 8
N_KV_HEADS = 2
Q_GROUP = N_Q_HEADS // N_KV_HEADS
ATTN_WIDTH = N_Q_HEADS * HEAD_DIM
KV_WIDTH = N_KV_HEADS * HEAD_DIM
WINDOW = 128
BLOCK = 128
ROPE_THETA = 10000.0
ROPE_AXIS_DIM = HEAD_DIM // 2
HYENA_WIDTH = 1024
HYENA_GROUPS = 8
SHORT_CONV = 3
FILTER_EMB = 33
FILTER_HIDDEN = 64
DECAY_TARGET = 1e-2
FAST_DECAY_PCT = 0.3
SLOW_DECAY_PCT = 1.5
D_FF = 4 * D_MODEL
EPS = 1e-6
N_MOD = 6
NEG_INF = -1e30
Q_END = ATTN_WIDTH
K_END = Q_END + KV_WIDTH
V_END = K_END + KV_WIDTH
HY_END = V_END + 3 * HYENA_WIDTH
GA_END = HY_END + D_MODEL
GH_END = GA_END + D_MODEL
IN_WIDTH = GH_END

kernel_name = "hybrid_gated_swa_hyena_dit"


def rms_norm(x, g):
    xf = x.astype(jnp.float32)
    r = lax.rsqrt(jnp.mean(xf * xf, axis=-1, keepdims=True) + EPS)
    return (xf * r).astype(x.dtype) * g


def modulate(h, shift, scale):
    return h * (1 + scale) + shift


def rope_1d(x, pos):
    half = x.shape[-1] // 2
    freqs = ROPE_THETA ** (-jnp.arange(half, dtype=jnp.float32) / half)
    ang = pos.astype(jnp.float32)[:, None] * freqs[None, :]
    cos = jnp.cos(ang)[:, None, :].astype(x.dtype)
    sin = jnp.sin(ang)[:, None, :].astype(x.dtype)
    x1, x2 = x[..., :half], x[..., half:]
    return jnp.concatenate([x1 * cos - x2 * sin, x1 * sin + x2 * cos], axis=-1)


def rope_2d(x, row, col):
    return jnp.concatenate([rope_1d(x[..., :ROPE_AXIS_DIM], row),
                            rope_1d(x[..., ROPE_AXIS_DIM:], col)], axis=-1)


def window_attention(q, k, v, k_ctx, v_ctx, sink):
    B, L = q.shape[:2]
    nb = L // BLOCK
    scale = HEAD_DIM ** -0.5
    qb = q.reshape(B, nb, BLOCK, N_KV_HEADS, Q_GROUP, HEAD_DIM)
    pad = ((0, 0), (BLOCK, BLOCK), (0, 0), (0, 0))
    kp = jnp.pad(k, pad).reshape(B, nb + 2, BLOCK, N_KV_HEADS, HEAD_DIM)
    vp = jnp.pad(v, pad).reshape(B, nb + 2, BLOCK, N_KV_HEADS, HEAD_DIM)
    kb = jnp.concatenate([kp[:, :-2], kp[:, 1:-1], kp[:, 2:]], axis=2)
    vb = jnp.concatenate([vp[:, :-2], vp[:, 1:-1], vp[:, 2:]], axis=2)
    s_loc = jnp.einsum('bnqkgd,bnjkd->bnkgqj', qb, kb).astype(jnp.float32) * scale
    s_ctx = jnp.einsum('bnqkgd,bckd->bnkgqc', qb, k_ctx).astype(jnp.float32) * scale
    q_pos = jnp.arange(nb)[:, None] * BLOCK + jnp.arange(BLOCK)[None, :]
    k_pos = (jnp.arange(nb)[:, None] - 1) * BLOCK + jnp.arange(3 * BLOCK)[None, :]
    diff = k_pos[:, None, :] - q_pos[:, :, None]
    valid = (jnp.abs(diff) <= WINDOW) & (k_pos[:, None, :] >= 0) & (k_pos[:, None, :] < L)
    s_loc = jnp.where(valid[None, :, None, None], s_loc, NEG_INF)
    s_sink = jnp.broadcast_to(sink.astype(jnp.float32).reshape(N_KV_HEADS, Q_GROUP, 1, 1),
                              s_loc.shape[:-1] + (1,))
    p = jax.nn.softmax(jnp.concatenate([s_loc, s_ctx, s_sink], axis=-1), axis=-1)
    n_loc = 3 * BLOCK
    n_ctx = k_ctx.shape[1]
    p_loc = p[..., :n_loc].astype(v.dtype)
    p_ctx = p[..., n_loc:n_loc + n_ctx].astype(v.dtype)
    o = (jnp.einsum('bnkgqj,bnjkd->bnqkgd', p_loc, vb)
         + jnp.einsum('bnkgqc,bckd->bnqkgd', p_ctx, v_ctx))
    return o.reshape(B, L, ATTN_WIDTH)


def context_attention(q, k, v, sink):
    B, C = q.shape[:2]
    scale = HEAD_DIM ** -0.5
    s = jnp.einsum('bqkgd,bckd->bkgqc', q, k).astype(jnp.float32) * scale
    s_sink = jnp.broadcast_to(sink.astype(jnp.float32).reshape(N_KV_HEADS, Q_GROUP, 1, 1),
                              s.shape[:-1] + (1,))
    p = jax.nn.softmax(jnp.concatenate([s, s_sink], axis=-1), axis=-1)
    o = jnp.einsum('bkgqc,bckd->bqkgd', p[..., :C].astype(v.dtype), v)
    return o.reshape(B, C, ATTN_WIDTH)


def short_conv(u, w, b):
    L = u.shape[1]
    up = jnp.pad(u, ((0, 0), (1, 1), (0, 0)))
    return up[:, :L] * w[0] + up[:, 1:L + 1] * w[1] + up[:, 2:] * w[2] + b


def hyena_filter(L, fw1, fb1, ff1, fw2, fb2, ff2, fw3, fb3, ff3, fw4):
    bands = (FILTER_EMB - 1) // 2
    t = jnp.linspace(0.0, 1.0, L, dtype=jnp.float32)[:, None]
    w = 2 * math.pi * jnp.arange(L, dtype=jnp.float32)[:, None] / L
    f = jnp.linspace(1e-4, bands - 1, bands, dtype=jnp.float32)[None, :]
    z = jnp.concatenate([t, jnp.cos(f * w), -jnp.sin(f * w)], axis=-1)
    h = jnp.sin(ff1 * (z @ fw1 + fb1))
    h = jnp.sin(ff2 * (h @ fw2 + fb2))
    h = jnp.sin(ff3 * (h @ fw3 + fb3))
    h = (h @ fw4).astype(jnp.float32)
    min_decay = math.log(DECAY_TARGET) / SLOW_DECAY_PCT
    max_decay = math.log(DECAY_TARGET) / FAST_DECAY_PCT
    deltas = jnp.tile(jnp.linspace(min_decay, max_decay, HYENA_WIDTH, dtype=jnp.float32), 2)
    h = h * jnp.exp(-t * jnp.abs(deltas)[None, :])
    h_fwd, h_bwd = h[:, :HYENA_WIDTH], h[:, HYENA_WIDTH:]
    norm = jnp.sum(jnp.abs(h_fwd), axis=0) + jnp.sum(jnp.abs(h_bwd[1:]), axis=0)
    return h_fwd / norm, h_bwd / norm


def bidirectional_fftconv(u, h_fwd, h_bwd):
    L = u.shape[1]
    k = jnp.concatenate([h_fwd, jnp.zeros_like(h_fwd[:1]), h_bwd[:0:-1]], axis=0)
    kf = jnp.fft.rfft(k, axis=0)
    uf = jnp.fft.rfft(u.astype(jnp.float32), n=2 * L, axis=1)
    y = jnp.fft.irfft(uf * kf[None], n=2 * L, axis=1)[:, :L]
    return y.astype(u.dtype)


def hyena(proj, conv_w, conv_b, filt, bias_d):
    u = short_conv(proj, conv_w, conv_b)
    x0, x1, v = jnp.split(u, 3, axis=-1)
    v = v * x1
    v = bidirectional_fftconv(v, filt[0], filt[1]) + v * bias_d
    return v * x0


def merge_branches(p, attn, hy, w_ao, w_ho, w_o):
    gate_a = jax.nn.sigmoid(p[..., HY_END:GA_END])
    gate_h = jax.nn.sigmoid(p[..., GA_END:GH_END])
    return (gate_a * (attn @ w_ao) + gate_h * (hy @ w_ho)) @ w_o


def sq_relu_mlp(h, w1, w2):
    return jnp.square(jax.nn.relu(h @ w1)) @ w2


def setup_inputs(seed: int = 0) -> dict:
    key = jax.random.key(seed)
    ks = jax.random.split(key, 32)
    f32 = jnp.float32
    nrm = lambda k, shape, s: jax.random.normal(k, shape, f32) * s
    return {
        "x": nrm(ks[0], (BATCH, SEQ, D_MODEL), 1.0),
        "c": nrm(ks[1], (BATCH, D_MODEL), 1.0),
        "ctx": nrm(ks[2], (BATCH, CTX_LEN, D_MODEL), 1.0),
        "c_ctx": nrm(ks[3], (D_MODEL,), 1.0),
        "w_mod": nrm(ks[4], (DEPTH, D_MODEL, N_MOD * D_MODEL), D_MODEL ** -0.5),
        "b_mod": nrm(ks[5], (DEPTH, N_MOD * D_MODEL), 0.01),
        "norm_g": 1.0 + nrm(ks[6], (DEPTH, 4, D_MODEL), 0.1),
        "w_in": nrm(ks[7], (DEPTH, D_MODEL, IN_WIDTH), D_MODEL ** -0.5),
        "attn_sink": nrm(ks[8], (DEPTH, N_Q_HEADS), 1.0),
        "hy_conv_w": nrm(ks[9], (DEPTH, SHORT_CONV, 3 * HYENA_WIDTH), SHORT_CONV ** -0.5),
        "hy_conv_b": nrm(ks[10], (DEPTH, 3 * HYENA_WIDTH), 0.01),
        "hy_fw1": nrm(ks[11], (DEPTH, FILTER_EMB, FILTER_HIDDEN), FILTER_EMB ** -0.5),
        "hy_fb1": nrm(ks[12], (DEPTH, FILTER_HIDDEN), 0.1),
        "hy_ff1": 1.0 + nrm(ks[13], (DEPTH, FILTER_HIDDEN), 0.1),
        "hy_fw2": nrm(ks[14], (DEPTH, FILTER_HIDDEN, FILTER_HIDDEN), FILTER_HIDDEN ** -0.5),
        "hy_fb2": nrm(ks[15], (DEPTH, FILTER_HIDDEN), 0.1),
        "hy_ff2": 1.0 + nrm(ks[16], (DEPTH, FILTER_HIDDEN), 0.1),
        "hy_fw3": nrm(ks[17], (DEPTH, FILTER_HIDDEN, FILTER_HIDDEN), FILTER_HIDDEN ** -0.5),
        "hy_fb3": nrm(ks[18], (DEPTH, FILTER_HIDDEN), 0.1),
        "hy_ff3": 1.0 + nrm(ks[19], (DEPTH, FILTER_HIDDEN), 0.1),
        "hy_fw4": nrm(ks[20], (DEPTH, FILTER_HIDDEN, 2 * HYENA_WIDTH), FILTER_HIDDEN ** -0.5),
        "hy_bias": nrm(ks[21], (DEPTH, HYENA_WIDTH), 1.0),
        "w_attn_out": nrm(ks[22], (DEPTH, ATTN_WIDTH, D_MODEL), ATTN_WIDTH ** -0.5),
        "w_hyena_out": nrm(ks[23], (DEPTH, HYENA_WIDTH, D_MODEL), HYENA_WIDTH ** -0.5),
        "w_out": nrm(ks[24], (DEPTH, D_MODEL, D_MODEL), D_MODEL ** -0.5),
        "w_ff1": nrm(ks[25], (DEPTH, D_MODEL, D_FF), D_MODEL ** -0.5),
        "w_ff2": nrm(ks[26], (DEPTH, D_FF, D_MODEL), D_FF ** -0.5),
    }


def reference(x, c, ctx, c_ctx, w_mod, b_mod, norm_g, w_in, attn_sink, hy_conv_w, hy_conv_b,
              hy_fw1, hy_fb1, hy_ff1, hy_fw2, hy_fb2, hy_ff2, hy_fw3, hy_fb3, hy_ff3, hy_fw4,
              hy_bias, w_attn_out, w_hyena_out, w_out, w_ff1, w_ff2):
    B, L, _ = x.shape
    C = ctx.shape[1]
    ROWS = L // GRID_W
    row = jnp.repeat(jnp.arange(ROWS, dtype=jnp.int32), GRID_W)
    col = jnp.tile(jnp.arange(GRID_W, dtype=jnp.int32), ROWS)
    x_lat, x_ctx = x, ctx
    for l in range(DEPTH):
        last = l == DEPTH - 1
        mod_lat = jax.nn.silu(c) @ w_mod[l] + b_mod[l]
        mod_ctx = jax.nn.silu(c_ctx) @ w_mod[l] + b_mod[l]
        sh1, sc1, g1, sh2, sc2, g2 = jnp.split(mod_lat[:, None, :], N_MOD, axis=-1)
        csh1, csc1, cg1, csh2, csc2, cg2 = jnp.split(mod_ctx[None, None, :], N_MOD, axis=-1)
        fparams = (hy_fw1[l], hy_fb1[l], hy_ff1[l], hy_fw2[l], hy_fb2[l], hy_ff2[l],
                   hy_fw3[l], hy_fb3[l], hy_ff3[l], hy_fw4[l])

        h_lat = modulate(rms_norm(x_lat, norm_g[l, 0]), sh1, sc1)
        h_ctx = modulate(rms_norm(x_ctx, norm_g[l, 0]), csh1, csc1)
        p_lat = h_lat @ w_in[l]
        if last:
            kv_ctx = h_ctx @ w_in[l][:, Q_END:V_END]
        else:
            p_ctx = h_ctx @ w_in[l]
            kv_ctx = p_ctx[..., Q_END:V_END]
        k_ctx = kv_ctx[..., :KV_WIDTH].reshape(B, C, N_KV_HEADS, HEAD_DIM)
        v_ctx = kv_ctx[..., KV_WIDTH:].reshape(B, C, N_KV_HEADS, HEAD_DIM)

        q_lat = rope_2d(p_lat[..., :Q_END].reshape(B, L, N_Q_HEADS, HEAD_DIM), row, col)
        q_lat = q_lat.reshape(B, L, N_KV_HEADS, Q_GROUP, HEAD_DIM)
        k_lat = rope_2d(p_lat[..., Q_END:K_END].reshape(B, L, N_KV_HEADS, HEAD_DIM), row, col)
        v_lat = p_lat[..., K_END:V_END].reshape(B, L, N_KV_HEADS, HEAD_DIM)
        attn_lat = window_attention(q_lat, k_lat, v_lat, k_ctx, v_ctx, attn_sink[l])
        hy_lat = hyena(p_lat[..., V_END:HY_END], hy_conv_w[l], hy_conv_b[l],
                       hyena_filter(L, *fparams), hy_bias[l])
        mix_lat = merge_branches(p_lat, attn_lat, hy_lat, w_attn_out[l], w_hyena_out[l], w_out[l])
        x_lat = x_lat + g1 * rms_norm(mix_lat, norm_g[l, 1])

        if not last:
            q_ctx = p_ctx[..., :Q_END].reshape(B, C, N_KV_HEADS, Q_GROUP, HEAD_DIM)
            attn_ctx = context_attention(q_ctx, k_ctx, v_ctx, attn_sink[l])
            hy_ctx = hyena(p_ctx[..., V_END:HY_END], hy_conv_w[l], hy_conv_b[l],
                           hyena_filter(C, *fparams), hy_bias[l])
            mix_ctx = merge_branches(p_ctx, attn_ctx, hy_ctx, w_attn_out[l], w_hyena_out[l], w_out[l])
            x_ctx = x_ctx + cg1 * rms_norm(mix_ctx, norm_g[l, 1])
            f_ctx = sq_relu_mlp(modulate(rms_norm(x_ctx, norm_g[l, 2]), csh2, csc2), w_ff1[l], w_ff2[l])
            x_ctx = x_ctx + cg2 * rms_norm(f_ctx, norm_g[l, 3])

        f_lat = sq_relu_mlp(modulate(rms_norm(x_lat, norm_g[l, 2]), sh2, sc2), w_ff1[l], w_ff2[l])
        x_lat = x_lat + g2 * rms_norm(f_lat, norm_g[l, 3])
    return x_lat
```

```python
import functools
import math

import jax
import jax.numpy as jnp
from jax import lax
from jax.experimental import pallas as pl
from jax.experimental.pallas import tpu as pltpu

F32 = jnp.float32
BF16 = jnp.bfloat16

D_MODEL = 2048
DEPTH = 2
GRID_W = 64
HEAD_DIM = 128
N_Q_HEADS = 8
N_KV_HEADS = 2
Q_GROUP = N_Q_HEADS // N_KV_HEADS
ATTN_WIDTH = N_Q_HEADS * HEAD_DIM
KV_WIDTH = N_KV_HEADS * HEAD_DIM
BLOCK = 128
ROPE_THETA = 10000.0
HYENA_WIDTH = 1024
FILTER_EMB = 33
FILTER_HIDDEN = 64
DECAY_TARGET = 1e-2
FAST_DECAY_PCT = 0.3
SLOW_DECAY_PCT = 1.5
D_FF = 4 * D_MODEL
EPS = 1e-6
N_MOD = 6
NEG_INF = -1e30
Q_END = ATTN_WIDTH
K_END = Q_END + KV_WIDTH
V_END = K_END + KV_WIDTH
HY_END = V_END + 3 * HYENA_WIDTH
GA_END = HY_END + D_MODEL
GH_END = GA_END + D_MODEL
IN_WIDTH = GH_END

LANES = 128
VMEM_LIMIT = 56 * 1024 * 1024

FFT_N2 = 16
FFT_KG = 16
HY_CB = 128
MOD_ROWS = 8


def _cparams(sem):
    return pltpu.CompilerParams(dimension_semantics=sem, vmem_limit_bytes=VMEM_LIMIT)


def _single(block_shape, index_map):
    return pl.BlockSpec(block_shape, index_map, pipeline_mode=pl.Buffered(1))


def _rms(x):
    return x * lax.rsqrt(jnp.mean(x * x, axis=-1, keepdims=True) + EPS)


def _mod_kernel(c_ref, w_ref, b_ref, o_ref):
    c = c_ref[...]
    s = c * jax.nn.sigmoid(c)
    o_ref[...] = jnp.dot(s.astype(BF16), w_ref[...].astype(BF16),
                         preferred_element_type=F32) + b_ref[...]


def _modulation(c_rows, w, b):
    n = w.shape[1]
    tn = 1024
    return pl.pallas_call(
        _mod_kernel,
        grid=(n // tn,),
        in_specs=[pl.BlockSpec((MOD_ROWS, D_MODEL), lambda j: (0, 0)),
                  pl.BlockSpec((D_MODEL, tn), lambda j: (0, j)),
                  pl.BlockSpec((1, tn), lambda j: (0, j))],
        out_specs=pl.BlockSpec((MOD_ROWS, tn), lambda j: (0, j)),
        out_shape=jax.ShapeDtypeStruct((MOD_ROWS, n), F32),
        name="modulation",
        compiler_params=_cparams(("arbitrary",)),
    )(c_rows, w, b.reshape(1, n))


def _normproj_kernel(x_ref, g_ref, sc_ref, sh_ref, w_ref, o_ref, h_ref):
    @pl.when(pl.program_id(1) == 0)
    def _():
        h = _rms(x_ref[...]) * g_ref[...]
        h_ref[...] = (h * (1.0 + sc_ref[...]) + sh_ref[...]).astype(BF16)

    o_ref[...] = jnp.dot(h_ref[...], w_ref[...].astype(BF16), preferred_element_type=F32)


def _normproj(x, g, scale, shift, w, tm, tn):
    m = x.shape[0]
    n = w.shape[1]
    nmod = scale.shape[0]
    blocks_per_mod = m // nmod // tm
    mod_spec = pl.BlockSpec((None, 1, D_MODEL), lambda i, j: (i // blocks_per_mod, 0, 0))
    return pl.pallas_call(
        _normproj_kernel,
        grid=(m // tm, n // tn),
        in_specs=[pl.BlockSpec((tm, D_MODEL), lambda i, j: (i, 0)),
                  pl.BlockSpec((1, D_MODEL), lambda i, j: (0, 0)),
                  mod_spec, mod_spec,
                  pl.BlockSpec((D_MODEL, tn), lambda i, j: (0, j))],
        out_specs=pl.BlockSpec((tm, tn), lambda i, j: (i, j)),
        out_shape=jax.ShapeDtypeStruct((m, n), F32),
        scratch_shapes=[pltpu.VMEM((tm, D_MODEL), BF16)],
        name="normproj",
        compiler_params=_cparams(("arbitrary", "arbitrary")),
    )(x, g.reshape(1, D_MODEL), scale, shift, w)


def _rope(x, cos, sin_signed, first_half):
    rot = jnp.where(first_half, pltpu.roll(x, HEAD_DIM - 32, axis=1), pltpu.roll(x, 32, axis=1))
    return x * cos + rot * sin_signed


def _softmax_pv(s, sink, v):
    m = jnp.maximum(jnp.max(s, axis=-1, keepdims=True), sink)
    e = jnp.exp(s - m)
    denom = jnp.sum(e, axis=-1, keepdims=True) + jnp.exp(sink - m)
    o = jnp.dot(e.astype(BF16), v, preferred_element_type=F32)
    return o / denom


def _win_attn_kernel(sink_ref, q_ref, kp_ref, kc_ref, kn_ref, vp_ref, vc_ref, vn_ref,
                     kx_ref, vx_ref, cos_ref, sin_ref, o_ref, *, nb):
    i = pl.program_id(1)
    scale = HEAD_DIM ** -0.5
    lane = lax.broadcasted_iota(jnp.int32, (BLOCK, HEAD_DIM), 1)
    first_half = (lane % 64) < 32

    def table(ref, blk):
        return ref[pl.ds(pl.multiple_of(blk * BLOCK, BLOCK), BLOCK), :]

    ip = jnp.maximum(i - 1, 0)
    inx = jnp.minimum(i + 1, nb - 1)
    cos_c, sin_c = table(cos_ref, i), table(sin_ref, i)
    cos_p, sin_p = table(cos_ref, ip), table(sin_ref, ip)
    cos_n, sin_n = table(cos_ref, inx), table(sin_ref, inx)

    qi = lax.broadcasted_iota(jnp.int32, (BLOCK, 3 * BLOCK + kx_ref.shape[0]), 0)
    kj = lax.broadcasted_iota(jnp.int32, (BLOCK, 3 * BLOCK + kx_ref.shape[0]), 1)
    never = 4 * BLOCK
    off_prev = jnp.where(i > 0, 0, never)
    off_next = jnp.where(i < nb - 1, 0, never)
    in_prev = kj < BLOCK
    in_next = (kj >= 2 * BLOCK) & (kj < 3 * BLOCK)
    valid = ((in_prev & (kj >= qi + off_prev))
             | (in_next & (kj - 2 * BLOCK + off_next <= qi))
             | ((kj >= BLOCK) & (kj < 2 * BLOCK)) | (kj >= 3 * BLOCK))

    for h in range(N_KV_HEADS):
        hs = slice(h * HEAD_DIM, (h + 1) * HEAD_DIM)
        k = jnp.concatenate([
            _rope(kp_ref[:, hs], cos_p, sin_p, first_half),
            _rope(kc_ref[:, hs], cos_c, sin_c, first_half),
            _rope(kn_ref[:, hs], cos_n, sin_n, first_half),
            kx_ref[:, hs]], axis=0).astype(BF16)
        v = jnp.concatenate([vp_ref[:, hs], vc_ref[:, hs], vn_ref[:, hs], vx_ref[:, hs]],
                            axis=0).astype(BF16)
        for g in range(Q_GROUP):
            head = h * Q_GROUP + g
            cs = slice(head * HEAD_DIM, (head + 1) * HEAD_DIM)
            q = _rope(q_ref[:, cs], cos_c, sin_c, first_half).astype(BF16)
            s = lax.dot_general(q, k, (((1,), (1,)), ((), ())), preferred_element_type=F32) * scale
            s = jnp.where(valid, s, NEG_INF)
            o_ref[:, cs] = _softmax_pv(s, sink_ref[head], v).astype(o_ref.dtype)


def _window_attention(p, kx, vx, sink, cos_t, sin_t):
    b, l, _ = p.shape
    c = kx.shape[1]
    nb = l // BLOCK
    kcol = Q_END // KV_WIDTH
    vcol = K_END // KV_WIDTH

    def kv_spec(col, shift):
        return pl.BlockSpec((None, BLOCK, KV_WIDTH),
                            lambda bi, i: (bi, jnp.clip(i + shift, 0, nb - 1), col))

    ctx_spec = pl.BlockSpec((None, c, KV_WIDTH), lambda bi, i: (bi, 0, 0))
    tab_spec = _single((l, HEAD_DIM), lambda bi, i: (0, 0))
    return pl.pallas_call(
        functools.partial(_win_attn_kernel, nb=nb),
        grid=(b, nb),
        in_specs=[pl.BlockSpec(memory_space=pltpu.SMEM),
                  pl.BlockSpec((None, BLOCK, ATTN_WIDTH), lambda bi, i: (bi, i, 0)),
                  kv_spec(kcol, -1), kv_spec(kcol, 0), kv_spec(kcol, 1),
                  kv_spec(vcol, -1), kv_spec(vcol, 0), kv_spec(vcol, 1),
                  ctx_spec, ctx_spec, tab_spec, tab_spec],
        out_specs=pl.BlockSpec((None, BLOCK, ATTN_WIDTH), lambda bi, i: (bi, i, 0)),
        out_shape=jax.ShapeDtypeStruct((b, l, ATTN_WIDTH), BF16),
        name="window_attention",
        compiler_params=_cparams(("arbitrary", "arbitrary")),
    )(sink, p, p, p, p, p, p, p, kx, vx, cos_t, sin_t)


def _ctx_attn_kernel(sink_ref, q_ref, k_ref, v_ref, o_ref):
    scale = HEAD_DIM ** -0.5
    for h in range(N_KV_HEADS):
        hs = slice(h * HEAD_DIM, (h + 1) * HEAD_DIM)
        k = k_ref[:, hs].astype(BF16)
        v = v_ref[:, hs].astype(BF16)
        for g in range(Q_GROUP):
            head = h * Q_GROUP + g
            cs = slice(head * HEAD_DIM, (head + 1) * HEAD_DIM)
            q = q_ref[:, cs].astype(BF16)
            s = lax.dot_general(q, k, (((1,), (1,)), ((), ())), preferred_element_type=F32) * scale
            o_ref[:, cs] = _softmax_pv(s, sink_ref[head], v).astype(o_ref.dtype)


def _context_attention(p, sink):
    b, c, _ = p.shape
    return pl.pallas_call(
        _ctx_attn_kernel,
        grid=(b,),
        in_specs=[pl.BlockSpec(memory_space=pltpu.SMEM),
                  pl.BlockSpec((None, c, ATTN_WIDTH), lambda bi: (bi, 0, 0)),
                  pl.BlockSpec((None, c, KV_WIDTH), lambda bi: (bi, 0, Q_END // KV_WIDTH)),
                  pl.BlockSpec((None, c, KV_WIDTH), lambda bi: (bi, 0, K_END // KV_WIDTH))],
        out_specs=pl.BlockSpec((None, c, ATTN_WIDTH), lambda bi: (bi, 0, 0)),
        out_shape=jax.ShapeDtypeStruct((b, c, ATTN_WIDTH), BF16),
        name="context_attention",
        compiler_params=_cparams(("arbitrary",)),
    )(sink, p, p, p)


def _fft_tables(l):
    n = 2 * l
    n1 = n // FFT_N2
    k1 = jnp.arange(n1, dtype=jnp.int32)[None, :, None]
    t = (FFT_N2 * jnp.arange(n1 // 2, dtype=jnp.int32)[None, None, :]
         + jnp.arange(FFT_N2, dtype=jnp.int32)[:, None, None])
    ang = ((k1 * t) % n).astype(F32) * (2.0 * math.pi / n)
    fwd1 = jnp.concatenate([jnp.cos(ang), -jnp.sin(ang)], axis=1)
    inv1 = jnp.swapaxes(fwd1, 1, 2) * (1.0 / n)
    tile = FFT_N2 * FFT_KG
    r = jnp.arange(tile, dtype=jnp.int32)
    same = (r[:, None] % FFT_KG) == (r[None, :] % FFT_KG)
    ang2 = (((r[:, None] // FFT_KG) * (r[None, :] // FFT_KG)) % FFT_N2).astype(F32) * (
        2.0 * math.pi / FFT_N2)
    cb = jnp.where(same, jnp.cos(ang2), 0.0)
    sb = jnp.where(same, jnp.sin(ang2), 0.0)
    fwd2 = jnp.concatenate([jnp.concatenate([cb, sb], axis=1),
                            jnp.concatenate([-sb, cb], axis=1)], axis=0)
    inv2 = jnp.concatenate([jnp.concatenate([cb, -sb], axis=1),
                            jnp.concatenate([sb, cb], axis=1)], axis=0)
    return fwd1.astype(BF16), inv1.astype(BF16), fwd2.astype(BF16), inv2.astype(BF16)


def _filter_features(l):
    bands = (FILTER_EMB - 1) // 2
    t = jnp.linspace(0.0, 1.0, l, dtype=F32)[:, None]
    w = 2 * math.pi * jnp.arange(l, dtype=F32)[:, None] / l
    f = jnp.linspace(1e-4, bands - 1, bands, dtype=F32)[None, :]
    z = jnp.concatenate([t, jnp.cos(f * w), -jnp.sin(f * w)], axis=-1)
    return jnp.pad(z, ((0, 0), (0, FILTER_HIDDEN - FILTER_EMB)))


def _filter_mlp_kernel(z_ref, w1_ref, b1_ref, f1_ref, w2_ref, b2_ref, f2_ref,
                       w3_ref, b3_ref, f3_ref, o_ref):
    h = jnp.sin(f1_ref[...] * (jnp.dot(z_ref[...], w1_ref[...], preferred_element_type=F32)
                               + b1_ref[...]))
    h = jnp.sin(f2_ref[...] * (jnp.dot(h, w2_ref[...], preferred_element_type=F32) + b2_ref[...]))
    o_ref[...] = jnp.sin(f3_ref[...] * (jnp.dot(h, w3_ref[...], preferred_element_type=F32)
                                        + b3_ref[...]))


def _filter_mlp(z, fw1, fb1, ff1, fw2, fb2, ff2, fw3, fb3, ff3):
    l = z.shape[0]
    row = lambda a: a.reshape(1, FILTER_HIDDEN)
    w1 = jnp.pad(fw1, ((0, FILTER_HIDDEN - FILTER_EMB), (0, 0)))
    return pl.pallas_call(
        _filter_mlp_kernel,
        out_shape=jax.ShapeDtypeStruct((l, FILTER_HIDDEN), F32),
        name="filter_mlp",
        compiler_params=pltpu.CompilerParams(vmem_limit_bytes=VMEM_LIMIT),
    )(z, w1, row(fb1), row(ff1), fw2, row(fb2), row(ff2), fw3, row(fb3), row(ff3))


def _fft_stage1(src_ref, f1_ref, a_ref, groups):
    n1 = groups * FFT_KG
    half = src_ref.shape[0] // FFT_N2

    def body(n2, carry):
        x = src_ref[pl.ds(n2, half, stride=FFT_N2), :].astype(BF16)
        res = jnp.dot(f1_ref[n2], x, preferred_element_type=F32)
        re, im = res[:n1], res[n1:]
        for g in range(groups):
            a_ref[g, 0, n2] = re[g * FFT_KG:(g + 1) * FFT_KG]
            a_ref[g, 1, n2] = im[g * FFT_KG:(g + 1) * FFT_KG]
        return carry

    lax.fori_loop(0, FFT_N2, body, 0)


def _filter_spec_kernel(h3_ref, wf_ref, wb_ref, df_ref, db_ref, f1_ref, f2_ref, kf_ref,
                        hf_ref, hb_ref, a_ref, *, l):
    groups = a_ref.shape[0]
    cb = a_ref.shape[-1]
    tile = FFT_N2 * FFT_KG
    row = lax.broadcasted_iota(jnp.int32, (l, 1), 0)
    t = row.astype(F32) * (1.0 / (l - 1))
    h3 = h3_ref[...]
    hf = jnp.dot(h3, wf_ref[...], preferred_element_type=F32) * jnp.exp(-t * df_ref[...])
    hb = jnp.dot(h3, wb_ref[...], preferred_element_type=F32) * jnp.exp(-t * db_ref[...])
    hb = jnp.where(row > 0, hb, 0.0)
    norm = jnp.sum(jnp.abs(hf), axis=0, keepdims=True) + jnp.sum(jnp.abs(hb), axis=0, keepdims=True)
    hf = hf / norm
    hb = hb / norm
    hf_ref[...] = hf
    hb_ref[...] = hb

    def stage2(g):
        a = a_ref[g].reshape(2 * tile, cb).astype(BF16)
        return jnp.dot(f2_ref[...], a, preferred_element_type=F32)

    def forward(g, carry):
        kf_ref[g] = stage2(g).reshape(2, tile, cb)
        return carry

    def backward(g, carry):
        x = stage2(g)
        kf_ref[g, 0] = kf_ref[g, 0] + x[:tile]
        kf_ref[g, 1] = kf_ref[g, 1] - x[tile:]
        return carry

    _fft_stage1(hf_ref, f1_ref, a_ref, groups)
    lax.fori_loop(0, groups, forward, 0)
    _fft_stage1(hb_ref, f1_ref, a_ref, groups)
    lax.fori_loop(0, groups, backward, 0)


def _filter_spectrum(h3, fw4, decay, tables):
    l = h3.shape[0]
    f1, _, f2, _ = tables
    n1 = 2 * l // FFT_N2
    groups = n1 // FFT_KG
    tile = FFT_N2 * FFT_KG
    nblk = HYENA_WIDTH // HY_CB
    return pl.pallas_call(
        functools.partial(_filter_spec_kernel, l=l),
        grid=(nblk,),
        in_specs=[_single((l, FILTER_HIDDEN), lambda c: (0, 0)),
                  pl.BlockSpec((FILTER_HIDDEN, HY_CB), lambda c: (0, c)),
                  pl.BlockSpec((FILTER_HIDDEN, HY_CB), lambda c: (0, c + nblk)),
                  pl.BlockSpec((1, HY_CB), lambda c: (0, c)),
                  pl.BlockSpec((1, HY_CB), lambda c: (0, c + nblk)),
                  _single(f1.shape, lambda c: (0, 0, 0)),
                  _single(f2.shape, lambda c: (0, 0))],
        out_specs=pl.BlockSpec((groups, 2, tile, HY_CB), lambda c: (0, 0, 0, c)),
        out_shape=jax.ShapeDtypeStruct((groups, 2, tile, HYENA_WIDTH), F32),
        scratch_shapes=[pltpu.VMEM((l, HY_CB), F32), pltpu.VMEM((l, HY_CB), F32),
                        pltpu.VMEM((groups, 2, FFT_N2, FFT_KG, HY_CB), F32)],
        name="filter_spectrum",
        compiler_params=_cparams(("arbitrary",)),
    )(h3, fw4, fw4, decay, decay, f1, f2)


def _short_conv(src_ref, w_ref, b_ref, l, rows):
    w0, w1, w2, b = w_ref[0:1, :], w_ref[1:2, :], w_ref[2:3, :], b_ref[...]
    r = lax.broadcasted_iota(jnp.int32, (rows, 1), 0)
    for s in range(0, l, rows):
        cur = src_ref[s:s + rows, :]
        before = src_ref[s - 1:s, :] if s > 0 else jnp.zeros((1, cur.shape[1]), F32)
        after = src_ref[s + rows:s + rows + 1, :] if s + rows < l else jnp.zeros((1, cur.shape[1]), F32)
        prev = jnp.where(r == 0, before, pltpu.roll(cur, 1, axis=0))
        nxt = jnp.where(r == rows - 1, after, pltpu.roll(cur, rows - 1, axis=0))
        yield s, prev * w0 + cur * w1 + nxt * w2 + b


def _hyena_kernel(x0_ref, x1_ref, v_ref, w0_ref, w1_ref, wv_ref, b0_ref, b1_ref, bv_ref,
                  bias_ref, kf_ref, f1_ref, g1_ref, f2_ref, g2_ref, o_ref,
                  x0c_ref, vg_ref, a_ref, *, l):
    groups = a_ref.shape[0]
    cb = a_ref.shape[-1]
    tile = FFT_N2 * FFT_KG
    n1 = groups * FFT_KG
    half = l // FFT_N2
    rows = min(l, 512)

    for s, u in _short_conv(x0_ref, w0_ref, b0_ref, l, rows):
        x0c_ref[s:s + rows, :] = u
    for (s, u1), (_, uv) in zip(_short_conv(x1_ref, w1_ref, b1_ref, l, rows),
                                _short_conv(v_ref, wv_ref, bv_ref, l, rows)):
        vg_ref[s:s + rows, :] = uv * u1

    _fft_stage1(vg_ref, f1_ref, a_ref, groups)

    def spectrum(g, carry):
        a = a_ref[g].reshape(2 * tile, cb).astype(BF16)
        x = jnp.dot(f2_ref[...], a, preferred_element_type=F32)
        xr, xi = x[:tile], x[tile:]
        kr, ki = kf_ref[g, 0], kf_ref[g, 1]
        y = jnp.concatenate([xr * kr - xi * ki, xr * ki + xi * kr], axis=0).astype(BF16)
        a_ref[g] = jnp.dot(g2_ref[...], y, preferred_element_type=F32).reshape(
            2, FFT_N2, FFT_KG, cb)
        return carry

    lax.fori_loop(0, groups, spectrum, 0)

    def synth(n2, carry):
        b = jnp.concatenate([a_ref[:, 0, n2].reshape(n1, cb), a_ref[:, 1, n2].reshape(n1, cb)],
                            axis=0).astype(BF16)
        y = jnp.dot(g1_ref[n2], b, preferred_element_type=F32)
        idx = pl.ds(n2, half, stride=FFT_N2)
        o_ref[idx, :] = (y + vg_ref[idx, :] * bias_ref[...]) * x0c_ref[idx, :]
        return carry

    lax.fori_loop(0, FFT_N2, synth, 0)


def _hyena(p, col0, conv_w, conv_b, bias_d, kf, tables):
    b, l, _ = p.shape
    f1, g1, f2, g2 = tables
    n1 = 2 * l // FFT_N2
    groups = n1 // FFT_KG
    tile = FFT_N2 * FFT_KG
    nblk = HYENA_WIDTH // HY_CB
    c0 = col0 // HY_CB

    def slab(part):
        return pl.BlockSpec((None, l, HY_CB), lambda c, bi: (bi, 0, c0 + part * nblk + c))

    def cw(part):
        return pl.BlockSpec((3, HY_CB), lambda c, bi: (0, part * nblk + c))

    def cbias(part):
        return pl.BlockSpec((1, HY_CB), lambda c, bi: (0, part * nblk + c))

    return pl.pallas_call(
        functools.partial(_hyena_kernel, l=l),
        grid=(nblk, b),
        in_specs=[slab(0), slab(1), slab(2), cw(0), cw(1), cw(2), cbias(0), cbias(1), cbias(2),
                  pl.BlockSpec((1, HY_CB), lambda c, bi: (0, c)),
                  pl.BlockSpec((groups, 2, tile, HY_CB), lambda c, bi: (0, 0, 0, c),
                               pipeline_mode=pl.Buffered(1)),
                  _single(f1.shape, lambda c, bi: (0, 0, 0)),
                  _single(g1.shape, lambda c, bi: (0, 0, 0)),
                  _single(f2.shape, lambda c, bi: (0, 0)),
                  _single(g2.shape, lambda c, bi: (0, 0))],
        out_specs=pl.BlockSpec((None, l, HY_CB), lambda c, bi: (bi, 0, c)),
        out_shape=jax.ShapeDtypeStruct((b, l, HYENA_WIDTH), F32),
        scratch_shapes=[pltpu.VMEM((l, HY_CB), F32), pltpu.VMEM((l, HY_CB), F32),
                        pltpu.VMEM((groups, 2, FFT_N2, FFT_KG, HY_CB), F32)],
        name="hyena_conv",
        compiler_params=_cparams(("arbitrary", "arbitrary")),
    )(p, p, p, conv_w, conv_w, conv_w, conv_b, conv_b, conv_b, bias_d, kf, f1, g1, f2, g2)


def _merge_kernel(a_ref, h_ref, ga_ref, gh_ref, wa_ref, wh_ref, o_ref):
    ya = jnp.dot(a_ref[...].astype(BF16), wa_ref[...].astype(BF16), preferred_element_type=F32)
    yh = jnp.dot(h_ref[...].astype(BF16), wh_ref[...].astype(BF16), preferred_element_type=F32)
    o_ref[...] = (jax.nn.sigmoid(ga_ref[...]) * ya
                  + jax.nn.sigmoid(gh_ref[...]) * yh).astype(o_ref.dtype)


def _merge(attn, hy, p, w_ao, w_ho, tm, tn):
    m = attn.shape[0]
    ga0 = HY_END // tn
    gh0 = GA_END // tn
    return pl.pallas_call(
        _merge_kernel,
        grid=(m // tm, D_MODEL // tn),
        in_specs=[pl.BlockSpec((tm, ATTN_WIDTH), lambda i, j: (i, 0)),
                  pl.BlockSpec((tm, HYENA_WIDTH), lambda i, j: (i, 0)),
                  pl.BlockSpec((tm, tn), lambda i, j: (i, ga0 + j)),
                  pl.BlockSpec((tm, tn), lambda i, j: (i, gh0 + j)),
                  pl.BlockSpec((ATTN_WIDTH, tn), lambda i, j: (0, j)),
                  pl.BlockSpec((HYENA_WIDTH, tn), lambda i, j: (0, j))],
        out_specs=pl.BlockSpec((tm, tn), lambda i, j: (i, j)),
        out_shape=jax.ShapeDtypeStruct((m, D_MODEL), BF16),
        name="branch_merge",
        compiler_params=_cparams(("arbitrary", "arbitrary")),
    )(attn, hy, p, p, w_ao, w_ho)


def _outproj_kernel(mix_ref, w_ref, x_ref, g_ref, gate_ref, o_ref, y_ref, *, tn):
    j = pl.program_id(1)
    nj = y_ref.shape[0]
    y_ref[j] = jnp.dot(mix_ref[...], w_ref[...].astype(BF16), preferred_element_type=F32)

    @pl.when(j == nj - 1)
    def _():
        ss = sum(jnp.sum(jnp.square(y_ref[k]), axis=-1, keepdims=True) for k in range(nj))
        r = lax.rsqrt(ss * (1.0 / D_MODEL) + EPS)
        for k in range(nj):
            cs = slice(k * tn, (k + 1) * tn)
            o_ref[:, cs] = x_ref[:, cs] + gate_ref[:, cs] * ((y_ref[k] * r) * g_ref[:, cs])


def _outproj(mix, w_o, x, g, gate, tm, tn):
    m = x.shape[0]
    nmod = gate.shape[0]
    blocks_per_mod = m // nmod // tm
    return pl.pallas_call(
        functools.partial(_outproj_kernel, tn=tn),
        grid=(m // tm, D_MODEL // tn),
        in_specs=[pl.BlockSpec((tm, D_MODEL), lambda i, j: (i, 0)),
                  pl.BlockSpec((D_MODEL, tn), lambda i, j: (0, j)),
                  pl.BlockSpec((tm, D_MODEL), lambda i, j: (i, 0)),
                  pl.BlockSpec((1, D_MODEL), lambda i, j: (0, 0)),
                  pl.BlockSpec((None, 1, D_MODEL), lambda i, j: (i // blocks_per_mod, 0, 0))],
        out_specs=pl.BlockSpec((tm, D_MODEL), lambda i, j: (i, 0)),
        out_shape=jax.ShapeDtypeStruct((m, D_MODEL), F32),
        scratch_shapes=[pltpu.VMEM((D_MODEL // tn, tm, tn), F32)],
        name="out_projection",
        compiler_params=_cparams(("arbitrary", "arbitrary")),
    )(mix, w_o, x, g.reshape(1, D_MODEL), gate)


def _mlp_kernel(x_ref, gin_ref, sc_ref, sh_ref, w1_ref, w2_ref, gout_ref, gate_ref, o_ref,
                h_ref, acc_ref):
    j = pl.program_id(1)

    @pl.when(j == 0)
    def _():
        h = _rms(x_ref[...]) * gin_ref[...]
        h_ref[...] = (h * (1.0 + sc_ref[...]) + sh_ref[...]).astype(BF16)

    a = jnp.dot(h_ref[...], w1_ref[...].astype(BF16), preferred_element_type=F32)
    a = jnp.square(jnp.maximum(a, 0.0)).astype(BF16)
    part = jnp.dot(a, w2_ref[...].astype(BF16), preferred_element_type=F32)

    @pl.when(j == 0)
    def _():
        acc_ref[...] = part

    @pl.when(j > 0)
    def _():
        acc_ref[...] += part

    @pl.when(j == pl.num_programs(1) - 1)
    def _():
        o_ref[...] = x_ref[...] + gate_ref[...] * (_rms(acc_ref[...]) * gout_ref[...])


def _mlp(x, g_in, scale, shift, w1, w2, g_out, gate, tm, tf):
    m = x.shape[0]
    nmod = gate.shape[0]
    blocks_per_mod = m // nmod // tm
    mod_spec = pl.BlockSpec((None, 1, D_MODEL), lambda i, j: (i // blocks_per_mod, 0, 0))
    row_spec = pl.BlockSpec((1, D_MODEL), lambda i, j: (0, 0))
    return pl.pallas_call(
        _mlp_kernel,
        grid=(m // tm, D_FF // tf),
        in_specs=[pl.BlockSpec((tm, D_MODEL), lambda i, j: (i, 0)),
                  row_spec, mod_spec, mod_spec,
                  pl.BlockSpec((D_MODEL, tf), lambda i, j: (0, j)),
                  pl.BlockSpec((tf, D_MODEL), lambda i, j: (j, 0)),
                  row_spec, mod_spec],
        out_specs=pl.BlockSpec((tm, D_MODEL), lambda i, j: (i, 0)),
        out_shape=jax.ShapeDtypeStruct((m, D_MODEL), F32),
        scratch_shapes=[pltpu.VMEM((tm, D_MODEL), BF16), pltpu.VMEM((tm, D_MODEL), F32)],
        name="channel_mlp",
        compiler_params=_cparams(("arbitrary", "arbitrary")),
    )(x, g_in.reshape(1, D_MODEL), scale, shift, w1, w2, g_out.reshape(1, D_MODEL), gate)


def _rope_tables(l):
    pos = jnp.arange(l, dtype=jnp.int32)
    quarter = HEAD_DIM // 4
    freqs = ROPE_THETA ** (-jnp.arange(quarter, dtype=F32) / quarter)
    ang_r = (pos // GRID_W).astype(F32)[:, None] * freqs[None, :]
    ang_c = (pos % GRID_W).astype(F32)[:, None] * freqs[None, :]
    cos_t = jnp.concatenate([jnp.cos(ang_r), jnp.cos(ang_r), jnp.cos(ang_c), jnp.cos(ang_c)], axis=-1)
    sin_t = jnp.concatenate([-jnp.sin(ang_r), jnp.sin(ang_r), -jnp.sin(ang_c), jnp.sin(ang_c)], axis=-1)
    return cos_t, sin_t


def _decay_rates():
    min_decay = math.log(DECAY_TARGET) / SLOW_DECAY_PCT
    max_decay = math.log(DECAY_TARGET) / FAST_DECAY_PCT
    deltas = jnp.tile(jnp.linspace(min_decay, max_decay, HYENA_WIDTH, dtype=F32), 2)
    return jnp.abs(deltas)[None, :]


def kernel(x, c, ctx, c_ctx, w_mod, b_mod, norm_g, w_in, attn_sink, hy_conv_w, hy_conv_b,
           hy_fw1, hy_fb1, hy_ff1, hy_fw2, hy_fb2, hy_ff2, hy_fw3, hy_fb3, hy_ff3, hy_fw4,
           hy_bias, w_attn_out, w_hyena_out, w_out, w_ff1, w_ff2):
    b, l, d = x.shape
    cl = ctx.shape[1]
    assert d == D_MODEL and l % 1024 == 0 and cl % 256 == 0 and b + 1 <= MOD_ROWS

    cos_t, sin_t = _rope_tables(l)
    decay = _decay_rates()
    tables_lat = _fft_tables(l)
    tables_ctx = _fft_tables(cl)
    z_lat = _filter_features(l)
    z_ctx = _filter_features(cl)

    c_rows = jnp.concatenate([c, c_ctx[None, :], jnp.zeros((MOD_ROWS - b - 1, d), F32)], axis=0)
    x_lat = x.reshape(b * l, d)
    x_ctx = ctx.reshape(b * cl, d)
    tm_ctx = b * cl

    for layer in range(DEPTH):
        last = layer == DEPTH - 1
        mod = _modulation(c_rows, w_mod[layer], b_mod[layer])
        mod_lat = [mod[:b, k * d:(k + 1) * d].reshape(b, 1, d) for k in range(N_MOD)]
        mod_ctx = [mod[b:b + 1, k * d:(k + 1) * d].reshape(1, 1, d) for k in range(N_MOD)]
        sh1, sc1, g1, sh2, sc2, g2 = mod_lat
        csh1, csc1, cg1, csh2, csc2, cg2 = mod_ctx
        g = norm_g[layer]
        fparams = (hy_fw1[layer], hy_fb1[layer], hy_ff1[layer], hy_fw2[layer], hy_fb2[layer],
                   hy_ff2[layer], hy_fw3[layer], hy_fb3[layer], hy_ff3[layer])
        bias_d = hy_bias[layer].reshape(1, HYENA_WIDTH)
        conv_b = hy_conv_b[layer].reshape(1, 3 * HYENA_WIDTH)

        p_lat = _normproj(x_lat, g[0], sc1, sh1, w_in[layer], 1024, 512)
        if last:
            kv_ctx = _normproj(x_ctx, g[0], csc1, csh1, w_in[layer][:, Q_END:V_END], tm_ctx, 512)
            kx, vx = kv_ctx[:, :KV_WIDTH], kv_ctx[:, KV_WIDTH:]
        else:
            p_ctx = _normproj(x_ctx, g[0], csc1, csh1, w_in[layer], tm_ctx, 512)
            kx, vx = p_ctx[:, Q_END:K_END], p_ctx[:, K_END:V_END]
        kx = kx.reshape(b, cl, KV_WIDTH)
        vx = vx.reshape(b, cl, KV_WIDTH)

        p3 = p_lat.reshape(b, l, IN_WIDTH)
        attn = _window_attention(p3, kx, vx, attn_sink[layer], cos_t, sin_t)
        kf = _filter_spectrum(_filter_mlp(z_lat, *fparams), hy_fw4[layer], decay, tables_lat)
        hy = _hyena(p3, V_END, hy_conv_w[layer], conv_b, bias_d, kf, tables_lat)
        mix = _merge(attn.reshape(b * l, ATTN_WIDTH), hy.reshape(b * l, HYENA_WIDTH), p_lat,
                     w_attn_out[layer], w_hyena_out[layer], 1024, 512)
        x_lat = _outproj(mix, w_out[layer], x_lat, g[1], g1, 512, 512)

        if not last:
            pc3 = p_ctx.reshape(b, cl, IN_WIDTH)
            attn_c = _context_attention(pc3, attn_sink[layer])
            kf_c = _filter_spectrum(_filter_mlp(z_ctx, *fparams), hy_fw4[layer], decay, tables_ctx)
            hy_c = _hyena(pc3, V_END, hy_conv_w[layer], conv_b, bias_d, kf_c, tables_ctx)
            mix_c = _merge(attn_c.reshape(b * cl, ATTN_WIDTH), hy_c.reshape(b * cl, HYENA_WIDTH),
                           p_ctx, w_attn_out[layer], w_hyena_out[layer], tm_ctx, 512)
            x_ctx = _outproj(mix_c, w_out[layer], x_ctx, g[1], cg1, tm_ctx, 512)
            x_ctx = _mlp(x_ctx, g[2], csc2, csh2, w_ff1[layer], w_ff2[layer], g[3], cg2, tm_ctx, 256)

        x_lat = _mlp(x_lat, g[2], sc2, sh2, w_ff1[layer], w_ff2[layer], g[3], g2, 512, 256)
    return x_lat.reshape(b, l, d)
```

```python
import functools
import math

import jax
import jax.numpy as jnp
from jax import lax
from jax.experimental import pallas as pl
from jax.experimental.pallas import tpu as pltpu

F32 = jnp.float32
BF16 = jnp.bfloat16

D_MODEL = 2048
DEPTH = 2
GRID_W = 64
HEAD_DIM = 128
N_Q_HEADS = 8
N_KV_HEADS = 2
Q_GROUP = N_Q_HEADS // N_KV_HEADS
ATTN_WIDTH = N_Q_HEADS * HEAD_DIM
KV_WIDTH = N_KV_HEADS * HEAD_DIM
BLOCK = 128
ROPE_THETA = 10000.0
HYENA_WIDTH = 1024
FILTER_EMB = 33
FILTER_HIDDEN = 64
DECAY_TARGET = 1e-2
FAST_DECAY_PCT = 0.3
SLOW_DECAY_PCT = 1.5
D_FF = 4 * D_MODEL
EPS = 1e-6
N_MOD = 6
NEG_INF = -1e30
Q_END = ATTN_WIDTH
K_END = Q_END + KV_WIDTH
V_END = K_END + KV_WIDTH
HY_END = V_END + 3 * HYENA_WIDTH
GA_END = HY_END + D_MODEL
GH_END = GA_END + D_MODEL
IN_WIDTH = GH_END

LANES = 128
VMEM_LIMIT = 56 * 1024 * 1024

FFT_N2 = 16
FFT_KG = 16
HY_CB = 128
MOD_ROWS = 8


def _cparams(sem):
    return pltpu.CompilerParams(dimension_semantics=sem, vmem_limit_bytes=VMEM_LIMIT)


def _single(block_shape, index_map):
    return pl.BlockSpec(block_shape, index_map, pipeline_mode=pl.Buffered(1))


def _rms(x):
    return x * lax.rsqrt(jnp.mean(x * x, axis=-1, keepdims=True) + EPS)


def _mod_kernel(c_ref, w_ref, b_ref, o_ref):
    c = c_ref[...]
    s = c * jax.nn.sigmoid(c)
    o_ref[...] = jnp.dot(s.astype(BF16), w_ref[...].astype(BF16),
                         preferred_element_type=F32) + b_ref[...]


def _modulation(c_rows, w, layer, b):
    n = w.shape[2]
    tn = 1024
    return pl.pallas_call(
        _mod_kernel,
        grid=(n // tn,),
        in_specs=[pl.BlockSpec((MOD_ROWS, D_MODEL), lambda j: (0, 0)),
                  pl.BlockSpec((None, D_MODEL, tn), lambda j: (layer, 0, j)),
                  pl.BlockSpec((1, tn), lambda j: (0, j))],
        out_specs=pl.BlockSpec((MOD_ROWS, tn), lambda j: (0, j)),
        out_shape=jax.ShapeDtypeStruct((MOD_ROWS, n), F32),
        name="modulation",
        compiler_params=_cparams(("arbitrary",)),
    )(c_rows, w, b.reshape(1, n))


def _normproj_kernel(x_ref, g_ref, sc_ref, sh_ref, w_ref, o_ref, h_ref):
    @pl.when(pl.program_id(1) == 0)
    def _():
        h = _rms(x_ref[...]) * g_ref[...]
        h_ref[...] = (h * (1.0 + sc_ref[...]) + sh_ref[...]).astype(BF16)

    o_ref[...] = jnp.dot(h_ref[...], w_ref[...].astype(BF16), preferred_element_type=F32)


def _normproj(x, g, scale, shift, w, layer, col0, n, tm, tn):
    m = x.shape[0]
    j0 = col0 // tn
    nmod = scale.shape[0]
    blocks_per_mod = m // nmod // tm
    mod_spec = pl.BlockSpec((None, 1, D_MODEL), lambda i, j: (i // blocks_per_mod, 0, 0))
    return pl.pallas_call(
        _normproj_kernel,
        grid=(m // tm, n // tn),
        in_specs=[pl.BlockSpec((tm, D_MODEL), lambda i, j: (i, 0)),
                  pl.BlockSpec((1, D_MODEL), lambda i, j: (0, 0)),
                  mod_spec, mod_spec,
                  pl.BlockSpec((None, D_MODEL, tn), lambda i, j: (layer, 0, j0 + j))],
        out_specs=pl.BlockSpec((tm, tn), lambda i, j: (i, j)),
        out_shape=jax.ShapeDtypeStruct((m, n), F32),
        scratch_shapes=[pltpu.VMEM((tm, D_MODEL), BF16)],
        name="normproj",
        compiler_params=_cparams(("arbitrary", "arbitrary")),
    )(x, g.reshape(1, D_MODEL), scale, shift, w)


def _rope(x, cos, sin_signed, first_half):
    rot = jnp.where(first_half, pltpu.roll(x, HEAD_DIM - 32, axis=1), pltpu.roll(x, 32, axis=1))
    return x * cos + rot * sin_signed


def _softmax_pv(s, sink, v):
    m = jnp.maximum(jnp.max(s, axis=-1, keepdims=True), sink)
    e = jnp.exp(s - m)
    denom = jnp.sum(e, axis=-1, keepdims=True) + jnp.exp(sink - m)
    o = jnp.dot(e.astype(BF16), v, preferred_element_type=F32)
    return o / denom


def _win_attn_kernel(sink_ref, q_ref, kp_ref, kc_ref, kn_ref, vp_ref, vc_ref, vn_ref,
                     kx_ref, vx_ref, cos_ref, sin_ref, o_ref, *, nb):
    i = pl.program_id(1)
    scale = HEAD_DIM ** -0.5
    lane = lax.broadcasted_iota(jnp.int32, (BLOCK, HEAD_DIM), 1)
    first_half = (lane % 64) < 32

    def table(ref, blk):
        return ref[pl.ds(pl.multiple_of(blk * BLOCK, BLOCK), BLOCK), :]

    ip = jnp.maximum(i - 1, 0)
    inx = jnp.minimum(i + 1, nb - 1)
    cos_c, sin_c = table(cos_ref, i), table(sin_ref, i)
    cos_p, sin_p = table(cos_ref, ip), table(sin_ref, ip)
    cos_n, sin_n = table(cos_ref, inx), table(sin_ref, inx)

    qi = lax.broadcasted_iota(jnp.int32, (BLOCK, 3 * BLOCK + kx_ref.shape[0]), 0)
    kj = lax.broadcasted_iota(jnp.int32, (BLOCK, 3 * BLOCK + kx_ref.shape[0]), 1)
    never = 4 * BLOCK
    off_prev = jnp.where(i > 0, 0, never)
    off_next = jnp.where(i < nb - 1, 0, never)
    in_prev = kj < BLOCK
    in_next = (kj >= 2 * BLOCK) & (kj < 3 * BLOCK)
    valid = ((in_prev & (kj >= qi + off_prev))
             | (in_next & (kj - 2 * BLOCK + off_next <= qi))
             | ((kj >= BLOCK) & (kj < 2 * BLOCK)) | (kj >= 3 * BLOCK))

    for h in range(N_KV_HEADS):
        hs = slice(h * HEAD_DIM, (h + 1) * HEAD_DIM)
        k = jnp.concatenate([
            _rope(kp_ref[:, hs], cos_p, sin_p, first_half),
            _rope(kc_ref[:, hs], cos_c, sin_c, first_half),
            _rope(kn_ref[:, hs], cos_n, sin_n, first_half),
            kx_ref[:, hs]], axis=0).astype(BF16)
        v = jnp.concatenate([vp_ref[:, hs], vc_ref[:, hs], vn_ref[:, hs], vx_ref[:, hs]],
                            axis=0).astype(BF16)
        for g in range(Q_GROUP):
            head = h * Q_GROUP + g
            cs = slice(head * HEAD_DIM, (head + 1) * HEAD_DIM)
            q = _rope(q_ref[:, cs], cos_c, sin_c, first_half).astype(BF16)
            s = lax.dot_general(q, k, (((1,), (1,)), ((), ())), preferred_element_type=F32) * scale
            s = jnp.where(valid, s, NEG_INF)
            o_ref[:, cs] = _softmax_pv(s, sink_ref[head], v).astype(o_ref.dtype)


def _window_attention(p, kx, vx, sink, cos_t, sin_t):
    b, l, _ = p.shape
    c = kx.shape[1]
    nb = l // BLOCK
    kcol = Q_END // KV_WIDTH
    vcol = K_END // KV_WIDTH

    def kv_spec(col, shift):
        return pl.BlockSpec((None, BLOCK, KV_WIDTH),
                            lambda bi, i: (bi, jnp.clip(i + shift, 0, nb - 1), col))

    ctx_spec = pl.BlockSpec((None, c, KV_WIDTH), lambda bi, i: (bi, 0, 0))
    tab_spec = _single((l, HEAD_DIM), lambda bi, i: (0, 0))
    return pl.pallas_call(
        functools.partial(_win_attn_kernel, nb=nb),
        grid=(b, nb),
        in_specs=[pl.BlockSpec(memory_space=pltpu.SMEM),
                  pl.BlockSpec((None, BLOCK, ATTN_WIDTH), lambda bi, i: (bi, i, 0)),
                  kv_spec(kcol, -1), kv_spec(kcol, 0), kv_spec(kcol, 1),
                  kv_spec(vcol, -1), kv_spec(vcol, 0), kv_spec(vcol, 1),
                  ctx_spec, ctx_spec, tab_spec, tab_spec],
        out_specs=pl.BlockSpec((None, BLOCK, ATTN_WIDTH), lambda bi, i: (bi, i, 0)),
        out_shape=jax.ShapeDtypeStruct((b, l, ATTN_WIDTH), BF16),
        name="window_attention",
        compiler_params=_cparams(("arbitrary", "arbitrary")),
    )(sink, p, p, p, p, p, p, p, kx, vx, cos_t, sin_t)


def _ctx_attn_kernel(sink_ref, q_ref, k_ref, v_ref, o_ref):
    scale = HEAD_DIM ** -0.5
    for h in range(N_KV_HEADS):
        hs = slice(h * HEAD_DIM, (h + 1) * HEAD_DIM)
        k = k_ref[:, hs].astype(BF16)
        v = v_ref[:, hs].astype(BF16)
        for g in range(Q_GROUP):
            head = h * Q_GROUP + g
            cs = slice(head * HEAD_DIM, (head + 1) * HEAD_DIM)
            q = q_ref[:, cs].astype(BF16)
            s = lax.dot_general(q, k, (((1,), (1,)), ((), ())), preferred_element_type=F32) * scale
            o_ref[:, cs] = _softmax_pv(s, sink_ref[head], v).astype(o_ref.dtype)


def _context_attention(p, sink):
    b, c, _ = p.shape
    return pl.pallas_call(
        _ctx_attn_kernel,
        grid=(b,),
        in_specs=[pl.BlockSpec(memory_space=pltpu.SMEM),
                  pl.BlockSpec((None, c, ATTN_WIDTH), lambda bi: (bi, 0, 0)),
                  pl.BlockSpec((None, c, KV_WIDTH), lambda bi: (bi, 0, Q_END // KV_WIDTH)),
                  pl.BlockSpec((None, c, KV_WIDTH), lambda bi: (bi, 0, K_END // KV_WIDTH))],
        out_specs=pl.BlockSpec((None, c, ATTN_WIDTH), lambda bi: (bi, 0, 0)),
        out_shape=jax.ShapeDtypeStruct((b, c, ATTN_WIDTH), BF16),
        name="context_attention",
        compiler_params=_cparams(("arbitrary",)),
    )(sink, p, p, p)


def _fft_tables(l):
    n = 2 * l
    n1 = n // FFT_N2
    k1 = jnp.arange(n1, dtype=jnp.int32)[None, :, None]
    t = (FFT_N2 * jnp.arange(n1 // 2, dtype=jnp.int32)[None, None, :]
         + jnp.arange(FFT_N2, dtype=jnp.int32)[:, None, None])
    ang = ((k1 * t) % n).astype(F32) * (2.0 * math.pi / n)
    fwd1 = jnp.concatenate([jnp.cos(ang), -jnp.sin(ang)], axis=1)
    inv1 = jnp.swapaxes(fwd1, 1, 2) * (1.0 / n)
    tile = FFT_N2 * FFT_KG
    r = jnp.arange(tile, dtype=jnp.int32)
    same = (r[:, None] % FFT_KG) == (r[None, :] % FFT_KG)
    ang2 = (((r[:, None] // FFT_KG) * (r[None, :] // FFT_KG)) % FFT_N2).astype(F32) * (
        2.0 * math.pi / FFT_N2)
    cb = jnp.where(same, jnp.cos(ang2), 0.0)
    sb = jnp.where(same, jnp.sin(ang2), 0.0)
    fwd2 = jnp.concatenate([jnp.concatenate([cb, sb], axis=1),
                            jnp.concatenate([-sb, cb], axis=1)], axis=0)
    inv2 = jnp.concatenate([jnp.concatenate([cb, -sb], axis=1),
                            jnp.concatenate([sb, cb], axis=1)], axis=0)
    return fwd1.astype(BF16), inv1.astype(BF16), fwd2.astype(BF16), inv2.astype(BF16)


def _filter_features(l):
    bands = (FILTER_EMB - 1) // 2
    t = jnp.linspace(0.0, 1.0, l, dtype=F32)[:, None]
    w = 2 * math.pi * jnp.arange(l, dtype=F32)[:, None] / l
    f = jnp.linspace(1e-4, bands - 1, bands, dtype=F32)[None, :]
    z = jnp.concatenate([t, jnp.cos(f * w), -jnp.sin(f * w)], axis=-1)
    return jnp.pad(z, ((0, 0), (0, FILTER_HIDDEN - FILTER_EMB)))


def _filter_mlp_kernel(z_ref, w1_ref, b1_ref, f1_ref, w2_ref, b2_ref, f2_ref,
                       w3_ref, b3_ref, f3_ref, o_ref):
    h = jnp.sin(f1_ref[...] * (jnp.dot(z_ref[...], w1_ref[...], preferred_element_type=F32)
                               + b1_ref[...]))
    h = jnp.sin(f2_ref[...] * (jnp.dot(h, w2_ref[...], preferred_element_type=F32) + b2_ref[...]))
    o_ref[...] = jnp.sin(f3_ref[...] * (jnp.dot(h, w3_ref[...], preferred_element_type=F32)
                                        + b3_ref[...]))


def _filter_mlp(z, fw1, fb1, ff1, fw2, fb2, ff2, fw3, fb3, ff3):
    l = z.shape[0]
    row = lambda a: a.reshape(1, FILTER_HIDDEN)
    w1 = jnp.pad(fw1, ((0, FILTER_HIDDEN - FILTER_EMB), (0, 0)))
    return pl.pallas_call(
        _filter_mlp_kernel,
        out_shape=jax.ShapeDtypeStruct((l, FILTER_HIDDEN), F32),
        name="filter_mlp",
        compiler_params=pltpu.CompilerParams(vmem_limit_bytes=VMEM_LIMIT),
    )(z, w1, row(fb1), row(ff1), fw2, row(fb2), row(ff2), fw3, row(fb3), row(ff3))


def _fft_stage1(src_ref, f1_ref, a_ref, groups):
    n1 = groups * FFT_KG
    half = src_ref.shape[0] // FFT_N2

    def body(n2, carry):
        x = src_ref[pl.ds(n2, half, stride=FFT_N2), :].astype(BF16)
        res = jnp.dot(f1_ref[n2], x, preferred_element_type=F32)
        re, im = res[:n1], res[n1:]
        for g in range(groups):
            a_ref[g, 0, n2] = re[g * FFT_KG:(g + 1) * FFT_KG]
            a_ref[g, 1, n2] = im[g * FFT_KG:(g + 1) * FFT_KG]
        return carry

    lax.fori_loop(0, FFT_N2, body, 0)


def _filter_spec_kernel(h3_ref, wf_ref, wb_ref, df_ref, db_ref, f1_ref, f2_ref, kf_ref,
                        hf_ref, hb_ref, a_ref, *, l):
    groups = a_ref.shape[0]
    cb = a_ref.shape[-1]
    tile = FFT_N2 * FFT_KG
    row = lax.broadcasted_iota(jnp.int32, (l, 1), 0)
    t = row.astype(F32) * (1.0 / (l - 1))
    h3 = h3_ref[...]
    hf = jnp.dot(h3, wf_ref[...], preferred_element_type=F32) * jnp.exp(-t * df_ref[...])
    hb = jnp.dot(h3, wb_ref[...], preferred_element_type=F32) * jnp.exp(-t * db_ref[...])
    hb = jnp.where(row > 0, hb, 0.0)
    norm = jnp.sum(jnp.abs(hf), axis=0, keepdims=True) + jnp.sum(jnp.abs(hb), axis=0, keepdims=True)
    hf = hf / norm
    hb = hb / norm
    hf_ref[...] = hf
    hb_ref[...] = hb

    def stage2(g):
        a = a_ref[g].reshape(2 * tile, cb).astype(BF16)
        return jnp.dot(f2_ref[...], a, preferred_element_type=F32)

    def forward(g, carry):
        kf_ref[g] = stage2(g).reshape(2, tile, cb)
        return carry

    def backward(g, carry):
        x = stage2(g)
        kf_ref[g, 0] = kf_ref[g, 0] + x[:tile]
        kf_ref[g, 1] = kf_ref[g, 1] - x[tile:]
        return carry

    _fft_stage1(hf_ref, f1_ref, a_ref, groups)
    lax.fori_loop(0, groups, forward, 0)
    _fft_stage1(hb_ref, f1_ref, a_ref, groups)
    lax.fori_loop(0, groups, backward, 0)


def _filter_spectrum(h3, fw4, decay, tables):
    l = h3.shape[0]
    f1, _, f2, _ = tables
    n1 = 2 * l // FFT_N2
    groups = n1 // FFT_KG
    tile = FFT_N2 * FFT_KG
    nblk = HYENA_WIDTH // HY_CB
    return pl.pallas_call(
        functools.partial(_filter_spec_kernel, l=l),
        grid=(nblk,),
        in_specs=[_single((l, FILTER_HIDDEN), lambda c: (0, 0)),
                  pl.BlockSpec((FILTER_HIDDEN, HY_CB), lambda c: (0, c)),
                  pl.BlockSpec((FILTER_HIDDEN, HY_CB), lambda c: (0, c + nblk)),
                  pl.BlockSpec((1, HY_CB), lambda c: (0, c)),
                  pl.BlockSpec((1, HY_CB), lambda c: (0, c + nblk)),
                  _single(f1.shape, lambda c: (0, 0, 0)),
                  _single(f2.shape, lambda c: (0, 0))],
        out_specs=pl.BlockSpec((groups, 2, tile, HY_CB), lambda c: (0, 0, 0, c)),
        out_shape=jax.ShapeDtypeStruct((groups, 2, tile, HYENA_WIDTH), F32),
        scratch_shapes=[pltpu.VMEM((l, HY_CB), F32), pltpu.VMEM((l, HY_CB), F32),
                        pltpu.VMEM((groups, 2, FFT_N2, FFT_KG, HY_CB), F32)],
        name="filter_spectrum",
        compiler_params=_cparams(("arbitrary",)),
    )(h3, fw4, fw4, decay, decay, f1, f2)


def _short_conv(src_ref, w_ref, b_ref, l, rows):
    w0, w1, w2, b = w_ref[0:1, :], w_ref[1:2, :], w_ref[2:3, :], b_ref[...]
    r = lax.broadcasted_iota(jnp.int32, (rows, 1), 0)
    for s in range(0, l, rows):
        cur = src_ref[s:s + rows, :]
        before = src_ref[s - 1:s, :] if s > 0 else jnp.zeros((1, cur.shape[1]), F32)
        after = src_ref[s + rows:s + rows + 1, :] if s + rows < l else jnp.zeros((1, cur.shape[1]), F32)
        prev = jnp.where(r == 0, before, pltpu.roll(cur, 1, axis=0))
        nxt = jnp.where(r == rows - 1, after, pltpu.roll(cur, rows - 1, axis=0))
        yield s, prev * w0 + cur * w1 + nxt * w2 + b


def _hyena_kernel(x0_ref, x1_ref, v_ref, w0_ref, w1_ref, wv_ref, b0_ref, b1_ref, bv_ref,
                  bias_ref, kf_ref, f1_ref, g1_ref, f2_ref, g2_ref, o_ref,
                  x0c_ref, vg_ref, a_ref, *, l):
    groups = a_ref.shape[0]
    cb = a_ref.shape[-1]
    tile = FFT_N2 * FFT_KG
    n1 = groups * FFT_KG
    half = l // FFT_N2
    rows = min(l, 512)

    for s, u in _short_conv(x0_ref, w0_ref, b0_ref, l, rows):
        x0c_ref[s:s + rows, :] = u
    for (s, u1), (_, uv) in zip(_short_conv(x1_ref, w1_ref, b1_ref, l, rows),
                                _short_conv(v_ref, wv_ref, bv_ref, l, rows)):
        vg_ref[s:s + rows, :] = uv * u1

    _fft_stage1(vg_ref, f1_ref, a_ref, groups)

    def spectrum(g, carry):
        a = a_ref[g].reshape(2 * tile, cb).astype(BF16)
        x = jnp.dot(f2_ref[...], a, preferred_element_type=F32)
        xr, xi = x[:tile], x[tile:]
        kr, ki = kf_ref[g, 0], kf_ref[g, 1]
        y = jnp.concatenate([xr * kr - xi * ki, xr * ki + xi * kr], axis=0).astype(BF16)
        a_ref[g] = jnp.dot(g2_ref[...], y, preferred_element_type=F32).reshape(
            2, FFT_N2, FFT_KG, cb)
        return carry

    lax.fori_loop(0, groups, spectrum, 0, unroll=2)

    def synth(n2, carry):
        b = jnp.concatenate([a_ref[:, 0, n2].reshape(n1, cb), a_ref[:, 1, n2].reshape(n1, cb)],
                            axis=0).astype(BF16)
        y = jnp.dot(g1_ref[n2], b, preferred_element_type=F32)
        idx = pl.ds(n2, half, stride=FFT_N2)
        o_ref[idx, :] = (y + vg_ref[idx, :] * bias_ref[...]) * x0c_ref[idx, :]
        return carry

    lax.fori_loop(0, FFT_N2, synth, 0)


def _hyena(p, col0, conv_w, conv_b, bias_d, kf, tables):
    b, l, _ = p.shape
    f1, g1, f2, g2 = tables
    n1 = 2 * l // FFT_N2
    groups = n1 // FFT_KG
    tile = FFT_N2 * FFT_KG
    nblk = HYENA_WIDTH // HY_CB
    c0 = col0 // HY_CB

    def slab(part):
        return pl.BlockSpec((None, l, HY_CB), lambda c, bi: (bi, 0, c0 + part * nblk + c))

    def cw(part):
        return pl.BlockSpec((3, HY_CB), lambda c, bi: (0, part * nblk + c))

    def cbias(part):
        return pl.BlockSpec((1, HY_CB), lambda c, bi: (0, part * nblk + c))

    return pl.pallas_call(
        functools.partial(_hyena_kernel, l=l),
        grid=(nblk, b),
        in_specs=[slab(0), slab(1), slab(2), cw(0), cw(1), cw(2), cbias(0), cbias(1), cbias(2),
                  pl.BlockSpec((1, HY_CB), lambda c, bi: (0, c)),
                  pl.BlockSpec((groups, 2, tile, HY_CB), lambda c, bi: (0, 0, 0, c),
                               pipeline_mode=pl.Buffered(1)),
                  _single(f1.shape, lambda c, bi: (0, 0, 0)),
                  _single(g1.shape, lambda c, bi: (0, 0, 0)),
                  _single(f2.shape, lambda c, bi: (0, 0)),
                  _single(g2.shape, lambda c, bi: (0, 0))],
        out_specs=pl.BlockSpec((None, l, HY_CB), lambda c, bi: (bi, 0, c)),
        out_shape=jax.ShapeDtypeStruct((b, l, HYENA_WIDTH), F32),
        scratch_shapes=[pltpu.VMEM((l, HY_CB), F32), pltpu.VMEM((l, HY_CB), F32),
                        pltpu.VMEM((groups, 2, FFT_N2, FFT_KG, HY_CB), F32)],
        name="hyena_conv",
        compiler_params=_cparams(("arbitrary", "arbitrary")),
    )(p, p, p, conv_w, conv_w, conv_w, conv_b, conv_b, conv_b, bias_d, kf, f1, g1, f2, g2)


def _merge_kernel(a_ref, h_ref, ga_ref, gh_ref, wa_ref, wh_ref, o_ref):
    ya = jnp.dot(a_ref[...], wa_ref[...].astype(BF16), preferred_element_type=F32)
    yh = jnp.dot(h_ref[...].astype(BF16), wh_ref[...].astype(BF16), preferred_element_type=F32)
    o_ref[...] = (jax.nn.sigmoid(ga_ref[...]) * ya
                  + jax.nn.sigmoid(gh_ref[...]) * yh).astype(o_ref.dtype)


def _merge(attn, hy, p, w_ao, w_ho, layer, tm, tn):
    m = attn.shape[0]
    ga0 = HY_END // tn
    gh0 = GA_END // tn
    return pl.pallas_call(
        _merge_kernel,
        grid=(m // tm, D_MODEL // tn),
        in_specs=[pl.BlockSpec((tm, ATTN_WIDTH), lambda i, j: (i, 0)),
                  pl.BlockSpec((tm, HYENA_WIDTH), lambda i, j: (i, 0)),
                  pl.BlockSpec((tm, tn), lambda i, j: (i, ga0 + j)),
                  pl.BlockSpec((tm, tn), lambda i, j: (i, gh0 + j)),
                  pl.BlockSpec((None, ATTN_WIDTH, tn), lambda i, j: (layer, 0, j)),
                  pl.BlockSpec((None, HYENA_WIDTH, tn), lambda i, j: (layer, 0, j))],
        out_specs=pl.BlockSpec((tm, tn), lambda i, j: (i, j)),
        out_shape=jax.ShapeDtypeStruct((m, D_MODEL), BF16),
        name="branch_merge",
        compiler_params=_cparams(("arbitrary", "arbitrary")),
    )(attn, hy, p, p, w_ao, w_ho)


def _outproj_kernel(mix_ref, w_ref, x_ref, g_ref, gate_ref, o_ref, y_ref, *, tn):
    j = pl.program_id(1)
    nj = y_ref.shape[0]
    y_ref[j] = jnp.dot(mix_ref[...], w_ref[...].astype(BF16), preferred_element_type=F32)

    @pl.when(j == nj - 1)
    def _():
        ss = sum(jnp.sum(jnp.square(y_ref[k]), axis=-1, keepdims=True) for k in range(nj))
        r = lax.rsqrt(ss * (1.0 / D_MODEL) + EPS)
        for k in range(nj):
            cs = slice(k * tn, (k + 1) * tn)
            o_ref[:, cs] = x_ref[:, cs] + gate_ref[:, cs] * ((y_ref[k] * r) * g_ref[:, cs])


def _outproj(mix, w_o, layer, x, g, gate, tm, tn):
    m = x.shape[0]
    nmod = gate.shape[0]
    blocks_per_mod = m // nmod // tm
    return pl.pallas_call(
        functools.partial(_outproj_kernel, tn=tn),
        grid=(m // tm, D_MODEL // tn),
        in_specs=[pl.BlockSpec((tm, D_MODEL), lambda i, j: (i, 0)),
                  pl.BlockSpec((None, D_MODEL, tn), lambda i, j: (layer, 0, j)),
                  pl.BlockSpec((tm, D_MODEL), lambda i, j: (i, 0)),
                  pl.BlockSpec((1, D_MODEL), lambda i, j: (0, 0)),
                  pl.BlockSpec((None, 1, D_MODEL), lambda i, j: (i // blocks_per_mod, 0, 0))],
        out_specs=pl.BlockSpec((tm, D_MODEL), lambda i, j: (i, 0)),
        out_shape=jax.ShapeDtypeStruct((m, D_MODEL), F32),
        scratch_shapes=[pltpu.VMEM((D_MODEL // tn, tm, tn), F32)],
        name="out_projection",
        compiler_params=_cparams(("arbitrary", "arbitrary")),
    )(mix, w_o, x, g.reshape(1, D_MODEL), gate)


def _mlp_kernel(x_ref, gin_ref, sc_ref, sh_ref, w1_ref, w2_ref, gout_ref, gate_ref, o_ref,
                h_ref, acc_ref):
    j = pl.program_id(1)

    @pl.when(j == 0)
    def _():
        h = _rms(x_ref[...]) * gin_ref[...]
        h_ref[...] = (h * (1.0 + sc_ref[...]) + sh_ref[...]).astype(BF16)

    a = jnp.dot(h_ref[...], w1_ref[...], preferred_element_type=F32)
    a = jnp.square(jnp.maximum(a, 0.0)).astype(BF16)
    part = jnp.dot(a, w2_ref[...], preferred_element_type=F32)

    @pl.when(j == 0)
    def _():
        acc_ref[...] = part

    @pl.when(j > 0)
    def _():
        acc_ref[...] += part

    @pl.when(j == pl.num_programs(1) - 1)
    def _():
        o_ref[...] = x_ref[...] + gate_ref[...] * (_rms(acc_ref[...]) * gout_ref[...])


def _mlp(x, g_in, scale, shift, w1, w2, layer, g_out, gate, tm, tf):
    m = x.shape[0]
    nmod = gate.shape[0]
    blocks_per_mod = m // nmod // tm
    mod_spec = pl.BlockSpec((None, 1, D_MODEL), lambda i, j: (i // blocks_per_mod, 0, 0))
    row_spec = pl.BlockSpec((1, D_MODEL), lambda i, j: (0, 0))
    return pl.pallas_call(
        _mlp_kernel,
        grid=(m // tm, D_FF // tf),
        in_specs=[pl.BlockSpec((tm, D_MODEL), lambda i, j: (i, 0)),
                  row_spec, mod_spec, mod_spec,
                  pl.BlockSpec((None, D_MODEL, tf), lambda i, j: (layer, 0, j)),
                  pl.BlockSpec((None, tf, D_MODEL), lambda i, j: (layer, j, 0)),
                  row_spec, mod_spec],
        out_specs=pl.BlockSpec((tm, D_MODEL), lambda i, j: (i, 0)),
        out_shape=jax.ShapeDtypeStruct((m, D_MODEL), F32),
        scratch_shapes=[pltpu.VMEM((tm, D_MODEL), BF16), pltpu.VMEM((tm, D_MODEL), F32)],
        name="channel_mlp",
        compiler_params=_cparams(("arbitrary", "arbitrary")),
    )(x, g_in.reshape(1, D_MODEL), scale, shift, w1, w2, g_out.reshape(1, D_MODEL), gate)


def _rope_tables(l):
    pos = jnp.arange(l, dtype=jnp.int32)
    quarter = HEAD_DIM // 4
    freqs = ROPE_THETA ** (-jnp.arange(quarter, dtype=F32) / quarter)
    ang_r = (pos // GRID_W).astype(F32)[:, None] * freqs[None, :]
    ang_c = (pos % GRID_W).astype(F32)[:, None] * freqs[None, :]
    cos_t = jnp.concatenate([jnp.cos(ang_r), jnp.cos(ang_r), jnp.cos(ang_c), jnp.cos(ang_c)], axis=-1)
    sin_t = jnp.concatenate([-jnp.sin(ang_r), jnp.sin(ang_r), -jnp.sin(ang_c), jnp.sin(ang_c)], axis=-1)
    return cos_t, sin_t


def _decay_rates():
    min_decay = math.log(DECAY_TARGET) / SLOW_DECAY_PCT
    max_decay = math.log(DECAY_TARGET) / FAST_DECAY_PCT
    deltas = jnp.tile(jnp.linspace(min_decay, max_decay, HYENA_WIDTH, dtype=F32), 2)
    return jnp.abs(deltas)[None, :]


def kernel(x, c, ctx, c_ctx, w_mod, b_mod, norm_g, w_in, attn_sink, hy_conv_w, hy_conv_b,
           hy_fw1, hy_fb1, hy_ff1, hy_fw2, hy_fb2, hy_ff2, hy_fw3, hy_fb3, hy_ff3, hy_fw4,
           hy_bias, w_attn_out, w_hyena_out, w_out, w_ff1, w_ff2):
    b, l, d = x.shape
    cl = ctx.shape[1]
    assert d == D_MODEL and l % 1024 == 0 and cl % 256 == 0 and b + 1 <= MOD_ROWS

    cos_t, sin_t = _rope_tables(l)
    decay = _decay_rates()
    tables_lat = _fft_tables(l)
    tables_ctx = _fft_tables(cl)
    z_lat = _filter_features(l)
    z_ctx = _filter_features(cl)

    c_rows = jnp.concatenate([c, c_ctx[None, :], jnp.zeros((MOD_ROWS - b - 1, d), F32)], axis=0)
    x_lat = x.reshape(b * l, d)
    x_ctx = ctx.reshape(b * cl, d)
    tm_ctx = b * cl
    w1b = w_ff1.astype(BF16)
    w2b = w_ff2.astype(BF16)

    for layer in range(DEPTH):
        last = layer == DEPTH - 1
        mod = _modulation(c_rows, w_mod, layer, b_mod[layer])
        mod_lat = [mod[:b, k * d:(k + 1) * d].reshape(b, 1, d) for k in range(N_MOD)]
        mod_ctx = [mod[b:b + 1, k * d:(k + 1) * d].reshape(1, 1, d) for k in range(N_MOD)]
        sh1, sc1, g1, sh2, sc2, g2 = mod_lat
        csh1, csc1, cg1, csh2, csc2, cg2 = mod_ctx
        g = norm_g[layer]
        fparams = (hy_fw1[layer], hy_fb1[layer], hy_ff1[layer], hy_fw2[layer], hy_fb2[layer],
                   hy_ff2[layer], hy_fw3[layer], hy_fb3[layer], hy_ff3[layer])
        bias_d = hy_bias[layer].reshape(1, HYENA_WIDTH)
        conv_b = hy_conv_b[layer].reshape(1, 3 * HYENA_WIDTH)

        p_lat = _normproj(x_lat, g[0], sc1, sh1, w_in, layer, 0, IN_WIDTH, 1024, 512)
        if last:
            kv_ctx = _normproj(x_ctx, g[0], csc1, csh1, w_in, layer, Q_END, V_END - Q_END,
                               tm_ctx, 512)
            kx, vx = kv_ctx[:, :KV_WIDTH], kv_ctx[:, KV_WIDTH:]
        else:
            p_ctx = _normproj(x_ctx, g[0], csc1, csh1, w_in, layer, 0, IN_WIDTH, tm_ctx, 512)
            kx, vx = p_ctx[:, Q_END:K_END], p_ctx[:, K_END:V_END]
        kx = kx.reshape(b, cl, KV_WIDTH)
        vx = vx.reshape(b, cl, KV_WIDTH)

        p3 = p_lat.reshape(b, l, IN_WIDTH)
        attn = _window_attention(p3, kx, vx, attn_sink[layer], cos_t, sin_t)
        kf = _filter_spectrum(_filter_mlp(z_lat, *fparams), hy_fw4[layer], decay, tables_lat)
        hy = _hyena(p3, V_END, hy_conv_w[layer], conv_b, bias_d, kf, tables_lat)
        mix = _merge(attn.reshape(b * l, ATTN_WIDTH), hy.reshape(b * l, HYENA_WIDTH), p_lat,
                     w_attn_out, w_hyena_out, layer, 1024, 512)
        x_lat = _outproj(mix, w_out, layer, x_lat, g[1], g1, 512, 512)

        if not last:
            pc3 = p_ctx.reshape(b, cl, IN_WIDTH)
            attn_c = _context_attention(pc3, attn_sink[layer])
            kf_c = _filter_spectrum(_filter_mlp(z_ctx, *fparams), hy_fw4[layer], decay, tables_ctx)
            hy_c = _hyena(pc3, V_END, hy_conv_w[layer], conv_b, bias_d, kf_c, tables_ctx)
            mix_c = _merge(attn_c.reshape(b * cl, ATTN_WIDTH), hy_c.reshape(b * cl, HYENA_WIDTH),
                           p_ctx, w_attn_out, w_hyena_out, layer, tm_ctx, 512)
            x_ctx = _outproj(mix_c, w_out, layer, x_ctx, g[1], cg1, tm_ctx, 512)
            x_ctx = _mlp(x_ctx, g[2], csc2, csh2, w1b, w2b, layer, g[3], cg2, tm_ctx, 1024)

        x_lat = _mlp(x_lat, g[2], sc2, sh2, w1b, w2b, layer, g[3], g2, 512, 1024)
    return x_lat.reshape(b, l, d)
```

```python
import functools
import math

import jax
import jax.numpy as jnp
from jax import lax
from jax.experimental import pallas as pl
from jax.experimental.pallas import tpu as pltpu

F32 = jnp.float32
BF16 = jnp.bfloat16

D_MODEL = 2048
DEPTH = 2
GRID_W = 64
HEAD_DIM = 128
N_Q_HEADS = 8
N_KV_HEADS = 2
Q_GROUP = N_Q_HEADS // N_KV_HEADS
ATTN_WIDTH = N_Q_HEADS * HEAD_DIM
KV_WIDTH = N_KV_HEADS * HEAD_DIM
BLOCK = 128
ROPE_THETA = 10000.0
HYENA_WIDTH = 1024
FILTER_EMB = 33
FILTER_HIDDEN = 64
DECAY_TARGET = 1e-2
FAST_DECAY_PCT = 0.3
SLOW_DECAY_PCT = 1.5
D_FF = 4 * D_MODEL
EPS = 1e-6
N_MOD = 6
NEG_INF = -1e30
Q_END = ATTN_WIDTH
K_END = Q_END + KV_WIDTH
V_END = K_END + KV_WIDTH
HY_END = V_END + 3 * HYENA_WIDTH
GA_END = HY_END + D_MODEL
GH_END = GA_END + D_MODEL
IN_WIDTH = GH_END

LANES = 128
VMEM_LIMIT = 56 * 1024 * 1024

FFT_N2 = 16
FFT_KG = 16
PROLOGUE_ROWS = 256
HY_CB = 128
MOD_ROWS = 8


def _cparams(sem):
    return pltpu.CompilerParams(dimension_semantics=sem, vmem_limit_bytes=VMEM_LIMIT)


def _single(block_shape, index_map):
    return pl.BlockSpec(block_shape, index_map, pipeline_mode=pl.Buffered(1))


def _rms(x):
    return x * lax.rsqrt(jnp.mean(x * x, axis=-1, keepdims=True) + EPS)


def _mod_kernel(c_ref, w_ref, b_ref, o_ref):
    c = c_ref[...]
    s = c * jax.nn.sigmoid(c)
    o_ref[...] = jnp.dot(s.astype(BF16), w_ref[...].astype(BF16),
                         preferred_element_type=F32) + b_ref[...]


def _modulation(c_rows, w, layer, b):
    n = w.shape[2]
    tn = 1024
    return pl.pallas_call(
        _mod_kernel,
        grid=(n // tn,),
        in_specs=[pl.BlockSpec((MOD_ROWS, D_MODEL), lambda j: (0, 0)),
                  pl.BlockSpec((None, D_MODEL, tn), lambda j: (layer, 0, j)),
                  pl.BlockSpec((1, tn), lambda j: (0, j))],
        out_specs=pl.BlockSpec((MOD_ROWS, tn), lambda j: (0, j)),
        out_shape=jax.ShapeDtypeStruct((MOD_ROWS, n), F32),
        name="modulation",
        compiler_params=_cparams(("arbitrary",)),
    )(c_rows, w, b.reshape(1, n))


def _normproj_kernel(x_ref, g_ref, sc_ref, sh_ref, w_ref, o_ref, h_ref):
    @pl.when(pl.program_id(1) == 0)
    def _():
        g, sc1, sh = g_ref[...], 1.0 + sc_ref[...], sh_ref[...]

        def chunk(r, carry):
            rows = pl.ds(pl.multiple_of(r * PROLOGUE_ROWS, PROLOGUE_ROWS), PROLOGUE_ROWS)
            h_ref[rows, :] = (_rms(x_ref[rows, :]) * g * sc1 + sh).astype(BF16)
            return carry

        lax.fori_loop(0, x_ref.shape[0] // PROLOGUE_ROWS, chunk, 0)

    o_ref[...] = jnp.dot(h_ref[...], w_ref[...].astype(BF16), preferred_element_type=F32)


def _normproj(x, g, scale, shift, w, layer, col0, n, tm, tn):
    m = x.shape[0]
    j0 = col0 // tn
    nmod = scale.shape[0]
    blocks_per_mod = m // nmod // tm
    mod_spec = pl.BlockSpec((None, 1, D_MODEL), lambda i, j: (i // blocks_per_mod, 0, 0))
    return pl.pallas_call(
        _normproj_kernel,
        grid=(m // tm, n // tn),
        in_specs=[pl.BlockSpec((tm, D_MODEL), lambda i, j: (i, 0), pipeline_mode=pl.Buffered(1)),
                  pl.BlockSpec((1, D_MODEL), lambda i, j: (0, 0)),
                  mod_spec, mod_spec,
                  pl.BlockSpec((None, D_MODEL, tn), lambda i, j: (layer, 0, j0 + j))],
        out_specs=pl.BlockSpec((tm, tn), lambda i, j: (i, j)),
        out_shape=jax.ShapeDtypeStruct((m, n), F32),
        scratch_shapes=[pltpu.VMEM((tm, D_MODEL), BF16)],
        name="normproj",
        compiler_params=_cparams(("arbitrary", "arbitrary")),
    )(x, g.reshape(1, D_MODEL), scale, shift, w)


def _rope(x, cos, sin_signed, first_half):
    rot = jnp.where(first_half, pltpu.roll(x, HEAD_DIM - 32, axis=1), pltpu.roll(x, 32, axis=1))
    return x * cos + rot * sin_signed


def _softmax_pv(s, sink, v):
    m = jnp.maximum(jnp.max(s, axis=-1, keepdims=True), sink)
    e = jnp.exp(s - m)
    denom = jnp.sum(e, axis=-1, keepdims=True) + jnp.exp(sink - m)
    o = jnp.dot(e.astype(BF16), v, preferred_element_type=F32)
    return o / denom


def _win_attn_kernel(sink_ref, q_ref, kp_ref, kc_ref, kn_ref, vp_ref, vc_ref, vn_ref,
                     kx_ref, vx_ref, cos_ref, sin_ref, o_ref, *, nb):
    i = pl.program_id(1)
    scale = HEAD_DIM ** -0.5
    lane = lax.broadcasted_iota(jnp.int32, (BLOCK, HEAD_DIM), 1)
    first_half = (lane % 64) < 32

    def table(ref, blk):
        return ref[pl.ds(pl.multiple_of(blk * BLOCK, BLOCK), BLOCK), :]

    ip = jnp.maximum(i - 1, 0)
    inx = jnp.minimum(i + 1, nb - 1)
    cos_c, sin_c = table(cos_ref, i), table(sin_ref, i)
    cos_p, sin_p = table(cos_ref, ip), table(sin_ref, ip)
    cos_n, sin_n = table(cos_ref, inx), table(sin_ref, inx)

    rows = Q_GROUP * BLOCK
    ri = lax.broadcasted_iota(jnp.int32, (rows, 3 * BLOCK + kx_ref.shape[0]), 0)
    qi = ri % BLOCK
    kj = lax.broadcasted_iota(jnp.int32, (rows, 3 * BLOCK + kx_ref.shape[0]), 1)
    head_in_group = lax.broadcasted_iota(jnp.int32, (rows, 1), 0) // BLOCK
    never = 4 * BLOCK
    off_prev = jnp.where(i > 0, 0, never)
    off_next = jnp.where(i < nb - 1, 0, never)
    in_prev = kj < BLOCK
    in_next = (kj >= 2 * BLOCK) & (kj < 3 * BLOCK)
    valid = ((in_prev & (kj >= qi + off_prev))
             | (in_next & (kj - 2 * BLOCK + off_next <= qi))
             | ((kj >= BLOCK) & (kj < 2 * BLOCK)) | (kj >= 3 * BLOCK))

    for h in range(N_KV_HEADS):
        hs = slice(h * HEAD_DIM, (h + 1) * HEAD_DIM)
        k = jnp.concatenate([
            _rope(kp_ref[:, hs], cos_p, sin_p, first_half),
            _rope(kc_ref[:, hs], cos_c, sin_c, first_half),
            _rope(kn_ref[:, hs], cos_n, sin_n, first_half),
            kx_ref[:, hs]], axis=0).astype(BF16)
        v = jnp.concatenate([vp_ref[:, hs], vc_ref[:, hs], vn_ref[:, hs], vx_ref[:, hs]],
                            axis=0).astype(BF16)
        heads = [h * Q_GROUP + g for g in range(Q_GROUP)]
        q = jnp.concatenate(
            [_rope(q_ref[:, hd * HEAD_DIM:(hd + 1) * HEAD_DIM], cos_c, sin_c, first_half)
             for hd in heads], axis=0).astype(BF16)
        sink = jnp.zeros((rows, 1), F32)
        for g, hd in enumerate(heads):
            sink = jnp.where(head_in_group == g, sink_ref[hd], sink)
        s = lax.dot_general(q, k, (((1,), (1,)), ((), ())), preferred_element_type=F32) * scale
        o = _softmax_pv(jnp.where(valid, s, NEG_INF), sink, v).astype(o_ref.dtype)
        for g, hd in enumerate(heads):
            o_ref[:, hd * HEAD_DIM:(hd + 1) * HEAD_DIM] = o[g * BLOCK:(g + 1) * BLOCK]


def _window_attention(p, kx, vx, sink, cos_t, sin_t):
    b, l, _ = p.shape
    c = kx.shape[1]
    nb = l // BLOCK
    kcol = Q_END // KV_WIDTH
    vcol = K_END // KV_WIDTH

    def kv_spec(col, shift):
        return pl.BlockSpec((None, BLOCK, KV_WIDTH),
                            lambda bi, i: (bi, jnp.clip(i + shift, 0, nb - 1), col))

    ctx_spec = pl.BlockSpec((None, c, KV_WIDTH), lambda bi, i: (bi, 0, 0))
    tab_spec = _single((l, HEAD_DIM), lambda bi, i: (0, 0))
    return pl.pallas_call(
        functools.partial(_win_attn_kernel, nb=nb),
        grid=(b, nb),
        in_specs=[pl.BlockSpec(memory_space=pltpu.SMEM),
                  pl.BlockSpec((None, BLOCK, ATTN_WIDTH), lambda bi, i: (bi, i, 0)),
                  kv_spec(kcol, -1), kv_spec(kcol, 0), kv_spec(kcol, 1),
                  kv_spec(vcol, -1), kv_spec(vcol, 0), kv_spec(vcol, 1),
                  ctx_spec, ctx_spec, tab_spec, tab_spec],
        out_specs=pl.BlockSpec((None, BLOCK, ATTN_WIDTH), lambda bi, i: (bi, i, 0)),
        out_shape=jax.ShapeDtypeStruct((b, l, ATTN_WIDTH), BF16),
        name="window_attention",
        compiler_params=_cparams(("arbitrary", "arbitrary")),
    )(sink, p, p, p, p, p, p, p, kx, vx, cos_t, sin_t)


def _ctx_attn_kernel(sink_ref, q_ref, k_ref, v_ref, o_ref):
    scale = HEAD_DIM ** -0.5
    for h in range(N_KV_HEADS):
        hs = slice(h * HEAD_DIM, (h + 1) * HEAD_DIM)
        k = k_ref[:, hs].astype(BF16)
        v = v_ref[:, hs].astype(BF16)
        for g in range(Q_GROUP):
            head = h * Q_GROUP + g
            cs = slice(head * HEAD_DIM, (head + 1) * HEAD_DIM)
            q = q_ref[:, cs].astype(BF16)
            s = lax.dot_general(q, k, (((1,), (1,)), ((), ())), preferred_element_type=F32) * scale
            o_ref[:, cs] = _softmax_pv(s, sink_ref[head], v).astype(o_ref.dtype)


def _context_attention(p, sink):
    b, c, _ = p.shape
    return pl.pallas_call(
        _ctx_attn_kernel,
        grid=(b,),
        in_specs=[pl.BlockSpec(memory_space=pltpu.SMEM),
                  pl.BlockSpec((None, c, ATTN_WIDTH), lambda bi: (bi, 0, 0)),
                  pl.BlockSpec((None, c, KV_WIDTH), lambda bi: (bi, 0, Q_END // KV_WIDTH)),
                  pl.BlockSpec((None, c, KV_WIDTH), lambda bi: (bi, 0, K_END // KV_WIDTH))],
        out_specs=pl.BlockSpec((None, c, ATTN_WIDTH), lambda bi: (bi, 0, 0)),
        out_shape=jax.ShapeDtypeStruct((b, c, ATTN_WIDTH), BF16),
        name="context_attention",
        compiler_params=_cparams(("arbitrary",)),
    )(sink, p, p, p)


def _fft_tables(l, n2):
    n = 2 * l
    n1 = n // n2
    k1 = jnp.arange(n1, dtype=jnp.int32)[None, :, None]
    t = (n2 * jnp.arange(n1 // 2, dtype=jnp.int32)[None, None, :]
         + jnp.arange(n2, dtype=jnp.int32)[:, None, None])
    ang = ((k1 * t) % n).astype(F32) * (2.0 * math.pi / n)
    fwd1 = jnp.concatenate([jnp.cos(ang), -jnp.sin(ang)], axis=1)
    inv1 = jnp.swapaxes(fwd1, 1, 2) * (1.0 / n)
    return fwd1.astype(BF16), inv1.astype(BF16)


def _filter_features(l):
    bands = (FILTER_EMB - 1) // 2
    t = jnp.linspace(0.0, 1.0, l, dtype=F32)[:, None]
    w = 2 * math.pi * jnp.arange(l, dtype=F32)[:, None] / l
    f = jnp.linspace(1e-4, bands - 1, bands, dtype=F32)[None, :]
    z = jnp.concatenate([t, jnp.cos(f * w), -jnp.sin(f * w)], axis=-1)
    return jnp.pad(z, ((0, 0), (0, FILTER_HIDDEN - FILTER_EMB)))


def _filter_mlp_kernel(z_ref, w1_ref, b1_ref, f1_ref, w2_ref, b2_ref, f2_ref,
                       w3_ref, b3_ref, f3_ref, o_ref):
    h = jnp.sin(f1_ref[...] * (jnp.dot(z_ref[...], w1_ref[...], preferred_element_type=F32)
                               + b1_ref[...]))
    h = jnp.sin(f2_ref[...] * (jnp.dot(h, w2_ref[...], preferred_element_type=F32) + b2_ref[...]))
    o_ref[...] = jnp.sin(f3_ref[...] * (jnp.dot(h, w3_ref[...], preferred_element_type=F32)
                                        + b3_ref[...]))


def _filter_mlp(z, fw1, fb1, ff1, fw2, fb2, ff2, fw3, fb3, ff3):
    l = z.shape[0]
    row = lambda a: a.reshape(1, FILTER_HIDDEN)
    w1 = jnp.pad(fw1, ((0, FILTER_HIDDEN - FILTER_EMB), (0, 0)))
    return pl.pallas_call(
        _filter_mlp_kernel,
        out_shape=jax.ShapeDtypeStruct((l, FILTER_HIDDEN), F32),
        name="filter_mlp",
        compiler_params=pltpu.CompilerParams(vmem_limit_bytes=VMEM_LIMIT),
    )(z, w1, row(fb1), row(ff1), fw2, row(fb2), row(ff2), fw3, row(fb3), row(ff3))


def _cmul_root16(z, p, inverse):
    zr, zi = z
    p = p % 16
    if inverse:
        p = (16 - p) % 16
    if p == 0:
        return zr, zi
    if p == 4:
        return zi, -zr
    if p == 8:
        return -zr, -zi
    if p == 12:
        return -zi, zr
    c = math.cos(2.0 * math.pi * p / 16)
    s = -math.sin(2.0 * math.pi * p / 16)
    return zr * c - zi * s, zr * s + zi * c


def _dft4(z, inverse):
    (ar, ai), (br, bi), (cr, ci), (dr, di) = z
    t0r, t0i = ar + cr, ai + ci
    t1r, t1i = ar - cr, ai - ci
    t2r, t2i = br + dr, bi + di
    t3r, t3i = br - dr, bi - di
    y0 = (t0r + t2r, t0i + t2i)
    y2 = (t0r - t2r, t0i - t2i)
    minus_i_t3 = (t1r + t3i, t1i - t3r)
    plus_i_t3 = (t1r - t3i, t1i + t3r)
    return [y0, plus_i_t3, y2, minus_i_t3] if inverse else [y0, minus_i_t3, y2, plus_i_t3]


def _dft_digit(z, inverse):
    if len(z) == 1:
        return z
    assert len(z) == 16
    t = [_dft4([z[4 * a + b] for a in range(4)], inverse) for b in range(4)]
    out = [None] * 16
    for c in range(4):
        y = _dft4([_cmul_root16(t[b][c], b * c, inverse) for b in range(4)], inverse)
        for d in range(4):
            out[c + 4 * d] = y[d]
    return out


def _fft_stage1(src_ref, f1_ref, a_ref):
    n2cnt = f1_ref.shape[0]
    groups = a_ref.shape[0]
    n1 = groups * FFT_KG
    half = src_ref.shape[0] // n2cnt

    def body(n2, carry):
        rows = pl.ds(n2, half, stride=n2cnt) if n2cnt > 1 else pl.ds(0, half)
        res = jnp.dot(f1_ref[n2], src_ref[rows, :].astype(BF16), preferred_element_type=F32)
        for g in range(groups):
            a_ref[g, 0, n2] = res[g * FFT_KG:(g + 1) * FFT_KG]
            a_ref[g, 1, n2] = res[n1 + g * FFT_KG:n1 + (g + 1) * FFT_KG]
        return carry

    if n2cnt == 1:
        body(0, 0)
    else:
        lax.fori_loop(0, n2cnt, body, 0, unroll=2)


def _load_digits(ref, g, rows):
    return [(ref[g, 0, d, rows, :], ref[g, 1, d, rows, :]) for d in range(ref.shape[2])]


def _filter_spec_kernel(h3_ref, wf_ref, wb_ref, df_ref, db_ref, f1_ref, kf_ref,
                        hf_ref, hb_ref, a_ref, *, l):
    groups = a_ref.shape[0]
    row = lax.broadcasted_iota(jnp.int32, (l, 1), 0)
    t = row.astype(F32) * (1.0 / (l - 1))
    h3 = h3_ref[...]
    hf = jnp.dot(h3, wf_ref[...], preferred_element_type=F32) * jnp.exp(-t * df_ref[...])
    hb = jnp.dot(h3, wb_ref[...], preferred_element_type=F32) * jnp.exp(-t * db_ref[...])
    hb = jnp.where(row > 0, hb, 0.0)
    norm = jnp.sum(jnp.abs(hf), axis=0, keepdims=True) + jnp.sum(jnp.abs(hb), axis=0, keepdims=True)
    hf_ref[...] = hf / norm
    hb_ref[...] = hb / norm

    def forward(g, carry):
        for r in range(0, FFT_KG, 8):
            rows = pl.ds(r, 8)
            for k2, (xr, xi) in enumerate(_dft_digit(_load_digits(a_ref, g, rows), False)):
                kf_ref[g, 0, k2, rows, :] = xr
                kf_ref[g, 1, k2, rows, :] = xi
        return carry

    def backward(g, carry):
        for r in range(0, FFT_KG, 8):
            rows = pl.ds(r, 8)
            for k2, (xr, xi) in enumerate(_dft_digit(_load_digits(a_ref, g, rows), False)):
                kf_ref[g, 0, k2, rows, :] = kf_ref[g, 0, k2, rows, :] + xr
                kf_ref[g, 1, k2, rows, :] = kf_ref[g, 1, k2, rows, :] - xi
        return carry

    _fft_stage1(hf_ref, f1_ref, a_ref)
    lax.fori_loop(0, groups, forward, 0)
    _fft_stage1(hb_ref, f1_ref, a_ref)
    lax.fori_loop(0, groups, backward, 0)


def _filter_spectrum(h3, fw4, decay, tables):
    l = h3.shape[0]
    f1, _ = tables
    n2 = f1.shape[0]
    groups = f1.shape[1] // 2 // FFT_KG
    nblk = HYENA_WIDTH // HY_CB
    spec_shape = (groups, 2, n2, FFT_KG)
    return pl.pallas_call(
        functools.partial(_filter_spec_kernel, l=l),
        grid=(nblk,),
        in_specs=[_single((l, FILTER_HIDDEN), lambda c: (0, 0)),
                  pl.BlockSpec((FILTER_HIDDEN, HY_CB), lambda c: (0, c)),
                  pl.BlockSpec((FILTER_HIDDEN, HY_CB), lambda c: (0, c + nblk)),
                  pl.BlockSpec((1, HY_CB), lambda c: (0, c)),
                  pl.BlockSpec((1, HY_CB), lambda c: (0, c + nblk)),
                  _single(f1.shape, lambda c: (0, 0, 0))],
        out_specs=pl.BlockSpec(spec_shape + (HY_CB,), lambda c: (0, 0, 0, 0, c)),
        out_shape=jax.ShapeDtypeStruct(spec_shape + (HYENA_WIDTH,), F32),
        scratch_shapes=[pltpu.VMEM((l, HY_CB), F32), pltpu.VMEM((l, HY_CB), F32),
                        pltpu.VMEM(spec_shape + (HY_CB,), F32)],
        name="filter_spectrum",
        compiler_params=_cparams(("arbitrary",)),
    )(h3, fw4, fw4, decay, decay, f1)


def _short_conv(src_ref, w_ref, b_ref, l, rows):
    w0, w1, w2, b = w_ref[0:1, :], w_ref[1:2, :], w_ref[2:3, :], b_ref[...]
    r = lax.broadcasted_iota(jnp.int32, (rows, 1), 0)
    for s in range(0, l, rows):
        cur = src_ref[s:s + rows, :]
        before = src_ref[s - 1:s, :] if s > 0 else jnp.zeros((1, cur.shape[1]), F32)
        after = src_ref[s + rows:s + rows + 1, :] if s + rows < l else jnp.zeros((1, cur.shape[1]), F32)
        prev = jnp.where(r == 0, before, pltpu.roll(cur, 1, axis=0))
        nxt = jnp.where(r == rows - 1, after, pltpu.roll(cur, rows - 1, axis=0))
        yield s, prev * w0 + cur * w1 + nxt * w2 + b


def _hyena_kernel(x0_ref, x1_ref, v_ref, w0_ref, w1_ref, wv_ref, b0_ref, b1_ref, bv_ref,
                  bias_ref, kf_ref, f1_ref, g1_ref, o_ref, x0c_ref, vg_ref, a_ref, *, l):
    groups = a_ref.shape[0]
    n2cnt = a_ref.shape[2]
    cb = a_ref.shape[-1]
    n1 = groups * FFT_KG
    half = l // n2cnt
    rows = min(l, 512)

    for s, u in _short_conv(x0_ref, w0_ref, b0_ref, l, rows):
        x0c_ref[s:s + rows, :] = u
    for (s, u1), (_, uv) in zip(_short_conv(x1_ref, w1_ref, b1_ref, l, rows),
                                _short_conv(v_ref, wv_ref, bv_ref, l, rows)):
        vg_ref[s:s + rows, :] = uv * u1

    _fft_stage1(vg_ref, f1_ref, a_ref)

    def spectrum(g, carry):
        for r in range(0, FFT_KG, 8):
            rws = pl.ds(r, 8)
            x = _dft_digit(_load_digits(a_ref, g, rws), False)
            k = _load_digits(kf_ref, g, rws)
            y = [(xr * kr - xi * ki, xr * ki + xi * kr) for (xr, xi), (kr, ki) in zip(x, k)]
            for d, (br, bi) in enumerate(_dft_digit(y, True)):
                a_ref[g, 0, d, rws, :] = br
                a_ref[g, 1, d, rws, :] = bi
        return carry

    lax.fori_loop(0, groups, spectrum, 0)

    def synth(n2, carry):
        b = jnp.concatenate([a_ref[:, 0, n2].reshape(n1, cb), a_ref[:, 1, n2].reshape(n1, cb)],
                            axis=0).astype(BF16)
        y = jnp.dot(g1_ref[n2], b, preferred_element_type=F32)
        idx = pl.ds(n2, half, stride=n2cnt) if n2cnt > 1 else pl.ds(0, half)
        o_ref[idx, :] = (y + vg_ref[idx, :] * bias_ref[...]) * x0c_ref[idx, :]
        return carry

    if n2cnt == 1:
        synth(0, 0)
    else:
        lax.fori_loop(0, n2cnt, synth, 0, unroll=2)


def _hyena(p, col0, conv_w, conv_b, bias_d, kf, tables):
    b, l, _ = p.shape
    f1, g1 = tables
    nblk = HYENA_WIDTH // HY_CB
    c0 = col0 // HY_CB
    spec_block = kf.shape[:-1] + (HY_CB,)

    def slab(part):
        return pl.BlockSpec((None, l, HY_CB), lambda c, bi: (bi, 0, c0 + part * nblk + c))

    def cw(part):
        return pl.BlockSpec((3, HY_CB), lambda c, bi: (0, part * nblk + c))

    def cbias(part):
        return pl.BlockSpec((1, HY_CB), lambda c, bi: (0, part * nblk + c))

    return pl.pallas_call(
        functools.partial(_hyena_kernel, l=l),
        grid=(nblk, b),
        in_specs=[slab(0), slab(1), slab(2), cw(0), cw(1), cw(2), cbias(0), cbias(1), cbias(2),
                  pl.BlockSpec((1, HY_CB), lambda c, bi: (0, c)),
                  pl.BlockSpec(spec_block, lambda c, bi: (0, 0, 0, 0, c),
                               pipeline_mode=pl.Buffered(1)),
                  _single(f1.shape, lambda c, bi: (0, 0, 0)),
                  _single(g1.shape, lambda c, bi: (0, 0, 0))],
        out_specs=pl.BlockSpec((None, l, HY_CB), lambda c, bi: (bi, 0, c)),
        out_shape=jax.ShapeDtypeStruct((b, l, HYENA_WIDTH), F32),
        scratch_shapes=[pltpu.VMEM((l, HY_CB), F32), pltpu.VMEM((l, HY_CB), F32),
                        pltpu.VMEM(spec_block, F32)],
        name="hyena_conv",
        compiler_params=_cparams(("arbitrary", "arbitrary")),
    )(p, p, p, conv_w, conv_w, conv_w, conv_b, conv_b, conv_b, bias_d, kf, f1, g1)


def _merge_kernel(a_ref, h_ref, ga_ref, gh_ref, wa_ref, wh_ref, o_ref):
    ya = jnp.dot(a_ref[...], wa_ref[...].astype(BF16), preferred_element_type=F32)
    yh = jnp.dot(h_ref[...].astype(BF16), wh_ref[...].astype(BF16), preferred_element_type=F32)
    o_ref[...] = (jax.nn.sigmoid(ga_ref[...]) * ya
                  + jax.nn.sigmoid(gh_ref[...]) * yh).astype(o_ref.dtype)


def _merge(attn, hy, p, w_ao, w_ho, layer, tm, tn):
    m = attn.shape[0]
    ga0 = HY_END // tn
    gh0 = GA_END // tn
    return pl.pallas_call(
        _merge_kernel,
        grid=(m // tm, D_MODEL // tn),
        in_specs=[pl.BlockSpec((tm, ATTN_WIDTH), lambda i, j: (i, 0)),
                  pl.BlockSpec((tm, HYENA_WIDTH), lambda i, j: (i, 0)),
                  pl.BlockSpec((tm, tn), lambda i, j: (i, ga0 + j)),
                  pl.BlockSpec((tm, tn), lambda i, j: (i, gh0 + j)),
                  pl.BlockSpec((None, ATTN_WIDTH, tn), lambda i, j: (layer, 0, j)),
                  pl.BlockSpec((None, HYENA_WIDTH, tn), lambda i, j: (layer, 0, j))],
        out_specs=pl.BlockSpec((tm, tn), lambda i, j: (i, j)),
        out_shape=jax.ShapeDtypeStruct((m, D_MODEL), BF16),
        name="branch_merge",
        compiler_params=_cparams(("arbitrary", "arbitrary")),
    )(attn, hy, p, p, w_ao, w_ho)


def _outproj_kernel(mix_ref, w_ref, x_ref, g_ref, gate_ref, o_ref, y_ref, *, tn):
    j = pl.program_id(1)
    nj = y_ref.shape[0]
    y_ref[j] = jnp.dot(mix_ref[...], w_ref[...].astype(BF16), preferred_element_type=F32)

    @pl.when(j == nj - 1)
    def _():
        ss = sum(jnp.sum(jnp.square(y_ref[k]), axis=-1, keepdims=True) for k in range(nj))
        r = lax.rsqrt(ss * (1.0 / D_MODEL) + EPS)
        for k in range(nj):
            cs = slice(k * tn, (k + 1) * tn)
            o_ref[:, cs] = x_ref[:, cs] + gate_ref[:, cs] * ((y_ref[k] * r) * g_ref[:, cs])


def _outproj(mix, w_o, layer, x, g, gate, tm, tn):
    m = x.shape[0]
    nmod = gate.shape[0]
    blocks_per_mod = m // nmod // tm
    return pl.pallas_call(
        functools.partial(_outproj_kernel, tn=tn),
        grid=(m // tm, D_MODEL // tn),
        in_specs=[pl.BlockSpec((tm, D_MODEL), lambda i, j: (i, 0)),
                  pl.BlockSpec((None, D_MODEL, tn), lambda i, j: (layer, 0, j)),
                  pl.BlockSpec((tm, D_MODEL), lambda i, j: (i, 0)),
                  pl.BlockSpec((1, D_MODEL), lambda i, j: (0, 0)),
                  pl.BlockSpec((None, 1, D_MODEL), lambda i, j: (i // blocks_per_mod, 0, 0))],
        out_specs=pl.BlockSpec((tm, D_MODEL), lambda i, j: (i, 0)),
        out_shape=jax.ShapeDtypeStruct((m, D_MODEL), F32),
        scratch_shapes=[pltpu.VMEM((D_MODEL // tn, tm, tn), F32)],
        name="out_projection",
        compiler_params=_cparams(("arbitrary", "arbitrary")),
    )(mix, w_o, x, g.reshape(1, D_MODEL), gate)


def _mlp_kernel(x_ref, gin_ref, sc_ref, sh_ref, w1_ref, w2_ref, gout_ref, gate_ref, o_ref,
                h_ref, acc_ref):
    j = pl.program_id(1)

    @pl.when(j == 0)
    def _():
        h = _rms(x_ref[...]) * gin_ref[...]
        h_ref[...] = (h * (1.0 + sc_ref[...]) + sh_ref[...]).astype(BF16)

    a = jnp.dot(h_ref[...], w1_ref[...], preferred_element_type=F32)
    a = jnp.square(jnp.maximum(a, 0.0)).astype(BF16)
    part = jnp.dot(a, w2_ref[...], preferred_element_type=F32)

    @pl.when(j == 0)
    def _():
        acc_ref[...] = part

    @pl.when(j > 0)
    def _():
        acc_ref[...] += part

    @pl.when(j == pl.num_programs(1) - 1)
    def _():
        o_ref[...] = x_ref[...] + gate_ref[...] * (_rms(acc_ref[...]) * gout_ref[...])


def _mlp(x, g_in, scale, shift, w1, w2, layer, g_out, gate, tm, tf):
    m = x.shape[0]
    nmod = gate.shape[0]
    blocks_per_mod = m // nmod // tm
    mod_spec = pl.BlockSpec((None, 1, D_MODEL), lambda i, j: (i // blocks_per_mod, 0, 0))
    row_spec = pl.BlockSpec((1, D_MODEL), lambda i, j: (0, 0))
    return pl.pallas_call(
        _mlp_kernel,
        grid=(m // tm, D_FF // tf),
        in_specs=[pl.BlockSpec((tm, D_MODEL), lambda i, j: (i, 0)),
                  row_spec, mod_spec, mod_spec,
                  pl.BlockSpec((None, D_MODEL, tf), lambda i, j: (layer, 0, j)),
                  pl.BlockSpec((None, tf, D_MODEL), lambda i, j: (layer, j, 0)),
                  row_spec, mod_spec],
        out_specs=pl.BlockSpec((tm, D_MODEL), lambda i, j: (i, 0)),
        out_shape=jax.ShapeDtypeStruct((m, D_MODEL), F32),
        scratch_shapes=[pltpu.VMEM((tm, D_MODEL), BF16), pltpu.VMEM((tm, D_MODEL), F32)],
        name="channel_mlp",
        compiler_params=_cparams(("arbitrary", "arbitrary")),
    )(x, g_in.reshape(1, D_MODEL), scale, shift, w1, w2, g_out.reshape(1, D_MODEL), gate)


def _rope_tables(l):
    pos = jnp.arange(l, dtype=jnp.int32)
    quarter = HEAD_DIM // 4
    freqs = ROPE_THETA ** (-jnp.arange(quarter, dtype=F32) / quarter)
    ang_r = (pos // GRID_W).astype(F32)[:, None] * freqs[None, :]
    ang_c = (pos % GRID_W).astype(F32)[:, None] * freqs[None, :]
    cos_t = jnp.concatenate([jnp.cos(ang_r), jnp.cos(ang_r), jnp.cos(ang_c), jnp.cos(ang_c)], axis=-1)
    sin_t = jnp.concatenate([-jnp.sin(ang_r), jnp.sin(ang_r), -jnp.sin(ang_c), jnp.sin(ang_c)], axis=-1)
    return cos_t, sin_t


def _decay_rates():
    min_decay = math.log(DECAY_TARGET) / SLOW_DECAY_PCT
    max_decay = math.log(DECAY_TARGET) / FAST_DECAY_PCT
    deltas = jnp.tile(jnp.linspace(min_decay, max_decay, HYENA_WIDTH, dtype=F32), 2)
    return jnp.abs(deltas)[None, :]


def kernel(x, c, ctx, c_ctx, w_mod, b_mod, norm_g, w_in, attn_sink, hy_conv_w, hy_conv_b,
           hy_fw1, hy_fb1, hy_ff1, hy_fw2, hy_fb2, hy_ff2, hy_fw3, hy_fb3, hy_ff3, hy_fw4,
           hy_bias, w_attn_out, w_hyena_out, w_out, w_ff1, w_ff2):
    b, l, d = x.shape
    cl = ctx.shape[1]
    assert d == D_MODEL and l % 1024 == 0 and cl % 256 == 0 and b + 1 <= MOD_ROWS

    cos_t, sin_t = _rope_tables(l)
    decay = _decay_rates()
    tables_lat = _fft_tables(l, FFT_N2)
    tables_ctx = _fft_tables(cl, 1)
    z_lat = _filter_features(l)
    z_ctx = _filter_features(cl)

    c_rows = jnp.concatenate([c, c_ctx[None, :], jnp.zeros((MOD_ROWS - b - 1, d), F32)], axis=0)
    x_lat = x.reshape(b * l, d)
    x_ctx = ctx.reshape(b * cl, d)
    tm_ctx = b * cl
    w1b = w_ff1.astype(BF16)
    w2b = w_ff2.astype(BF16)

    for layer in range(DEPTH):
        last = layer == DEPTH - 1
        mod = _modulation(c_rows, w_mod, layer, b_mod[layer])
        mod_lat = [mod[:b, k * d:(k + 1) * d].reshape(b, 1, d) for k in range(N_MOD)]
        mod_ctx = [mod[b:b + 1, k * d:(k + 1) * d].reshape(1, 1, d) for k in range(N_MOD)]
        sh1, sc1, g1, sh2, sc2, g2 = mod_lat
        csh1, csc1, cg1, csh2, csc2, cg2 = mod_ctx
        g = norm_g[layer]
        fparams = (hy_fw1[layer], hy_fb1[layer], hy_ff1[layer], hy_fw2[layer], hy_fb2[layer],
                   hy_ff2[layer], hy_fw3[layer], hy_fb3[layer], hy_ff3[layer])
        bias_d = hy_bias[layer].reshape(1, HYENA_WIDTH)
        conv_b = hy_conv_b[layer].reshape(1, 3 * HYENA_WIDTH)

        p_lat = _normproj(x_lat, g[0], sc1, sh1, w_in, layer, 0, IN_WIDTH, 2048, 512)
        if last:
            kv_ctx = _normproj(x_ctx, g[0], csc1, csh1, w_in, layer, Q_END, V_END - Q_END,
                               tm_ctx, 512)
            kx, vx = kv_ctx[:, :KV_WIDTH], kv_ctx[:, KV_WIDTH:]
        else:
            p_ctx = _normproj(x_ctx, g[0], csc1, csh1, w_in, layer, 0, IN_WIDTH, tm_ctx, 512)
            kx, vx = p_ctx[:, Q_END:K_END], p_ctx[:, K_END:V_END]
        kx = kx.reshape(b, cl, KV_WIDTH)
        vx = vx.reshape(b, cl, KV_WIDTH)

        p3 = p_lat.reshape(b, l, IN_WIDTH)
        attn = _window_attention(p3, kx, vx, attn_sink[layer], cos_t, sin_t)
        kf = _filter_spectrum(_filter_mlp(z_lat, *fparams), hy_fw4[layer], decay, tables_lat)
        hy = _hyena(p3, V_END, hy_conv_w[layer], conv_b, bias_d, kf, tables_lat)
        mix = _merge(attn.reshape(b * l, ATTN_WIDTH), hy.reshape(b * l, HYENA_WIDTH), p_lat,
                     w_attn_out, w_hyena_out, layer, 1024, 512)
        x_lat = _outproj(mix, w_out, layer, x_lat, g[1], g1, 512, 512)

        if not last:
            pc3 = p_ctx.reshape(b, cl, IN_WIDTH)
            attn_c = _context_attention(pc3, attn_sink[layer])
            kf_c = _filter_spectrum(_filter_mlp(z_ctx, *fparams), hy_fw4[layer], decay, tables_ctx)
            hy_c = _hyena(pc3, V_END, hy_conv_w[layer], conv_b, bias_d, kf_c, tables_ctx)
            mix_c = _merge(attn_c.reshape(b * cl, ATTN_WIDTH), hy_c.reshape(b * cl, HYENA_WIDTH),
                           p_ctx, w_attn_out, w_hyena_out, layer, tm_ctx, 512)
            x_ctx = _outproj(mix_c, w_out, layer, x_ctx, g[1], cg1, tm_ctx, 512)
            x_ctx = _mlp(x_ctx, g[2], csc2, csh2, w1b, w2b, layer, g[3], cg2, tm_ctx, 1024)

        x_lat = _mlp(x_lat, g[2], sc2, sh2, w1b, w2b, layer, g[3], g2, 512, 1024)
    return x_lat.reshape(b, l, d)
```

```python
import functools
import math

import jax
import jax.numpy as jnp
from jax import lax
from jax.experimental import pallas as pl
from jax.experimental.pallas import tpu as pltpu

F32 = jnp.float32
BF16 = jnp.bfloat16

D_MODEL = 2048
DEPTH = 2
GRID_W = 64
HEAD_DIM = 128
N_Q_HEADS = 8
N_KV_HEADS = 2
Q_GROUP = N_Q_HEADS // N_KV_HEADS
ATTN_WIDTH = N_Q_HEADS * HEAD_DIM
KV_WIDTH = N_KV_HEADS * HEAD_DIM
BLOCK = 128
ROPE_THETA = 10000.0
HYENA_WIDTH = 1024
FILTER_EMB = 33
FILTER_HIDDEN = 64
DECAY_TARGET = 1e-2
FAST_DECAY_PCT = 0.3
SLOW_DECAY_PCT = 1.5
D_FF = 4 * D_MODEL
EPS = 1e-6
N_MOD = 6
NEG_INF = -1e30
Q_END = ATTN_WIDTH
K_END = Q_END + KV_WIDTH
V_END = K_END + KV_WIDTH
HY_END = V_END + 3 * HYENA_WIDTH
GA_END = HY_END + D_MODEL
GH_END = GA_END + D_MODEL
IN_WIDTH = GH_END

LANES = 128
VMEM_LIMIT = 56 * 1024 * 1024

FFT_N2 = 16
FFT_KG = 16
PROLOGUE_ROWS = 256
MLP_COLS = 512
HY_CB = 128
MOD_ROWS = 8


def _cparams(sem):
    return pltpu.CompilerParams(dimension_semantics=sem, vmem_limit_bytes=VMEM_LIMIT)


def _single(block_shape, index_map):
    return pl.BlockSpec(block_shape, index_map, pipeline_mode=pl.Buffered(1))


def _rms(x):
    return x * lax.rsqrt(jnp.mean(x * x, axis=-1, keepdims=True) + EPS)


def _row_chunks(nrows, fn):
    def chunk(r, carry):
        fn(pl.ds(pl.multiple_of(r * PROLOGUE_ROWS, PROLOGUE_ROWS), PROLOGUE_ROWS))
        return carry

    lax.fori_loop(0, nrows // PROLOGUE_ROWS, chunk, 0)


def _mod_kernel(c_ref, w_ref, b_ref, o_ref):
    c = c_ref[...]
    s = c * jax.nn.sigmoid(c)
    o_ref[...] = jnp.dot(s.astype(BF16), w_ref[...].astype(BF16),
                         preferred_element_type=F32) + b_ref[...]


def _modulation(c_rows, w, layer, b):
    n = w.shape[2]
    tn = 1024
    return pl.pallas_call(
        _mod_kernel,
        grid=(n // tn,),
        in_specs=[pl.BlockSpec((MOD_ROWS, D_MODEL), lambda j: (0, 0)),
                  pl.BlockSpec((None, D_MODEL, tn), lambda j: (layer, 0, j)),
                  pl.BlockSpec((1, tn), lambda j: (0, j))],
        out_specs=pl.BlockSpec((MOD_ROWS, tn), lambda j: (0, j)),
        out_shape=jax.ShapeDtypeStruct((MOD_ROWS, n), F32),
        name="modulation",
        compiler_params=_cparams(("arbitrary",)),
    )(c_rows, w, b.reshape(1, n))


def _normproj_kernel(x_ref, g_ref, sc_ref, sh_ref, w_ref, o_ref, h_ref):
    @pl.when(pl.program_id(1) == 0)
    def _():
        g, sc1, sh = g_ref[...], 1.0 + sc_ref[...], sh_ref[...]

        def prologue(rows):
            h_ref[rows, :] = (_rms(x_ref[rows, :]) * g * sc1 + sh).astype(BF16)

        _row_chunks(x_ref.shape[0], prologue)

    o_ref[...] = jnp.dot(h_ref[...], w_ref[...].astype(BF16), preferred_element_type=F32)


def _normproj(x, g, scale, shift, w, layer, col0, n, tm, tn):
    m = x.shape[0]
    j0 = col0 // tn
    nmod = scale.shape[0]
    blocks_per_mod = m // nmod // tm
    mod_spec = pl.BlockSpec((None, 1, D_MODEL), lambda i, j: (i // blocks_per_mod, 0, 0))
    return pl.pallas_call(
        _normproj_kernel,
        grid=(m // tm, n // tn),
        in_specs=[pl.BlockSpec((tm, D_MODEL), lambda i, j: (i, 0), pipeline_mode=pl.Buffered(1)),
                  pl.BlockSpec((1, D_MODEL), lambda i, j: (0, 0)),
                  mod_spec, mod_spec,
                  pl.BlockSpec((None, D_MODEL, tn), lambda i, j: (layer, 0, j0 + j))],
        out_specs=pl.BlockSpec((tm, tn), lambda i, j: (i, j)),
        out_shape=jax.ShapeDtypeStruct((m, n), F32),
        scratch_shapes=[pltpu.VMEM((tm, D_MODEL), BF16)],
        name="normproj",
        compiler_params=_cparams(("arbitrary", "arbitrary")),
    )(x, g.reshape(1, D_MODEL), scale, shift, w)


def _rope(x, cos, sin_signed, first_half):
    rot = jnp.where(first_half, pltpu.roll(x, HEAD_DIM - 32, axis=1), pltpu.roll(x, 32, axis=1))
    return x * cos + rot * sin_signed


def _softmax_pv(s, sink, v):
    m = jnp.maximum(jnp.max(s, axis=-1, keepdims=True), sink)
    e = jnp.exp(s - m)
    denom = jnp.sum(e, axis=-1, keepdims=True) + jnp.exp(sink - m)
    o = jnp.dot(e.astype(BF16), v, preferred_element_type=F32)
    return o / denom


def _band_bias(nctx):
    qi = (jnp.arange(Q_GROUP * BLOCK, dtype=jnp.int32) % BLOCK)[None, :, None]
    kj = jnp.arange(3 * BLOCK + nctx, dtype=jnp.int32)[None, None, :]
    variant = jnp.arange(3, dtype=jnp.int32)[:, None, None]
    in_prev = kj < BLOCK
    in_next = (kj >= 2 * BLOCK) & (kj < 3 * BLOCK)
    valid = jnp.where(in_prev, (kj >= qi) & (variant != 0),
                      jnp.where(in_next, (kj - 2 * BLOCK <= qi) & (variant != 2), True))
    return jnp.where(valid, 0.0, NEG_INF).astype(F32)


def _win_attn_kernel(sink_ref, q_ref, kp_ref, kc_ref, kn_ref, vp_ref, vc_ref, vn_ref,
                     kx_ref, vx_ref, cos_ref, sin_ref, bias_ref, o_ref, *, nb):
    i = pl.program_id(1)
    scale = HEAD_DIM ** -0.5
    lane = lax.broadcasted_iota(jnp.int32, (BLOCK, HEAD_DIM), 1)
    first_half = (lane % 64) < 32

    def table(ref, blk):
        return ref[pl.ds(pl.multiple_of(blk * BLOCK, BLOCK), BLOCK), :]

    ip = jnp.maximum(i - 1, 0)
    inx = jnp.minimum(i + 1, nb - 1)
    cos_c, sin_c = table(cos_ref, i), table(sin_ref, i)
    cos_p, sin_p = table(cos_ref, ip), table(sin_ref, ip)
    cos_n, sin_n = table(cos_ref, inx), table(sin_ref, inx)

    rows = Q_GROUP * BLOCK
    head_in_group = lax.broadcasted_iota(jnp.int32, (rows, 1), 0) // BLOCK
    bias = bias_ref[jnp.where(i == 0, 0, jnp.where(i == nb - 1, 2, 1))]

    for h in range(N_KV_HEADS):
        hs = slice(h * HEAD_DIM, (h + 1) * HEAD_DIM)
        k = jnp.concatenate([
            _rope(kp_ref[:, hs], cos_p, sin_p, first_half),
            _rope(kc_ref[:, hs], cos_c, sin_c, first_half),
            _rope(kn_ref[:, hs], cos_n, sin_n, first_half),
            kx_ref[:, hs]], axis=0).astype(BF16)
        v = jnp.concatenate([vp_ref[:, hs], vc_ref[:, hs], vn_ref[:, hs], vx_ref[:, hs]],
                            axis=0).astype(BF16)
        heads = [h * Q_GROUP + g for g in range(Q_GROUP)]
        q = jnp.concatenate(
            [_rope(q_ref[:, hd * HEAD_DIM:(hd + 1) * HEAD_DIM], cos_c, sin_c, first_half)
             for hd in heads], axis=0).astype(BF16)
        sink = jnp.zeros((rows, 1), F32)
        for g, hd in enumerate(heads):
            sink = jnp.where(head_in_group == g, sink_ref[hd], sink)
        s = lax.dot_general(q, k, (((1,), (1,)), ((), ())), preferred_element_type=F32) * scale
        o = _softmax_pv(s + bias, sink, v).astype(o_ref.dtype)
        for g, hd in enumerate(heads):
            o_ref[:, hd * HEAD_DIM:(hd + 1) * HEAD_DIM] = o[g * BLOCK:(g + 1) * BLOCK]


def _window_attention(p, kx, vx, sink, cos_t, sin_t):
    b, l, _ = p.shape
    c = kx.shape[1]
    nb = l // BLOCK
    kcol = Q_END // KV_WIDTH
    vcol = K_END // KV_WIDTH

    def kv_spec(col, shift):
        return pl.BlockSpec((None, BLOCK, KV_WIDTH),
                            lambda bi, i: (bi, jnp.clip(i + shift, 0, nb - 1), col))

    assert nb >= 2
    bias = _band_bias(c)
    ctx_spec = pl.BlockSpec((None, c, KV_WIDTH), lambda bi, i: (bi, 0, 0))
    tab_spec = _single((l, HEAD_DIM), lambda bi, i: (0, 0))
    return pl.pallas_call(
        functools.partial(_win_attn_kernel, nb=nb),
        grid=(b, nb),
        in_specs=[pl.BlockSpec(memory_space=pltpu.SMEM),
                  pl.BlockSpec((None, BLOCK, ATTN_WIDTH), lambda bi, i: (bi, i, 0)),
                  kv_spec(kcol, -1), kv_spec(kcol, 0), kv_spec(kcol, 1),
                  kv_spec(vcol, -1), kv_spec(vcol, 0), kv_spec(vcol, 1),
                  ctx_spec, ctx_spec, tab_spec, tab_spec,
                  _single(bias.shape, lambda bi, i: (0, 0, 0))],
        out_specs=pl.BlockSpec((None, BLOCK, ATTN_WIDTH), lambda bi, i: (bi, i, 0)),
        out_shape=jax.ShapeDtypeStruct((b, l, ATTN_WIDTH), BF16),
        name="window_attention",
        compiler_params=_cparams(("arbitrary", "arbitrary")),
    )(sink, p, p, p, p, p, p, p, kx, vx, cos_t, sin_t, bias)


def _ctx_attn_kernel(sink_ref, q_ref, k_ref, v_ref, o_ref):
    scale = HEAD_DIM ** -0.5
    for h in range(N_KV_HEADS):
        hs = slice(h * HEAD_DIM, (h + 1) * HEAD_DIM)
        k = k_ref[:, hs].astype(BF16)
        v = v_ref[:, hs].astype(BF16)
        for g in range(Q_GROUP):
            head = h * Q_GROUP + g
            cs = slice(head * HEAD_DIM, (head + 1) * HEAD_DIM)
            q = q_ref[:, cs].astype(BF16)
            s = lax.dot_general(q, k, (((1,), (1,)), ((), ())), preferred_element_type=F32) * scale
            o_ref[:, cs] = _softmax_pv(s, sink_ref[head], v).astype(o_ref.dtype)


def _context_attention(p, sink):
    b, c, _ = p.shape
    return pl.pallas_call(
        _ctx_attn_kernel,
        grid=(b,),
        in_specs=[pl.BlockSpec(memory_space=pltpu.SMEM),
                  pl.BlockSpec((None, c, ATTN_WIDTH), lambda bi: (bi, 0, 0)),
                  pl.BlockSpec((None, c, KV_WIDTH), lambda bi: (bi, 0, Q_END // KV_WIDTH)),
                  pl.BlockSpec((None, c, KV_WIDTH), lambda bi: (bi, 0, K_END // KV_WIDTH))],
        out_specs=pl.BlockSpec((None, c, ATTN_WIDTH), lambda bi: (bi, 0, 0)),
        out_shape=jax.ShapeDtypeStruct((b, c, ATTN_WIDTH), BF16),
        name="context_attention",
        compiler_params=_cparams(("arbitrary",)),
    )(sink, p, p, p)


def _fft_tables(l, n2):
    n = 2 * l
    n1 = n // n2
    k1 = jnp.arange(n1, dtype=jnp.int32)[None, :, None]
    t = (n2 * jnp.arange(n1 // 2, dtype=jnp.int32)[None, None, :]
         + jnp.arange(n2, dtype=jnp.int32)[:, None, None])
    ang = ((k1 * t) % n).astype(F32) * (2.0 * math.pi / n)
    fwd1 = jnp.concatenate([jnp.cos(ang), -jnp.sin(ang)], axis=1)
    inv1 = jnp.swapaxes(fwd1, 1, 2) * (1.0 / n)
    return fwd1.astype(BF16), inv1.astype(BF16)


def _filter_features(l):
    bands = (FILTER_EMB - 1) // 2
    t = jnp.linspace(0.0, 1.0, l, dtype=F32)[:, None]
    w = 2 * math.pi * jnp.arange(l, dtype=F32)[:, None] / l
    f = jnp.linspace(1e-4, bands - 1, bands, dtype=F32)[None, :]
    z = jnp.concatenate([t, jnp.cos(f * w), -jnp.sin(f * w)], axis=-1)
    return jnp.pad(z, ((0, 0), (0, FILTER_HIDDEN - FILTER_EMB)))


def _filter_mlp_kernel(z_ref, w1_ref, b1_ref, f1_ref, w2_ref, b2_ref, f2_ref,
                       w3_ref, b3_ref, f3_ref, o_ref):
    h = jnp.sin(f1_ref[...] * (jnp.dot(z_ref[...], w1_ref[...], preferred_element_type=F32)
                               + b1_ref[...]))
    h = jnp.sin(f2_ref[...] * (jnp.dot(h, w2_ref[...], preferred_element_type=F32) + b2_ref[...]))
    o_ref[...] = jnp.sin(f3_ref[...] * (jnp.dot(h, w3_ref[...], preferred_element_type=F32)
                                        + b3_ref[...]))


def _filter_mlp(z, fw1, fb1, ff1, fw2, fb2, ff2, fw3, fb3, ff3):
    l = z.shape[0]
    row = lambda a: a.reshape(1, FILTER_HIDDEN)
    w1 = jnp.pad(fw1, ((0, FILTER_HIDDEN - FILTER_EMB), (0, 0)))
    return pl.pallas_call(
        _filter_mlp_kernel,
        out_shape=jax.ShapeDtypeStruct((l, FILTER_HIDDEN), F32),
        name="filter_mlp",
        compiler_params=pltpu.CompilerParams(vmem_limit_bytes=VMEM_LIMIT),
    )(z, w1, row(fb1), row(ff1), fw2, row(fb2), row(ff2), fw3, row(fb3), row(ff3))


def _cmul_root16(z, p, inverse):
    zr, zi = z
    p = p % 16
    if inverse:
        p = (16 - p) % 16
    if p == 0:
        return zr, zi
    if p == 4:
        return zi, -zr
    if p == 8:
        return -zr, -zi
    if p == 12:
        return -zi, zr
    c = math.cos(2.0 * math.pi * p / 16)
    s = -math.sin(2.0 * math.pi * p / 16)
    return zr * c - zi * s, zr * s + zi * c


def _dft4(z, inverse):
    (ar, ai), (br, bi), (cr, ci), (dr, di) = z
    t0r, t0i = ar + cr, ai + ci
    t1r, t1i = ar - cr, ai - ci
    t2r, t2i = br + dr, bi + di
    t3r, t3i = br - dr, bi - di
    y0 = (t0r + t2r, t0i + t2i)
    y2 = (t0r - t2r, t0i - t2i)
    minus_i_t3 = (t1r + t3i, t1i - t3r)
    plus_i_t3 = (t1r - t3i, t1i + t3r)
    return [y0, plus_i_t3, y2, minus_i_t3] if inverse else [y0, minus_i_t3, y2, plus_i_t3]


def _dft_digit(z, inverse):
    if len(z) == 1:
        return z
    assert len(z) == 16
    t = [_dft4([z[4 * a + b] for a in range(4)], inverse) for b in range(4)]
    out = [None] * 16
    for c in range(4):
        y = _dft4([_cmul_root16(t[b][c], b * c, inverse) for b in range(4)], inverse)
        for d in range(4):
            out[c + 4 * d] = y[d]
    return out


def _fft_stage1(src_ref, f1_ref, a_ref):
    n2cnt = f1_ref.shape[0]
    groups = a_ref.shape[0]
    n1 = groups * FFT_KG
    half = src_ref.shape[0] // n2cnt

    def body(n2, carry):
        rows = pl.ds(n2, half, stride=n2cnt) if n2cnt > 1 else pl.ds(0, half)
        res = jnp.dot(f1_ref[n2], src_ref[rows, :].astype(BF16), preferred_element_type=F32)
        for g in range(groups):
            a_ref[g, 0, n2] = res[g * FFT_KG:(g + 1) * FFT_KG]
            a_ref[g, 1, n2] = res[n1 + g * FFT_KG:n1 + (g + 1) * FFT_KG]
        return carry

    if n2cnt == 1:
        body(0, 0)
    else:
        lax.fori_loop(0, n2cnt, body, 0, unroll=2)


def _load_digits(ref, g, rows):
    return [(ref[g, 0, d, rows, :], ref[g, 1, d, rows, :]) for d in range(ref.shape[2])]


def _filter_spec_kernel(h3_ref, wf_ref, wb_ref, df_ref, db_ref, f1_ref, kf_ref,
                        hf_ref, hb_ref, a_ref, *, l):
    groups = a_ref.shape[0]
    row = lax.broadcasted_iota(jnp.int32, (l, 1), 0)
    t = row.astype(F32) * (1.0 / (l - 1))
    h3 = h3_ref[...]
    hf = jnp.dot(h3, wf_ref[...], preferred_element_type=F32) * jnp.exp(-t * df_ref[...])
    hb = jnp.dot(h3, wb_ref[...], preferred_element_type=F32) * jnp.exp(-t * db_ref[...])
    hb = jnp.where(row > 0, hb, 0.0)
    norm = jnp.sum(jnp.abs(hf), axis=0, keepdims=True) + jnp.sum(jnp.abs(hb), axis=0, keepdims=True)
    hf_ref[...] = hf / norm
    hb_ref[...] = hb / norm

    def forward(g, carry):
        for r in range(0, FFT_KG, 8):
            rows = pl.ds(r, 8)
            for k2, (xr, xi) in enumerate(_dft_digit(_load_digits(a_ref, g, rows), False)):
                kf_ref[g, 0, k2, rows, :] = xr
                kf_ref[g, 1, k2, rows, :] = xi
        return carry

    def backward(g, carry):
        for r in range(0, FFT_KG, 8):
            rows = pl.ds(r, 8)
            for k2, (xr, xi) in enumerate(_dft_digit(_load_digits(a_ref, g, rows), False)):
                kf_ref[g, 0, k2, rows, :] = kf_ref[g, 0, k2, rows, :] + xr
                kf_ref[g, 1, k2, rows, :] = kf_ref[g, 1, k2, rows, :] - xi
        return carry

    _fft_stage1(hf_ref, f1_ref, a_ref)
    lax.fori_loop(0, groups, forward, 0)
    _fft_stage1(hb_ref, f1_ref, a_ref)
    lax.fori_loop(0, groups, backward, 0)


def _filter_spectrum(h3, fw4, decay, tables):
    l = h3.shape[0]
    f1, _ = tables
    n2 = f1.shape[0]
    groups = f1.shape[1] // 2 // FFT_KG
    nblk = HYENA_WIDTH // HY_CB
    spec_shape = (groups, 2, n2, FFT_KG)
    return pl.pallas_call(
        functools.partial(_filter_spec_kernel, l=l),
        grid=(nblk,),
        in_specs=[_single((l, FILTER_HIDDEN), lambda c: (0, 0)),
                  pl.BlockSpec((FILTER_HIDDEN, HY_CB), lambda c: (0, c)),
                  pl.BlockSpec((FILTER_HIDDEN, HY_CB), lambda c: (0, c + nblk)),
                  pl.BlockSpec((1, HY_CB), lambda c: (0, c)),
                  pl.BlockSpec((1, HY_CB), lambda c: (0, c + nblk)),
                  _single(f1.shape, lambda c: (0, 0, 0))],
        out_specs=pl.BlockSpec(spec_shape + (HY_CB,), lambda c: (0, 0, 0, 0, c)),
        out_shape=jax.ShapeDtypeStruct(spec_shape + (HYENA_WIDTH,), F32),
        scratch_shapes=[pltpu.VMEM((l, HY_CB), F32), pltpu.VMEM((l, HY_CB), F32),
                        pltpu.VMEM(spec_shape + (HY_CB,), F32)],
        name="filter_spectrum",
        compiler_params=_cparams(("arbitrary",)),
    )(h3, fw4, fw4, decay, decay, f1)


def _short_conv(src_ref, w_ref, b_ref, l, rows):
    w0, w1, w2, b = w_ref[0:1, :], w_ref[1:2, :], w_ref[2:3, :], b_ref[...]
    r = lax.broadcasted_iota(jnp.int32, (rows, 1), 0)
    for s in range(0, l, rows):
        cur = src_ref[s:s + rows, :]
        before = src_ref[s - 1:s, :] if s > 0 else jnp.zeros((1, cur.shape[1]), F32)
        after = src_ref[s + rows:s + rows + 1, :] if s + rows < l else jnp.zeros((1, cur.shape[1]), F32)
        prev = jnp.where(r == 0, before, pltpu.roll(cur, 1, axis=0))
        nxt = jnp.where(r == rows - 1, after, pltpu.roll(cur, rows - 1, axis=0))
        yield s, prev * w0 + cur * w1 + nxt * w2 + b


def _hyena_kernel(x0_ref, x1_ref, v_ref, w0_ref, w1_ref, wv_ref, b0_ref, b1_ref, bv_ref,
                  bias_ref, kf_ref, f1_ref, g1_ref, o_ref, x0c_ref, vg_ref, a_ref, *, l):
    groups = a_ref.shape[0]
    n2cnt = a_ref.shape[2]
    cb = a_ref.shape[-1]
    n1 = groups * FFT_KG
    half = l // n2cnt
    rows = min(l, 512)

    for s, u in _short_conv(x0_ref, w0_ref, b0_ref, l, rows):
        x0c_ref[s:s + rows, :] = u
    for (s, u1), (_, uv) in zip(_short_conv(x1_ref, w1_ref, b1_ref, l, rows),
                                _short_conv(v_ref, wv_ref, bv_ref, l, rows)):
        vg_ref[s:s + rows, :] = uv * u1

    _fft_stage1(vg_ref, f1_ref, a_ref)

    def spectrum(g, carry):
        for r in range(0, FFT_KG, 8):
            rws = pl.ds(r, 8)
            x = _dft_digit(_load_digits(a_ref, g, rws), False)
            k = _load_digits(kf_ref, g, rws)
            y = [(xr * kr - xi * ki, xr * ki + xi * kr) for (xr, xi), (kr, ki) in zip(x, k)]
            for d, (br, bi) in enumerate(_dft_digit(y, True)):
                a_ref[g, 0, d, rws, :] = br
                a_ref[g, 1, d, rws, :] = bi
        return carry

    lax.fori_loop(0, groups, spectrum, 0)

    def synth(n2, carry):
        b = jnp.concatenate([a_ref[:, 0, n2].reshape(n1, cb), a_ref[:, 1, n2].reshape(n1, cb)],
                            axis=0).astype(BF16)
        y = jnp.dot(g1_ref[n2], b, preferred_element_type=F32)
        idx = pl.ds(n2, half, stride=n2cnt) if n2cnt > 1 else pl.ds(0, half)
        o_ref[idx, :] = (y + vg_ref[idx, :] * bias_ref[...]) * x0c_ref[idx, :]
        return carry

    if n2cnt == 1:
        synth(0, 0)
    else:
        lax.fori_loop(0, n2cnt, synth, 0, unroll=2)


def _hyena(p, col0, conv_w, conv_b, bias_d, kf, tables):
    b, l, _ = p.shape
    f1, g1 = tables
    nblk = HYENA_WIDTH // HY_CB
    c0 = col0 // HY_CB
    spec_block = kf.shape[:-1] + (HY_CB,)

    def slab(part):
        return pl.BlockSpec((None, l, HY_CB), lambda c, bi: (bi, 0, c0 + part * nblk + c))

    def cw(part):
        return pl.BlockSpec((3, HY_CB), lambda c, bi: (0, part * nblk + c))

    def cbias(part):
        return pl.BlockSpec((1, HY_CB), lambda c, bi: (0, part * nblk + c))

    return pl.pallas_call(
        functools.partial(_hyena_kernel, l=l),
        grid=(nblk, b),
        in_specs=[slab(0), slab(1), slab(2), cw(0), cw(1), cw(2), cbias(0), cbias(1), cbias(2),
                  pl.BlockSpec((1, HY_CB), lambda c, bi: (0, c)),
                  pl.BlockSpec(spec_block, lambda c, bi: (0, 0, 0, 0, c),
                               pipeline_mode=pl.Buffered(1)),
                  _single(f1.shape, lambda c, bi: (0, 0, 0)),
                  _single(g1.shape, lambda c, bi: (0, 0, 0))],
        out_specs=pl.BlockSpec((None, l, HY_CB), lambda c, bi: (bi, 0, c)),
        out_shape=jax.ShapeDtypeStruct((b, l, HYENA_WIDTH), F32),
        scratch_shapes=[pltpu.VMEM((l, HY_CB), F32), pltpu.VMEM((l, HY_CB), F32),
                        pltpu.VMEM(spec_block, F32)],
        name="hyena_conv",
        compiler_params=_cparams(("arbitrary", "arbitrary")),
    )(p, p, p, conv_w, conv_w, conv_w, conv_b, conv_b, conv_b, bias_d, kf, f1, g1)


def _merge_kernel(a_ref, h_ref, ga_ref, gh_ref, wa_ref, wh_ref, o_ref):
    ya = jnp.dot(a_ref[...], wa_ref[...].astype(BF16), preferred_element_type=F32)
    yh = jnp.dot(h_ref[...].astype(BF16), wh_ref[...].astype(BF16), preferred_element_type=F32)
    o_ref[...] = (jax.nn.sigmoid(ga_ref[...]) * ya
                  + jax.nn.sigmoid(gh_ref[...]) * yh).astype(o_ref.dtype)


def _merge(attn, hy, p, w_ao, w_ho, layer, tm, tn):
    m = attn.shape[0]
    ga0 = HY_END // tn
    gh0 = GA_END // tn
    return pl.pallas_call(
        _merge_kernel,
        grid=(m // tm, D_MODEL // tn),
        in_specs=[pl.BlockSpec((tm, ATTN_WIDTH), lambda i, j: (i, 0)),
                  pl.BlockSpec((tm, HYENA_WIDTH), lambda i, j: (i, 0)),
                  pl.BlockSpec((tm, tn), lambda i, j: (i, ga0 + j)),
                  pl.BlockSpec((tm, tn), lambda i, j: (i, gh0 + j)),
                  pl.BlockSpec((None, ATTN_WIDTH, tn), lambda i, j: (layer, 0, j)),
                  pl.BlockSpec((None, HYENA_WIDTH, tn), lambda i, j: (layer, 0, j))],
        out_specs=pl.BlockSpec((tm, tn), lambda i, j: (i, j)),
        out_shape=jax.ShapeDtypeStruct((m, D_MODEL), BF16),
        name="branch_merge",
        compiler_params=_cparams(("arbitrary", "arbitrary")),
    )(attn, hy, p, p, w_ao, w_ho)


def _outproj_kernel(mix_ref, w_ref, x_ref, g_ref, gate_ref, o_ref, y_ref, *, tn):
    j = pl.program_id(1)
    nj = y_ref.shape[0]
    y_ref[j] = jnp.dot(mix_ref[...], w_ref[...].astype(BF16), preferred_element_type=F32)

    @pl.when(j == nj - 1)
    def _():
        ss = sum(jnp.sum(jnp.square(y_ref[k]), axis=-1, keepdims=True) for k in range(nj))
        r = lax.rsqrt(ss * (1.0 / D_MODEL) + EPS)
        for k in range(nj):
            cs = slice(k * tn, (k + 1) * tn)
            o_ref[:, cs] = x_ref[:, cs] + gate_ref[:, cs] * ((y_ref[k] * r) * g_ref[:, cs])


def _outproj(mix, w_o, layer, x, g, gate, tm, tn):
    m = x.shape[0]
    nmod = gate.shape[0]
    blocks_per_mod = m // nmod // tm
    return pl.pallas_call(
        functools.partial(_outproj_kernel, tn=tn),
        grid=(m // tm, D_MODEL // tn),
        in_specs=[pl.BlockSpec((tm, D_MODEL), lambda i, j: (i, 0), pipeline_mode=pl.Buffered(1)),
                  pl.BlockSpec((None, D_MODEL, tn), lambda i, j: (layer, 0, j)),
                  pl.BlockSpec((tm, D_MODEL), lambda i, j: (i, 0), pipeline_mode=pl.Buffered(1)),
                  pl.BlockSpec((1, D_MODEL), lambda i, j: (0, 0)),
                  pl.BlockSpec((None, 1, D_MODEL), lambda i, j: (i // blocks_per_mod, 0, 0))],
        out_specs=pl.BlockSpec((tm, D_MODEL), lambda i, j: (i, 0)),
        out_shape=jax.ShapeDtypeStruct((m, D_MODEL), F32),
        scratch_shapes=[pltpu.VMEM((D_MODEL // tn, tm, tn), F32)],
        name="out_projection",
        compiler_params=_cparams(("arbitrary", "arbitrary")),
    )(mix, w_o, x, g.reshape(1, D_MODEL), gate)


def _mlp_kernel(x_ref, gin_ref, sc_ref, sh_ref, w1_ref, w2_ref, gout_ref, gate_ref, o_ref, h_ref):
    j = pl.program_id(1)
    tm = x_ref.shape[0]

    @pl.when(j == 0)
    def _():
        gin, sc1, sh = gin_ref[...], 1.0 + sc_ref[...], sh_ref[...]

        def prologue(rows):
            h_ref[rows, :] = (_rms(x_ref[rows, :]) * gin * sc1 + sh).astype(BF16)
            o_ref[rows, :] = jnp.zeros((PROLOGUE_ROWS, D_MODEL), F32)

        _row_chunks(tm, prologue)

    a = jnp.dot(h_ref[...], w1_ref[...], preferred_element_type=F32)
    a = jnp.square(jnp.maximum(a, 0.0)).astype(BF16)
    for n in range(0, D_MODEL, MLP_COLS):
        o_ref[:, n:n + MLP_COLS] += jnp.dot(a, w2_ref[:, n:n + MLP_COLS],
                                            preferred_element_type=F32)

    @pl.when(j == pl.num_programs(1) - 1)
    def _():
        gout, gate = gout_ref[...], gate_ref[...]

        def epilogue(rows):
            o_ref[rows, :] = x_ref[rows, :] + gate * (_rms(o_ref[rows, :]) * gout)

        _row_chunks(tm, epilogue)


def _mlp(x, g_in, scale, shift, w1, w2, layer, g_out, gate, tm, tf):
    m = x.shape[0]
    nmod = gate.shape[0]
    blocks_per_mod = m // nmod // tm
    mod_spec = pl.BlockSpec((None, 1, D_MODEL), lambda i, j: (i // blocks_per_mod, 0, 0))
    row_spec = pl.BlockSpec((1, D_MODEL), lambda i, j: (0, 0))
    return pl.pallas_call(
        _mlp_kernel,
        grid=(m // tm, D_FF // tf),
        in_specs=[pl.BlockSpec((tm, D_MODEL), lambda i, j: (i, 0), pipeline_mode=pl.Buffered(1)),
                  row_spec, mod_spec, mod_spec,
                  pl.BlockSpec((None, D_MODEL, tf), lambda i, j: (layer, 0, j)),
                  pl.BlockSpec((None, tf, D_MODEL), lambda i, j: (layer, j, 0)),
                  row_spec, mod_spec],
        out_specs=pl.BlockSpec((tm, D_MODEL), lambda i, j: (i, 0)),
        out_shape=jax.ShapeDtypeStruct((m, D_MODEL), F32),
        scratch_shapes=[pltpu.VMEM((tm, D_MODEL), BF16)],
        name="channel_mlp",
        compiler_params=_cparams(("arbitrary", "arbitrary")),
    )(x, g_in.reshape(1, D_MODEL), scale, shift, w1, w2, g_out.reshape(1, D_MODEL), gate)


def _rope_tables(l):
    pos = jnp.arange(l, dtype=jnp.int32)
    quarter = HEAD_DIM // 4
    freqs = ROPE_THETA ** (-jnp.arange(quarter, dtype=F32) / quarter)
    ang_r = (pos // GRID_W).astype(F32)[:, None] * freqs[None, :]
    ang_c = (pos % GRID_W).astype(F32)[:, None] * freqs[None, :]
    cos_t = jnp.concatenate([jnp.cos(ang_r), jnp.cos(ang_r), jnp.cos(ang_c), jnp.cos(ang_c)], axis=-1)
    sin_t = jnp.concatenate([-jnp.sin(ang_r), jnp.sin(ang_r), -jnp.sin(ang_c), jnp.sin(ang_c)], axis=-1)
    return cos_t, sin_t


def _decay_rates():
    min_decay = math.log(DECAY_TARGET) / SLOW_DECAY_PCT
    max_decay = math.log(DECAY_TARGET) / FAST_DECAY_PCT
    deltas = jnp.tile(jnp.linspace(min_decay, max_decay, HYENA_WIDTH, dtype=F32), 2)
    return jnp.abs(deltas)[None, :]


def kernel(x, c, ctx, c_ctx, w_mod, b_mod, norm_g, w_in, attn_sink, hy_conv_w, hy_conv_b,
           hy_fw1, hy_fb1, hy_ff1, hy_fw2, hy_fb2, hy_ff2, hy_fw3, hy_fb3, hy_ff3, hy_fw4,
           hy_bias, w_attn_out, w_hyena_out, w_out, w_ff1, w_ff2):
    b, l, d = x.shape
    cl = ctx.shape[1]
    assert d == D_MODEL and l % 1024 == 0 and cl % 256 == 0 and b + 1 <= MOD_ROWS

    cos_t, sin_t = _rope_tables(l)
    decay = _decay_rates()
    tables_lat = _fft_tables(l, FFT_N2)
    tables_ctx = _fft_tables(cl, 1)
    z_lat = _filter_features(l)
    z_ctx = _filter_features(cl)

    c_rows = jnp.concatenate([c, c_ctx[None, :], jnp.zeros((MOD_ROWS - b - 1, d), F32)], axis=0)
    x_lat = x.reshape(b * l, d)
    x_ctx = ctx.reshape(b * cl, d)
    tm_ctx = b * cl
    w1b = w_ff1.astype(BF16)
    w2b = w_ff2.astype(BF16)

    for layer in range(DEPTH):
        last = layer == DEPTH - 1
        mod = _modulation(c_rows, w_mod, layer, b_mod[layer])
        mod_lat = [mod[:b, k * d:(k + 1) * d].reshape(b, 1, d) for k in range(N_MOD)]
        mod_ctx = [mod[b:b + 1, k * d:(k + 1) * d].reshape(1, 1, d) for k in range(N_MOD)]
        sh1, sc1, g1, sh2, sc2, g2 = mod_lat
        csh1, csc1, cg1, csh2, csc2, cg2 = mod_ctx
        g = norm_g[layer]
        fparams = (hy_fw1[layer], hy_fb1[layer], hy_ff1[layer], hy_fw2[layer], hy_fb2[layer],
                   hy_ff2[layer], hy_fw3[layer], hy_fb3[layer], hy_ff3[layer])
        bias_d = hy_bias[layer].reshape(1, HYENA_WIDTH)
        conv_b = hy_conv_b[layer].reshape(1, 3 * HYENA_WIDTH)

        p_lat = _normproj(x_lat, g[0], sc1, sh1, w_in, layer, 0, IN_WIDTH, 2048, 512)
        if last:
            kv_ctx = _normproj(x_ctx, g[0], csc1, csh1, w_in, layer, Q_END, V_END - Q_END,
                               tm_ctx, 512)
            kx, vx = kv_ctx[:, :KV_WIDTH], kv_ctx[:, KV_WIDTH:]
        else:
            p_ctx = _normproj(x_ctx, g[0], csc1, csh1, w_in, layer, 0, IN_WIDTH, tm_ctx, 512)
            kx, vx = p_ctx[:, Q_END:K_END], p_ctx[:, K_END:V_END]
        kx = kx.reshape(b, cl, KV_WIDTH)
        vx = vx.reshape(b, cl, KV_WIDTH)

        p3 = p_lat.reshape(b, l, IN_WIDTH)
        attn = _window_attention(p3, kx, vx, attn_sink[layer], cos_t, sin_t)
        kf = _filter_spectrum(_filter_mlp(z_lat, *fparams), hy_fw4[layer], decay, tables_lat)
        hy = _hyena(p3, V_END, hy_conv_w[layer], conv_b, bias_d, kf, tables_lat)
        mix = _merge(attn.reshape(b * l, ATTN_WIDTH), hy.reshape(b * l, HYENA_WIDTH), p_lat,
                     w_attn_out, w_hyena_out, layer, 1024, 512)
        x_lat = _outproj(mix, w_out, layer, x_lat, g[1], g1, 1024, 512)

        if not last:
            pc3 = p_ctx.reshape(b, cl, IN_WIDTH)
            attn_c = _context_attention(pc3, attn_sink[layer])
            kf_c = _filter_spectrum(_filter_mlp(z_ctx, *fparams), hy_fw4[layer], decay, tables_ctx)
            hy_c = _hyena(pc3, V_END, hy_conv_w[layer], conv_b, bias_d, kf_c, tables_ctx)
            mix_c = _merge(attn_c.reshape(b * cl, ATTN_WIDTH), hy_c.reshape(b * cl, HYENA_WIDTH),
                           p_ctx, w_attn_out, w_hyena_out, layer, tm_ctx, 512)
            x_ctx = _outproj(mix_c, w_out, layer, x_ctx, g[1], cg1, tm_ctx, 512)
            x_ctx = _mlp(x_ctx, g[2], csc2, csh2, w1b, w2b, layer, g[3], cg2, tm_ctx, 1024)

        x_lat = _mlp(x_lat, g[2], sc2, sh2, w1b, w2b, layer, g[3], g2, 1024, 512)
    return x_lat.reshape(b, l, d)
```

```python
import functools
import math

import jax
import jax.numpy as jnp
import numpy as np
from jax import lax
from jax.experimental import pallas as pl
from jax.experimental.pallas import tpu as pltpu

F32 = jnp.float32
BF16 = jnp.bfloat16

D_MODEL = 2048
DEPTH = 2
GRID_W = 64
HEAD_DIM = 128
N_Q_HEADS = 8
N_KV_HEADS = 2
Q_GROUP = N_Q_HEADS // N_KV_HEADS
ATTN_WIDTH = N_Q_HEADS * HEAD_DIM
KV_WIDTH = N_KV_HEADS * HEAD_DIM
BLOCK = 128
ROPE_THETA = 10000.0
HYENA_WIDTH = 1024
FILTER_EMB = 33
FILTER_HIDDEN = 64
DECAY_TARGET = 1e-2
FAST_DECAY_PCT = 0.3
SLOW_DECAY_PCT = 1.5
D_FF = 4 * D_MODEL
EPS = 1e-6
N_MOD = 6
NEG_INF = -1e30
Q_END = ATTN_WIDTH
K_END = Q_END + KV_WIDTH
V_END = K_END + KV_WIDTH
HY_END = V_END + 3 * HYENA_WIDTH
GA_END = HY_END + D_MODEL
GH_END = GA_END + D_MODEL
IN_WIDTH = GH_END

LANES = 128
VMEM_LIMIT = 56 * 1024 * 1024

FFT_N2 = 16
FFT_KG = 16
PROLOGUE_ROWS = 256
MLP_COLS = 512
HY_CB = 128
MOD_ROWS = 8


def _cparams(sem):
    return pltpu.CompilerParams(dimension_semantics=sem, vmem_limit_bytes=VMEM_LIMIT)


def _single(block_shape, index_map):
    return pl.BlockSpec(block_shape, index_map, pipeline_mode=pl.Buffered(1))


def _rms(x):
    return x * lax.rsqrt(jnp.mean(x * x, axis=-1, keepdims=True) + EPS)


def _row_chunks(nrows, fn):
    def chunk(r, carry):
        fn(pl.ds(pl.multiple_of(r * PROLOGUE_ROWS, PROLOGUE_ROWS), PROLOGUE_ROWS))
        return carry

    lax.fori_loop(0, nrows // PROLOGUE_ROWS, chunk, 0)


def _mod_kernel(c_ref, w_ref, b_ref, o_ref):
    c = c_ref[...]
    s = c * jax.nn.sigmoid(c)
    o_ref[...] = jnp.dot(s.astype(BF16), w_ref[...].astype(BF16),
                         preferred_element_type=F32) + b_ref[...]


def _modulation(c_rows, w, layer, b):
    n = w.shape[2]
    tn = 1024
    return pl.pallas_call(
        _mod_kernel,
        grid=(n // tn,),
        in_specs=[pl.BlockSpec((MOD_ROWS, D_MODEL), lambda j: (0, 0)),
                  pl.BlockSpec((None, D_MODEL, tn), lambda j: (layer, 0, j)),
                  pl.BlockSpec((1, tn), lambda j: (0, j))],
        out_specs=pl.BlockSpec((MOD_ROWS, tn), lambda j: (0, j)),
        out_shape=jax.ShapeDtypeStruct((MOD_ROWS, n), F32),
        name="modulation",
        compiler_params=_cparams(("arbitrary",)),
    )(c_rows, w, b.reshape(1, n))


def _normproj_kernel(x_ref, g_ref, sc_ref, sh_ref, w_ref, o_ref, h_ref):
    @pl.when(pl.program_id(1) == 0)
    def _():
        g, sc1, sh = g_ref[...], 1.0 + sc_ref[...], sh_ref[...]

        def prologue(rows):
            h_ref[rows, :] = (_rms(x_ref[rows, :]) * g * sc1 + sh).astype(BF16)

        _row_chunks(x_ref.shape[0], prologue)

    o_ref[...] = jnp.dot(h_ref[...], w_ref[...].astype(BF16), preferred_element_type=F32)


def _normproj(x, g, scale, shift, w, layer, col0, n, tm, tn):
    m = x.shape[0]
    j0 = col0 // tn
    nmod = scale.shape[0]
    blocks_per_mod = m // nmod // tm
    mod_spec = pl.BlockSpec((None, 1, D_MODEL), lambda i, j: (i // blocks_per_mod, 0, 0))
    return pl.pallas_call(
        _normproj_kernel,
        grid=(m // tm, n // tn),
        in_specs=[pl.BlockSpec((tm, D_MODEL), lambda i, j: (i, 0), pipeline_mode=pl.Buffered(1)),
                  pl.BlockSpec((1, D_MODEL), lambda i, j: (0, 0)),
                  mod_spec, mod_spec,
                  pl.BlockSpec((None, D_MODEL, tn), lambda i, j: (layer, 0, j0 + j))],
        out_specs=pl.BlockSpec((tm, tn), lambda i, j: (i, j)),
        out_shape=jax.ShapeDtypeStruct((m, n), F32),
        scratch_shapes=[pltpu.VMEM((tm, D_MODEL), BF16)],
        name="normproj",
        compiler_params=_cparams(("arbitrary", "arbitrary")),
    )(x, g.reshape(1, D_MODEL), scale, shift, w)


def _inproj_kernel(x_ref, g_ref, sc_ref, sh_ref, w_ref, qkv_ref, hy_ref, gate_ref, h_ref, *,
                   qkv_tiles, hy_tiles):
    j = pl.program_id(1)

    @pl.when(j == 0)
    def _():
        g, sc1, sh = g_ref[...], 1.0 + sc_ref[...], sh_ref[...]

        def prologue(rows):
            h_ref[rows, :] = (_rms(x_ref[rows, :]) * g * sc1 + sh).astype(BF16)

        _row_chunks(x_ref.shape[0], prologue)

    res = jnp.dot(h_ref[...], w_ref[...].astype(BF16), preferred_element_type=F32)

    @pl.when(j < qkv_tiles)
    def _():
        qkv_ref[...] = res

    @pl.when((j >= qkv_tiles) & (j < qkv_tiles + hy_tiles))
    def _():
        for c in range(hy_ref.shape[0]):
            hy_ref[c] = res[:, c * HY_CB:(c + 1) * HY_CB]

    @pl.when(j >= qkv_tiles + hy_tiles)
    def _():
        gate_ref[...] = jax.nn.sigmoid(res).astype(BF16)


def _inproj(x, g, scale, shift, w, layer, tm, tn):
    m = x.shape[0]
    nmod = scale.shape[0]
    blocks_per_mod = m // nmod // tm
    qkv_tiles = V_END // tn
    hy_tiles = (HY_END - V_END) // tn
    gate_tiles = (GH_END - HY_END) // tn
    slabs = tn // HY_CB
    mod_spec = pl.BlockSpec((None, 1, D_MODEL), lambda i, j: (i // blocks_per_mod, 0, 0))
    return pl.pallas_call(
        functools.partial(_inproj_kernel, qkv_tiles=qkv_tiles, hy_tiles=hy_tiles),
        grid=(m // tm, IN_WIDTH // tn),
        in_specs=[pl.BlockSpec((tm, D_MODEL), lambda i, j: (i, 0), pipeline_mode=pl.Buffered(1)),
                  pl.BlockSpec((1, D_MODEL), lambda i, j: (0, 0)),
                  mod_spec, mod_spec,
                  pl.BlockSpec((None, D_MODEL, tn), lambda i, j: (layer, 0, j))],
        out_specs=[
            pl.BlockSpec((tm, tn), lambda i, j: (i, jnp.minimum(j, qkv_tiles - 1))),
            pl.BlockSpec((slabs, tm, HY_CB),
                         lambda i, j: (jnp.clip(j - qkv_tiles, 0, hy_tiles - 1), i, 0)),
            pl.BlockSpec((tm, tn),
                         lambda i, j: (i, jnp.clip(j - qkv_tiles - hy_tiles, 0, gate_tiles - 1)))],
        out_shape=[jax.ShapeDtypeStruct((m, V_END), F32),
                   jax.ShapeDtypeStruct((3 * HYENA_WIDTH // HY_CB, m, HY_CB), F32),
                   jax.ShapeDtypeStruct((m, 2 * D_MODEL), BF16)],
        scratch_shapes=[pltpu.VMEM((tm, D_MODEL), BF16)],
        name="in_projection",
        compiler_params=_cparams(("arbitrary", "arbitrary")),
    )(x, g.reshape(1, D_MODEL), scale, shift, w)


def _rope(x, cos, sin_signed, first_half):
    rot = jnp.where(first_half, pltpu.roll(x, HEAD_DIM - 32, axis=1), pltpu.roll(x, 32, axis=1))
    return x * cos + rot * sin_signed


def _softmax_pv(s, sink, v):
    m = jnp.maximum(jnp.max(s, axis=-1, keepdims=True), sink)
    e = jnp.exp(s - m)
    denom = jnp.sum(e, axis=-1, keepdims=True) + jnp.exp(sink - m)
    o = jnp.dot(e.astype(BF16), v, preferred_element_type=F32)
    return o / denom


def _band_bias(nctx):
    qi = (jnp.arange(Q_GROUP * BLOCK, dtype=jnp.int32) % BLOCK)[None, :, None]
    kj = jnp.arange(3 * BLOCK + nctx, dtype=jnp.int32)[None, None, :]
    variant = jnp.arange(3, dtype=jnp.int32)[:, None, None]
    in_prev = kj < BLOCK
    in_next = (kj >= 2 * BLOCK) & (kj < 3 * BLOCK)
    valid = jnp.where(in_prev, (kj >= qi) & (variant != 0),
                      jnp.where(in_next, (kj - 2 * BLOCK <= qi) & (variant != 2), True))
    return jnp.where(valid, 0.0, NEG_INF).astype(F32)


def _win_attn_kernel(sink_ref, q_ref, kp_ref, kc_ref, kn_ref, vp_ref, vc_ref, vn_ref,
                     kx_ref, vx_ref, cos_ref, sin_ref, bias_ref, o_ref, *, nb):
    i = pl.program_id(1)
    scale = HEAD_DIM ** -0.5
    lane = lax.broadcasted_iota(jnp.int32, (BLOCK, HEAD_DIM), 1)
    first_half = (lane % 64) < 32

    def table(ref, blk):
        return ref[pl.ds(pl.multiple_of(blk * BLOCK, BLOCK), BLOCK), :]

    ip = jnp.maximum(i - 1, 0)
    inx = jnp.minimum(i + 1, nb - 1)
    cos_c, sin_c = table(cos_ref, i), table(sin_ref, i)
    cos_p, sin_p = table(cos_ref, ip), table(sin_ref, ip)
    cos_n, sin_n = table(cos_ref, inx), table(sin_ref, inx)

    rows = Q_GROUP * BLOCK
    head_in_group = lax.broadcasted_iota(jnp.int32, (rows, 1), 0) // BLOCK
    bias = bias_ref[jnp.where(i == 0, 0, jnp.where(i == nb - 1, 2, 1))]

    for h in range(N_KV_HEADS):
        hs = slice(h * HEAD_DIM, (h + 1) * HEAD_DIM)
        k = jnp.concatenate([
            _rope(kp_ref[:, hs], cos_p, sin_p, first_half),
            _rope(kc_ref[:, hs], cos_c, sin_c, first_half),
            _rope(kn_ref[:, hs], cos_n, sin_n, first_half),
            kx_ref[:, hs]], axis=0).astype(BF16)
        v = jnp.concatenate([vp_ref[:, hs], vc_ref[:, hs], vn_ref[:, hs], vx_ref[:, hs]],
                            axis=0).astype(BF16)
        heads = [h * Q_GROUP + g for g in range(Q_GROUP)]
        q = jnp.concatenate(
            [_rope(q_ref[:, hd * HEAD_DIM:(hd + 1) * HEAD_DIM], cos_c, sin_c, first_half)
             for hd in heads], axis=0).astype(BF16)
        sink = jnp.zeros((rows, 1), F32)
        for g, hd in enumerate(heads):
            sink = jnp.where(head_in_group == g, sink_ref[hd], sink)
        s = lax.dot_general(q, k, (((1,), (1,)), ((), ())), preferred_element_type=F32) * scale
        o = _softmax_pv(s + bias, sink, v).astype(o_ref.dtype)
        for g, hd in enumerate(heads):
            o_ref[:, hd * HEAD_DIM:(hd + 1) * HEAD_DIM] = o[g * BLOCK:(g + 1) * BLOCK]


def _window_attention(p, kx, vx, sink, cos_t, sin_t):
    b, l, _ = p.shape
    c = kx.shape[1]
    nb = l // BLOCK
    kcol = Q_END // KV_WIDTH
    vcol = K_END // KV_WIDTH

    def kv_spec(col, shift):
        return pl.BlockSpec((None, BLOCK, KV_WIDTH),
                            lambda bi, i: (bi, jnp.clip(i + shift, 0, nb - 1), col))

    assert nb >= 2
    bias = _band_bias(c)
    ctx_spec = pl.BlockSpec((None, c, KV_WIDTH), lambda bi, i: (bi, 0, 0))
    tab_spec = _single((l, HEAD_DIM), lambda bi, i: (0, 0))
    return pl.pallas_call(
        functools.partial(_win_attn_kernel, nb=nb),
        grid=(b, nb),
        in_specs=[pl.BlockSpec(memory_space=pltpu.SMEM),
                  pl.BlockSpec((None, BLOCK, ATTN_WIDTH), lambda bi, i: (bi, i, 0)),
                  kv_spec(kcol, -1), kv_spec(kcol, 0), kv_spec(kcol, 1),
                  kv_spec(vcol, -1), kv_spec(vcol, 0), kv_spec(vcol, 1),
                  ctx_spec, ctx_spec, tab_spec, tab_spec,
                  _single(bias.shape, lambda bi, i: (0, 0, 0))],
        out_specs=pl.BlockSpec((None, BLOCK, ATTN_WIDTH), lambda bi, i: (bi, i, 0)),
        out_shape=jax.ShapeDtypeStruct((b, l, ATTN_WIDTH), BF16),
        name="window_attention",
        compiler_params=_cparams(("arbitrary", "arbitrary")),
    )(sink, p, p, p, p, p, p, p, kx, vx, cos_t, sin_t, bias)


def _ctx_attn_kernel(sink_ref, q_ref, k_ref, v_ref, o_ref):
    scale = HEAD_DIM ** -0.5
    for h in range(N_KV_HEADS):
        hs = slice(h * HEAD_DIM, (h + 1) * HEAD_DIM)
        k = k_ref[:, hs].astype(BF16)
        v = v_ref[:, hs].astype(BF16)
        for g in range(Q_GROUP):
            head = h * Q_GROUP + g
            cs = slice(head * HEAD_DIM, (head + 1) * HEAD_DIM)
            q = q_ref[:, cs].astype(BF16)
            s = lax.dot_general(q, k, (((1,), (1,)), ((), ())), preferred_element_type=F32) * scale
            o_ref[:, cs] = _softmax_pv(s, sink_ref[head], v).astype(o_ref.dtype)


def _context_attention(p, sink):
    b, c, _ = p.shape
    return pl.pallas_call(
        _ctx_attn_kernel,
        grid=(b,),
        in_specs=[pl.BlockSpec(memory_space=pltpu.SMEM),
                  pl.BlockSpec((None, c, ATTN_WIDTH), lambda bi: (bi, 0, 0)),
                  pl.BlockSpec((None, c, KV_WIDTH), lambda bi: (bi, 0, Q_END // KV_WIDTH)),
                  pl.BlockSpec((None, c, KV_WIDTH), lambda bi: (bi, 0, K_END // KV_WIDTH))],
        out_specs=pl.BlockSpec((None, c, ATTN_WIDTH), lambda bi: (bi, 0, 0)),
        out_shape=jax.ShapeDtypeStruct((b, c, ATTN_WIDTH), BF16),
        name="context_attention",
        compiler_params=_cparams(("arbitrary",)),
    )(sink, p, p, p)


def _fft_tables(l, n2):
    n = 2 * l
    n1 = n // n2
    k1 = np.arange(n1, dtype=np.int64)[None, :, None]
    t = (n2 * np.arange(n1 // 2, dtype=np.int64)[None, None, :]
         + np.arange(n2, dtype=np.int64)[:, None, None])
    ang = ((k1 * t) % n).astype(np.float64) * (2.0 * math.pi / n)
    fwd1 = np.concatenate([np.cos(ang), -np.sin(ang)], axis=1)
    inv1 = np.swapaxes(fwd1, 1, 2) * (1.0 / n)
    as_operand = lambda a: jnp.asarray(np.ascontiguousarray(a, dtype=np.float32)).astype(BF16)
    return as_operand(fwd1), as_operand(inv1)


def _filter_features(l):
    bands = (FILTER_EMB - 1) // 2
    t = jnp.linspace(0.0, 1.0, l, dtype=F32)[:, None]
    w = 2 * math.pi * jnp.arange(l, dtype=F32)[:, None] / l
    f = jnp.linspace(1e-4, bands - 1, bands, dtype=F32)[None, :]
    z = jnp.concatenate([t, jnp.cos(f * w), -jnp.sin(f * w)], axis=-1)
    return jnp.pad(z, ((0, 0), (0, FILTER_HIDDEN - FILTER_EMB)))


def _filter_mlp_kernel(z_ref, w1_ref, b1_ref, f1_ref, w2_ref, b2_ref, f2_ref,
                       w3_ref, b3_ref, f3_ref, o_ref):
    h = jnp.sin(f1_ref[...] * (jnp.dot(z_ref[...], w1_ref[...], preferred_element_type=F32)
                               + b1_ref[...]))
    h = jnp.sin(f2_ref[...] * (jnp.dot(h, w2_ref[...], preferred_element_type=F32) + b2_ref[...]))
    o_ref[...] = jnp.sin(f3_ref[...] * (jnp.dot(h, w3_ref[...], preferred_element_type=F32)
                                        + b3_ref[...]))


def _filter_mlp(z, fw1, fb1, ff1, fw2, fb2, ff2, fw3, fb3, ff3):
    l = z.shape[0]
    row = lambda a: a.reshape(1, FILTER_HIDDEN)
    w1 = jnp.pad(fw1, ((0, FILTER_HIDDEN - FILTER_EMB), (0, 0)))
    return pl.pallas_call(
        _filter_mlp_kernel,
        out_shape=jax.ShapeDtypeStruct((l, FILTER_HIDDEN), F32),
        name="filter_mlp",
        compiler_params=pltpu.CompilerParams(vmem_limit_bytes=VMEM_LIMIT),
    )(z, w1, row(fb1), row(ff1), fw2, row(fb2), row(ff2), fw3, row(fb3), row(ff3))


def _cmul_root16(z, p, inverse):
    zr, zi = z
    p = p % 16
    if inverse:
        p = (16 - p) % 16
    if p == 0:
        return zr, zi
    if p == 4:
        return zi, -zr
    if p == 8:
        return -zr, -zi
    if p == 12:
        return -zi, zr
    c = math.cos(2.0 * math.pi * p / 16)
    s = -math.sin(2.0 * math.pi * p / 16)
    return zr * c - zi * s, zr * s + zi * c


def _dft4(z, inverse):
    (ar, ai), (br, bi), (cr, ci), (dr, di) = z
    t0r, t0i = ar + cr, ai + ci
    t1r, t1i = ar - cr, ai - ci
    t2r, t2i = br + dr, bi + di
    t3r, t3i = br - dr, bi - di
    y0 = (t0r + t2r, t0i + t2i)
    y2 = (t0r - t2r, t0i - t2i)
    minus_i_t3 = (t1r + t3i, t1i - t3r)
    plus_i_t3 = (t1r - t3i, t1i + t3r)
    return [y0, plus_i_t3, y2, minus_i_t3] if inverse else [y0, minus_i_t3, y2, plus_i_t3]


def _dft_digit(z, inverse):
    if len(z) == 1:
        return z
    assert len(z) == 16
    t = [_dft4([z[4 * a + b] for a in range(4)], inverse) for b in range(4)]
    out = [None] * 16
    for c in range(4):
        y = _dft4([_cmul_root16(t[b][c], b * c, inverse) for b in range(4)], inverse)
        for d in range(4):
            out[c + 4 * d] = y[d]
    return out


def _fft_stage1(src_ref, f1_ref, a_ref):
    n2cnt = f1_ref.shape[0]
    groups = a_ref.shape[0]
    n1 = groups * FFT_KG
    half = src_ref.shape[0] // n2cnt

    def body(n2, carry):
        rows = pl.ds(n2, half, stride=n2cnt) if n2cnt > 1 else pl.ds(0, half)
        res = jnp.dot(f1_ref[n2], src_ref[rows, :].astype(BF16), preferred_element_type=F32)
        for g in range(groups):
            a_ref[g, 0, n2] = res[g * FFT_KG:(g + 1) * FFT_KG]
            a_ref[g, 1, n2] = res[n1 + g * FFT_KG:n1 + (g + 1) * FFT_KG]
        return carry

    if n2cnt == 1:
        body(0, 0)
    else:
        lax.fori_loop(0, n2cnt, body, 0, unroll=2)


def _load_digits(ref, g, rows):
    return [(ref[g, 0, d, rows, :], ref[g, 1, d, rows, :]) for d in range(ref.shape[2])]


def _filter_spec_kernel(h3_ref, wf_ref, wb_ref, df_ref, db_ref, f1_ref, kf_ref,
                        hf_ref, hb_ref, a_ref, *, l):
    groups = a_ref.shape[0]
    row = lax.broadcasted_iota(jnp.int32, (l, 1), 0)
    t = row.astype(F32) * (1.0 / (l - 1))
    h3 = h3_ref[...]
    hf = jnp.dot(h3, wf_ref[...], preferred_element_type=F32) * jnp.exp(-t * df_ref[...])
    hb = jnp.dot(h3, wb_ref[...], preferred_element_type=F32) * jnp.exp(-t * db_ref[...])
    hb = jnp.where(row > 0, hb, 0.0)
    norm = jnp.sum(jnp.abs(hf), axis=0, keepdims=True) + jnp.sum(jnp.abs(hb), axis=0, keepdims=True)
    hf_ref[...] = hf / norm
    hb_ref[...] = hb / norm

    def forward(g, carry):
        for r in range(0, FFT_KG, 8):
            rows = pl.ds(r, 8)
            for k2, (xr, xi) in enumerate(_dft_digit(_load_digits(a_ref, g, rows), False)):
                kf_ref[g, 0, k2, rows, :] = xr
                kf_ref[g, 1, k2, rows, :] = xi
        return carry

    def backward(g, carry):
        for r in range(0, FFT_KG, 8):
            rows = pl.ds(r, 8)
            for k2, (xr, xi) in enumerate(_dft_digit(_load_digits(a_ref, g, rows), False)):
                kf_ref[g, 0, k2, rows, :] = kf_ref[g, 0, k2, rows, :] + xr
                kf_ref[g, 1, k2, rows, :] = kf_ref[g, 1, k2, rows, :] - xi
        return carry

    _fft_stage1(hf_ref, f1_ref, a_ref)
    lax.fori_loop(0, groups, forward, 0)
    _fft_stage1(hb_ref, f1_ref, a_ref)
    lax.fori_loop(0, groups, backward, 0)


def _filter_spectrum(h3, fw4, decay, tables):
    l = h3.shape[0]
    f1, _ = tables
    n2 = f1.shape[0]
    groups = f1.shape[1] // 2 // FFT_KG
    nblk = HYENA_WIDTH // HY_CB
    spec_shape = (groups, 2, n2, FFT_KG)
    return pl.pallas_call(
        functools.partial(_filter_spec_kernel, l=l),
        grid=(nblk,),
        in_specs=[_single((l, FILTER_HIDDEN), lambda c: (0, 0)),
                  pl.BlockSpec((FILTER_HIDDEN, HY_CB), lambda c: (0, c)),
                  pl.BlockSpec((FILTER_HIDDEN, HY_CB), lambda c: (0, c + nblk)),
                  pl.BlockSpec((1, HY_CB), lambda c: (0, c)),
                  pl.BlockSpec((1, HY_CB), lambda c: (0, c + nblk)),
                  _single(f1.shape, lambda c: (0, 0, 0))],
        out_specs=pl.BlockSpec(spec_shape + (HY_CB,), lambda c: (0, 0, 0, 0, c)),
        out_shape=jax.ShapeDtypeStruct(spec_shape + (HYENA_WIDTH,), F32),
        scratch_shapes=[pltpu.VMEM((l, HY_CB), F32), pltpu.VMEM((l, HY_CB), F32),
                        pltpu.VMEM(spec_shape + (HY_CB,), F32)],
        name="filter_spectrum",
        compiler_params=_cparams(("arbitrary",)),
    )(h3, fw4, fw4, decay, decay, f1)


def _short_conv(src_ref, w_ref, b_ref, l, rows):
    w0, w1, w2, b = w_ref[0:1, :], w_ref[1:2, :], w_ref[2:3, :], b_ref[...]
    r = lax.broadcasted_iota(jnp.int32, (rows, 1), 0)
    for s in range(0, l, rows):
        cur = src_ref[s:s + rows, :]
        before = src_ref[s - 1:s, :] if s > 0 else jnp.zeros((1, cur.shape[1]), F32)
        after = src_ref[s + rows:s + rows + 1, :] if s + rows < l else jnp.zeros((1, cur.shape[1]), F32)
        prev = jnp.where(r == 0, before, pltpu.roll(cur, 1, axis=0))
        nxt = jnp.where(r == rows - 1, after, pltpu.roll(cur, rows - 1, axis=0))
        yield s, prev * w0 + cur * w1 + nxt * w2 + b


def _hyena_kernel(x0_ref, x1_ref, v_ref, w0_ref, w1_ref, wv_ref, b0_ref, b1_ref, bv_ref,
                  bias_ref, kf_ref, f1_ref, g1_ref, o_ref, x0c_ref, vg_ref, a_ref, *, l):
    groups = a_ref.shape[0]
    n2cnt = a_ref.shape[2]
    cb = a_ref.shape[-1]
    n1 = groups * FFT_KG
    half = l // n2cnt
    rows = min(l, 512)

    for s, u in _short_conv(x0_ref, w0_ref, b0_ref, l, rows):
        x0c_ref[s:s + rows, :] = u
    for (s, u1), (_, uv) in zip(_short_conv(x1_ref, w1_ref, b1_ref, l, rows),
                                _short_conv(v_ref, wv_ref, bv_ref, l, rows)):
        vg_ref[s:s + rows, :] = uv * u1

    _fft_stage1(vg_ref, f1_ref, a_ref)

    def spectrum(g, carry):
        for r in range(0, FFT_KG, 8):
            rws = pl.ds(r, 8)
            x = _dft_digit(_load_digits(a_ref, g, rws), False)
            k = _load_digits(kf_ref, g, rws)
            y = [(xr * kr - xi * ki, xr * ki + xi * kr) for (xr, xi), (kr, ki) in zip(x, k)]
            for d, (br, bi) in enumerate(_dft_digit(y, True)):
                a_ref[g, 0, d, rws, :] = br
                a_ref[g, 1, d, rws, :] = bi
        return carry

    lax.fori_loop(0, groups, spectrum, 0)

    def synth(n2, carry):
        b = jnp.concatenate([a_ref[:, 0, n2].reshape(n1, cb), a_ref[:, 1, n2].reshape(n1, cb)],
                            axis=0).astype(BF16)
        y = jnp.dot(g1_ref[n2], b, preferred_element_type=F32)
        idx = pl.ds(n2, half, stride=n2cnt) if n2cnt > 1 else pl.ds(0, half)
        o_ref[idx, :] = (y + vg_ref[idx, :] * bias_ref[...]) * x0c_ref[idx, :]
        return carry

    if n2cnt == 1:
        synth(0, 0)
    else:
        lax.fori_loop(0, n2cnt, synth, 0, unroll=2)


def _hyena(slabs, b, l, conv_w, conv_b, bias_d, kf, tables):
    f1, g1 = tables
    nblk = HYENA_WIDTH // HY_CB
    spec_block = kf.shape[:-1] + (HY_CB,)

    def slab(part):
        return pl.BlockSpec((None, l, HY_CB), lambda c, bi: (part * nblk + c, bi, 0))

    def cw(part):
        return pl.BlockSpec((3, HY_CB), lambda c, bi: (0, part * nblk + c))

    def cbias(part):
        return pl.BlockSpec((1, HY_CB), lambda c, bi: (0, part * nblk + c))

    return pl.pallas_call(
        functools.partial(_hyena_kernel, l=l),
        grid=(nblk, b),
        in_specs=[slab(0), slab(1), slab(2), cw(0), cw(1), cw(2), cbias(0), cbias(1), cbias(2),
                  pl.BlockSpec((1, HY_CB), lambda c, bi: (0, c)),
                  pl.BlockSpec(spec_block, lambda c, bi: (0, 0, 0, 0, c),
                               pipeline_mode=pl.Buffered(1)),
                  _single(f1.shape, lambda c, bi: (0, 0, 0)),
                  _single(g1.shape, lambda c, bi: (0, 0, 0))],
        out_specs=pl.BlockSpec((None, l, HY_CB), lambda c, bi: (bi, 0, c)),
        out_shape=jax.ShapeDtypeStruct((b, l, HYENA_WIDTH), F32),
        scratch_shapes=[pltpu.VMEM((l, HY_CB), F32), pltpu.VMEM((l, HY_CB), F32),
                        pltpu.VMEM(spec_block, F32)],
        name="hyena_conv",
        compiler_params=_cparams(("arbitrary", "arbitrary")),
    )(slabs, slabs, slabs, conv_w, conv_w, conv_w, conv_b, conv_b, conv_b, bias_d, kf, f1, g1)


def _merge_kernel(a_ref, h_ref, ga_ref, gh_ref, wa_ref, wh_ref, o_ref):
    ya = jnp.dot(a_ref[...], wa_ref[...], preferred_element_type=F32)
    yh = jnp.dot(h_ref[...].astype(BF16), wh_ref[...], preferred_element_type=F32)
    o_ref[...] = (ga_ref[...].astype(F32) * ya + gh_ref[...].astype(F32) * yh).astype(o_ref.dtype)


def _merge(attn, hy, gates, w_ao, w_ho, layer, tm, tn):
    m = attn.shape[0]
    ga0 = 0
    gh0 = D_MODEL // tn
    return pl.pallas_call(
        _merge_kernel,
        grid=(m // tm, D_MODEL // tn),
        in_specs=[pl.BlockSpec((tm, ATTN_WIDTH), lambda i, j: (i, 0)),
                  pl.BlockSpec((tm, HYENA_WIDTH), lambda i, j: (i, 0)),
                  pl.BlockSpec((tm, tn), lambda i, j: (i, ga0 + j)),
                  pl.BlockSpec((tm, tn), lambda i, j: (i, gh0 + j)),
                  pl.BlockSpec((None, ATTN_WIDTH, tn), lambda i, j: (layer, 0, j)),
                  pl.BlockSpec((None, HYENA_WIDTH, tn), lambda i, j: (layer, 0, j))],
        out_specs=pl.BlockSpec((tm, tn), lambda i, j: (i, j)),
        out_shape=jax.ShapeDtypeStruct((m, D_MODEL), BF16),
        name="branch_merge",
        compiler_params=_cparams(("arbitrary", "arbitrary")),
    )(attn, hy, gates, gates, w_ao, w_ho)


def _outproj_kernel(mix_ref, w_ref, x_ref, g_ref, gate_ref, o_ref, y_ref, *, tn):
    j = pl.program_id(1)
    nj = y_ref.shape[0]
    y_ref[j] = jnp.dot(mix_ref[...], w_ref[...], preferred_element_type=F32)

    @pl.when(j == nj - 1)
    def _():
        ss = sum(jnp.sum(jnp.square(y_ref[k]), axis=-1, keepdims=True) for k in range(nj))
        r = lax.rsqrt(ss * (1.0 / D_MODEL) + EPS)
        for k in range(nj):
            cs = slice(k * tn, (k + 1) * tn)
            o_ref[:, cs] = x_ref[:, cs] + gate_ref[:, cs] * ((y_ref[k] * r) * g_ref[:, cs])


def _outproj(mix, w_o, layer, x, g, gate, tm, tn):
    m = x.shape[0]
    nmod = gate.shape[0]
    blocks_per_mod = m // nmod // tm
    return pl.pallas_call(
        functools.partial(_outproj_kernel, tn=tn),
        grid=(m // tm, D_MODEL // tn),
        in_specs=[pl.BlockSpec((tm, D_MODEL), lambda i, j: (i, 0)),
                  pl.BlockSpec((None, D_MODEL, tn), lambda i, j: (layer, 0, j)),
                  pl.BlockSpec((tm, D_MODEL), lambda i, j: (i, 0)),
                  pl.BlockSpec((1, D_MODEL), lambda i, j: (0, 0)),
                  pl.BlockSpec((None, 1, D_MODEL), lambda i, j: (i // blocks_per_mod, 0, 0))],
        out_specs=pl.BlockSpec((tm, D_MODEL), lambda i, j: (i, 0)),
        out_shape=jax.ShapeDtypeStruct((m, D_MODEL), F32),
        scratch_shapes=[pltpu.VMEM((D_MODEL // tn, tm, tn), F32)],
        name="out_projection",
        compiler_params=_cparams(("arbitrary", "arbitrary")),
    )(mix, w_o, x, g.reshape(1, D_MODEL), gate)


def _mlp_kernel(x_ref, gin_ref, sc_ref, sh_ref, w1_ref, w2_ref, gout_ref, gate_ref, o_ref, h_ref):
    j = pl.program_id(1)
    tm = x_ref.shape[0]

    @pl.when(j == 0)
    def _():
        gin, sc1, sh = gin_ref[...], 1.0 + sc_ref[...], sh_ref[...]

        def prologue(rows):
            h_ref[rows, :] = (_rms(x_ref[rows, :]) * gin * sc1 + sh).astype(BF16)
            o_ref[rows, :] = jnp.zeros((PROLOGUE_ROWS, D_MODEL), F32)

        _row_chunks(tm, prologue)

    a = jnp.dot(h_ref[...], w1_ref[...], preferred_element_type=F32)
    a = jnp.square(jnp.maximum(a, 0.0)).astype(BF16)
    for n in range(0, D_MODEL, MLP_COLS):
        o_ref[:, n:n + MLP_COLS] += jnp.dot(a, w2_ref[:, n:n + MLP_COLS],
                                            preferred_element_type=F32)

    @pl.when(j == pl.num_programs(1) - 1)
    def _():
        gout, gate = gout_ref[...], gate_ref[...]

        def epilogue(rows):
            o_ref[rows, :] = x_ref[rows, :] + gate * (_rms(o_ref[rows, :]) * gout)

        _row_chunks(tm, epilogue)


def _mlp(x, g_in, scale, shift, w1, w2, layer, g_out, gate, tm, tf):
    m = x.shape[0]
    nmod = gate.shape[0]
    blocks_per_mod = m // nmod // tm
    mod_spec = pl.BlockSpec((None, 1, D_MODEL), lambda i, j: (i // blocks_per_mod, 0, 0))
    row_spec = pl.BlockSpec((1, D_MODEL), lambda i, j: (0, 0))
    return pl.pallas_call(
        _mlp_kernel,
        grid=(m // tm, D_FF // tf),
        in_specs=[pl.BlockSpec((tm, D_MODEL), lambda i, j: (i, 0), pipeline_mode=pl.Buffered(1)),
                  row_spec, mod_spec, mod_spec,
                  pl.BlockSpec((None, D_MODEL, tf), lambda i, j: (layer, 0, j)),
                  pl.BlockSpec((None, tf, D_MODEL), lambda i, j: (layer, j, 0)),
                  row_spec, mod_spec],
        out_specs=pl.BlockSpec((tm, D_MODEL), lambda i, j: (i, 0)),
        out_shape=jax.ShapeDtypeStruct((m, D_MODEL), F32),
        scratch_shapes=[pltpu.VMEM((tm, D_MODEL), BF16)],
        name="channel_mlp",
        compiler_params=_cparams(("arbitrary", "arbitrary")),
    )(x, g_in.reshape(1, D_MODEL), scale, shift, w1, w2, g_out.reshape(1, D_MODEL), gate)


def _rope_tables(l):
    pos = jnp.arange(l, dtype=jnp.int32)
    quarter = HEAD_DIM // 4
    freqs = ROPE_THETA ** (-jnp.arange(quarter, dtype=F32) / quarter)
    ang_r = (pos // GRID_W).astype(F32)[:, None] * freqs[None, :]
    ang_c = (pos % GRID_W).astype(F32)[:, None] * freqs[None, :]
    cos_t = jnp.concatenate([jnp.cos(ang_r), jnp.cos(ang_r), jnp.cos(ang_c), jnp.cos(ang_c)], axis=-1)
    sin_t = jnp.concatenate([-jnp.sin(ang_r), jnp.sin(ang_r), -jnp.sin(ang_c), jnp.sin(ang_c)], axis=-1)
    return cos_t, sin_t


def _decay_rates():
    min_decay = math.log(DECAY_TARGET) / SLOW_DECAY_PCT
    max_decay = math.log(DECAY_TARGET) / FAST_DECAY_PCT
    deltas = jnp.tile(jnp.linspace(min_decay, max_decay, HYENA_WIDTH, dtype=F32), 2)
    return jnp.abs(deltas)[None, :]


def kernel(x, c, ctx, c_ctx, w_mod, b_mod, norm_g, w_in, attn_sink, hy_conv_w, hy_conv_b,
           hy_fw1, hy_fb1, hy_ff1, hy_fw2, hy_fb2, hy_ff2, hy_fw3, hy_fb3, hy_ff3, hy_fw4,
           hy_bias, w_attn_out, w_hyena_out, w_out, w_ff1, w_ff2):
    b, l, d = x.shape
    cl = ctx.shape[1]
    assert d == D_MODEL and l % 1024 == 0 and cl % 256 == 0 and b + 1 <= MOD_ROWS

    cos_t, sin_t = _rope_tables(l)
    decay = _decay_rates()
    tables_lat = _fft_tables(l, FFT_N2)
    tables_ctx = _fft_tables(cl, 1)
    z_lat = _filter_features(l)
    z_ctx = _filter_features(cl)

    c_rows = jnp.concatenate([c, c_ctx[None, :], jnp.zeros((MOD_ROWS - b - 1, d), F32)], axis=0)
    x_lat = x.reshape(b * l, d)
    x_ctx = ctx.reshape(b * cl, d)
    tm_ctx = b * cl
    w1b = w_ff1.astype(BF16)
    w2b = w_ff2.astype(BF16)
    w_aob = w_attn_out.astype(BF16)
    w_hob = w_hyena_out.astype(BF16)
    w_ob = w_out.astype(BF16)

    for layer in range(DEPTH):
        last = layer == DEPTH - 1
        mod = _modulation(c_rows, w_mod, layer, b_mod[layer])
        mod_lat = [mod[:b, k * d:(k + 1) * d].reshape(b, 1, d) for k in range(N_MOD)]
        mod_ctx = [mod[b:b + 1, k * d:(k + 1) * d].reshape(1, 1, d) for k in range(N_MOD)]
        sh1, sc1, g1, sh2, sc2, g2 = mod_lat
        csh1, csc1, cg1, csh2, csc2, cg2 = mod_ctx
        g = norm_g[layer]
        fparams = (hy_fw1[layer], hy_fb1[layer], hy_ff1[layer], hy_fw2[layer], hy_fb2[layer],
                   hy_ff2[layer], hy_fw3[layer], hy_fb3[layer], hy_ff3[layer])
        bias_d = hy_bias[layer].reshape(1, HYENA_WIDTH)
        conv_b = hy_conv_b[layer].reshape(1, 3 * HYENA_WIDTH)

        qkv, slabs, gates = _inproj(x_lat, g[0], sc1, sh1, w_in, layer, 2048, 256)
        if last:
            qkv_c = _normproj(x_ctx, g[0], csc1, csh1, w_in, layer, Q_END, V_END - Q_END,
                              tm_ctx, 512)
            kx, vx = qkv_c[:, :KV_WIDTH], qkv_c[:, KV_WIDTH:]
        else:
            qkv_c, slabs_c, gates_c = _inproj(x_ctx, g[0], csc1, csh1, w_in, layer, tm_ctx, 512)
            kx, vx = qkv_c[:, Q_END:K_END], qkv_c[:, K_END:V_END]
        kx = kx.reshape(b, cl, KV_WIDTH)
        vx = vx.reshape(b, cl, KV_WIDTH)

        attn = _window_attention(qkv.reshape(b, l, V_END), kx, vx, attn_sink[layer], cos_t, sin_t)
        kf = _filter_spectrum(_filter_mlp(z_lat, *fparams), hy_fw4[layer], decay, tables_lat)
        hy = _hyena(slabs, b, l, hy_conv_w[layer], conv_b, bias_d, kf, tables_lat)
        mix = _merge(attn.reshape(b * l, ATTN_WIDTH), hy.reshape(b * l, HYENA_WIDTH), gates,
                     w_aob, w_hob, layer, 1024, 512)
        x_lat = _outproj(mix, w_ob, layer, x_lat, g[1], g1, 512, 512)

        if not last:
            attn_c = _context_attention(qkv_c.reshape(b, cl, V_END), attn_sink[layer])
            kf_c = _filter_spectrum(_filter_mlp(z_ctx, *fparams), hy_fw4[layer], decay, tables_ctx)
            hy_c = _hyena(slabs_c, b, cl, hy_conv_w[layer], conv_b, bias_d, kf_c, tables_ctx)
            mix_c = _merge(attn_c.reshape(b * cl, ATTN_WIDTH), hy_c.reshape(b * cl, HYENA_WIDTH),
                           gates_c, w_aob, w_hob, layer, tm_ctx, 512)
            x_ctx = _outproj(mix_c, w_ob, layer, x_ctx, g[1], cg1, tm_ctx, 512)
            x_ctx = _mlp(x_ctx, g[2], csc2, csh2, w1b, w2b, layer, g[3], cg2, tm_ctx, 1024)

        x_lat = _mlp(x_lat, g[2], sc2, sh2, w1b, w2b, layer, g[3], g2, 1024, 512)
    return x_lat.reshape(b, l, d)
```

```python
import functools
import math

import jax
import jax.numpy as jnp
import numpy as np
from jax import lax
from jax.experimental import pallas as pl
from jax.experimental.pallas import tpu as pltpu

F32 = jnp.float32
BF16 = jnp.bfloat16

D_MODEL = 2048
DEPTH = 2
GRID_W = 64
HEAD_DIM = 128
N_Q_HEADS = 8
N_KV_HEADS = 2
Q_GROUP = N_Q_HEADS // N_KV_HEADS
ATTN_WIDTH = N_Q_HEADS * HEAD_DIM
KV_WIDTH = N_KV_HEADS * HEAD_DIM
BLOCK = 128
ROPE_THETA = 10000.0
HYENA_WIDTH = 1024
FILTER_EMB = 33
FILTER_HIDDEN = 64
DECAY_TARGET = 1e-2
FAST_DECAY_PCT = 0.3
SLOW_DECAY_PCT = 1.5
D_FF = 4 * D_MODEL
EPS = 1e-6
N_MOD = 6
NEG_INF = -1e30
Q_END = ATTN_WIDTH
K_END = Q_END + KV_WIDTH
V_END = K_END + KV_WIDTH
HY_END = V_END + 3 * HYENA_WIDTH
GA_END = HY_END + D_MODEL
GH_END = GA_END + D_MODEL
IN_WIDTH = GH_END

LANES = 128
VMEM_LIMIT = 56 * 1024 * 1024

FFT_N2 = 16
FFT_KG = 16
PROLOGUE_ROWS = 256
MLP_COLS = 512
HY_CB = 128
MOD_ROWS = 8


def _cparams(sem):
    return pltpu.CompilerParams(dimension_semantics=sem, vmem_limit_bytes=VMEM_LIMIT)


def _single(block_shape, index_map):
    return pl.BlockSpec(block_shape, index_map, pipeline_mode=pl.Buffered(1))


def _rms(x):
    return x * lax.rsqrt(jnp.mean(x * x, axis=-1, keepdims=True) + EPS)


def _row_chunks(nrows, fn):
    def chunk(r, carry):
        fn(pl.ds(pl.multiple_of(r * PROLOGUE_ROWS, PROLOGUE_ROWS), PROLOGUE_ROWS))
        return carry

    lax.fori_loop(0, nrows // PROLOGUE_ROWS, chunk, 0)


def _mod_kernel(c_ref, w_ref, b_ref, o_ref):
    c = c_ref[...]
    s = c * jax.nn.sigmoid(c)
    o_ref[...] = jnp.dot(s.astype(BF16), w_ref[...].astype(BF16),
                         preferred_element_type=F32) + b_ref[...]


def _modulation(c_rows, w, layer, b):
    n = w.shape[2]
    tn = 1024
    return pl.pallas_call(
        _mod_kernel,
        grid=(n // tn,),
        in_specs=[pl.BlockSpec((MOD_ROWS, D_MODEL), lambda j: (0, 0)),
                  pl.BlockSpec((None, D_MODEL, tn), lambda j: (layer, 0, j)),
                  pl.BlockSpec((1, tn), lambda j: (0, j))],
        out_specs=pl.BlockSpec((MOD_ROWS, tn), lambda j: (0, j)),
        out_shape=jax.ShapeDtypeStruct((MOD_ROWS, n), F32),
        name="modulation",
        compiler_params=_cparams(("arbitrary",)),
    )(c_rows, w, b.reshape(1, n))


def _normproj_kernel(x_ref, g_ref, sc_ref, sh_ref, w_ref, o_ref, h_ref):
    @pl.when(pl.program_id(1) == 0)
    def _():
        g, sc1, sh = g_ref[...], 1.0 + sc_ref[...], sh_ref[...]

        def prologue(rows):
            h_ref[rows, :] = (_rms(x_ref[rows, :]) * g * sc1 + sh).astype(BF16)

        _row_chunks(x_ref.shape[0], prologue)

    o_ref[...] = jnp.dot(h_ref[...], w_ref[...].astype(BF16), preferred_element_type=F32)


def _normproj(x, g, scale, shift, w, layer, col0, n, tm, tn):
    m = x.shape[0]
    j0 = col0 // tn
    nmod = scale.shape[0]
    blocks_per_mod = m // nmod // tm
    mod_spec = pl.BlockSpec((None, 1, D_MODEL), lambda i, j: (i // blocks_per_mod, 0, 0))
    return pl.pallas_call(
        _normproj_kernel,
        grid=(m // tm, n // tn),
        in_specs=[pl.BlockSpec((tm, D_MODEL), lambda i, j: (i, 0), pipeline_mode=pl.Buffered(1)),
                  pl.BlockSpec((1, D_MODEL), lambda i, j: (0, 0)),
                  mod_spec, mod_spec,
                  pl.BlockSpec((None, D_MODEL, tn), lambda i, j: (layer, 0, j0 + j))],
        out_specs=pl.BlockSpec((tm, tn), lambda i, j: (i, j)),
        out_shape=jax.ShapeDtypeStruct((m, n), F32),
        scratch_shapes=[pltpu.VMEM((tm, D_MODEL), BF16)],
        name="normproj",
        compiler_params=_cparams(("arbitrary", "arbitrary")),
    )(x, g.reshape(1, D_MODEL), scale, shift, w)


def _rope(x, cos, sin_signed, first_half):
    rot = jnp.where(first_half, pltpu.roll(x, HEAD_DIM - 32, axis=1), pltpu.roll(x, 32, axis=1))
    return x * cos + rot * sin_signed


def _softmax_pv(s, sink, v):
    m = jnp.maximum(jnp.max(s, axis=-1, keepdims=True), sink)
    e = jnp.exp(s - m)
    denom = jnp.sum(e, axis=-1, keepdims=True) + jnp.exp(sink - m)
    o = jnp.dot(e.astype(BF16), v, preferred_element_type=F32)
    return o / denom


def _band_bias(nctx):
    qi = (jnp.arange(Q_GROUP * BLOCK, dtype=jnp.int32) % BLOCK)[None, :, None]
    kj = jnp.arange(3 * BLOCK + nctx, dtype=jnp.int32)[None, None, :]
    variant = jnp.arange(3, dtype=jnp.int32)[:, None, None]
    in_prev = kj < BLOCK
    in_next = (kj >= 2 * BLOCK) & (kj < 3 * BLOCK)
    valid = jnp.where(in_prev, (kj >= qi) & (variant != 0),
                      jnp.where(in_next, (kj - 2 * BLOCK <= qi) & (variant != 2), True))
    return jnp.where(valid, 0.0, NEG_INF).astype(F32)


def _win_attn_kernel(sink_ref, q_ref, kp_ref, kc_ref, kn_ref, vp_ref, vc_ref, vn_ref,
                     kx_ref, vx_ref, cos_ref, sin_ref, bias_ref, o_ref, *, nb):
    i = pl.program_id(1)
    scale = HEAD_DIM ** -0.5
    lane = lax.broadcasted_iota(jnp.int32, (BLOCK, HEAD_DIM), 1)
    first_half = (lane % 64) < 32

    def table(ref, blk):
        return ref[pl.ds(pl.multiple_of(blk * BLOCK, BLOCK), BLOCK), :]

    ip = jnp.maximum(i - 1, 0)
    inx = jnp.minimum(i + 1, nb - 1)
    cos_c, sin_c = table(cos_ref, i), table(sin_ref, i)
    cos_p, sin_p = table(cos_ref, ip), table(sin_ref, ip)
    cos_n, sin_n = table(cos_ref, inx), table(sin_ref, inx)

    rows = Q_GROUP * BLOCK
    head_in_group = lax.broadcasted_iota(jnp.int32, (rows, 1), 0) // BLOCK
    bias = bias_ref[jnp.where(i == 0, 0, jnp.where(i == nb - 1, 2, 1))]

    for h in range(N_KV_HEADS):
        hs = slice(h * HEAD_DIM, (h + 1) * HEAD_DIM)
        k = jnp.concatenate([
            _rope(kp_ref[:, hs], cos_p, sin_p, first_half),
            _rope(kc_ref[:, hs], cos_c, sin_c, first_half),
            _rope(kn_ref[:, hs], cos_n, sin_n, first_half),
            kx_ref[:, hs]], axis=0).astype(BF16)
        v = jnp.concatenate([vp_ref[:, hs], vc_ref[:, hs], vn_ref[:, hs], vx_ref[:, hs]],
                            axis=0).astype(BF16)
        heads = [h * Q_GROUP + g for g in range(Q_GROUP)]
        q = jnp.concatenate(
            [_rope(q_ref[:, hd * HEAD_DIM:(hd + 1) * HEAD_DIM], cos_c, sin_c, first_half)
             for hd in heads], axis=0).astype(BF16)
        sink = jnp.zeros((rows, 1), F32)
        for g, hd in enumerate(heads):
            sink = jnp.where(head_in_group == g, sink_ref[hd], sink)
        s = lax.dot_general(q, k, (((1,), (1,)), ((), ())), preferred_element_type=F32) * scale
        o = _softmax_pv(s + bias, sink, v).astype(o_ref.dtype)
        for g, hd in enumerate(heads):
            o_ref[:, hd * HEAD_DIM:(hd + 1) * HEAD_DIM] = o[g * BLOCK:(g + 1) * BLOCK]


def _window_attention(p, kx, vx, sink, cos_t, sin_t):
    b, l, _ = p.shape
    c = kx.shape[1]
    nb = l // BLOCK
    kcol = Q_END // KV_WIDTH
    vcol = K_END // KV_WIDTH

    def kv_spec(col, shift):
        return pl.BlockSpec((None, BLOCK, KV_WIDTH),
                            lambda bi, i: (bi, jnp.clip(i + shift, 0, nb - 1), col))

    assert nb >= 2
    bias = _band_bias(c)
    ctx_spec = pl.BlockSpec((None, c, KV_WIDTH), lambda bi, i: (bi, 0, 0))
    tab_spec = _single((l, HEAD_DIM), lambda bi, i: (0, 0))
    return pl.pallas_call(
        functools.partial(_win_attn_kernel, nb=nb),
        grid=(b, nb),
        in_specs=[pl.BlockSpec(memory_space=pltpu.SMEM),
                  pl.BlockSpec((None, BLOCK, ATTN_WIDTH), lambda bi, i: (bi, i, 0)),
                  kv_spec(kcol, -1), kv_spec(kcol, 0), kv_spec(kcol, 1),
                  kv_spec(vcol, -1), kv_spec(vcol, 0), kv_spec(vcol, 1),
                  ctx_spec, ctx_spec, tab_spec, tab_spec,
                  _single(bias.shape, lambda bi, i: (0, 0, 0))],
        out_specs=pl.BlockSpec((None, BLOCK, ATTN_WIDTH), lambda bi, i: (bi, i, 0)),
        out_shape=jax.ShapeDtypeStruct((b, l, ATTN_WIDTH), BF16),
        name="window_attention",
        compiler_params=_cparams(("arbitrary", "arbitrary")),
    )(sink, p, p, p, p, p, p, p, kx, vx, cos_t, sin_t, bias)


def _ctx_attn_kernel(sink_ref, q_ref, k_ref, v_ref, o_ref):
    scale = HEAD_DIM ** -0.5
    for h in range(N_KV_HEADS):
        hs = slice(h * HEAD_DIM, (h + 1) * HEAD_DIM)
        k = k_ref[:, hs].astype(BF16)
        v = v_ref[:, hs].astype(BF16)
        for g in range(Q_GROUP):
            head = h * Q_GROUP + g
            cs = slice(head * HEAD_DIM, (head + 1) * HEAD_DIM)
            q = q_ref[:, cs].astype(BF16)
            s = lax.dot_general(q, k, (((1,), (1,)), ((), ())), preferred_element_type=F32) * scale
            o_ref[:, cs] = _softmax_pv(s, sink_ref[head], v).astype(o_ref.dtype)


def _context_attention(p, sink):
    b, c, _ = p.shape
    return pl.pallas_call(
        _ctx_attn_kernel,
        grid=(b,),
        in_specs=[pl.BlockSpec(memory_space=pltpu.SMEM),
                  pl.BlockSpec((None, c, ATTN_WIDTH), lambda bi: (bi, 0, 0)),
                  pl.BlockSpec((None, c, KV_WIDTH), lambda bi: (bi, 0, Q_END // KV_WIDTH)),
                  pl.BlockSpec((None, c, KV_WIDTH), lambda bi: (bi, 0, K_END // KV_WIDTH))],
        out_specs=pl.BlockSpec((None, c, ATTN_WIDTH), lambda bi: (bi, 0, 0)),
        out_shape=jax.ShapeDtypeStruct((b, c, ATTN_WIDTH), BF16),
        name="context_attention",
        compiler_params=_cparams(("arbitrary",)),
    )(sink, p, p, p)


def _fft_tables(l, n2):
    n = 2 * l
    n1 = n // n2
    k1 = np.arange(n1, dtype=np.int64)[None, :, None]
    t = (n2 * np.arange(n1 // 2, dtype=np.int64)[None, None, :]
         + np.arange(n2, dtype=np.int64)[:, None, None])
    ang = ((k1 * t) % n).astype(np.float64) * (2.0 * math.pi / n)
    fwd1 = np.concatenate([np.cos(ang), -np.sin(ang)], axis=1)
    inv1 = np.swapaxes(fwd1, 1, 2) * (1.0 / n)
    as_operand = lambda a: jnp.asarray(np.ascontiguousarray(a, dtype=np.float32)).astype(BF16)
    return as_operand(fwd1), as_operand(inv1)


def _filter_features(l):
    bands = (FILTER_EMB - 1) // 2
    t = jnp.linspace(0.0, 1.0, l, dtype=F32)[:, None]
    w = 2 * math.pi * jnp.arange(l, dtype=F32)[:, None] / l
    f = jnp.linspace(1e-4, bands - 1, bands, dtype=F32)[None, :]
    z = jnp.concatenate([t, jnp.cos(f * w), -jnp.sin(f * w)], axis=-1)
    return jnp.pad(z, ((0, 0), (0, FILTER_HIDDEN - FILTER_EMB)))


def _filter_mlp_kernel(z_ref, w1_ref, b1_ref, f1_ref, w2_ref, b2_ref, f2_ref,
                       w3_ref, b3_ref, f3_ref, o_ref):
    h = jnp.sin(f1_ref[...] * (jnp.dot(z_ref[...], w1_ref[...], preferred_element_type=F32)
                               + b1_ref[...]))
    h = jnp.sin(f2_ref[...] * (jnp.dot(h, w2_ref[...], preferred_element_type=F32) + b2_ref[...]))
    o_ref[...] = jnp.sin(f3_ref[...] * (jnp.dot(h, w3_ref[...], preferred_element_type=F32)
                                        + b3_ref[...]))


def _filter_mlp(z, fw1, fb1, ff1, fw2, fb2, ff2, fw3, fb3, ff3):
    l = z.shape[0]
    row = lambda a: a.reshape(1, FILTER_HIDDEN)
    w1 = jnp.pad(fw1, ((0, FILTER_HIDDEN - FILTER_EMB), (0, 0)))
    return pl.pallas_call(
        _filter_mlp_kernel,
        out_shape=jax.ShapeDtypeStruct((l, FILTER_HIDDEN), F32),
        name="filter_mlp",
        compiler_params=pltpu.CompilerParams(vmem_limit_bytes=VMEM_LIMIT),
    )(z, w1, row(fb1), row(ff1), fw2, row(fb2), row(ff2), fw3, row(fb3), row(ff3))


def _cmul_root16(z, p, inverse):
    zr, zi = z
    p = p % 16
    if inverse:
        p = (16 - p) % 16
    if p == 0:
        return zr, zi
    if p == 4:
        return zi, -zr
    if p == 8:
        return -zr, -zi
    if p == 12:
        return -zi, zr
    c = math.cos(2.0 * math.pi * p / 16)
    s = -math.sin(2.0 * math.pi * p / 16)
    return zr * c - zi * s, zr * s + zi * c


def _dft4(z, inverse):
    (ar, ai), (br, bi), (cr, ci), (dr, di) = z
    t0r, t0i = ar + cr, ai + ci
    t1r, t1i = ar - cr, ai - ci
    t2r, t2i = br + dr, bi + di
    t3r, t3i = br - dr, bi - di
    y0 = (t0r + t2r, t0i + t2i)
    y2 = (t0r - t2r, t0i - t2i)
    minus_i_t3 = (t1r + t3i, t1i - t3r)
    plus_i_t3 = (t1r - t3i, t1i + t3r)
    return [y0, plus_i_t3, y2, minus_i_t3] if inverse else [y0, minus_i_t3, y2, plus_i_t3]


def _dft_digit(z, inverse):
    if len(z) == 1:
        return z
    assert len(z) == 16
    t = [_dft4([z[4 * a + b] for a in range(4)], inverse) for b in range(4)]
    out = [None] * 16
    for c in range(4):
        y = _dft4([_cmul_root16(t[b][c], b * c, inverse) for b in range(4)], inverse)
        for d in range(4):
            out[c + 4 * d] = y[d]
    return out


def _fft_stage1(src_ref, f1_ref, a_ref):
    n2cnt = f1_ref.shape[0]
    groups = a_ref.shape[0]
    n1 = groups * FFT_KG
    half = src_ref.shape[0] // n2cnt

    def body(n2, carry):
        rows = pl.ds(n2, half, stride=n2cnt) if n2cnt > 1 else pl.ds(0, half)
        res = jnp.dot(f1_ref[n2], src_ref[rows, :].astype(BF16), preferred_element_type=F32)
        for g in range(groups):
            a_ref[g, 0, n2] = res[g * FFT_KG:(g + 1) * FFT_KG]
            a_ref[g, 1, n2] = res[n1 + g * FFT_KG:n1 + (g + 1) * FFT_KG]
        return carry

    if n2cnt == 1:
        body(0, 0)
    else:
        lax.fori_loop(0, n2cnt, body, 0, unroll=2)


def _load_digits(ref, g, rows):
    return [(ref[g, 0, d, rows, :], ref[g, 1, d, rows, :]) for d in range(ref.shape[2])]


def _filter_spec_kernel(h3_ref, wf_ref, wb_ref, df_ref, db_ref, f1_ref, kf_ref,
                        hf_ref, hb_ref, a_ref, *, l):
    groups = a_ref.shape[0]
    row = lax.broadcasted_iota(jnp.int32, (l, 1), 0)
    t = row.astype(F32) * (1.0 / (l - 1))
    h3 = h3_ref[...]
    hf = jnp.dot(h3, wf_ref[...], preferred_element_type=F32) * jnp.exp(-t * df_ref[...])
    hb = jnp.dot(h3, wb_ref[...], preferred_element_type=F32) * jnp.exp(-t * db_ref[...])
    hb = jnp.where(row > 0, hb, 0.0)
    norm = jnp.sum(jnp.abs(hf), axis=0, keepdims=True) + jnp.sum(jnp.abs(hb), axis=0, keepdims=True)
    hf_ref[...] = hf / norm
    hb_ref[...] = hb / norm

    def forward(g, carry):
        for r in range(0, FFT_KG, 8):
            rows = pl.ds(r, 8)
            for k2, (xr, xi) in enumerate(_dft_digit(_load_digits(a_ref, g, rows), False)):
                kf_ref[g, 0, k2, rows, :] = xr
                kf_ref[g, 1, k2, rows, :] = xi
        return carry

    def backward(g, carry):
        for r in range(0, FFT_KG, 8):
            rows = pl.ds(r, 8)
            for k2, (xr, xi) in enumerate(_dft_digit(_load_digits(a_ref, g, rows), False)):
                kf_ref[g, 0, k2, rows, :] = kf_ref[g, 0, k2, rows, :] + xr
                kf_ref[g, 1, k2, rows, :] = kf_ref[g, 1, k2, rows, :] - xi
        return carry

    _fft_stage1(hf_ref, f1_ref, a_ref)
    lax.fori_loop(0, groups, forward, 0)
    _fft_stage1(hb_ref, f1_ref, a_ref)
    lax.fori_loop(0, groups, backward, 0)


def _filter_spectrum(h3, fw4, decay, tables):
    l = h3.shape[0]
    f1, _ = tables
    n2 = f1.shape[0]
    groups = f1.shape[1] // 2 // FFT_KG
    nblk = HYENA_WIDTH // HY_CB
    spec_shape = (groups, 2, n2, FFT_KG)
    return pl.pallas_call(
        functools.partial(_filter_spec_kernel, l=l),
        grid=(nblk,),
        in_specs=[_single((l, FILTER_HIDDEN), lambda c: (0, 0)),
                  pl.BlockSpec((FILTER_HIDDEN, HY_CB), lambda c: (0, c)),
                  pl.BlockSpec((FILTER_HIDDEN, HY_CB), lambda c: (0, c + nblk)),
                  pl.BlockSpec((1, HY_CB), lambda c: (0, c)),
                  pl.BlockSpec((1, HY_CB), lambda c: (0, c + nblk)),
                  _single(f1.shape, lambda c: (0, 0, 0))],
        out_specs=pl.BlockSpec(spec_shape + (HY_CB,), lambda c: (0, 0, 0, 0, c)),
        out_shape=jax.ShapeDtypeStruct(spec_shape + (HYENA_WIDTH,), F32),
        scratch_shapes=[pltpu.VMEM((l, HY_CB), F32), pltpu.VMEM((l, HY_CB), F32),
                        pltpu.VMEM(spec_shape + (HY_CB,), F32)],
        name="filter_spectrum",
        compiler_params=_cparams(("arbitrary",)),
    )(h3, fw4, fw4, decay, decay, f1)


def _short_conv(src_ref, w_ref, b_ref, l, rows):
    w0, w1, w2, b = w_ref[0:1, :], w_ref[1:2, :], w_ref[2:3, :], b_ref[...]
    r = lax.broadcasted_iota(jnp.int32, (rows, 1), 0)
    for s in range(0, l, rows):
        cur = src_ref[s:s + rows, :]
        before = src_ref[s - 1:s, :] if s > 0 else jnp.zeros((1, cur.shape[1]), F32)
        after = src_ref[s + rows:s + rows + 1, :] if s + rows < l else jnp.zeros((1, cur.shape[1]), F32)
        prev = jnp.where(r == 0, before, pltpu.roll(cur, 1, axis=0))
        nxt = jnp.where(r == rows - 1, after, pltpu.roll(cur, rows - 1, axis=0))
        yield s, prev * w0 + cur * w1 + nxt * w2 + b


def _hyena_kernel(x0_ref, x1_ref, v_ref, w0_ref, w1_ref, wv_ref, b0_ref, b1_ref, bv_ref,
                  bias_ref, kf_ref, f1_ref, g1_ref, o_ref, x0c_ref, vg_ref, a_ref, *, l):
    groups = a_ref.shape[0]
    n2cnt = a_ref.shape[2]
    cb = a_ref.shape[-1]
    n1 = groups * FFT_KG
    half = l // n2cnt
    rows = min(l, 512)

    for s, u in _short_conv(x0_ref, w0_ref, b0_ref, l, rows):
        x0c_ref[s:s + rows, :] = u
    for (s, u1), (_, uv) in zip(_short_conv(x1_ref, w1_ref, b1_ref, l, rows),
                                _short_conv(v_ref, wv_ref, bv_ref, l, rows)):
        vg_ref[s:s + rows, :] = uv * u1

    _fft_stage1(vg_ref, f1_ref, a_ref)

    def spectrum(g, carry):
        for r in range(0, FFT_KG, 8):
            rws = pl.ds(r, 8)
            x = _dft_digit(_load_digits(a_ref, g, rws), False)
            k = _load_digits(kf_ref, g, rws)
            y = [(xr * kr - xi * ki, xr * ki + xi * kr) for (xr, xi), (kr, ki) in zip(x, k)]
            for d, (br, bi) in enumerate(_dft_digit(y, True)):
                a_ref[g, 0, d, rws, :] = br
                a_ref[g, 1, d, rws, :] = bi
        return carry

    lax.fori_loop(0, groups, spectrum, 0)

    def synth(n2, carry):
        b = jnp.concatenate([a_ref[:, 0, n2].reshape(n1, cb), a_ref[:, 1, n2].reshape(n1, cb)],
                            axis=0).astype(BF16)
        y = jnp.dot(g1_ref[n2], b, preferred_element_type=F32)
        idx = pl.ds(n2, half, stride=n2cnt) if n2cnt > 1 else pl.ds(0, half)
        o_ref[idx, :] = (y + vg_ref[idx, :] * bias_ref[...]) * x0c_ref[idx, :]
        return carry

    if n2cnt == 1:
        synth(0, 0)
    else:
        lax.fori_loop(0, n2cnt, synth, 0, unroll=2)


def _hyena(p, col0, conv_w, conv_b, bias_d, kf, tables):
    b, l, _ = p.shape
    f1, g1 = tables
    nblk = HYENA_WIDTH // HY_CB
    c0 = col0 // HY_CB
    spec_block = kf.shape[:-1] + (HY_CB,)

    def slab(part):
        return pl.BlockSpec((None, l, HY_CB), lambda c, bi: (bi, 0, c0 + part * nblk + c))

    def cw(part):
        return pl.BlockSpec((3, HY_CB), lambda c, bi: (0, part * nblk + c))

    def cbias(part):
        return pl.BlockSpec((1, HY_CB), lambda c, bi: (0, part * nblk + c))

    return pl.pallas_call(
        functools.partial(_hyena_kernel, l=l),
        grid=(nblk, b),
        in_specs=[slab(0), slab(1), slab(2), cw(0), cw(1), cw(2), cbias(0), cbias(1), cbias(2),
                  pl.BlockSpec((1, HY_CB), lambda c, bi: (0, c)),
                  pl.BlockSpec(spec_block, lambda c, bi: (0, 0, 0, 0, c),
                               pipeline_mode=pl.Buffered(1)),
                  _single(f1.shape, lambda c, bi: (0, 0, 0)),
                  _single(g1.shape, lambda c, bi: (0, 0, 0))],
        out_specs=pl.BlockSpec((None, l, HY_CB), lambda c, bi: (bi, 0, c)),
        out_shape=jax.ShapeDtypeStruct((b, l, HYENA_WIDTH), F32),
        scratch_shapes=[pltpu.VMEM((l, HY_CB), F32), pltpu.VMEM((l, HY_CB), F32),
                        pltpu.VMEM(spec_block, F32)],
        name="hyena_conv",
        compiler_params=_cparams(("arbitrary", "arbitrary")),
    )(p, p, p, conv_w, conv_w, conv_w, conv_b, conv_b, conv_b, bias_d, kf, f1, g1)


def _merge_kernel(a_ref, h_ref, ga_ref, gh_ref, wa_ref, wh_ref, o_ref):
    ya = jnp.dot(a_ref[...], wa_ref[...], preferred_element_type=F32)
    yh = jnp.dot(h_ref[...].astype(BF16), wh_ref[...], preferred_element_type=F32)
    o_ref[...] = (jax.nn.sigmoid(ga_ref[...]) * ya
                  + jax.nn.sigmoid(gh_ref[...]) * yh).astype(o_ref.dtype)


def _merge(attn, hy, p, w_ao, w_ho, layer, tm, tn):
    m = attn.shape[0]
    ga0 = HY_END // tn
    gh0 = GA_END // tn
    return pl.pallas_call(
        _merge_kernel,
        grid=(m // tm, D_MODEL // tn),
        in_specs=[pl.BlockSpec((tm, ATTN_WIDTH), lambda i, j: (i, 0)),
                  pl.BlockSpec((tm, HYENA_WIDTH), lambda i, j: (i, 0)),
                  pl.BlockSpec((tm, tn), lambda i, j: (i, ga0 + j)),
                  pl.BlockSpec((tm, tn), lambda i, j: (i, gh0 + j)),
                  pl.BlockSpec((None, ATTN_WIDTH, tn), lambda i, j: (layer, 0, j)),
                  pl.BlockSpec((None, HYENA_WIDTH, tn), lambda i, j: (layer, 0, j))],
        out_specs=pl.BlockSpec((tm, tn), lambda i, j: (i, j)),
        out_shape=jax.ShapeDtypeStruct((m, D_MODEL), BF16),
        name="branch_merge",
        compiler_params=_cparams(("arbitrary", "arbitrary")),
    )(attn, hy, p, p, w_ao, w_ho)


def _outproj_kernel(mix_ref, w_ref, x_ref, g_ref, gate_ref, o_ref, y_ref, *, tn):
    j = pl.program_id(1)
    nj = y_ref.shape[0]
    y_ref[j] = jnp.dot(mix_ref[...], w_ref[...], preferred_element_type=F32)

    @pl.when(j == nj - 1)
    def _():
        ss = sum(jnp.sum(jnp.square(y_ref[k]), axis=-1, keepdims=True) for k in range(nj))
        r = lax.rsqrt(ss * (1.0 / D_MODEL) + EPS)
        for k in range(nj):
            cs = slice(k * tn, (k + 1) * tn)
            o_ref[:, cs] = x_ref[:, cs] + gate_ref[:, cs] * ((y_ref[k] * r) * g_ref[:, cs])


def _outproj(mix, w_o, layer, x, g, gate, tm, tn):
    m = x.shape[0]
    nmod = gate.shape[0]
    blocks_per_mod = m // nmod // tm
    return pl.pallas_call(
        functools.partial(_outproj_kernel, tn=tn),
        grid=(m // tm, D_MODEL // tn),
        in_specs=[pl.BlockSpec((tm, D_MODEL), lambda i, j: (i, 0)),
                  pl.BlockSpec((None, D_MODEL, tn), lambda i, j: (layer, 0, j)),
                  pl.BlockSpec((tm, D_MODEL), lambda i, j: (i, 0)),
                  pl.BlockSpec((1, D_MODEL), lambda i, j: (0, 0)),
                  pl.BlockSpec((None, 1, D_MODEL), lambda i, j: (i // blocks_per_mod, 0, 0))],
        out_specs=pl.BlockSpec((tm, D_MODEL), lambda i, j: (i, 0)),
        out_shape=jax.ShapeDtypeStruct((m, D_MODEL), F32),
        scratch_shapes=[pltpu.VMEM((D_MODEL // tn, tm, tn), F32)],
        name="out_projection",
        compiler_params=_cparams(("arbitrary", "arbitrary")),
    )(mix, w_o, x, g.reshape(1, D_MODEL), gate)


def _mlp_kernel(x_ref, gin_ref, sc_ref, sh_ref, w1_ref, w2_ref, gout_ref, gate_ref, o_ref, h_ref):
    j = pl.program_id(1)
    tm = x_ref.shape[0]

    @pl.when(j == 0)
    def _():
        gin, sc1, sh = gin_ref[...], 1.0 + sc_ref[...], sh_ref[...]

        def prologue(rows):
            h_ref[rows, :] = (_rms(x_ref[rows, :]) * gin * sc1 + sh).astype(BF16)
            o_ref[rows, :] = jnp.zeros((PROLOGUE_ROWS, D_MODEL), F32)

        _row_chunks(tm, prologue)

    a = jnp.dot(h_ref[...], w1_ref[...].astype(BF16), preferred_element_type=F32)
    a = jnp.square(jnp.maximum(a, 0.0)).astype(BF16)
    for n in range(0, D_MODEL, MLP_COLS):
        o_ref[:, n:n + MLP_COLS] += jnp.dot(a, w2_ref[:, n:n + MLP_COLS].astype(BF16),
                                            preferred_element_type=F32)

    @pl.when(j == pl.num_programs(1) - 1)
    def _():
        gout, gate = gout_ref[...], gate_ref[...]

        def epilogue(rows):
            o_ref[rows, :] = x_ref[rows, :] + gate * (_rms(o_ref[rows, :]) * gout)

        _row_chunks(tm, epilogue)


def _mlp(x, g_in, scale, shift, w1, w2, layer, g_out, gate, tm, tf):
    m = x.shape[0]
    nmod = gate.shape[0]
    blocks_per_mod = m // nmod // tm
    mod_spec = pl.BlockSpec((None, 1, D_MODEL), lambda i, j: (i // blocks_per_mod, 0, 0))
    row_spec = pl.BlockSpec((1, D_MODEL), lambda i, j: (0, 0))
    return pl.pallas_call(
        _mlp_kernel,
        grid=(m // tm, D_FF // tf),
        in_specs=[pl.BlockSpec((tm, D_MODEL), lambda i, j: (i, 0), pipeline_mode=pl.Buffered(1)),
                  row_spec, mod_spec, mod_spec,
                  pl.BlockSpec((None, D_MODEL, tf), lambda i, j: (layer, 0, j)),
                  pl.BlockSpec((None, tf, D_MODEL), lambda i, j: (layer, j, 0)),
                  row_spec, mod_spec],
        out_specs=pl.BlockSpec((tm, D_MODEL), lambda i, j: (i, 0)),
        out_shape=jax.ShapeDtypeStruct((m, D_MODEL), F32),
        scratch_shapes=[pltpu.VMEM((tm, D_MODEL), BF16)],
        name="channel_mlp",
        compiler_params=_cparams(("arbitrary", "arbitrary")),
    )(x, g_in.reshape(1, D_MODEL), scale, shift, w1, w2, g_out.reshape(1, D_MODEL), gate)


def _rope_tables(l):
    pos = jnp.arange(l, dtype=jnp.int32)
    quarter = HEAD_DIM // 4
    freqs = ROPE_THETA ** (-jnp.arange(quarter, dtype=F32) / quarter)
    ang_r = (pos // GRID_W).astype(F32)[:, None] * freqs[None, :]
    ang_c = (pos % GRID_W).astype(F32)[:, None] * freqs[None, :]
    cos_t = jnp.concatenate([jnp.cos(ang_r), jnp.cos(ang_r), jnp.cos(ang_c), jnp.cos(ang_c)], axis=-1)
    sin_t = jnp.concatenate([-jnp.sin(ang_r), jnp.sin(ang_r), -jnp.sin(ang_c), jnp.sin(ang_c)], axis=-1)
    return cos_t, sin_t


def _decay_rates():
    min_decay = math.log(DECAY_TARGET) / SLOW_DECAY_PCT
    max_decay = math.log(DECAY_TARGET) / FAST_DECAY_PCT
    deltas = jnp.tile(jnp.linspace(min_decay, max_decay, HYENA_WIDTH, dtype=F32), 2)
    return jnp.abs(deltas)[None, :]


def kernel(x, c, ctx, c_ctx, w_mod, b_mod, norm_g, w_in, attn_sink, hy_conv_w, hy_conv_b,
           hy_fw1, hy_fb1, hy_ff1, hy_fw2, hy_fb2, hy_ff2, hy_fw3, hy_fb3, hy_ff3, hy_fw4,
           hy_bias, w_attn_out, w_hyena_out, w_out, w_ff1, w_ff2):
    b, l, d = x.shape
    cl = ctx.shape[1]
    assert d == D_MODEL and l % 1024 == 0 and cl % 256 == 0 and b + 1 <= MOD_ROWS

    cos_t, sin_t = _rope_tables(l)
    decay = _decay_rates()
    tables_lat = _fft_tables(l, FFT_N2)
    tables_ctx = _fft_tables(cl, 1)
    z_lat = _filter_features(l)
    z_ctx = _filter_features(cl)

    c_rows = jnp.concatenate([c, c_ctx[None, :], jnp.zeros((MOD_ROWS - b - 1, d), F32)], axis=0)
    x_lat = x.reshape(b * l, d)
    x_ctx = ctx.reshape(b * cl, d)
    tm_ctx = b * cl
    w_aob = w_attn_out.astype(BF16)
    w_hob = w_hyena_out.astype(BF16)
    w_ob = w_out.astype(BF16)

    for layer in range(DEPTH):
        last = layer == DEPTH - 1
        mod = _modulation(c_rows, w_mod, layer, b_mod[layer])
        mod_lat = [mod[:b, k * d:(k + 1) * d].reshape(b, 1, d) for k in range(N_MOD)]
        mod_ctx = [mod[b:b + 1, k * d:(k + 1) * d].reshape(1, 1, d) for k in range(N_MOD)]
        sh1, sc1, g1, sh2, sc2, g2 = mod_lat
        csh1, csc1, cg1, csh2, csc2, cg2 = mod_ctx
        g = norm_g[layer]
        fparams = (hy_fw1[layer], hy_fb1[layer], hy_ff1[layer], hy_fw2[layer], hy_fb2[layer],
                   hy_ff2[layer], hy_fw3[layer], hy_fb3[layer], hy_ff3[layer])
        bias_d = hy_bias[layer].reshape(1, HYENA_WIDTH)
        conv_b = hy_conv_b[layer].reshape(1, 3 * HYENA_WIDTH)

        p_lat = _normproj(x_lat, g[0], sc1, sh1, w_in, layer, 0, IN_WIDTH, 2048, 512)
        if last:
            kv_ctx = _normproj(x_ctx, g[0], csc1, csh1, w_in, layer, Q_END, V_END - Q_END,
                               tm_ctx, 512)
            kx, vx = kv_ctx[:, :KV_WIDTH], kv_ctx[:, KV_WIDTH:]
        else:
            p_ctx = _normproj(x_ctx, g[0], csc1, csh1, w_in, layer, 0, IN_WIDTH, tm_ctx, 512)
            kx, vx = p_ctx[:, Q_END:K_END], p_ctx[:, K_END:V_END]
        kx = kx.reshape(b, cl, KV_WIDTH)
        vx = vx.reshape(b, cl, KV_WIDTH)

        p3 = p_lat.reshape(b, l, IN_WIDTH)
        attn = _window_attention(p3, kx, vx, attn_sink[layer], cos_t, sin_t)
        kf = _filter_spectrum(_filter_mlp(z_lat, *fparams), hy_fw4[layer], decay, tables_lat)
        hy = _hyena(p3, V_END, hy_conv_w[layer], conv_b, bias_d, kf, tables_lat)
        mix = _merge(attn.reshape(b * l, ATTN_WIDTH), hy.reshape(b * l, HYENA_WIDTH), p_lat,
                     w_aob, w_hob, layer, 1024, 512)
        x_lat = _outproj(mix, w_ob, layer, x_lat, g[1], g1, 512, 512)

        if not last:
            pc3 = p_ctx.reshape(b, cl, IN_WIDTH)
            attn_c = _context_attention(pc3, attn_sink[layer])
            kf_c = _filter_spectrum(_filter_mlp(z_ctx, *fparams), hy_fw4[layer], decay, tables_ctx)
            hy_c = _hyena(pc3, V_END, hy_conv_w[layer], conv_b, bias_d, kf_c, tables_ctx)
            mix_c = _merge(attn_c.reshape(b * cl, ATTN_WIDTH), hy_c.reshape(b * cl, HYENA_WIDTH),
                           p_ctx, w_aob, w_hob, layer, tm_ctx, 512)
            x_ctx = _outproj(mix_c, w_ob, layer, x_ctx, g[1], cg1, tm_ctx, 512)
            x_ctx = _mlp(x_ctx, g[2], csc2, csh2, w_ff1, w_ff2, layer, g[3], cg2, tm_ctx, 512)

        x_lat = _mlp(x_lat, g[2], sc2, sh2, w_ff1, w_ff2, layer, g[3], g2, 1024, 512)
    return x_lat.reshape(b, l, d)
```

```python
import functools
import math

import jax
import jax.numpy as jnp
import numpy as np
from jax import lax
from jax.experimental import pallas as pl
from jax.experimental.pallas import tpu as pltpu

F32 = jnp.float32
BF16 = jnp.bfloat16

D_MODEL = 2048
DEPTH = 2
GRID_W = 64
HEAD_DIM = 128
N_Q_HEADS = 8
N_KV_HEADS = 2
Q_GROUP = N_Q_HEADS // N_KV_HEADS
ATTN_WIDTH = N_Q_HEADS * HEAD_DIM
KV_WIDTH = N_KV_HEADS * HEAD_DIM
BLOCK = 128
ROPE_THETA = 10000.0
HYENA_WIDTH = 1024
FILTER_EMB = 33
FILTER_HIDDEN = 64
DECAY_TARGET = 1e-2
FAST_DECAY_PCT = 0.3
SLOW_DECAY_PCT = 1.5
D_FF = 4 * D_MODEL
EPS = 1e-6
N_MOD = 6
NEG_INF = -1e30
Q_END = ATTN_WIDTH
K_END = Q_END + KV_WIDTH
V_END = K_END + KV_WIDTH
HY_END = V_END + 3 * HYENA_WIDTH
GA_END = HY_END + D_MODEL
GH_END = GA_END + D_MODEL
IN_WIDTH = GH_END

LANES = 128
VMEM_LIMIT = 56 * 1024 * 1024

FFT_N2 = 16
FFT_KG = 16
PROLOGUE_ROWS = 256
MLP_COLS = 512
HY_CB = 128
MOD_ROWS = 8


def _cparams(sem):
    return pltpu.CompilerParams(dimension_semantics=sem, vmem_limit_bytes=VMEM_LIMIT)


def _single(block_shape, index_map):
    return pl.BlockSpec(block_shape, index_map, pipeline_mode=pl.Buffered(1))


def _rms(x):
    return x * lax.rsqrt(jnp.mean(x * x, axis=-1, keepdims=True) + EPS)


def _row_chunks(nrows, fn):
    def chunk(r, carry):
        fn(pl.ds(pl.multiple_of(r * PROLOGUE_ROWS, PROLOGUE_ROWS), PROLOGUE_ROWS))
        return carry

    lax.fori_loop(0, nrows // PROLOGUE_ROWS, chunk, 0)


def _mod_kernel(c_ref, w_ref, b_ref, o_ref):
    c = c_ref[...]
    s = c * jax.nn.sigmoid(c)
    o_ref[...] = jnp.dot(s.astype(BF16), w_ref[...].astype(BF16),
                         preferred_element_type=F32) + b_ref[...]


def _modulation(c_rows, w, layer, b):
    n = w.shape[2]
    tn = 1024
    return pl.pallas_call(
        _mod_kernel,
        grid=(n // tn,),
        in_specs=[pl.BlockSpec((MOD_ROWS, D_MODEL), lambda j: (0, 0)),
                  pl.BlockSpec((None, D_MODEL, tn), lambda j: (layer, 0, j)),
                  pl.BlockSpec((1, tn), lambda j: (0, j))],
        out_specs=pl.BlockSpec((MOD_ROWS, tn), lambda j: (0, j)),
        out_shape=jax.ShapeDtypeStruct((MOD_ROWS, n), F32),
        name="modulation",
        compiler_params=_cparams(("arbitrary",)),
    )(c_rows, w, b.reshape(1, n))


def _normproj_kernel(x_ref, g_ref, sc_ref, sh_ref, w_ref, o_ref, h_ref):
    @pl.when(pl.program_id(1) == 0)
    def _():
        g, sc1, sh = g_ref[...], 1.0 + sc_ref[...], sh_ref[...]

        def prologue(rows):
            h_ref[rows, :] = (_rms(x_ref[rows, :]) * g * sc1 + sh).astype(BF16)

        _row_chunks(x_ref.shape[0], prologue)

    o_ref[...] = jnp.dot(h_ref[...], w_ref[...].astype(BF16), preferred_element_type=F32)


def _normproj(x, g, scale, shift, w, layer, col0, n, tm, tn):
    m = x.shape[0]
    j0 = col0 // tn
    nmod = scale.shape[0]
    blocks_per_mod = m // nmod // tm
    mod_spec = pl.BlockSpec((None, 1, D_MODEL), lambda i, j: (i // blocks_per_mod, 0, 0))
    return pl.pallas_call(
        _normproj_kernel,
        grid=(m // tm, n // tn),
        in_specs=[pl.BlockSpec((tm, D_MODEL), lambda i, j: (i, 0), pipeline_mode=pl.Buffered(1)),
                  pl.BlockSpec((1, D_MODEL), lambda i, j: (0, 0)),
                  mod_spec, mod_spec,
                  pl.BlockSpec((None, D_MODEL, tn), lambda i, j: (layer, 0, j0 + j))],
        out_specs=pl.BlockSpec((tm, tn), lambda i, j: (i, j)),
        out_shape=jax.ShapeDtypeStruct((m, n), F32),
        scratch_shapes=[pltpu.VMEM((tm, D_MODEL), BF16)],
        name="normproj",
        compiler_params=_cparams(("arbitrary", "arbitrary")),
    )(x, g.reshape(1, D_MODEL), scale, shift, w)


def _rope(x, cos, sin_signed, first_half):
    rot = jnp.where(first_half, pltpu.roll(x, HEAD_DIM - 32, axis=1), pltpu.roll(x, 32, axis=1))
    return x * cos + rot * sin_signed


def _softmax_pv(s, sink, v):
    m = jnp.maximum(jnp.max(s, axis=-1, keepdims=True), sink)
    e = jnp.exp(s - m)
    denom = jnp.sum(e, axis=-1, keepdims=True) + jnp.exp(sink - m)
    o = jnp.dot(e.astype(BF16), v, preferred_element_type=F32)
    return o / denom


def _band_bias(nctx):
    qi = (jnp.arange(Q_GROUP * BLOCK, dtype=jnp.int32) % BLOCK)[None, :, None]
    kj = jnp.arange(3 * BLOCK + nctx, dtype=jnp.int32)[None, None, :]
    variant = jnp.arange(3, dtype=jnp.int32)[:, None, None]
    in_prev = kj < BLOCK
    in_next = (kj >= 2 * BLOCK) & (kj < 3 * BLOCK)
    valid = jnp.where(in_prev, (kj >= qi) & (variant != 0),
                      jnp.where(in_next, (kj - 2 * BLOCK <= qi) & (variant != 2), True))
    return jnp.where(valid, 0.0, NEG_INF).astype(F32)


def _win_attn_kernel(sink_ref, q_ref, kp_ref, kc_ref, kn_ref, vp_ref, vc_ref, vn_ref,
                     kx_ref, vx_ref, cos_ref, sin_ref, bias_ref, o_ref, *, nb):
    i = pl.program_id(1)
    scale = HEAD_DIM ** -0.5
    lane = lax.broadcasted_iota(jnp.int32, (BLOCK, HEAD_DIM), 1)
    first_half = (lane % 64) < 32

    def table(ref, blk):
        return ref[pl.ds(pl.multiple_of(blk * BLOCK, BLOCK), BLOCK), :]

    ip = jnp.maximum(i - 1, 0)
    inx = jnp.minimum(i + 1, nb - 1)
    cos_c, sin_c = table(cos_ref, i), table(sin_ref, i)
    cos_p, sin_p = table(cos_ref, ip), table(sin_ref, ip)
    cos_n, sin_n = table(cos_ref, inx), table(sin_ref, inx)

    rows = Q_GROUP * BLOCK
    head_in_group = lax.broadcasted_iota(jnp.int32, (rows, 1), 0) // BLOCK
    bias = bias_ref[jnp.where(i == 0, 0, jnp.where(i == nb - 1, 2, 1))]

    for h in range(N_KV_HEADS):
        hs = slice(h * HEAD_DIM, (h + 1) * HEAD_DIM)
        k = jnp.concatenate([
            _rope(kp_ref[:, hs], cos_p, sin_p, first_half),
            _rope(kc_ref[:, hs], cos_c, sin_c, first_half),
            _rope(kn_ref[:, hs], cos_n, sin_n, first_half),
            kx_ref[:, hs]], axis=0).astype(BF16)
        v = jnp.concatenate([vp_ref[:, hs], vc_ref[:, hs], vn_ref[:, hs], vx_ref[:, hs]],
                            axis=0).astype(BF16)
        heads = [h * Q_GROUP + g for g in range(Q_GROUP)]
        q = jnp.concatenate(
            [_rope(q_ref[:, hd * HEAD_DIM:(hd + 1) * HEAD_DIM], cos_c, sin_c, first_half)
             for hd in heads], axis=0).astype(BF16)
        sink = jnp.zeros((rows, 1), F32)
        for g, hd in enumerate(heads):
            sink = jnp.where(head_in_group == g, sink_ref[hd], sink)
        s = lax.dot_general(q, k, (((1,), (1,)), ((), ())), preferred_element_type=F32) * scale
        o = _softmax_pv(s + bias, sink, v).astype(o_ref.dtype)
        for g, hd in enumerate(heads):
            o_ref[:, hd * HEAD_DIM:(hd + 1) * HEAD_DIM] = o[g * BLOCK:(g + 1) * BLOCK]


def _window_attention(p, kx, vx, sink, cos_t, sin_t):
    b, l, _ = p.shape
    c = kx.shape[1]
    nb = l // BLOCK
    kcol = Q_END // KV_WIDTH
    vcol = K_END // KV_WIDTH

    def kv_spec(col, shift):
        return pl.BlockSpec((None, BLOCK, KV_WIDTH),
                            lambda bi, i: (bi, jnp.clip(i + shift, 0, nb - 1), col))

    assert nb >= 2
    bias = _band_bias(c)
    ctx_spec = pl.BlockSpec((None, c, KV_WIDTH), lambda bi, i: (bi, 0, 0))
    tab_spec = _single((l, HEAD_DIM), lambda bi, i: (0, 0))
    return pl.pallas_call(
        functools.partial(_win_attn_kernel, nb=nb),
        grid=(b, nb),
        in_specs=[pl.BlockSpec(memory_space=pltpu.SMEM),
                  pl.BlockSpec((None, BLOCK, ATTN_WIDTH), lambda bi, i: (bi, i, 0)),
                  kv_spec(kcol, -1), kv_spec(kcol, 0), kv_spec(kcol, 1),
                  kv_spec(vcol, -1), kv_spec(vcol, 0), kv_spec(vcol, 1),
                  ctx_spec, ctx_spec, tab_spec, tab_spec,
                  _single(bias.shape, lambda bi, i: (0, 0, 0))],
        out_specs=pl.BlockSpec((None, BLOCK, ATTN_WIDTH), lambda bi, i: (bi, i, 0)),
        out_shape=jax.ShapeDtypeStruct((b, l, ATTN_WIDTH), BF16),
        name="window_attention",
        compiler_params=_cparams(("arbitrary", "arbitrary")),
    )(sink, p, p, p, p, p, p, p, kx, vx, cos_t, sin_t, bias)


def _ctx_attn_kernel(sink_ref, q_ref, k_ref, v_ref, o_ref):
    scale = HEAD_DIM ** -0.5
    for h in range(N_KV_HEADS):
        hs = slice(h * HEAD_DIM, (h + 1) * HEAD_DIM)
        k = k_ref[:, hs].astype(BF16)
        v = v_ref[:, hs].astype(BF16)
        for g in range(Q_GROUP):
            head = h * Q_GROUP + g
            cs = slice(head * HEAD_DIM, (head + 1) * HEAD_DIM)
            q = q_ref[:, cs].astype(BF16)
            s = lax.dot_general(q, k, (((1,), (1,)), ((), ())), preferred_element_type=F32) * scale
            o_ref[:, cs] = _softmax_pv(s, sink_ref[head], v).astype(o_ref.dtype)


def _context_attention(p, sink):
    b, c, _ = p.shape
    return pl.pallas_call(
        _ctx_attn_kernel,
        grid=(b,),
        in_specs=[pl.BlockSpec(memory_space=pltpu.SMEM),
                  pl.BlockSpec((None, c, ATTN_WIDTH), lambda bi: (bi, 0, 0)),
                  pl.BlockSpec((None, c, KV_WIDTH), lambda bi: (bi, 0, Q_END // KV_WIDTH)),
                  pl.BlockSpec((None, c, KV_WIDTH), lambda bi: (bi, 0, K_END // KV_WIDTH))],
        out_specs=pl.BlockSpec((None, c, ATTN_WIDTH), lambda bi: (bi, 0, 0)),
        out_shape=jax.ShapeDtypeStruct((b, c, ATTN_WIDTH), BF16),
        name="context_attention",
        compiler_params=_cparams(("arbitrary",)),
    )(sink, p, p, p)


def _fft_tables(l, n2):
    n = 2 * l
    n1 = n // n2
    k1 = np.arange(n1 // 2, dtype=np.int64)[None, :, None]
    t = (n2 * np.arange(n1 // 2, dtype=np.int64)[None, None, :]
         + np.arange(n2, dtype=np.int64)[:, None, None])
    ang = (((2 * k1 + 1) * t) % (2 * n)).astype(np.float64) * (math.pi / n)
    fwd1 = np.concatenate([np.cos(ang), -np.sin(ang)], axis=1)
    inv1 = np.swapaxes(fwd1, 1, 2) * (2.0 / n)
    as_operand = lambda a: jnp.asarray(np.ascontiguousarray(a, dtype=np.float32)).astype(BF16)
    return as_operand(fwd1), as_operand(inv1)


def _filter_features(l):
    bands = (FILTER_EMB - 1) // 2
    t = jnp.linspace(0.0, 1.0, l, dtype=F32)[:, None]
    w = 2 * math.pi * jnp.arange(l, dtype=F32)[:, None] / l
    f = jnp.linspace(1e-4, bands - 1, bands, dtype=F32)[None, :]
    z = jnp.concatenate([t, jnp.cos(f * w), -jnp.sin(f * w)], axis=-1)
    return jnp.pad(z, ((0, 0), (0, FILTER_HIDDEN - FILTER_EMB)))


def _filter_mlp_kernel(z_ref, w1_ref, b1_ref, f1_ref, w2_ref, b2_ref, f2_ref,
                       w3_ref, b3_ref, f3_ref, o_ref):
    h = jnp.sin(f1_ref[...] * (jnp.dot(z_ref[...], w1_ref[...], preferred_element_type=F32)
                               + b1_ref[...]))
    h = jnp.sin(f2_ref[...] * (jnp.dot(h, w2_ref[...], preferred_element_type=F32) + b2_ref[...]))
    o_ref[...] = jnp.sin(f3_ref[...] * (jnp.dot(h, w3_ref[...], preferred_element_type=F32)
                                        + b3_ref[...]))


def _filter_mlp(z, fw1, fb1, ff1, fw2, fb2, ff2, fw3, fb3, ff3):
    l = z.shape[0]
    row = lambda a: a.reshape(1, FILTER_HIDDEN)
    w1 = jnp.pad(fw1, ((0, FILTER_HIDDEN - FILTER_EMB), (0, 0)))
    return pl.pallas_call(
        _filter_mlp_kernel,
        out_shape=jax.ShapeDtypeStruct((l, FILTER_HIDDEN), F32),
        name="filter_mlp",
        compiler_params=pltpu.CompilerParams(vmem_limit_bytes=VMEM_LIMIT),
    )(z, w1, row(fb1), row(ff1), fw2, row(fb2), row(ff2), fw3, row(fb3), row(ff3))


def _cmul_root16(z, p, inverse):
    zr, zi = z
    p = p % 16
    if inverse:
        p = (16 - p) % 16
    if p == 0:
        return zr, zi
    if p == 4:
        return zi, -zr
    if p == 8:
        return -zr, -zi
    if p == 12:
        return -zi, zr
    c = math.cos(2.0 * math.pi * p / 16)
    s = -math.sin(2.0 * math.pi * p / 16)
    return zr * c - zi * s, zr * s + zi * c


def _dft4(z, inverse):
    (ar, ai), (br, bi), (cr, ci), (dr, di) = z
    t0r, t0i = ar + cr, ai + ci
    t1r, t1i = ar - cr, ai - ci
    t2r, t2i = br + dr, bi + di
    t3r, t3i = br - dr, bi - di
    y0 = (t0r + t2r, t0i + t2i)
    y2 = (t0r - t2r, t0i - t2i)
    minus_i_t3 = (t1r + t3i, t1i - t3r)
    plus_i_t3 = (t1r - t3i, t1i + t3r)
    return [y0, plus_i_t3, y2, minus_i_t3] if inverse else [y0, minus_i_t3, y2, plus_i_t3]


def _dft_digit(z, inverse):
    if len(z) == 1:
        return z
    assert len(z) == 16
    t = [_dft4([z[4 * a + b] for a in range(4)], inverse) for b in range(4)]
    out = [None] * 16
    for c in range(4):
        y = _dft4([_cmul_root16(t[b][c], b * c, inverse) for b in range(4)], inverse)
        for d in range(4):
            out[c + 4 * d] = y[d]
    return out


def _fft_stage1(src_ref, f1_ref, a_ref):
    n2cnt = f1_ref.shape[0]
    groups = a_ref.shape[0]
    n1 = groups * FFT_KG
    half = src_ref.shape[0] // n2cnt

    def body(n2, carry):
        rows = pl.ds(n2, half, stride=n2cnt) if n2cnt > 1 else pl.ds(0, half)
        res = jnp.dot(f1_ref[n2], src_ref[rows, :].astype(BF16), preferred_element_type=F32)
        for g in range(groups):
            a_ref[g, 0, n2] = res[g * FFT_KG:(g + 1) * FFT_KG]
            a_ref[g, 1, n2] = res[n1 + g * FFT_KG:n1 + (g + 1) * FFT_KG]
        return carry

    if n2cnt == 1:
        body(0, 0)
    else:
        lax.fori_loop(0, n2cnt, body, 0, unroll=2)


def _load_digits(ref, g, rows):
    return [(ref[g, 0, d, rows, :], ref[g, 1, d, rows, :]) for d in range(ref.shape[2])]


def _filter_spec_kernel(h3_ref, wf_ref, wb_ref, df_ref, db_ref, f1_ref, kf_ref,
                        hf_ref, hb_ref, a_ref, *, l):
    groups = a_ref.shape[0]
    row = lax.broadcasted_iota(jnp.int32, (l, 1), 0)
    t = row.astype(F32) * (1.0 / (l - 1))
    h3 = h3_ref[...]
    hf = jnp.dot(h3, wf_ref[...], preferred_element_type=F32) * jnp.exp(-t * df_ref[...])
    hb = jnp.dot(h3, wb_ref[...], preferred_element_type=F32) * jnp.exp(-t * db_ref[...])
    hb = jnp.where(row > 0, hb, 0.0)
    norm = jnp.sum(jnp.abs(hf), axis=0, keepdims=True) + jnp.sum(jnp.abs(hb), axis=0, keepdims=True)
    hf_ref[...] = hf / norm
    hb_ref[...] = hb / norm

    def forward(g, carry):
        for r in range(0, FFT_KG, 8):
            rows = pl.ds(r, 8)
            for k2, (xr, xi) in enumerate(_dft_digit(_load_digits(a_ref, g, rows), False)):
                kf_ref[g, 0, k2, rows, :] = xr
                kf_ref[g, 1, k2, rows, :] = xi
        return carry

    def backward(g, carry):
        for r in range(0, FFT_KG, 8):
            rows = pl.ds(r, 8)
            for k2, (xr, xi) in enumerate(_dft_digit(_load_digits(a_ref, g, rows), False)):
                kf_ref[g, 0, k2, rows, :] = kf_ref[g, 0, k2, rows, :] + xr
                kf_ref[g, 1, k2, rows, :] = kf_ref[g, 1, k2, rows, :] - xi
        return carry

    _fft_stage1(hf_ref, f1_ref, a_ref)
    lax.fori_loop(0, groups, forward, 0)
    _fft_stage1(hb_ref, f1_ref, a_ref)
    lax.fori_loop(0, groups, backward, 0)


def _filter_spectrum(h3, fw4, decay, tables):
    l = h3.shape[0]
    f1, _ = tables
    n2 = f1.shape[0]
    groups = f1.shape[1] // 2 // FFT_KG
    nblk = HYENA_WIDTH // HY_CB
    spec_shape = (groups, 2, n2, FFT_KG)
    return pl.pallas_call(
        functools.partial(_filter_spec_kernel, l=l),
        grid=(nblk,),
        in_specs=[_single((l, FILTER_HIDDEN), lambda c: (0, 0)),
                  pl.BlockSpec((FILTER_HIDDEN, HY_CB), lambda c: (0, c)),
                  pl.BlockSpec((FILTER_HIDDEN, HY_CB), lambda c: (0, c + nblk)),
                  pl.BlockSpec((1, HY_CB), lambda c: (0, c)),
                  pl.BlockSpec((1, HY_CB), lambda c: (0, c + nblk)),
                  _single(f1.shape, lambda c: (0, 0, 0))],
        out_specs=pl.BlockSpec(spec_shape + (HY_CB,), lambda c: (0, 0, 0, 0, c)),
        out_shape=jax.ShapeDtypeStruct(spec_shape + (HYENA_WIDTH,), F32),
        scratch_shapes=[pltpu.VMEM((l, HY_CB), F32), pltpu.VMEM((l, HY_CB), F32),
                        pltpu.VMEM(spec_shape + (HY_CB,), F32)],
        name="filter_spectrum",
        compiler_params=_cparams(("arbitrary",)),
    )(h3, fw4, fw4, decay, decay, f1)


def _short_conv(src_ref, w_ref, b_ref, l, rows):
    w0, w1, w2, b = w_ref[0:1, :], w_ref[1:2, :], w_ref[2:3, :], b_ref[...]
    r = lax.broadcasted_iota(jnp.int32, (rows, 1), 0)
    for s in range(0, l, rows):
        cur = src_ref[s:s + rows, :]
        before = src_ref[s - 1:s, :] if s > 0 else jnp.zeros((1, cur.shape[1]), F32)
        after = src_ref[s + rows:s + rows + 1, :] if s + rows < l else jnp.zeros((1, cur.shape[1]), F32)
        prev = jnp.where(r == 0, before, pltpu.roll(cur, 1, axis=0))
        nxt = jnp.where(r == rows - 1, after, pltpu.roll(cur, rows - 1, axis=0))
        yield s, prev * w0 + cur * w1 + nxt * w2 + b


def _hyena_kernel(x0_ref, x1_ref, v_ref, w0_ref, w1_ref, wv_ref, b0_ref, b1_ref, bv_ref,
                  bias_ref, kf_ref, f1_ref, g1_ref, o_ref, x0c_ref, vg_ref, a_ref, *, l):
    groups = a_ref.shape[0]
    n2cnt = a_ref.shape[2]
    cb = a_ref.shape[-1]
    n1 = groups * FFT_KG
    half = l // n2cnt
    rows = min(l, 512)

    for s, u in _short_conv(x0_ref, w0_ref, b0_ref, l, rows):
        x0c_ref[s:s + rows, :] = u
    for (s, u1), (_, uv) in zip(_short_conv(x1_ref, w1_ref, b1_ref, l, rows),
                                _short_conv(v_ref, wv_ref, bv_ref, l, rows)):
        vg_ref[s:s + rows, :] = uv * u1

    _fft_stage1(vg_ref, f1_ref, a_ref)

    def spectrum(g, carry):
        for r in range(0, FFT_KG, 8):
            rws = pl.ds(r, 8)
            x = _dft_digit(_load_digits(a_ref, g, rws), False)
            k = _load_digits(kf_ref, g, rws)
            y = [(xr * kr - xi * ki, xr * ki + xi * kr) for (xr, xi), (kr, ki) in zip(x, k)]
            for d, (br, bi) in enumerate(_dft_digit(y, True)):
                a_ref[g, 0, d, rws, :] = br
                a_ref[g, 1, d, rws, :] = bi
        return carry

    lax.fori_loop(0, groups, spectrum, 0)

    def synth(n2, carry):
        b = jnp.concatenate([a_ref[:, 0, n2].reshape(n1, cb), a_ref[:, 1, n2].reshape(n1, cb)],
                            axis=0).astype(BF16)
        y = jnp.dot(g1_ref[n2], b, preferred_element_type=F32)
        idx = pl.ds(n2, half, stride=n2cnt) if n2cnt > 1 else pl.ds(0, half)
        o_ref[idx, :] = (y + vg_ref[idx, :] * bias_ref[...]) * x0c_ref[idx, :]
        return carry

    if n2cnt == 1:
        synth(0, 0)
    else:
        lax.fori_loop(0, n2cnt, synth, 0, unroll=2)


def _hyena(p, col0, conv_w, conv_b, bias_d, kf, tables):
    b, l, _ = p.shape
    f1, g1 = tables
    nblk = HYENA_WIDTH // HY_CB
    c0 = col0 // HY_CB
    spec_block = kf.shape[:-1] + (HY_CB,)

    def slab(part):
        return pl.BlockSpec((None, l, HY_CB), lambda c, bi: (bi, 0, c0 + part * nblk + c))

    def cw(part):
        return pl.BlockSpec((3, HY_CB), lambda c, bi: (0, part * nblk + c))

    def cbias(part):
        return pl.BlockSpec((1, HY_CB), lambda c, bi: (0, part * nblk + c))

    return pl.pallas_call(
        functools.partial(_hyena_kernel, l=l),
        grid=(nblk, b),
        in_specs=[slab(0), slab(1), slab(2), cw(0), cw(1), cw(2), cbias(0), cbias(1), cbias(2),
                  pl.BlockSpec((1, HY_CB), lambda c, bi: (0, c)),
                  pl.BlockSpec(spec_block, lambda c, bi: (0, 0, 0, 0, c)),
                  _single(f1.shape, lambda c, bi: (0, 0, 0)),
                  _single(g1.shape, lambda c, bi: (0, 0, 0))],
        out_specs=pl.BlockSpec((None, l, HY_CB), lambda c, bi: (bi, 0, c)),
        out_shape=jax.ShapeDtypeStruct((b, l, HYENA_WIDTH), F32),
        scratch_shapes=[pltpu.VMEM((l, HY_CB), F32), pltpu.VMEM((l, HY_CB), F32),
                        pltpu.VMEM(spec_block, F32)],
        name="hyena_conv",
        compiler_params=_cparams(("arbitrary", "arbitrary")),
    )(p, p, p, conv_w, conv_w, conv_w, conv_b, conv_b, conv_b, bias_d, kf, f1, g1)


def _merge_kernel(a_ref, h_ref, ga_ref, gh_ref, wa_ref, wh_ref, o_ref):
    ya = jnp.dot(a_ref[...], wa_ref[...], preferred_element_type=F32)
    yh = jnp.dot(h_ref[...].astype(BF16), wh_ref[...], preferred_element_type=F32)
    o_ref[...] = (jax.nn.sigmoid(ga_ref[...]) * ya
                  + jax.nn.sigmoid(gh_ref[...]) * yh).astype(o_ref.dtype)


def _merge(attn, hy, p, w_ao, w_ho, layer, tm, tn):
    m = attn.shape[0]
    ga0 = HY_END // tn
    gh0 = GA_END // tn
    return pl.pallas_call(
        _merge_kernel,
        grid=(m // tm, D_MODEL // tn),
        in_specs=[pl.BlockSpec((tm, ATTN_WIDTH), lambda i, j: (i, 0)),
                  pl.BlockSpec((tm, HYENA_WIDTH), lambda i, j: (i, 0)),
                  pl.BlockSpec((tm, tn), lambda i, j: (i, ga0 + j)),
                  pl.BlockSpec((tm, tn), lambda i, j: (i, gh0 + j)),
                  pl.BlockSpec((None, ATTN_WIDTH, tn), lambda i, j: (layer, 0, j)),
                  pl.BlockSpec((None, HYENA_WIDTH, tn), lambda i, j: (layer, 0, j))],
        out_specs=pl.BlockSpec((tm, tn), lambda i, j: (i, j)),
        out_shape=jax.ShapeDtypeStruct((m, D_MODEL), BF16),
        name="branch_merge",
        compiler_params=_cparams(("arbitrary", "arbitrary")),
    )(attn, hy, p, p, w_ao, w_ho)


def _outproj_kernel(mix_ref, w_ref, x_ref, g_ref, gate_ref, o_ref, y_ref, *, tn):
    j = pl.program_id(1)
    nj = y_ref.shape[0]
    y_ref[j] = jnp.dot(mix_ref[...], w_ref[...], preferred_element_type=F32)

    @pl.when(j == nj - 1)
    def _():
        ss = sum(jnp.sum(jnp.square(y_ref[k]), axis=-1, keepdims=True) for k in range(nj))
        r = lax.rsqrt(ss * (1.0 / D_MODEL) + EPS)
        for k in range(nj):
            cs = slice(k * tn, (k + 1) * tn)
            o_ref[:, cs] = x_ref[:, cs] + gate_ref[:, cs] * ((y_ref[k] * r) * g_ref[:, cs])


def _outproj(mix, w_o, layer, x, g, gate, tm, tn):
    m = x.shape[0]
    nmod = gate.shape[0]
    blocks_per_mod = m // nmod // tm
    return pl.pallas_call(
        functools.partial(_outproj_kernel, tn=tn),
        grid=(m // tm, D_MODEL // tn),
        in_specs=[pl.BlockSpec((tm, D_MODEL), lambda i, j: (i, 0)),
                  pl.BlockSpec((None, D_MODEL, tn), lambda i, j: (layer, 0, j)),
                  pl.BlockSpec((tm, D_MODEL), lambda i, j: (i, 0)),
                  pl.BlockSpec((1, D_MODEL), lambda i, j: (0, 0)),
                  pl.BlockSpec((None, 1, D_MODEL), lambda i, j: (i // blocks_per_mod, 0, 0))],
        out_specs=pl.BlockSpec((tm, D_MODEL), lambda i, j: (i, 0)),
        out_shape=jax.ShapeDtypeStruct((m, D_MODEL), F32),
        scratch_shapes=[pltpu.VMEM((D_MODEL // tn, tm, tn), F32)],
        name="out_projection",
        compiler_params=_cparams(("arbitrary", "arbitrary")),
    )(mix, w_o, x, g.reshape(1, D_MODEL), gate)


def _mlp_kernel(x_ref, gin_ref, sc_ref, sh_ref, w1_ref, w2_ref, gout_ref, gate_ref, o_ref, h_ref):
    j = pl.program_id(1)
    tm = x_ref.shape[0]

    @pl.when(j == 0)
    def _():
        gin, sc1, sh = gin_ref[...], 1.0 + sc_ref[...], sh_ref[...]

        def prologue(rows):
            h_ref[rows, :] = (_rms(x_ref[rows, :]) * gin * sc1 + sh).astype(BF16)
            o_ref[rows, :] = jnp.zeros((PROLOGUE_ROWS, D_MODEL), F32)

        _row_chunks(tm, prologue)

    a = jnp.dot(h_ref[...], w1_ref[...].astype(BF16), preferred_element_type=F32)
    a = jnp.square(jnp.maximum(a, 0.0)).astype(BF16)
    for n in range(0, D_MODEL, MLP_COLS):
        o_ref[:, n:n + MLP_COLS] += jnp.dot(a, w2_ref[:, n:n + MLP_COLS].astype(BF16),
                                            preferred_element_type=F32)

    @pl.when(j == pl.num_programs(1) - 1)
    def _():
        gout, gate = gout_ref[...], gate_ref[...]

        def epilogue(rows):
            o_ref[rows, :] = x_ref[rows, :] + gate * (_rms(o_ref[rows, :]) * gout)

        _row_chunks(tm, epilogue)


def _mlp(x, g_in, scale, shift, w1, w2, layer, g_out, gate, tm, tf):
    m = x.shape[0]
    nmod = gate.shape[0]
    blocks_per_mod = m // nmod // tm
    mod_spec = pl.BlockSpec((None, 1, D_MODEL), lambda i, j: (i // blocks_per_mod, 0, 0))
    row_spec = pl.BlockSpec((1, D_MODEL), lambda i, j: (0, 0))
    return pl.pallas_call(
        _mlp_kernel,
        grid=(m // tm, D_FF // tf),
        in_specs=[pl.BlockSpec((tm, D_MODEL), lambda i, j: (i, 0), pipeline_mode=pl.Buffered(1)),
                  row_spec, mod_spec, mod_spec,
                  pl.BlockSpec((None, D_MODEL, tf), lambda i, j: (layer, 0, j)),
                  pl.BlockSpec((None, tf, D_MODEL), lambda i, j: (layer, j, 0)),
                  row_spec, mod_spec],
        out_specs=pl.BlockSpec((tm, D_MODEL), lambda i, j: (i, 0)),
        out_shape=jax.ShapeDtypeStruct((m, D_MODEL), F32),
        scratch_shapes=[pltpu.VMEM((tm, D_MODEL), BF16)],
        name="channel_mlp",
        compiler_params=_cparams(("arbitrary", "arbitrary")),
    )(x, g_in.reshape(1, D_MODEL), scale, shift, w1, w2, g_out.reshape(1, D_MODEL), gate)


def _rope_tables(l):
    pos = jnp.arange(l, dtype=jnp.int32)
    quarter = HEAD_DIM // 4
    freqs = ROPE_THETA ** (-jnp.arange(quarter, dtype=F32) / quarter)
    ang_r = (pos // GRID_W).astype(F32)[:, None] * freqs[None, :]
    ang_c = (pos % GRID_W).astype(F32)[:, None] * freqs[None, :]
    cos_t = jnp.concatenate([jnp.cos(ang_r), jnp.cos(ang_r), jnp.cos(ang_c), jnp.cos(ang_c)], axis=-1)
    sin_t = jnp.concatenate([-jnp.sin(ang_r), jnp.sin(ang_r), -jnp.sin(ang_c), jnp.sin(ang_c)], axis=-1)
    return cos_t, sin_t


def _decay_rates():
    min_decay = math.log(DECAY_TARGET) / SLOW_DECAY_PCT
    max_decay = math.log(DECAY_TARGET) / FAST_DECAY_PCT
    deltas = jnp.tile(jnp.linspace(min_decay, max_decay, HYENA_WIDTH, dtype=F32), 2)
    return jnp.abs(deltas)[None, :]


def kernel(x, c, ctx, c_ctx, w_mod, b_mod, norm_g, w_in, attn_sink, hy_conv_w, hy_conv_b,
           hy_fw1, hy_fb1, hy_ff1, hy_fw2, hy_fb2, hy_ff2, hy_fw3, hy_fb3, hy_ff3, hy_fw4,
           hy_bias, w_attn_out, w_hyena_out, w_out, w_ff1, w_ff2):
    b, l, d = x.shape
    cl = ctx.shape[1]
    assert d == D_MODEL and l % 1024 == 0 and cl % 256 == 0 and b + 1 <= MOD_ROWS

    cos_t, sin_t = _rope_tables(l)
    decay = _decay_rates()
    tables_lat = _fft_tables(l, FFT_N2)
    tables_ctx = _fft_tables(cl, 1)
    z_lat = _filter_features(l)
    z_ctx = _filter_features(cl)

    c_rows = jnp.concatenate([c, c_ctx[None, :], jnp.zeros((MOD_ROWS - b - 1, d), F32)], axis=0)
    x_lat = x.reshape(b * l, d)
    x_ctx = ctx.reshape(b * cl, d)
    tm_ctx = b * cl
    w_aob = w_attn_out.astype(BF16)
    w_hob = w_hyena_out.astype(BF16)
    w_ob = w_out.astype(BF16)

    for layer in range(DEPTH):
        last = layer == DEPTH - 1
        mod = _modulation(c_rows, w_mod, layer, b_mod[layer])
        mod_lat = [mod[:b, k * d:(k + 1) * d].reshape(b, 1, d) for k in range(N_MOD)]
        mod_ctx = [mod[b:b + 1, k * d:(k + 1) * d].reshape(1, 1, d) for k in range(N_MOD)]
        sh1, sc1, g1, sh2, sc2, g2 = mod_lat
        csh1, csc1, cg1, csh2, csc2, cg2 = mod_ctx
        g = norm_g[layer]
        fparams = (hy_fw1[layer], hy_fb1[layer], hy_ff1[layer], hy_fw2[layer], hy_fb2[layer],
                   hy_ff2[layer], hy_fw3[layer], hy_fb3[layer], hy_ff3[layer])
        bias_d = hy_bias[layer].reshape(1, HYENA_WIDTH)
        conv_b = hy_conv_b[layer].reshape(1, 3 * HYENA_WIDTH)

        p_lat = _normproj(x_lat, g[0], sc1, sh1, w_in, layer, 0, IN_WIDTH, 2048, 512)
        if last:
            kv_ctx = _normproj(x_ctx, g[0], csc1, csh1, w_in, layer, Q_END, V_END - Q_END,
                               tm_ctx, 512)
            kx, vx = kv_ctx[:, :KV_WIDTH], kv_ctx[:, KV_WIDTH:]
        else:
            p_ctx = _normproj(x_ctx, g[0], csc1, csh1, w_in, layer, 0, IN_WIDTH, tm_ctx, 512)
            kx, vx = p_ctx[:, Q_END:K_END], p_ctx[:, K_END:V_END]
        kx = kx.reshape(b, cl, KV_WIDTH)
        vx = vx.reshape(b, cl, KV_WIDTH)

        p3 = p_lat.reshape(b, l, IN_WIDTH)
        attn = _window_attention(p3, kx, vx, attn_sink[layer], cos_t, sin_t)
        kf = _filter_spectrum(_filter_mlp(z_lat, *fparams), hy_fw4[layer], decay, tables_lat)
        hy = _hyena(p3, V_END, hy_conv_w[layer], conv_b, bias_d, kf, tables_lat)
        mix = _merge(attn.reshape(b * l, ATTN_WIDTH), hy.reshape(b * l, HYENA_WIDTH), p_lat,
                     w_aob, w_hob, layer, 1024, 512)
        x_lat = _outproj(mix, w_ob, layer, x_lat, g[1], g1, 512, 512)

        if not last:
            pc3 = p_ctx.reshape(b, cl, IN_WIDTH)
            attn_c = _context_attention(pc3, attn_sink[layer])
            kf_c = _filter_spectrum(_filter_mlp(z_ctx, *fparams), hy_fw4[layer], decay, tables_ctx)
            hy_c = _hyena(pc3, V_END, hy_conv_w[layer], conv_b, bias_d, kf_c, tables_ctx)
            mix_c = _merge(attn_c.reshape(b * cl, ATTN_WIDTH), hy_c.reshape(b * cl, HYENA_WIDTH),
                           p_ctx, w_aob, w_hob, layer, tm_ctx, 512)
            x_ctx = _outproj(mix_c, w_ob, layer, x_ctx, g[1], cg1, tm_ctx, 512)
            x_ctx = _mlp(x_ctx, g[2], csc2, csh2, w_ff1, w_ff2, layer, g[3], cg2, tm_ctx, 512)

        x_lat = _mlp(x_lat, g[2], sc2, sh2, w_ff1, w_ff2, layer, g[3], g2, 1024, 512)
    return x_lat.reshape(b, l, d)
```

```python
import functools
import math

import jax
import jax.numpy as jnp
import numpy as np
from jax import lax
from jax.experimental import pallas as pl
from jax.experimental.pallas import tpu as pltpu

F32 = jnp.float32
BF16 = jnp.bfloat16

D_MODEL = 2048
DEPTH = 2
GRID_W = 64
HEAD_DIM = 128
N_Q_HEADS = 8
N_KV_HEADS = 2
Q_GROUP = N_Q_HEADS // N_KV_HEADS
ATTN_WIDTH = N_Q_HEADS * HEAD_DIM
KV_WIDTH = N_KV_HEADS * HEAD_DIM
BLOCK = 128
ROPE_THETA = 10000.0
HYENA_WIDTH = 1024
FILTER_EMB = 33
FILTER_HIDDEN = 64
DECAY_TARGET = 1e-2
FAST_DECAY_PCT = 0.3
SLOW_DECAY_PCT = 1.5
D_FF = 4 * D_MODEL
EPS = 1e-6
N_MOD = 6
NEG_INF = -1e30
Q_END = ATTN_WIDTH
K_END = Q_END + KV_WIDTH
V_END = K_END + KV_WIDTH
HY_END = V_END + 3 * HYENA_WIDTH
GA_END = HY_END + D_MODEL
GH_END = GA_END + D_MODEL
IN_WIDTH = GH_END

LANES = 128
VMEM_LIMIT = 56 * 1024 * 1024

FFT_N2 = 16
FFT_KG = 16
PROLOGUE_ROWS = 256
MLP_COLS = 512
HY_CB = 128
MOD_ROWS = 8


def _cparams(sem):
    return pltpu.CompilerParams(dimension_semantics=sem, vmem_limit_bytes=VMEM_LIMIT)


def _single(block_shape, index_map):
    return pl.BlockSpec(block_shape, index_map, pipeline_mode=pl.Buffered(1))


def _rms(x):
    return x * lax.rsqrt(jnp.mean(x * x, axis=-1, keepdims=True) + EPS)


def _row_chunks(nrows, fn):
    def chunk(r, carry):
        fn(pl.ds(pl.multiple_of(r * PROLOGUE_ROWS, PROLOGUE_ROWS), PROLOGUE_ROWS))
        return carry

    lax.fori_loop(0, nrows // PROLOGUE_ROWS, chunk, 0)


def _mod_kernel(c_ref, w_ref, b_ref, o_ref):
    c = c_ref[...]
    s = c * jax.nn.sigmoid(c)
    o_ref[...] = jnp.dot(s.astype(BF16), w_ref[...].astype(BF16),
                         preferred_element_type=F32) + b_ref[...]


def _modulation(c_rows, w, layer, b):
    n = w.shape[2]
    tn = 1024
    return pl.pallas_call(
        _mod_kernel,
        grid=(n // tn,),
        in_specs=[pl.BlockSpec((MOD_ROWS, D_MODEL), lambda j: (0, 0)),
                  pl.BlockSpec((None, D_MODEL, tn), lambda j: (layer, 0, j)),
                  pl.BlockSpec((1, tn), lambda j: (0, j))],
        out_specs=pl.BlockSpec((MOD_ROWS, tn), lambda j: (0, j)),
        out_shape=jax.ShapeDtypeStruct((MOD_ROWS, n), F32),
        name="modulation",
        compiler_params=_cparams(("arbitrary",)),
    )(c_rows, w, b.reshape(1, n))


def _normproj_kernel(x_ref, g_ref, sc_ref, sh_ref, w_ref, o_ref, h_ref):
    @pl.when(pl.program_id(1) == 0)
    def _():
        g, sc1, sh = g_ref[...], 1.0 + sc_ref[...], sh_ref[...]

        def prologue(rows):
            h_ref[rows, :] = (_rms(x_ref[rows, :]) * g * sc1 + sh).astype(BF16)

        _row_chunks(x_ref.shape[0], prologue)

    o_ref[...] = jnp.dot(h_ref[...], w_ref[...].astype(BF16), preferred_element_type=F32)


def _normproj(x, g, scale, shift, w, layer, col0, n, tm, tn):
    m = x.shape[0]
    j0 = col0 // tn
    nmod = scale.shape[0]
    blocks_per_mod = m // nmod // tm
    mod_spec = pl.BlockSpec((None, 1, D_MODEL), lambda i, j: (i // blocks_per_mod, 0, 0))
    return pl.pallas_call(
        _normproj_kernel,
        grid=(m // tm, n // tn),
        in_specs=[pl.BlockSpec((tm, D_MODEL), lambda i, j: (i, 0), pipeline_mode=pl.Buffered(1)),
                  pl.BlockSpec((1, D_MODEL), lambda i, j: (0, 0)),
                  mod_spec, mod_spec,
                  pl.BlockSpec((None, D_MODEL, tn), lambda i, j: (layer, 0, j0 + j))],
        out_specs=pl.BlockSpec((tm, tn), lambda i, j: (i, j)),
        out_shape=jax.ShapeDtypeStruct((m, n), F32),
        scratch_shapes=[pltpu.VMEM((tm, D_MODEL), BF16)],
        name="normproj",
        compiler_params=_cparams(("arbitrary", "arbitrary")),
    )(x, g.reshape(1, D_MODEL), scale, shift, w)


def _rope(x, cos, sin_signed, first_half):
    rot = jnp.where(first_half, pltpu.roll(x, HEAD_DIM - 32, axis=1), pltpu.roll(x, 32, axis=1))
    return x * cos + rot * sin_signed


def _softmax_pv(s, sink, v):
    m = jnp.maximum(jnp.max(s, axis=-1, keepdims=True), sink)
    e = jnp.exp(s - m)
    denom = jnp.sum(e, axis=-1, keepdims=True) + jnp.exp(sink - m)
    o = jnp.dot(e.astype(BF16), v, preferred_element_type=F32)
    return o / denom


def _band_bias(nctx):
    qi = (jnp.arange(Q_GROUP * BLOCK, dtype=jnp.int32) % BLOCK)[None, :, None]
    kj = jnp.arange(3 * BLOCK + nctx, dtype=jnp.int32)[None, None, :]
    variant = jnp.arange(3, dtype=jnp.int32)[:, None, None]
    in_prev = kj < BLOCK
    in_next = (kj >= 2 * BLOCK) & (kj < 3 * BLOCK)
    valid = jnp.where(in_prev, (kj >= qi) & (variant != 0),
                      jnp.where(in_next, (kj - 2 * BLOCK <= qi) & (variant != 2), True))
    return jnp.where(valid, 0.0, NEG_INF).astype(F32)


def _win_attn_kernel(sink_ref, q_ref, kp_ref, kc_ref, kn_ref, vp_ref, vc_ref, vn_ref,
                     kx_ref, vx_ref, cos_ref, sin_ref, bias_ref, o_ref, *, nb):
    i = pl.program_id(1)
    scale = HEAD_DIM ** -0.5
    lane = lax.broadcasted_iota(jnp.int32, (BLOCK, HEAD_DIM), 1)
    first_half = (lane % 64) < 32

    def table(ref, blk):
        return ref[pl.ds(pl.multiple_of(blk * BLOCK, BLOCK), BLOCK), :]

    ip = jnp.maximum(i - 1, 0)
    inx = jnp.minimum(i + 1, nb - 1)
    cos_c, sin_c = table(cos_ref, i), table(sin_ref, i)
    cos_p, sin_p = table(cos_ref, ip), table(sin_ref, ip)
    cos_n, sin_n = table(cos_ref, inx), table(sin_ref, inx)

    rows = Q_GROUP * BLOCK
    head_in_group = lax.broadcasted_iota(jnp.int32, (rows, 1), 0) // BLOCK
    bias = bias_ref[jnp.where(i == 0, 0, jnp.where(i == nb - 1, 2, 1))]

    for h in range(N_KV_HEADS):
        hs = slice(h * HEAD_DIM, (h + 1) * HEAD_DIM)
        k = jnp.concatenate([
            _rope(kp_ref[:, hs], cos_p, sin_p, first_half),
            _rope(kc_ref[:, hs], cos_c, sin_c, first_half),
            _rope(kn_ref[:, hs], cos_n, sin_n, first_half),
            kx_ref[:, hs]], axis=0).astype(BF16)
        v = jnp.concatenate([vp_ref[:, hs], vc_ref[:, hs], vn_ref[:, hs], vx_ref[:, hs]],
                            axis=0).astype(BF16)
        heads = [h * Q_GROUP + g for g in range(Q_GROUP)]
        q = jnp.concatenate(
            [_rope(q_ref[:, hd * HEAD_DIM:(hd + 1) * HEAD_DIM], cos_c, sin_c, first_half)
             for hd in heads], axis=0).astype(BF16)
        sink = jnp.zeros((rows, 1), F32)
        for g, hd in enumerate(heads):
            sink = jnp.where(head_in_group == g, sink_ref[hd], sink)
        s = lax.dot_general(q, k, (((1,), (1,)), ((), ())), preferred_element_type=F32) * scale
        o = _softmax_pv(s + bias, sink, v).astype(o_ref.dtype)
        for g, hd in enumerate(heads):
            o_ref[:, hd * HEAD_DIM:(hd + 1) * HEAD_DIM] = o[g * BLOCK:(g + 1) * BLOCK]


def _window_attention(p, kx, vx, sink, cos_t, sin_t):
    b, l, _ = p.shape
    c = kx.shape[1]
    nb = l // BLOCK
    kcol = Q_END // KV_WIDTH
    vcol = K_END // KV_WIDTH

    def kv_spec(col, shift):
        return pl.BlockSpec((None, BLOCK, KV_WIDTH),
                            lambda bi, i: (bi, jnp.clip(i + shift, 0, nb - 1), col))

    assert nb >= 2
    bias = _band_bias(c)
    ctx_spec = pl.BlockSpec((None, c, KV_WIDTH), lambda bi, i: (bi, 0, 0))
    tab_spec = _single((l, HEAD_DIM), lambda bi, i: (0, 0))
    return pl.pallas_call(
        functools.partial(_win_attn_kernel, nb=nb),
        grid=(b, nb),
        in_specs=[pl.BlockSpec(memory_space=pltpu.SMEM),
                  pl.BlockSpec((None, BLOCK, ATTN_WIDTH), lambda bi, i: (bi, i, 0)),
                  kv_spec(kcol, -1), kv_spec(kcol, 0), kv_spec(kcol, 1),
                  kv_spec(vcol, -1), kv_spec(vcol, 0), kv_spec(vcol, 1),
                  ctx_spec, ctx_spec, tab_spec, tab_spec,
                  _single(bias.shape, lambda bi, i: (0, 0, 0))],
        out_specs=pl.BlockSpec((None, BLOCK, ATTN_WIDTH), lambda bi, i: (bi, i, 0)),
        out_shape=jax.ShapeDtypeStruct((b, l, ATTN_WIDTH), BF16),
        name="window_attention",
        compiler_params=_cparams(("arbitrary", "arbitrary")),
    )(sink, p, p, p, p, p, p, p, kx, vx, cos_t, sin_t, bias)


def _ctx_attn_kernel(sink_ref, q_ref, k_ref, v_ref, o_ref):
    scale = HEAD_DIM ** -0.5
    for h in range(N_KV_HEADS):
        hs = slice(h * HEAD_DIM, (h + 1) * HEAD_DIM)
        k = k_ref[:, hs].astype(BF16)
        v = v_ref[:, hs].astype(BF16)
        for g in range(Q_GROUP):
            head = h * Q_GROUP + g
            cs = slice(head * HEAD_DIM, (head + 1) * HEAD_DIM)
            q = q_ref[:, cs].astype(BF16)
            s = lax.dot_general(q, k, (((1,), (1,)), ((), ())), preferred_element_type=F32) * scale
            o_ref[:, cs] = _softmax_pv(s, sink_ref[head], v).astype(o_ref.dtype)


def _context_attention(p, sink):
    b, c, _ = p.shape
    return pl.pallas_call(
        _ctx_attn_kernel,
        grid=(b,),
        in_specs=[pl.BlockSpec(memory_space=pltpu.SMEM),
                  pl.BlockSpec((None, c, ATTN_WIDTH), lambda bi: (bi, 0, 0)),
                  pl.BlockSpec((None, c, KV_WIDTH), lambda bi: (bi, 0, Q_END // KV_WIDTH)),
                  pl.BlockSpec((None, c, KV_WIDTH), lambda bi: (bi, 0, K_END // KV_WIDTH))],
        out_specs=pl.BlockSpec((None, c, ATTN_WIDTH), lambda bi: (bi, 0, 0)),
        out_shape=jax.ShapeDtypeStruct((b, c, ATTN_WIDTH), BF16),
        name="context_attention",
        compiler_params=_cparams(("arbitrary",)),
    )(sink, p, p, p)


def _fft_tables(l, n2):
    n = 2 * l
    n1 = n // n2
    k1 = np.arange(n1 // 2, dtype=np.int64)[None, :, None]
    t = (n2 * np.arange(n1 // 2, dtype=np.int64)[None, None, :]
         + np.arange(n2, dtype=np.int64)[:, None, None])
    ang = (((2 * k1 + 1) * t) % (2 * n)).astype(np.float64) * (math.pi / n)
    fwd1 = np.concatenate([np.cos(ang), -np.sin(ang)], axis=1)
    inv1 = np.swapaxes(fwd1, 1, 2) * (2.0 / n)
    as_operand = lambda a: jnp.asarray(np.ascontiguousarray(a, dtype=np.float32)).astype(BF16)
    return as_operand(fwd1), as_operand(inv1)


def _filter_features(l):
    bands = (FILTER_EMB - 1) // 2
    t = jnp.linspace(0.0, 1.0, l, dtype=F32)[:, None]
    w = 2 * math.pi * jnp.arange(l, dtype=F32)[:, None] / l
    f = jnp.linspace(1e-4, bands - 1, bands, dtype=F32)[None, :]
    z = jnp.concatenate([t, jnp.cos(f * w), -jnp.sin(f * w)], axis=-1)
    return jnp.pad(z, ((0, 0), (0, FILTER_HIDDEN - FILTER_EMB)))


def _filter_mlp_kernel(z_ref, w1_ref, b1_ref, f1_ref, w2_ref, b2_ref, f2_ref,
                       w3_ref, b3_ref, f3_ref, o_ref):
    h = jnp.sin(f1_ref[...] * (jnp.dot(z_ref[...], w1_ref[...], preferred_element_type=F32)
                               + b1_ref[...]))
    h = jnp.sin(f2_ref[...] * (jnp.dot(h, w2_ref[...], preferred_element_type=F32) + b2_ref[...]))
    o_ref[...] = jnp.sin(f3_ref[...] * (jnp.dot(h, w3_ref[...], preferred_element_type=F32)
                                        + b3_ref[...]))


def _filter_mlp(z, fw1, fb1, ff1, fw2, fb2, ff2, fw3, fb3, ff3):
    l = z.shape[0]
    row = lambda a: a.reshape(1, FILTER_HIDDEN)
    w1 = jnp.pad(fw1, ((0, FILTER_HIDDEN - FILTER_EMB), (0, 0)))
    return pl.pallas_call(
        _filter_mlp_kernel,
        out_shape=jax.ShapeDtypeStruct((l, FILTER_HIDDEN), F32),
        name="filter_mlp",
        compiler_params=pltpu.CompilerParams(vmem_limit_bytes=VMEM_LIMIT),
    )(z, w1, row(fb1), row(ff1), fw2, row(fb2), row(ff2), fw3, row(fb3), row(ff3))


def _cmul_root16(z, p, inverse):
    zr, zi = z
    p = p % 16
    if inverse:
        p = (16 - p) % 16
    if p == 0:
        return zr, zi
    if p == 4:
        return zi, -zr
    if p == 8:
        return -zr, -zi
    if p == 12:
        return -zi, zr
    c = math.cos(2.0 * math.pi * p / 16)
    s = -math.sin(2.0 * math.pi * p / 16)
    return zr * c - zi * s, zr * s + zi * c


def _dft4(z, inverse):
    (ar, ai), (br, bi), (cr, ci), (dr, di) = z
    t0r, t0i = ar + cr, ai + ci
    t1r, t1i = ar - cr, ai - ci
    t2r, t2i = br + dr, bi + di
    t3r, t3i = br - dr, bi - di
    y0 = (t0r + t2r, t0i + t2i)
    y2 = (t0r - t2r, t0i - t2i)
    minus_i_t3 = (t1r + t3i, t1i - t3r)
    plus_i_t3 = (t1r - t3i, t1i + t3r)
    return [y0, plus_i_t3, y2, minus_i_t3] if inverse else [y0, minus_i_t3, y2, plus_i_t3]


def _dft_digit(z, inverse):
    if len(z) == 1:
        return z
    assert len(z) == 16
    t = [_dft4([z[4 * a + b] for a in range(4)], inverse) for b in range(4)]
    out = [None] * 16
    for c in range(4):
        y = _dft4([_cmul_root16(t[b][c], b * c, inverse) for b in range(4)], inverse)
        for d in range(4):
            out[c + 4 * d] = y[d]
    return out


def _fft_stage1(src_ref, f1_ref, a_ref):
    n2cnt = f1_ref.shape[0]
    groups = a_ref.shape[0]
    n1 = groups * FFT_KG
    half = src_ref.shape[0] // n2cnt

    def body(n2, carry):
        rows = pl.ds(n2, half, stride=n2cnt) if n2cnt > 1 else pl.ds(0, half)
        res = jnp.dot(f1_ref[n2], src_ref[rows, :].astype(BF16), preferred_element_type=F32)
        for g in range(groups):
            a_ref[g, 0, n2] = res[g * FFT_KG:(g + 1) * FFT_KG]
            a_ref[g, 1, n2] = res[n1 + g * FFT_KG:n1 + (g + 1) * FFT_KG]
        return carry

    if n2cnt == 1:
        body(0, 0)
    else:
        lax.fori_loop(0, n2cnt, body, 0, unroll=4)


def _load_digits(ref, g, rows):
    return [(ref[g, 0, d, rows, :], ref[g, 1, d, rows, :]) for d in range(ref.shape[2])]


def _filter_spec_kernel(h3_ref, wf_ref, wb_ref, df_ref, db_ref, f1_ref, kf_ref,
                        hf_ref, hb_ref, a_ref, *, l):
    groups = a_ref.shape[0]
    row = lax.broadcasted_iota(jnp.int32, (l, 1), 0)
    t = row.astype(F32) * (1.0 / (l - 1))
    h3 = h3_ref[...]
    hf = jnp.dot(h3, wf_ref[...], preferred_element_type=F32) * jnp.exp(-t * df_ref[...])
    hb = jnp.dot(h3, wb_ref[...], preferred_element_type=F32) * jnp.exp(-t * db_ref[...])
    hb = jnp.where(row > 0, hb, 0.0)
    norm = jnp.sum(jnp.abs(hf), axis=0, keepdims=True) + jnp.sum(jnp.abs(hb), axis=0, keepdims=True)
    hf_ref[...] = hf / norm
    hb_ref[...] = hb / norm

    def forward(g, carry):
        for r in range(0, FFT_KG, 8):
            rows = pl.ds(r, 8)
            for k2, (xr, xi) in enumerate(_dft_digit(_load_digits(a_ref, g, rows), False)):
                kf_ref[g, 0, k2, rows, :] = xr
                kf_ref[g, 1, k2, rows, :] = xi
        return carry

    def backward(g, carry):
        for r in range(0, FFT_KG, 8):
            rows = pl.ds(r, 8)
            for k2, (xr, xi) in enumerate(_dft_digit(_load_digits(a_ref, g, rows), False)):
                kf_ref[g, 0, k2, rows, :] = kf_ref[g, 0, k2, rows, :] + xr
                kf_ref[g, 1, k2, rows, :] = kf_ref[g, 1, k2, rows, :] - xi
        return carry

    _fft_stage1(hf_ref, f1_ref, a_ref)
    lax.fori_loop(0, groups, forward, 0)
    _fft_stage1(hb_ref, f1_ref, a_ref)
    lax.fori_loop(0, groups, backward, 0)


def _filter_spectrum(h3, fw4, decay, tables):
    l = h3.shape[0]
    f1, _ = tables
    n2 = f1.shape[0]
    groups = f1.shape[1] // 2 // FFT_KG
    nblk = HYENA_WIDTH // HY_CB
    spec_shape = (groups, 2, n2, FFT_KG)
    return pl.pallas_call(
        functools.partial(_filter_spec_kernel, l=l),
        grid=(nblk,),
        in_specs=[_single((l, FILTER_HIDDEN), lambda c: (0, 0)),
                  pl.BlockSpec((FILTER_HIDDEN, HY_CB), lambda c: (0, c)),
                  pl.BlockSpec((FILTER_HIDDEN, HY_CB), lambda c: (0, c + nblk)),
                  pl.BlockSpec((1, HY_CB), lambda c: (0, c)),
                  pl.BlockSpec((1, HY_CB), lambda c: (0, c + nblk)),
                  _single(f1.shape, lambda c: (0, 0, 0))],
        out_specs=pl.BlockSpec(spec_shape + (HY_CB,), lambda c: (0, 0, 0, 0, c)),
        out_shape=jax.ShapeDtypeStruct(spec_shape + (HYENA_WIDTH,), F32),
        scratch_shapes=[pltpu.VMEM((l, HY_CB), F32), pltpu.VMEM((l, HY_CB), F32),
                        pltpu.VMEM(spec_shape + (HY_CB,), F32)],
        name="filter_spectrum",
        compiler_params=_cparams(("arbitrary",)),
    )(h3, fw4, fw4, decay, decay, f1)


def _short_conv(src_ref, w_ref, b_ref, l, rows):
    w0, w1, w2, b = w_ref[0:1, :], w_ref[1:2, :], w_ref[2:3, :], b_ref[...]
    r = lax.broadcasted_iota(jnp.int32, (rows, 1), 0)
    for s in range(0, l, rows):
        cur = src_ref[s:s + rows, :]
        before = src_ref[s - 1:s, :] if s > 0 else jnp.zeros((1, cur.shape[1]), F32)
        after = src_ref[s + rows:s + rows + 1, :] if s + rows < l else jnp.zeros((1, cur.shape[1]), F32)
        prev = jnp.where(r == 0, before, pltpu.roll(cur, 1, axis=0))
        nxt = jnp.where(r == rows - 1, after, pltpu.roll(cur, rows - 1, axis=0))
        yield s, prev * w0 + cur * w1 + nxt * w2 + b


def _hyena_kernel(x0_ref, x1_ref, v_ref, w0_ref, w1_ref, wv_ref, b0_ref, b1_ref, bv_ref,
                  bias_ref, kf_ref, f1_ref, g1_ref, o_ref, x0c_ref, vg_ref, a_ref, *, l):
    groups = a_ref.shape[0]
    n2cnt = a_ref.shape[2]
    cb = a_ref.shape[-1]
    n1 = groups * FFT_KG
    half = l // n2cnt
    rows = min(l, 512)

    for s, u in _short_conv(x0_ref, w0_ref, b0_ref, l, rows):
        x0c_ref[s:s + rows, :] = u
    for (s, u1), (_, uv) in zip(_short_conv(x1_ref, w1_ref, b1_ref, l, rows),
                                _short_conv(v_ref, wv_ref, bv_ref, l, rows)):
        vg_ref[s:s + rows, :] = uv * u1

    _fft_stage1(vg_ref, f1_ref, a_ref)

    def spectrum(g, carry):
        for r in range(0, FFT_KG, 8):
            rws = pl.ds(r, 8)
            x = _dft_digit(_load_digits(a_ref, g, rws), False)
            k = _load_digits(kf_ref, g, rws)
            y = [(xr * kr - xi * ki, xr * ki + xi * kr) for (xr, xi), (kr, ki) in zip(x, k)]
            for d, (br, bi) in enumerate(_dft_digit(y, True)):
                a_ref[g, 0, d, rws, :] = br
                a_ref[g, 1, d, rws, :] = bi
        return carry

    lax.fori_loop(0, groups, spectrum, 0)

    def synth(n2, carry):
        b = jnp.concatenate([a_ref[:, 0, n2].reshape(n1, cb), a_ref[:, 1, n2].reshape(n1, cb)],
                            axis=0).astype(BF16)
        y = jnp.dot(g1_ref[n2], b, preferred_element_type=F32)
        idx = pl.ds(n2, half, stride=n2cnt) if n2cnt > 1 else pl.ds(0, half)
        o_ref[idx, :] = (y + vg_ref[idx, :] * bias_ref[...]) * x0c_ref[idx, :]
        return carry

    if n2cnt == 1:
        synth(0, 0)
    else:
        lax.fori_loop(0, n2cnt, synth, 0, unroll=4)


def _hyena(p, col0, conv_w, conv_b, bias_d, kf, tables):
    b, l, _ = p.shape
    f1, g1 = tables
    nblk = HYENA_WIDTH // HY_CB
    c0 = col0 // HY_CB
    spec_block = kf.shape[:-1] + (HY_CB,)

    def slab(part):
        return pl.BlockSpec((None, l, HY_CB), lambda c, bi: (bi, 0, c0 + part * nblk + c))

    def cw(part):
        return pl.BlockSpec((3, HY_CB), lambda c, bi: (0, part * nblk + c))

    def cbias(part):
        return pl.BlockSpec((1, HY_CB), lambda c, bi: (0, part * nblk + c))

    return pl.pallas_call(
        functools.partial(_hyena_kernel, l=l),
        grid=(nblk, b),
        in_specs=[slab(0), slab(1), slab(2), cw(0), cw(1), cw(2), cbias(0), cbias(1), cbias(2),
                  pl.BlockSpec((1, HY_CB), lambda c, bi: (0, c)),
                  pl.BlockSpec(spec_block, lambda c, bi: (0, 0, 0, 0, c)),
                  _single(f1.shape, lambda c, bi: (0, 0, 0)),
                  _single(g1.shape, lambda c, bi: (0, 0, 0))],
        out_specs=pl.BlockSpec((None, l, HY_CB), lambda c, bi: (bi, 0, c)),
        out_shape=jax.ShapeDtypeStruct((b, l, HYENA_WIDTH), F32),
        scratch_shapes=[pltpu.VMEM((l, HY_CB), F32), pltpu.VMEM((l, HY_CB), F32),
                        pltpu.VMEM(spec_block, F32)],
        name="hyena_conv",
        compiler_params=_cparams(("arbitrary", "arbitrary")),
    )(p, p, p, conv_w, conv_w, conv_w, conv_b, conv_b, conv_b, bias_d, kf, f1, g1)


def _merge_kernel(a_ref, h_ref, ga_ref, gh_ref, wa_ref, wh_ref, o_ref):
    ya = jnp.dot(a_ref[...], wa_ref[...], preferred_element_type=F32)
    yh = jnp.dot(h_ref[...].astype(BF16), wh_ref[...], preferred_element_type=F32)
    o_ref[...] = (jax.nn.sigmoid(ga_ref[...]) * ya
                  + jax.nn.sigmoid(gh_ref[...]) * yh).astype(o_ref.dtype)


def _merge(attn, hy, p, w_ao, w_ho, layer, tm, tn):
    m = attn.shape[0]
    ga0 = HY_END // tn
    gh0 = GA_END // tn
    return pl.pallas_call(
        _merge_kernel,
        grid=(m // tm, D_MODEL // tn),
        in_specs=[pl.BlockSpec((tm, ATTN_WIDTH), lambda i, j: (i, 0)),
                  pl.BlockSpec((tm, HYENA_WIDTH), lambda i, j: (i, 0)),
                  pl.BlockSpec((tm, tn), lambda i, j: (i, ga0 + j)),
                  pl.BlockSpec((tm, tn), lambda i, j: (i, gh0 + j)),
                  pl.BlockSpec((None, ATTN_WIDTH, tn), lambda i, j: (layer, 0, j)),
                  pl.BlockSpec((None, HYENA_WIDTH, tn), lambda i, j: (layer, 0, j))],
        out_specs=pl.BlockSpec((tm, tn), lambda i, j: (i, j)),
        out_shape=jax.ShapeDtypeStruct((m, D_MODEL), BF16),
        name="branch_merge",
        compiler_params=_cparams(("arbitrary", "arbitrary")),
    )(attn, hy, p, p, w_ao, w_ho)


def _outproj_kernel(mix_ref, w_ref, x_ref, g_ref, gate_ref, o_ref, y_ref, *, tn):
    j = pl.program_id(1)
    nj = y_ref.shape[0]
    y_ref[j] = jnp.dot(mix_ref[...], w_ref[...], preferred_element_type=F32)

    @pl.when(j == nj - 1)
    def _():
        ss = sum(jnp.sum(jnp.square(y_ref[k]), axis=-1, keepdims=True) for k in range(nj))
        r = lax.rsqrt(ss * (1.0 / D_MODEL) + EPS)
        for k in range(nj):
            cs = slice(k * tn, (k + 1) * tn)
            o_ref[:, cs] = x_ref[:, cs] + gate_ref[:, cs] * ((y_ref[k] * r) * g_ref[:, cs])


def _outproj(mix, w_o, layer, x, g, gate, tm, tn):
    m = x.shape[0]
    nmod = gate.shape[0]
    blocks_per_mod = m // nmod // tm
    return pl.pallas_call(
        functools.partial(_outproj_kernel, tn=tn),
        grid=(m // tm, D_MODEL // tn),
        in_specs=[pl.BlockSpec((tm, D_MODEL), lambda i, j: (i, 0)),
                  pl.BlockSpec((None, D_MODEL, tn), lambda i, j: (layer, 0, j),
                               pipeline_mode=pl.Buffered(1) if tn == D_MODEL else None),
                  pl.BlockSpec((tm, D_MODEL), lambda i, j: (i, 0)),
                  pl.BlockSpec((1, D_MODEL), lambda i, j: (0, 0)),
                  pl.BlockSpec((None, 1, D_MODEL), lambda i, j: (i // blocks_per_mod, 0, 0))],
        out_specs=pl.BlockSpec((tm, D_MODEL), lambda i, j: (i, 0)),
        out_shape=jax.ShapeDtypeStruct((m, D_MODEL), F32),
        scratch_shapes=[pltpu.VMEM((D_MODEL // tn, tm, tn), F32)],
        name="out_projection",
        compiler_params=_cparams(("arbitrary", "arbitrary")),
    )(mix, w_o, x, g.reshape(1, D_MODEL), gate)


def _mlp_kernel(x_ref, gin_ref, sc_ref, sh_ref, w1_ref, w2_ref, gout_ref, gate_ref, o_ref, h_ref):
    j = pl.program_id(1)
    tm = x_ref.shape[0]

    @pl.when(j == 0)
    def _():
        gin, sc1, sh = gin_ref[...], 1.0 + sc_ref[...], sh_ref[...]

        def prologue(rows):
            h_ref[rows, :] = (_rms(x_ref[rows, :]) * gin * sc1 + sh).astype(BF16)
            o_ref[rows, :] = jnp.zeros((PROLOGUE_ROWS, D_MODEL), F32)

        _row_chunks(tm, prologue)

    a = jnp.dot(h_ref[...], w1_ref[...].astype(BF16), preferred_element_type=F32)
    a = jnp.square(jnp.maximum(a, 0.0)).astype(BF16)
    for n in range(0, D_MODEL, MLP_COLS):
        o_ref[:, n:n + MLP_COLS] += jnp.dot(a, w2_ref[:, n:n + MLP_COLS].astype(BF16),
                                            preferred_element_type=F32)

    @pl.when(j == pl.num_programs(1) - 1)
    def _():
        gout, gate = gout_ref[...], gate_ref[...]

        def epilogue(rows):
            o_ref[rows, :] = x_ref[rows, :] + gate * (_rms(o_ref[rows, :]) * gout)

        _row_chunks(tm, epilogue)


def _mlp(x, g_in, scale, shift, w1, w2, layer, g_out, gate, tm, tf):
    m = x.shape[0]
    nmod = gate.shape[0]
    blocks_per_mod = m // nmod // tm
    mod_spec = pl.BlockSpec((None, 1, D_MODEL), lambda i, j: (i // blocks_per_mod, 0, 0))
    row_spec = pl.BlockSpec((1, D_MODEL), lambda i, j: (0, 0))
    return pl.pallas_call(
        _mlp_kernel,
        grid=(m // tm, D_FF // tf),
        in_specs=[pl.BlockSpec((tm, D_MODEL), lambda i, j: (i, 0), pipeline_mode=pl.Buffered(1)),
                  row_spec, mod_spec, mod_spec,
                  pl.BlockSpec((None, D_MODEL, tf), lambda i, j: (layer, 0, j)),
                  pl.BlockSpec((None, tf, D_MODEL), lambda i, j: (layer, j, 0)),
                  row_spec, mod_spec],
        out_specs=pl.BlockSpec((tm, D_MODEL), lambda i, j: (i, 0)),
        out_shape=jax.ShapeDtypeStruct((m, D_MODEL), F32),
        scratch_shapes=[pltpu.VMEM((tm, D_MODEL), BF16)],
        name="channel_mlp",
        compiler_params=_cparams(("arbitrary", "arbitrary")),
    )(x, g_in.reshape(1, D_MODEL), scale, shift, w1, w2, g_out.reshape(1, D_MODEL), gate)


def _rope_tables(l):
    pos = jnp.arange(l, dtype=jnp.int32)
    quarter = HEAD_DIM // 4
    freqs = ROPE_THETA ** (-jnp.arange(quarter, dtype=F32) / quarter)
    ang_r = (pos // GRID_W).astype(F32)[:, None] * freqs[None, :]
    ang_c = (pos % GRID_W).astype(F32)[:, None] * freqs[None, :]
    cos_t = jnp.concatenate([jnp.cos(ang_r), jnp.cos(ang_r), jnp.cos(ang_c), jnp.cos(ang_c)], axis=-1)
    sin_t = jnp.concatenate([-jnp.sin(ang_r), jnp.sin(ang_r), -jnp.sin(ang_c), jnp.sin(ang_c)], axis=-1)
    return cos_t, sin_t


def _decay_rates():
    min_decay = math.log(DECAY_TARGET) / SLOW_DECAY_PCT
    max_decay = math.log(DECAY_TARGET) / FAST_DECAY_PCT
    deltas = jnp.tile(jnp.linspace(min_decay, max_decay, HYENA_WIDTH, dtype=F32), 2)
    return jnp.abs(deltas)[None, :]


def kernel(x, c, ctx, c_ctx, w_mod, b_mod, norm_g, w_in, attn_sink, hy_conv_w, hy_conv_b,
           hy_fw1, hy_fb1, hy_ff1, hy_fw2, hy_fb2, hy_ff2, hy_fw3, hy_fb3, hy_ff3, hy_fw4,
           hy_bias, w_attn_out, w_hyena_out, w_out, w_ff1, w_ff2):
    b, l, d = x.shape
    cl = ctx.shape[1]
    assert d == D_MODEL and l % 1024 == 0 and cl % 256 == 0 and b + 1 <= MOD_ROWS

    cos_t, sin_t = _rope_tables(l)
    decay = _decay_rates()
    tables_lat = _fft_tables(l, FFT_N2)
    tables_ctx = _fft_tables(cl, 1)
    z_lat = _filter_features(l)
    z_ctx = _filter_features(cl)

    c_rows = jnp.concatenate([c, c_ctx[None, :], jnp.zeros((MOD_ROWS - b - 1, d), F32)], axis=0)
    x_lat = x.reshape(b * l, d)
    x_ctx = ctx.reshape(b * cl, d)
    tm_ctx = b * cl
    w_aob = w_attn_out.astype(BF16)
    w_hob = w_hyena_out.astype(BF16)
    w_ob = w_out.astype(BF16)

    for layer in range(DEPTH):
        last = layer == DEPTH - 1
        mod = _modulation(c_rows, w_mod, layer, b_mod[layer])
        mod_lat = [mod[:b, k * d:(k + 1) * d].reshape(b, 1, d) for k in range(N_MOD)]
        mod_ctx = [mod[b:b + 1, k * d:(k + 1) * d].reshape(1, 1, d) for k in range(N_MOD)]
        sh1, sc1, g1, sh2, sc2, g2 = mod_lat
        csh1, csc1, cg1, csh2, csc2, cg2 = mod_ctx
        g = norm_g[layer]
        fparams = (hy_fw1[layer], hy_fb1[layer], hy_ff1[layer], hy_fw2[layer], hy_fb2[layer],
                   hy_ff2[layer], hy_fw3[layer], hy_fb3[layer], hy_ff3[layer])
        bias_d = hy_bias[layer].reshape(1, HYENA_WIDTH)
        conv_b = hy_conv_b[layer].reshape(1, 3 * HYENA_WIDTH)

        p_lat = _normproj(x_lat, g[0], sc1, sh1, w_in, layer, 0, IN_WIDTH, 2048, 512)
        if last:
            kv_ctx = _normproj(x_ctx, g[0], csc1, csh1, w_in, layer, Q_END, V_END - Q_END,
                               tm_ctx, 512)
            kx, vx = kv_ctx[:, :KV_WIDTH], kv_ctx[:, KV_WIDTH:]
        else:
            p_ctx = _normproj(x_ctx, g[0], csc1, csh1, w_in, layer, 0, IN_WIDTH, tm_ctx, 512)
            kx, vx = p_ctx[:, Q_END:K_END], p_ctx[:, K_END:V_END]
        kx = kx.reshape(b, cl, KV_WIDTH)
        vx = vx.reshape(b, cl, KV_WIDTH)

        p3 = p_lat.reshape(b, l, IN_WIDTH)
        attn = _window_attention(p3, kx, vx, attn_sink[layer], cos_t, sin_t)
        kf = _filter_spectrum(_filter_mlp(z_lat, *fparams), hy_fw4[layer], decay, tables_lat)
        hy = _hyena(p3, V_END, hy_conv_w[layer], conv_b, bias_d, kf, tables_lat)
        mix = _merge(attn.reshape(b * l, ATTN_WIDTH), hy.reshape(b * l, HYENA_WIDTH), p_lat,
                     w_aob, w_hob, layer, 1024, 512)
        x_lat = _outproj(mix, w_ob, layer, x_lat, g[1], g1, 512, D_MODEL)

        if not last:
            pc3 = p_ctx.reshape(b, cl, IN_WIDTH)
            attn_c = _context_attention(pc3, attn_sink[layer])
            kf_c = _filter_spectrum(_filter_mlp(z_ctx, *fparams), hy_fw4[layer], decay, tables_ctx)
            hy_c = _hyena(pc3, V_END, hy_conv_w[layer], conv_b, bias_d, kf_c, tables_ctx)
            mix_c = _merge(attn_c.reshape(b * cl, ATTN_WIDTH), hy_c.reshape(b * cl, HYENA_WIDTH),
                           p_ctx, w_aob, w_hob, layer, tm_ctx, 512)
            x_ctx = _outproj(mix_c, w_ob, layer, x_ctx, g[1], cg1, tm_ctx, D_MODEL)
            x_ctx = _mlp(x_ctx, g[2], csc2, csh2, w_ff1, w_ff2, layer, g[3], cg2, tm_ctx, 512)

        x_lat = _mlp(x_lat, g[2], sc2, sh2, w_ff1, w_ff2, layer, g[3], g2, 1024, 512)
    return x_lat.reshape(b, l, d)
```

```python
import functools
import math

import jax
import jax.numpy as jnp
import numpy as np
from jax import lax
from jax.experimental import pallas as pl
from jax.experimental.pallas import tpu as pltpu

F32 = jnp.float32
BF16 = jnp.bfloat16

D_MODEL = 2048
DEPTH = 2
GRID_W = 64
HEAD_DIM = 128
N_Q_HEADS = 8
N_KV_HEADS = 2
Q_GROUP = N_Q_HEADS // N_KV_HEADS
ATTN_WIDTH = N_Q_HEADS * HEAD_DIM
KV_WIDTH = N_KV_HEADS * HEAD_DIM
BLOCK = 128
ROPE_THETA = 10000.0
HYENA_WIDTH = 1024
FILTER_EMB = 33
FILTER_HIDDEN = 64
DECAY_TARGET = 1e-2
FAST_DECAY_PCT = 0.3
SLOW_DECAY_PCT = 1.5
D_FF = 4 * D_MODEL
EPS = 1e-6
N_MOD = 6
NEG_INF = -1e30
Q_END = ATTN_WIDTH
K_END = Q_END + KV_WIDTH
V_END = K_END + KV_WIDTH
HY_END = V_END + 3 * HYENA_WIDTH
GA_END = HY_END + D_MODEL
GH_END = GA_END + D_MODEL
IN_WIDTH = GH_END

LANES = 128
VMEM_LIMIT = 56 * 1024 * 1024

FFT_N2 = 16
FFT_KG = 16
PROLOGUE_ROWS = 256
MLP_COLS = 512
GATE_COLS = 512
HY_CB = 128
MOD_ROWS = 8


def _cparams(sem):
    return pltpu.CompilerParams(dimension_semantics=sem, vmem_limit_bytes=VMEM_LIMIT)


def _single(block_shape, index_map):
    return pl.BlockSpec(block_shape, index_map, pipeline_mode=pl.Buffered(1))


def _rms(x):
    return x * lax.rsqrt(jnp.mean(x * x, axis=-1, keepdims=True) + EPS)


def _row_chunks(nrows, fn):
    def chunk(r, carry):
        fn(pl.ds(pl.multiple_of(r * PROLOGUE_ROWS, PROLOGUE_ROWS), PROLOGUE_ROWS))
        return carry

    lax.fori_loop(0, nrows // PROLOGUE_ROWS, chunk, 0)


def _mod_kernel(c_ref, w_ref, b_ref, o_ref):
    c = c_ref[...]
    s = c * jax.nn.sigmoid(c)
    o_ref[...] = jnp.dot(s.astype(BF16), w_ref[...].astype(BF16),
                         preferred_element_type=F32) + b_ref[...]


def _modulation(c_rows, w, layer, b):
    n = w.shape[2]
    tn = 1024
    return pl.pallas_call(
        _mod_kernel,
        grid=(n // tn,),
        in_specs=[pl.BlockSpec((MOD_ROWS, D_MODEL), lambda j: (0, 0)),
                  pl.BlockSpec((None, D_MODEL, tn), lambda j: (layer, 0, j)),
                  pl.BlockSpec((1, tn), lambda j: (0, j))],
        out_specs=pl.BlockSpec((MOD_ROWS, tn), lambda j: (0, j)),
        out_shape=jax.ShapeDtypeStruct((MOD_ROWS, n), F32),
        name="modulation",
        compiler_params=_cparams(("arbitrary",)),
    )(c_rows, w, b.reshape(1, n))


def _normproj_kernel(x_ref, g_ref, sc_ref, sh_ref, w_ref, o_ref, h_ref):
    @pl.when(pl.program_id(1) == 0)
    def _():
        g, sc1, sh = g_ref[...], 1.0 + sc_ref[...], sh_ref[...]

        def prologue(rows):
            h_ref[rows, :] = (_rms(x_ref[rows, :]) * g * sc1 + sh).astype(BF16)

        _row_chunks(x_ref.shape[0], prologue)

    o_ref[...] = jnp.dot(h_ref[...], w_ref[...].astype(BF16), preferred_element_type=F32)


def _normproj(x, g, scale, shift, w, layer, col0, n, tm, tn):
    m = x.shape[0]
    j0 = col0 // tn
    nmod = scale.shape[0]
    blocks_per_mod = m // nmod // tm
    mod_spec = pl.BlockSpec((None, 1, D_MODEL), lambda i, j: (i // blocks_per_mod, 0, 0))
    return pl.pallas_call(
        _normproj_kernel,
        grid=(m // tm, n // tn),
        in_specs=[pl.BlockSpec((tm, D_MODEL), lambda i, j: (i, 0), pipeline_mode=pl.Buffered(1)),
                  pl.BlockSpec((1, D_MODEL), lambda i, j: (0, 0)),
                  mod_spec, mod_spec,
                  pl.BlockSpec((None, D_MODEL, tn), lambda i, j: (layer, 0, j0 + j))],
        out_specs=pl.BlockSpec((tm, tn), lambda i, j: (i, j)),
        out_shape=jax.ShapeDtypeStruct((m, n), F32),
        scratch_shapes=[pltpu.VMEM((tm, D_MODEL), BF16)],
        name="normproj",
        compiler_params=_cparams(("arbitrary", "arbitrary")),
    )(x, g.reshape(1, D_MODEL), scale, shift, w)


def _rope(x, cos, sin_signed, first_half):
    rot = jnp.where(first_half, pltpu.roll(x, HEAD_DIM - 32, axis=1), pltpu.roll(x, 32, axis=1))
    return x * cos + rot * sin_signed


def _softmax_pv(s, sink, v):
    m = jnp.maximum(jnp.max(s, axis=-1, keepdims=True), sink)
    e = jnp.exp(s - m)
    denom = jnp.sum(e, axis=-1, keepdims=True) + jnp.exp(sink - m)
    o = jnp.dot(e.astype(BF16), v, preferred_element_type=F32)
    return o / denom


def _band_bias(nctx):
    qi = (jnp.arange(Q_GROUP * BLOCK, dtype=jnp.int32) % BLOCK)[None, :, None]
    kj = jnp.arange(3 * BLOCK + nctx, dtype=jnp.int32)[None, None, :]
    variant = jnp.arange(3, dtype=jnp.int32)[:, None, None]
    in_prev = kj < BLOCK
    in_next = (kj >= 2 * BLOCK) & (kj < 3 * BLOCK)
    valid = jnp.where(in_prev, (kj >= qi) & (variant != 0),
                      jnp.where(in_next, (kj - 2 * BLOCK <= qi) & (variant != 2), True))
    return jnp.where(valid, 0.0, NEG_INF).astype(F32)


def _win_attn_kernel(sink_ref, q_ref, kp_ref, kc_ref, kn_ref, vp_ref, vc_ref, vn_ref,
                     kx_ref, vx_ref, cos_ref, sin_ref, bias_ref, o_ref, *, nb):
    i = pl.program_id(1)
    scale = HEAD_DIM ** -0.5
    lane = lax.broadcasted_iota(jnp.int32, (BLOCK, HEAD_DIM), 1)
    first_half = (lane % 64) < 32

    def table(ref, blk):
        return ref[pl.ds(pl.multiple_of(blk * BLOCK, BLOCK), BLOCK), :]

    ip = jnp.maximum(i - 1, 0)
    inx = jnp.minimum(i + 1, nb - 1)
    cos_c, sin_c = table(cos_ref, i), table(sin_ref, i)
    cos_p, sin_p = table(cos_ref, ip), table(sin_ref, ip)
    cos_n, sin_n = table(cos_ref, inx), table(sin_ref, inx)

    rows = Q_GROUP * BLOCK
    head_in_group = lax.broadcasted_iota(jnp.int32, (rows, 1), 0) // BLOCK
    bias = bias_ref[jnp.where(i == 0, 0, jnp.where(i == nb - 1, 2, 1))]

    for h in range(N_KV_HEADS):
        hs = slice(h * HEAD_DIM, (h + 1) * HEAD_DIM)
        k = jnp.concatenate([
            _rope(kp_ref[:, hs], cos_p, sin_p, first_half),
            _rope(kc_ref[:, hs], cos_c, sin_c, first_half),
            _rope(kn_ref[:, hs], cos_n, sin_n, first_half),
            kx_ref[:, hs]], axis=0).astype(BF16)
        v = jnp.concatenate([vp_ref[:, hs], vc_ref[:, hs], vn_ref[:, hs], vx_ref[:, hs]],
                            axis=0).astype(BF16)
        heads = [h * Q_GROUP + g for g in range(Q_GROUP)]
        q = jnp.concatenate(
            [_rope(q_ref[:, hd * HEAD_DIM:(hd + 1) * HEAD_DIM], cos_c, sin_c, first_half)
             for hd in heads], axis=0).astype(BF16)
        sink = jnp.zeros((rows, 1), F32)
        for g, hd in enumerate(heads):
            sink = jnp.where(head_in_group == g, sink_ref[hd], sink)
        s = lax.dot_general(q, k, (((1,), (1,)), ((), ())), preferred_element_type=F32) * scale
        o = _softmax_pv(s + bias, sink, v).astype(o_ref.dtype)
        for g, hd in enumerate(heads):
            o_ref[:, hd * HEAD_DIM:(hd + 1) * HEAD_DIM] = o[g * BLOCK:(g + 1) * BLOCK]


def _window_attention(p, kx, vx, sink, cos_t, sin_t):
    b, l, _ = p.shape
    c = kx.shape[1]
    nb = l // BLOCK
    kcol = Q_END // KV_WIDTH
    vcol = K_END // KV_WIDTH

    def kv_spec(col, shift):
        return pl.BlockSpec((None, BLOCK, KV_WIDTH),
                            lambda bi, i: (bi, jnp.clip(i + shift, 0, nb - 1), col))

    assert nb >= 2
    bias = _band_bias(c)
    ctx_spec = pl.BlockSpec((None, c, KV_WIDTH), lambda bi, i: (bi, 0, 0))
    tab_spec = _single((l, HEAD_DIM), lambda bi, i: (0, 0))
    return pl.pallas_call(
        functools.partial(_win_attn_kernel, nb=nb),
        grid=(b, nb),
        in_specs=[pl.BlockSpec(memory_space=pltpu.SMEM),
                  pl.BlockSpec((None, BLOCK, ATTN_WIDTH), lambda bi, i: (bi, i, 0)),
                  kv_spec(kcol, -1), kv_spec(kcol, 0), kv_spec(kcol, 1),
                  kv_spec(vcol, -1), kv_spec(vcol, 0), kv_spec(vcol, 1),
                  ctx_spec, ctx_spec, tab_spec, tab_spec,
                  _single(bias.shape, lambda bi, i: (0, 0, 0))],
        out_specs=pl.BlockSpec((None, BLOCK, ATTN_WIDTH), lambda bi, i: (bi, i, 0)),
        out_shape=jax.ShapeDtypeStruct((b, l, ATTN_WIDTH), BF16),
        name="window_attention",
        compiler_params=_cparams(("arbitrary", "arbitrary")),
    )(sink, p, p, p, p, p, p, p, kx, vx, cos_t, sin_t, bias)


def _ctx_attn_kernel(sink_ref, q_ref, k_ref, v_ref, o_ref):
    scale = HEAD_DIM ** -0.5
    for h in range(N_KV_HEADS):
        hs = slice(h * HEAD_DIM, (h + 1) * HEAD_DIM)
        k = k_ref[:, hs].astype(BF16)
        v = v_ref[:, hs].astype(BF16)
        for g in range(Q_GROUP):
            head = h * Q_GROUP + g
            cs = slice(head * HEAD_DIM, (head + 1) * HEAD_DIM)
            q = q_ref[:, cs].astype(BF16)
            s = lax.dot_general(q, k, (((1,), (1,)), ((), ())), preferred_element_type=F32) * scale
            o_ref[:, cs] = _softmax_pv(s, sink_ref[head], v).astype(o_ref.dtype)


def _context_attention(p, sink):
    b, c, _ = p.shape
    return pl.pallas_call(
        _ctx_attn_kernel,
        grid=(b,),
        in_specs=[pl.BlockSpec(memory_space=pltpu.SMEM),
                  pl.BlockSpec((None, c, ATTN_WIDTH), lambda bi: (bi, 0, 0)),
                  pl.BlockSpec((None, c, KV_WIDTH), lambda bi: (bi, 0, Q_END // KV_WIDTH)),
                  pl.BlockSpec((None, c, KV_WIDTH), lambda bi: (bi, 0, K_END // KV_WIDTH))],
        out_specs=pl.BlockSpec((None, c, ATTN_WIDTH), lambda bi: (bi, 0, 0)),
        out_shape=jax.ShapeDtypeStruct((b, c, ATTN_WIDTH), BF16),
        name="context_attention",
        compiler_params=_cparams(("arbitrary",)),
    )(sink, p, p, p)


def _fft_tables(l, n2):
    n = 2 * l
    n1 = n // n2
    k1 = np.arange(n1 // 2, dtype=np.int64)[None, :, None]
    t = (n2 * np.arange(n1 // 2, dtype=np.int64)[None, None, :]
         + np.arange(n2, dtype=np.int64)[:, None, None])
    ang = (((2 * k1 + 1) * t) % (2 * n)).astype(np.float64) * (math.pi / n)
    fwd1 = np.concatenate([np.cos(ang), -np.sin(ang)], axis=1)
    inv1 = np.swapaxes(fwd1, 1, 2) * (2.0 / n)
    as_operand = lambda a: jnp.asarray(np.ascontiguousarray(a, dtype=np.float32)).astype(BF16)
    return as_operand(fwd1), as_operand(inv1)


def _filter_features(l):
    bands = (FILTER_EMB - 1) // 2
    t = jnp.linspace(0.0, 1.0, l, dtype=F32)[:, None]
    w = 2 * math.pi * jnp.arange(l, dtype=F32)[:, None] / l
    f = jnp.linspace(1e-4, bands - 1, bands, dtype=F32)[None, :]
    z = jnp.concatenate([t, jnp.cos(f * w), -jnp.sin(f * w)], axis=-1)
    return jnp.pad(z, ((0, 0), (0, FILTER_HIDDEN - FILTER_EMB)))


def _filter_mlp_kernel(z_ref, w1_ref, b1_ref, f1_ref, w2_ref, b2_ref, f2_ref,
                       w3_ref, b3_ref, f3_ref, o_ref):
    h = jnp.sin(f1_ref[...] * (jnp.dot(z_ref[...], w1_ref[...], preferred_element_type=F32)
                               + b1_ref[...]))
    h = jnp.sin(f2_ref[...] * (jnp.dot(h, w2_ref[...], preferred_element_type=F32) + b2_ref[...]))
    o_ref[...] = jnp.sin(f3_ref[...] * (jnp.dot(h, w3_ref[...], preferred_element_type=F32)
                                        + b3_ref[...]))


def _filter_mlp(z, fw1, fb1, ff1, fw2, fb2, ff2, fw3, fb3, ff3):
    l = z.shape[0]
    row = lambda a: a.reshape(1, FILTER_HIDDEN)
    w1 = jnp.pad(fw1, ((0, FILTER_HIDDEN - FILTER_EMB), (0, 0)))
    return pl.pallas_call(
        _filter_mlp_kernel,
        out_shape=jax.ShapeDtypeStruct((l, FILTER_HIDDEN), F32),
        name="filter_mlp",
        compiler_params=pltpu.CompilerParams(vmem_limit_bytes=VMEM_LIMIT),
    )(z, w1, row(fb1), row(ff1), fw2, row(fb2), row(ff2), fw3, row(fb3), row(ff3))


def _cmul_root16(z, p, inverse):
    zr, zi = z
    p = p % 16
    if inverse:
        p = (16 - p) % 16
    if p == 0:
        return zr, zi
    if p == 4:
        return zi, -zr
    if p == 8:
        return -zr, -zi
    if p == 12:
        return -zi, zr
    c = math.cos(2.0 * math.pi * p / 16)
    s = -math.sin(2.0 * math.pi * p / 16)
    return zr * c - zi * s, zr * s + zi * c


def _dft4(z, inverse):
    (ar, ai), (br, bi), (cr, ci), (dr, di) = z
    t0r, t0i = ar + cr, ai + ci
    t1r, t1i = ar - cr, ai - ci
    t2r, t2i = br + dr, bi + di
    t3r, t3i = br - dr, bi - di
    y0 = (t0r + t2r, t0i + t2i)
    y2 = (t0r - t2r, t0i - t2i)
    minus_i_t3 = (t1r + t3i, t1i - t3r)
    plus_i_t3 = (t1r - t3i, t1i + t3r)
    return [y0, plus_i_t3, y2, minus_i_t3] if inverse else [y0, minus_i_t3, y2, plus_i_t3]


def _dft_digit(z, inverse):
    if len(z) == 1:
        return z
    assert len(z) == 16
    t = [_dft4([z[4 * a + b] for a in range(4)], inverse) for b in range(4)]
    out = [None] * 16
    for c in range(4):
        y = _dft4([_cmul_root16(t[b][c], b * c, inverse) for b in range(4)], inverse)
        for d in range(4):
            out[c + 4 * d] = y[d]
    return out


def _fft_stage1(src_ref, f1_ref, a_ref):
    n2cnt = f1_ref.shape[0]
    groups = a_ref.shape[0]
    n1 = groups * FFT_KG
    half = src_ref.shape[0] // n2cnt

    def body(n2, carry):
        rows = pl.ds(n2, half, stride=n2cnt) if n2cnt > 1 else pl.ds(0, half)
        res = jnp.dot(f1_ref[n2], src_ref[rows, :].astype(BF16), preferred_element_type=F32)
        for g in range(groups):
            a_ref[g, 0, n2] = res[g * FFT_KG:(g + 1) * FFT_KG]
            a_ref[g, 1, n2] = res[n1 + g * FFT_KG:n1 + (g + 1) * FFT_KG]
        return carry

    if n2cnt == 1:
        body(0, 0)
    else:
        lax.fori_loop(0, n2cnt, body, 0, unroll=4)


def _load_digits(ref, g, rows):
    return [(ref[g, 0, d, rows, :], ref[g, 1, d, rows, :]) for d in range(ref.shape[2])]


def _filter_spec_kernel(h3_ref, wf_ref, wb_ref, df_ref, db_ref, f1_ref, kf_ref,
                        hf_ref, hb_ref, a_ref, *, l):
    groups = a_ref.shape[0]
    row = lax.broadcasted_iota(jnp.int32, (l, 1), 0)
    t = row.astype(F32) * (1.0 / (l - 1))
    h3 = h3_ref[...]
    hf = jnp.dot(h3, wf_ref[...], preferred_element_type=F32) * jnp.exp(-t * df_ref[...])
    hb = jnp.dot(h3, wb_ref[...], preferred_element_type=F32) * jnp.exp(-t * db_ref[...])
    hb = jnp.where(row > 0, hb, 0.0)
    norm = jnp.sum(jnp.abs(hf), axis=0, keepdims=True) + jnp.sum(jnp.abs(hb), axis=0, keepdims=True)
    hf_ref[...] = hf / norm
    hb_ref[...] = hb / norm

    def forward(g, carry):
        for r in range(0, FFT_KG, 8):
            rows = pl.ds(r, 8)
            for k2, (xr, xi) in enumerate(_dft_digit(_load_digits(a_ref, g, rows), False)):
                kf_ref[g, 0, k2, rows, :] = xr
                kf_ref[g, 1, k2, rows, :] = xi
        return carry

    def backward(g, carry):
        for r in range(0, FFT_KG, 8):
            rows = pl.ds(r, 8)
            for k2, (xr, xi) in enumerate(_dft_digit(_load_digits(a_ref, g, rows), False)):
                kf_ref[g, 0, k2, rows, :] = kf_ref[g, 0, k2, rows, :] + xr
                kf_ref[g, 1, k2, rows, :] = kf_ref[g, 1, k2, rows, :] - xi
        return carry

    _fft_stage1(hf_ref, f1_ref, a_ref)
    lax.fori_loop(0, groups, forward, 0)
    _fft_stage1(hb_ref, f1_ref, a_ref)
    lax.fori_loop(0, groups, backward, 0)


def _filter_spectrum(h3, fw4, decay, tables):
    l = h3.shape[0]
    f1, _ = tables
    n2 = f1.shape[0]
    groups = f1.shape[1] // 2 // FFT_KG
    nblk = HYENA_WIDTH // HY_CB
    spec_shape = (groups, 2, n2, FFT_KG)
    return pl.pallas_call(
        functools.partial(_filter_spec_kernel, l=l),
        grid=(nblk,),
        in_specs=[_single((l, FILTER_HIDDEN), lambda c: (0, 0)),
                  pl.BlockSpec((FILTER_HIDDEN, HY_CB), lambda c: (0, c)),
                  pl.BlockSpec((FILTER_HIDDEN, HY_CB), lambda c: (0, c + nblk)),
                  pl.BlockSpec((1, HY_CB), lambda c: (0, c)),
                  pl.BlockSpec((1, HY_CB), lambda c: (0, c + nblk)),
                  _single(f1.shape, lambda c: (0, 0, 0))],
        out_specs=pl.BlockSpec(spec_shape + (HY_CB,), lambda c: (0, 0, 0, 0, c)),
        out_shape=jax.ShapeDtypeStruct(spec_shape + (HYENA_WIDTH,), F32),
        scratch_shapes=[pltpu.VMEM((l, HY_CB), F32), pltpu.VMEM((l, HY_CB), F32),
                        pltpu.VMEM(spec_shape + (HY_CB,), F32)],
        name="filter_spectrum",
        compiler_params=_cparams(("arbitrary",)),
    )(h3, fw4, fw4, decay, decay, f1)


def _short_conv(src_ref, w_ref, b_ref, l, rows):
    w0, w1, w2, b = w_ref[0:1, :], w_ref[1:2, :], w_ref[2:3, :], b_ref[...]
    r = lax.broadcasted_iota(jnp.int32, (rows, 1), 0)
    for s in range(0, l, rows):
        cur = src_ref[s:s + rows, :]
        if s > 0:
            prev = src_ref[s - 1:s - 1 + rows, :]
        else:
            prev = jnp.where(r == 0, 0.0, pltpu.roll(cur, 1, axis=0))
        if s + rows < l:
            nxt = src_ref[s + 1:s + 1 + rows, :]
        else:
            nxt = jnp.where(r == rows - 1, 0.0, pltpu.roll(cur, rows - 1, axis=0))
        yield s, prev * w0 + cur * w1 + nxt * w2 + b


def _hyena_kernel(x0_ref, x1_ref, v_ref, w0_ref, w1_ref, wv_ref, b0_ref, b1_ref, bv_ref,
                  bias_ref, kf_ref, f1_ref, g1_ref, o_ref, x0c_ref, vg_ref, a_ref, *, l):
    groups = a_ref.shape[0]
    n2cnt = a_ref.shape[2]
    cb = a_ref.shape[-1]
    n1 = groups * FFT_KG
    half = l // n2cnt
    rows = min(l, 512)

    for s, u in _short_conv(x0_ref, w0_ref, b0_ref, l, rows):
        x0c_ref[s:s + rows, :] = u
    for (s, u1), (_, uv) in zip(_short_conv(x1_ref, w1_ref, b1_ref, l, rows),
                                _short_conv(v_ref, wv_ref, bv_ref, l, rows)):
        vg_ref[s:s + rows, :] = uv * u1

    _fft_stage1(vg_ref, f1_ref, a_ref)

    def spectrum(g, carry):
        for r in range(0, FFT_KG, 8):
            rws = pl.ds(r, 8)
            x = _dft_digit(_load_digits(a_ref, g, rws), False)
            k = _load_digits(kf_ref, g, rws)
            y = [(xr * kr - xi * ki, xr * ki + xi * kr) for (xr, xi), (kr, ki) in zip(x, k)]
            for d, (br, bi) in enumerate(_dft_digit(y, True)):
                a_ref[g, 0, d, rws, :] = br
                a_ref[g, 1, d, rws, :] = bi
        return carry

    lax.fori_loop(0, groups, spectrum, 0)

    def synth(n2, carry):
        b = jnp.concatenate([a_ref[:, 0, n2].reshape(n1, cb), a_ref[:, 1, n2].reshape(n1, cb)],
                            axis=0).astype(BF16)
        y = jnp.dot(g1_ref[n2], b, preferred_element_type=F32)
        idx = pl.ds(n2, half, stride=n2cnt) if n2cnt > 1 else pl.ds(0, half)
        o_ref[idx, :] = (y + vg_ref[idx, :] * bias_ref[...]) * x0c_ref[idx, :]
        return carry

    if n2cnt == 1:
        synth(0, 0)
    else:
        lax.fori_loop(0, n2cnt, synth, 0, unroll=4)


def _hyena(p, col0, conv_w, conv_b, bias_d, kf, tables):
    b, l, _ = p.shape
    f1, g1 = tables
    nblk = HYENA_WIDTH // HY_CB
    c0 = col0 // HY_CB
    spec_block = kf.shape[:-1] + (HY_CB,)

    def slab(part):
        return pl.BlockSpec((None, l, HY_CB), lambda c, bi: (bi, 0, c0 + part * nblk + c))

    def cw(part):
        return pl.BlockSpec((3, HY_CB), lambda c, bi: (0, part * nblk + c))

    def cbias(part):
        return pl.BlockSpec((1, HY_CB), lambda c, bi: (0, part * nblk + c))

    return pl.pallas_call(
        functools.partial(_hyena_kernel, l=l),
        grid=(nblk, b),
        in_specs=[slab(0), slab(1), slab(2), cw(0), cw(1), cw(2), cbias(0), cbias(1), cbias(2),
                  pl.BlockSpec((1, HY_CB), lambda c, bi: (0, c)),
                  pl.BlockSpec(spec_block, lambda c, bi: (0, 0, 0, 0, c)),
                  _single(f1.shape, lambda c, bi: (0, 0, 0)),
                  _single(g1.shape, lambda c, bi: (0, 0, 0))],
        out_specs=pl.BlockSpec((None, l, HY_CB), lambda c, bi: (bi, 0, c)),
        out_shape=jax.ShapeDtypeStruct((b, l, HYENA_WIDTH), F32),
        scratch_shapes=[pltpu.VMEM((l, HY_CB), F32), pltpu.VMEM((l, HY_CB), F32),
                        pltpu.VMEM(spec_block, F32)],
        name="hyena_conv",
        compiler_params=_cparams(("arbitrary", "arbitrary")),
    )(p, p, p, conv_w, conv_w, conv_w, conv_b, conv_b, conv_b, bias_d, kf, f1, g1)


def _merge_kernel(a_ref, h_ref, wa_ref, wh_ref, *refs):
    gate_refs, o_ref = refs[:-1], refs[-1]
    nt = len(gate_refs) // 2
    tg = gate_refs[0].shape[1]
    a = a_ref[...]
    h = h_ref[...].astype(BF16)
    for k in range(nt):
        cs = slice(k * tg, (k + 1) * tg)
        ya = jnp.dot(a, wa_ref[:, cs], preferred_element_type=F32)
        yh = jnp.dot(h, wh_ref[:, cs], preferred_element_type=F32)
        o_ref[:, cs] = (jax.nn.sigmoid(gate_refs[k][...]) * ya
                        + jax.nn.sigmoid(gate_refs[nt + k][...]) * yh).astype(o_ref.dtype)


def _merge(attn, hy, p, w_ao, w_ho, layer, tm):
    m = attn.shape[0]
    nt = D_MODEL // GATE_COLS

    def gate_spec(col0, k):
        return pl.BlockSpec((tm, GATE_COLS), lambda i: (i, col0 // GATE_COLS + k))

    gate_specs = ([gate_spec(HY_END, k) for k in range(nt)]
                  + [gate_spec(GA_END, k) for k in range(nt)])
    return pl.pallas_call(
        _merge_kernel,
        grid=(m // tm,),
        in_specs=[pl.BlockSpec((tm, ATTN_WIDTH), lambda i: (i, 0)),
                  pl.BlockSpec((tm, HYENA_WIDTH), lambda i: (i, 0)),
                  _single((None, ATTN_WIDTH, D_MODEL), lambda i: (layer, 0, 0)),
                  _single((None, HYENA_WIDTH, D_MODEL), lambda i: (layer, 0, 0))] + gate_specs,
        out_specs=pl.BlockSpec((tm, D_MODEL), lambda i: (i, 0)),
        out_shape=jax.ShapeDtypeStruct((m, D_MODEL), BF16),
        name="branch_merge",
        compiler_params=_cparams(("arbitrary",)),
    )(attn, hy, w_ao, w_ho, *([p] * (2 * nt)))


def _outproj_kernel(mix_ref, w_ref, x_ref, g_ref, gate_ref, o_ref, y_ref, *, tn):
    j = pl.program_id(1)
    nj = y_ref.shape[0]
    y_ref[j] = jnp.dot(mix_ref[...], w_ref[...], preferred_element_type=F32)

    @pl.when(j == nj - 1)
    def _():
        ss = sum(jnp.sum(jnp.square(y_ref[k]), axis=-1, keepdims=True) for k in range(nj))
        r = lax.rsqrt(ss * (1.0 / D_MODEL) + EPS)
        for k in range(nj):
            cs = slice(k * tn, (k + 1) * tn)
            o_ref[:, cs] = x_ref[:, cs] + gate_ref[:, cs] * ((y_ref[k] * r) * g_ref[:, cs])


def _outproj(mix, w_o, layer, x, g, gate, tm, tn):
    m = x.shape[0]
    nmod = gate.shape[0]
    blocks_per_mod = m // nmod // tm
    return pl.pallas_call(
        functools.partial(_outproj_kernel, tn=tn),
        grid=(m // tm, D_MODEL // tn),
        in_specs=[pl.BlockSpec((tm, D_MODEL), lambda i, j: (i, 0)),
                  pl.BlockSpec((None, D_MODEL, tn), lambda i, j: (layer, 0, j),
                               pipeline_mode=pl.Buffered(1) if tn == D_MODEL else None),
                  pl.BlockSpec((tm, D_MODEL), lambda i, j: (i, 0)),
                  pl.BlockSpec((1, D_MODEL), lambda i, j: (0, 0)),
                  pl.BlockSpec((None, 1, D_MODEL), lambda i, j: (i // blocks_per_mod, 0, 0))],
        out_specs=pl.BlockSpec((tm, D_MODEL), lambda i, j: (i, 0)),
        out_shape=jax.ShapeDtypeStruct((m, D_MODEL), F32),
        scratch_shapes=[pltpu.VMEM((D_MODEL // tn, tm, tn), F32)],
        name="out_projection",
        compiler_params=_cparams(("arbitrary", "arbitrary")),
    )(mix, w_o, x, g.reshape(1, D_MODEL), gate)


def _mlp_kernel(x_ref, gin_ref, sc_ref, sh_ref, w1_ref, w2_ref, gout_ref, gate_ref, o_ref, h_ref):
    j = pl.program_id(1)
    tm = x_ref.shape[0]

    @pl.when(j == 0)
    def _():
        gin, sc1, sh = gin_ref[...], 1.0 + sc_ref[...], sh_ref[...]

        def prologue(rows):
            h_ref[rows, :] = (_rms(x_ref[rows, :]) * gin * sc1 + sh).astype(BF16)
            o_ref[rows, :] = jnp.zeros((PROLOGUE_ROWS, D_MODEL), F32)

        _row_chunks(tm, prologue)

    a = jnp.dot(h_ref[...], w1_ref[...].astype(BF16), preferred_element_type=F32)
    a = jnp.square(jnp.maximum(a, 0.0)).astype(BF16)
    for n in range(0, D_MODEL, MLP_COLS):
        o_ref[:, n:n + MLP_COLS] += jnp.dot(a, w2_ref[:, n:n + MLP_COLS].astype(BF16),
                                            preferred_element_type=F32)

    @pl.when(j == pl.num_programs(1) - 1)
    def _():
        gout, gate = gout_ref[...], gate_ref[...]

        def epilogue(rows):
            o_ref[rows, :] = x_ref[rows, :] + gate * (_rms(o_ref[rows, :]) * gout)

        _row_chunks(tm, epilogue)


def _mlp(x, g_in, scale, shift, w1, w2, layer, g_out, gate, tm, tf):
    m = x.shape[0]
    nmod = gate.shape[0]
    blocks_per_mod = m // nmod // tm
    mod_spec = pl.BlockSpec((None, 1, D_MODEL), lambda i, j: (i // blocks_per_mod, 0, 0))
    row_spec = pl.BlockSpec((1, D_MODEL), lambda i, j: (0, 0))
    return pl.pallas_call(
        _mlp_kernel,
        grid=(m // tm, D_FF // tf),
        in_specs=[pl.BlockSpec((tm, D_MODEL), lambda i, j: (i, 0), pipeline_mode=pl.Buffered(1)),
                  row_spec, mod_spec, mod_spec,
                  pl.BlockSpec((None, D_MODEL, tf), lambda i, j: (layer, 0, j)),
                  pl.BlockSpec((None, tf, D_MODEL), lambda i, j: (layer, j, 0)),
                  row_spec, mod_spec],
        out_specs=pl.BlockSpec((tm, D_MODEL), lambda i, j: (i, 0)),
        out_shape=jax.ShapeDtypeStruct((m, D_MODEL), F32),
        scratch_shapes=[pltpu.VMEM((tm, D_MODEL), BF16)],
        name="channel_mlp",
        compiler_params=_cparams(("arbitrary", "arbitrary")),
    )(x, g_in.reshape(1, D_MODEL), scale, shift, w1, w2, g_out.reshape(1, D_MODEL), gate)


def _rope_tables(l):
    pos = jnp.arange(l, dtype=jnp.int32)
    quarter = HEAD_DIM // 4
    freqs = ROPE_THETA ** (-jnp.arange(quarter, dtype=F32) / quarter)
    ang_r = (pos // GRID_W).astype(F32)[:, None] * freqs[None, :]
    ang_c = (pos % GRID_W).astype(F32)[:, None] * freqs[None, :]
    cos_t = jnp.concatenate([jnp.cos(ang_r), jnp.cos(ang_r), jnp.cos(ang_c), jnp.cos(ang_c)], axis=-1)
    sin_t = jnp.concatenate([-jnp.sin(ang_r), jnp.sin(ang_r), -jnp.sin(ang_c), jnp.sin(ang_c)], axis=-1)
    return cos_t, sin_t


def _decay_rates():
    min_decay = math.log(DECAY_TARGET) / SLOW_DECAY_PCT
    max_decay = math.log(DECAY_TARGET) / FAST_DECAY_PCT
    deltas = jnp.tile(jnp.linspace(min_decay, max_decay, HYENA_WIDTH, dtype=F32), 2)
    return jnp.abs(deltas)[None, :]


def kernel(x, c, ctx, c_ctx, w_mod, b_mod, norm_g, w_in, attn_sink, hy_conv_w, hy_conv_b,
           hy_fw1, hy_fb1, hy_ff1, hy_fw2, hy_fb2, hy_ff2, hy_fw3, hy_fb3, hy_ff3, hy_fw4,
           hy_bias, w_attn_out, w_hyena_out, w_out, w_ff1, w_ff2):
    b, l, d = x.shape
    cl = ctx.shape[1]
    assert d == D_MODEL and l % 1024 == 0 and cl % 256 == 0 and b + 1 <= MOD_ROWS

    cos_t, sin_t = _rope_tables(l)
    decay = _decay_rates()
    tables_lat = _fft_tables(l, FFT_N2)
    tables_ctx = _fft_tables(cl, 1)
    z_lat = _filter_features(l)
    z_ctx = _filter_features(cl)

    c_rows = jnp.concatenate([c, c_ctx[None, :], jnp.zeros((MOD_ROWS - b - 1, d), F32)], axis=0)
    x_lat = x.reshape(b * l, d)
    x_ctx = ctx.reshape(b * cl, d)
    tm_ctx = b * cl
    w_aob = w_attn_out.astype(BF16)
    w_hob = w_hyena_out.astype(BF16)
    w_ob = w_out.astype(BF16)

    for layer in range(DEPTH):
        last = layer == DEPTH - 1
        mod = _modulation(c_rows, w_mod, layer, b_mod[layer])
        mod_lat = [mod[:b, k * d:(k + 1) * d].reshape(b, 1, d) for k in range(N_MOD)]
        mod_ctx = [mod[b:b + 1, k * d:(k + 1) * d].reshape(1, 1, d) for k in range(N_MOD)]
        sh1, sc1, g1, sh2, sc2, g2 = mod_lat
        csh1, csc1, cg1, csh2, csc2, cg2 = mod_ctx
        g = norm_g[layer]
        fparams = (hy_fw1[layer], hy_fb1[layer], hy_ff1[layer], hy_fw2[layer], hy_fb2[layer],
                   hy_ff2[layer], hy_fw3[layer], hy_fb3[layer], hy_ff3[layer])
        bias_d = hy_bias[layer].reshape(1, HYENA_WIDTH)
        conv_b = hy_conv_b[layer].reshape(1, 3 * HYENA_WIDTH)

        p_lat = _normproj(x_lat, g[0], sc1, sh1, w_in, layer, 0, IN_WIDTH, 2048, 512)
        if last:
            kv_ctx = _normproj(x_ctx, g[0], csc1, csh1, w_in, layer, Q_END, V_END - Q_END,
                               tm_ctx, 512)
            kx, vx = kv_ctx[:, :KV_WIDTH], kv_ctx[:, KV_WIDTH:]
        else:
            p_ctx = _normproj(x_ctx, g[0], csc1, csh1, w_in, layer, 0, IN_WIDTH, tm_ctx, 512)
            kx, vx = p_ctx[:, Q_END:K_END], p_ctx[:, K_END:V_END]
        kx = kx.reshape(b, cl, KV_WIDTH)
        vx = vx.reshape(b, cl, KV_WIDTH)

        p3 = p_lat.reshape(b, l, IN_WIDTH)
        attn = _window_attention(p3, kx, vx, attn_sink[layer], cos_t, sin_t)
        kf = _filter_spectrum(_filter_mlp(z_lat, *fparams), hy_fw4[layer], decay, tables_lat)
        hy = _hyena(p3, V_END, hy_conv_w[layer], conv_b, bias_d, kf, tables_lat)
        mix = _merge(attn.reshape(b * l, ATTN_WIDTH), hy.reshape(b * l, HYENA_WIDTH), p_lat,
                     w_aob, w_hob, layer, 512)
        x_lat = _outproj(mix, w_ob, layer, x_lat, g[1], g1, 512, D_MODEL)

        if not last:
            pc3 = p_ctx.reshape(b, cl, IN_WIDTH)
            attn_c = _context_attention(pc3, attn_sink[layer])
            kf_c = _filter_spectrum(_filter_mlp(z_ctx, *fparams), hy_fw4[layer], decay, tables_ctx)
            hy_c = _hyena(pc3, V_END, hy_conv_w[layer], conv_b, bias_d, kf_c, tables_ctx)
            mix_c = _merge(attn_c.reshape(b * cl, ATTN_WIDTH), hy_c.reshape(b * cl, HYENA_WIDTH),
                           p_ctx, w_aob, w_hob, layer, tm_ctx)
            x_ctx = _outproj(mix_c, w_ob, layer, x_ctx, g[1], cg1, tm_ctx, D_MODEL)
            x_ctx = _mlp(x_ctx, g[2], csc2, csh2, w_ff1, w_ff2, layer, g[3], cg2, tm_ctx, 512)

        x_lat = _mlp(x_lat, g[2], sc2, sh2, w_ff1, w_ff2, layer, g[3], g2, 1024, 512)
    return x_lat.reshape(b, l, d)
```

```python
import functools
import math

import jax
import jax.numpy as jnp
import numpy as np
from jax import lax
from jax.experimental import pallas as pl
from jax.experimental.pallas import tpu as pltpu

F32 = jnp.float32
BF16 = jnp.bfloat16

D_MODEL = 2048
DEPTH = 2
GRID_W = 64
HEAD_DIM = 128
N_Q_HEADS = 8
N_KV_HEADS = 2
Q_GROUP = N_Q_HEADS // N_KV_HEADS
ATTN_WIDTH = N_Q_HEADS * HEAD_DIM
KV_WIDTH = N_KV_HEADS * HEAD_DIM
BLOCK = 128
ROPE_THETA = 10000.0
HYENA_WIDTH = 1024
FILTER_EMB = 33
FILTER_HIDDEN = 64
DECAY_TARGET = 1e-2
FAST_DECAY_PCT = 0.3
SLOW_DECAY_PCT = 1.5
D_FF = 4 * D_MODEL
EPS = 1e-6
N_MOD = 6
NEG_INF = -1e30
Q_END = ATTN_WIDTH
K_END = Q_END + KV_WIDTH
V_END = K_END + KV_WIDTH
HY_END = V_END + 3 * HYENA_WIDTH
GA_END = HY_END + D_MODEL
GH_END = GA_END + D_MODEL
IN_WIDTH = GH_END
LOG2E = math.log2(math.e)
LOGIT_SCALE = HEAD_DIM ** -0.5 * LOG2E

LANES = 128
VMEM_LIMIT = 56 * 1024 * 1024

FFT_N2 = 16
FFT_KG = 16
PROLOGUE_ROWS = 256
MLP_COLS = 512
GATE_COLS = 512
HY_CB = 128
MOD_ROWS = 8


def _cparams(sem):
    return pltpu.CompilerParams(dimension_semantics=sem, vmem_limit_bytes=VMEM_LIMIT)


def _single(block_shape, index_map):
    return pl.BlockSpec(block_shape, index_map, pipeline_mode=pl.Buffered(1))


def _rms(x):
    return x * lax.rsqrt(jnp.mean(x * x, axis=-1, keepdims=True) + EPS)


def _row_chunks(nrows, fn):
    def chunk(r, carry):
        fn(pl.ds(pl.multiple_of(r * PROLOGUE_ROWS, PROLOGUE_ROWS), PROLOGUE_ROWS))
        return carry

    lax.fori_loop(0, nrows // PROLOGUE_ROWS, chunk, 0)


def _mod_kernel(c_ref, w_ref, b_ref, o_ref):
    c = c_ref[...]
    s = c * jax.nn.sigmoid(c)
    o_ref[...] = jnp.dot(s.astype(BF16), w_ref[...].astype(BF16),
                         preferred_element_type=F32) + b_ref[...]


def _modulation(c_rows, w, layer, b):
    n = w.shape[2]
    tn = 1024
    return pl.pallas_call(
        _mod_kernel,
        grid=(n // tn,),
        in_specs=[pl.BlockSpec((MOD_ROWS, D_MODEL), lambda j: (0, 0)),
                  pl.BlockSpec((None, D_MODEL, tn), lambda j: (layer, 0, j)),
                  pl.BlockSpec((1, tn), lambda j: (0, j))],
        out_specs=pl.BlockSpec((MOD_ROWS, tn), lambda j: (0, j)),
        out_shape=jax.ShapeDtypeStruct((MOD_ROWS, n), F32),
        name="modulation",
        compiler_params=_cparams(("arbitrary",)),
    )(c_rows, w, b.reshape(1, n))


def _normproj_kernel(x_ref, g_ref, sc_ref, sh_ref, w_ref, o_ref, h_ref):
    @pl.when(pl.program_id(1) == 0)
    def _():
        g, sc1, sh = g_ref[...], 1.0 + sc_ref[...], sh_ref[...]

        def prologue(rows):
            h_ref[rows, :] = (_rms(x_ref[rows, :]) * g * sc1 + sh).astype(BF16)

        _row_chunks(x_ref.shape[0], prologue)

    o_ref[...] = jnp.dot(h_ref[...], w_ref[...].astype(BF16), preferred_element_type=F32)


def _normproj(x, g, scale, shift, w, layer, col0, n, tm, tn):
    m = x.shape[0]
    j0 = col0 // tn
    nmod = scale.shape[0]
    blocks_per_mod = m // nmod // tm
    mod_spec = pl.BlockSpec((None, 1, D_MODEL), lambda i, j: (i // blocks_per_mod, 0, 0))
    return pl.pallas_call(
        _normproj_kernel,
        grid=(m // tm, n // tn),
        in_specs=[pl.BlockSpec((tm, D_MODEL), lambda i, j: (i, 0), pipeline_mode=pl.Buffered(1)),
                  pl.BlockSpec((1, D_MODEL), lambda i, j: (0, 0)),
                  mod_spec, mod_spec,
                  pl.BlockSpec((None, D_MODEL, tn), lambda i, j: (layer, 0, j0 + j))],
        out_specs=pl.BlockSpec((tm, tn), lambda i, j: (i, j)),
        out_shape=jax.ShapeDtypeStruct((m, n), F32),
        scratch_shapes=[pltpu.VMEM((tm, D_MODEL), BF16)],
        name="normproj",
        compiler_params=_cparams(("arbitrary", "arbitrary")),
    )(x, g.reshape(1, D_MODEL), scale, shift, w)


def _rope(x, cos, sin_signed, first_half):
    rot = jnp.where(first_half, pltpu.roll(x, HEAD_DIM - 32, axis=1), pltpu.roll(x, 32, axis=1))
    return x * cos + rot * sin_signed


def _softmax_pv(s, sink, v):
    m = jnp.maximum(jnp.max(s, axis=-1, keepdims=True), sink)
    e = jnp.exp2(s - m)
    denom = jnp.sum(e, axis=-1, keepdims=True) + jnp.exp2(sink - m)
    o = jnp.dot(e.astype(BF16), v, preferred_element_type=F32)
    return o / denom


def _band_bias(nctx):
    qi = (jnp.arange(Q_GROUP * BLOCK, dtype=jnp.int32) % BLOCK)[None, :, None]
    kj = jnp.arange(3 * BLOCK + nctx, dtype=jnp.int32)[None, None, :]
    variant = jnp.arange(3, dtype=jnp.int32)[:, None, None]
    in_prev = kj < BLOCK
    in_next = (kj >= 2 * BLOCK) & (kj < 3 * BLOCK)
    valid = jnp.where(in_prev, (kj >= qi) & (variant != 0),
                      jnp.where(in_next, (kj - 2 * BLOCK <= qi) & (variant != 2), True))
    return jnp.where(valid, 0.0, NEG_INF).astype(F32)


def _win_attn_kernel(sink_ref, q_ref, kvp_ref, kvc_ref, kvn_ref, kx_ref, vx_ref, cos_ref, sin_ref,
                     bias_ref, o_ref, *, nb):
    i = pl.program_id(1)
    lane = lax.broadcasted_iota(jnp.int32, (BLOCK, HEAD_DIM), 1)
    first_half = (lane % 64) < 32

    def table(ref, blk):
        return ref[pl.ds(pl.multiple_of(blk * BLOCK, BLOCK), BLOCK), :]

    ip = jnp.maximum(i - 1, 0)
    inx = jnp.minimum(i + 1, nb - 1)
    cos_c, sin_c = table(cos_ref, i), table(sin_ref, i)
    cos_p, sin_p = table(cos_ref, ip), table(sin_ref, ip)
    cos_n, sin_n = table(cos_ref, inx), table(sin_ref, inx)

    rows = Q_GROUP * BLOCK
    head_in_group = lax.broadcasted_iota(jnp.int32, (rows, 1), 0) // BLOCK
    bias = bias_ref[jnp.where(i == 0, 0, jnp.where(i == nb - 1, 2, 1))]

    for h in range(N_KV_HEADS):
        hs = slice(h * HEAD_DIM, (h + 1) * HEAD_DIM)
        vs = slice(KV_WIDTH + h * HEAD_DIM, KV_WIDTH + (h + 1) * HEAD_DIM)
        k = jnp.concatenate([
            _rope(kvp_ref[:, hs], cos_p, sin_p, first_half),
            _rope(kvc_ref[:, hs], cos_c, sin_c, first_half),
            _rope(kvn_ref[:, hs], cos_n, sin_n, first_half),
            kx_ref[:, hs]], axis=0).astype(BF16)
        v = jnp.concatenate([kvp_ref[:, vs], kvc_ref[:, vs], kvn_ref[:, vs], vx_ref[:, hs]],
                            axis=0).astype(BF16)
        heads = [h * Q_GROUP + g for g in range(Q_GROUP)]
        q = jnp.concatenate(
            [_rope(q_ref[:, hd * HEAD_DIM:(hd + 1) * HEAD_DIM], cos_c, sin_c, first_half)
             for hd in heads], axis=0).astype(BF16)
        sink = jnp.zeros((rows, 1), F32)
        for g, hd in enumerate(heads):
            sink = jnp.where(head_in_group == g, sink_ref[hd] * LOG2E, sink)
        s = lax.dot_general(q, k, (((1,), (1,)), ((), ())), preferred_element_type=F32)
        o = _softmax_pv(s * LOGIT_SCALE + bias, sink, v).astype(o_ref.dtype)
        for g, hd in enumerate(heads):
            o_ref[:, hd * HEAD_DIM:(hd + 1) * HEAD_DIM] = o[g * BLOCK:(g + 1) * BLOCK]


def _window_attention(p, kx, vx, sink, cos_t, sin_t):
    b, l, _ = p.shape
    c = kx.shape[1]
    nb = l // BLOCK
    def kv_spec(shift):
        return pl.BlockSpec((None, BLOCK, 2 * KV_WIDTH),
                            lambda bi, i: (bi, jnp.clip(i + shift, 0, nb - 1), Q_END // (2 * KV_WIDTH)))

    assert nb >= 2
    bias = _band_bias(c)
    ctx_spec = pl.BlockSpec((None, c, KV_WIDTH), lambda bi, i: (bi, 0, 0))
    tab_spec = _single((l, HEAD_DIM), lambda bi, i: (0, 0))
    return pl.pallas_call(
        functools.partial(_win_attn_kernel, nb=nb),
        grid=(b, nb),
        in_specs=[pl.BlockSpec(memory_space=pltpu.SMEM),
                  pl.BlockSpec((None, BLOCK, ATTN_WIDTH), lambda bi, i: (bi, i, 0)),
                  kv_spec(-1), kv_spec(0), kv_spec(1),
                  ctx_spec, ctx_spec, tab_spec, tab_spec,
                  _single(bias.shape, lambda bi, i: (0, 0, 0))],
        out_specs=pl.BlockSpec((None, BLOCK, ATTN_WIDTH), lambda bi, i: (bi, i, 0)),
        out_shape=jax.ShapeDtypeStruct((b, l, ATTN_WIDTH), BF16),
        name="window_attention",
        compiler_params=_cparams(("arbitrary", "arbitrary")),
    )(sink, p, p, p, p, kx, vx, cos_t, sin_t, bias)


def _ctx_attn_kernel(sink_ref, q_ref, k_ref, v_ref, o_ref):
    for h in range(N_KV_HEADS):
        hs = slice(h * HEAD_DIM, (h + 1) * HEAD_DIM)
        k = k_ref[:, hs].astype(BF16)
        v = v_ref[:, hs].astype(BF16)
        for g in range(Q_GROUP):
            head = h * Q_GROUP + g
            cs = slice(head * HEAD_DIM, (head + 1) * HEAD_DIM)
            q = q_ref[:, cs].astype(BF16)
            s = lax.dot_general(q, k, (((1,), (1,)), ((), ())), preferred_element_type=F32)
            o_ref[:, cs] = _softmax_pv(s * LOGIT_SCALE, sink_ref[head] * LOG2E, v).astype(o_ref.dtype)


def _context_attention(p, sink):
    b, c, _ = p.shape
    return pl.pallas_call(
        _ctx_attn_kernel,
        grid=(b,),
        in_specs=[pl.BlockSpec(memory_space=pltpu.SMEM),
                  pl.BlockSpec((None, c, ATTN_WIDTH), lambda bi: (bi, 0, 0)),
                  pl.BlockSpec((None, c, KV_WIDTH), lambda bi: (bi, 0, Q_END // KV_WIDTH)),
                  pl.BlockSpec((None, c, KV_WIDTH), lambda bi: (bi, 0, K_END // KV_WIDTH))],
        out_specs=pl.BlockSpec((None, c, ATTN_WIDTH), lambda bi: (bi, 0, 0)),
        out_shape=jax.ShapeDtypeStruct((b, c, ATTN_WIDTH), BF16),
        name="context_attention",
        compiler_params=_cparams(("arbitrary",)),
    )(sink, p, p, p)


def _fft_tables(l, n2):
    n = 2 * l
    n1 = n // n2
    k1 = np.arange(n1 // 2, dtype=np.int64)[None, :, None]
    t = (n2 * np.arange(n1 // 2, dtype=np.int64)[None, None, :]
         + np.arange(n2, dtype=np.int64)[:, None, None])
    ang = (((2 * k1 + 1) * t) % (2 * n)).astype(np.float64) * (math.pi / n)
    fwd1 = np.concatenate([np.cos(ang), -np.sin(ang)], axis=1)
    inv1 = np.swapaxes(fwd1, 1, 2) * (2.0 / n)
    as_operand = lambda a: jnp.asarray(np.ascontiguousarray(a, dtype=np.float32)).astype(BF16)
    return as_operand(fwd1), as_operand(inv1)


def _filter_features(l):
    bands = (FILTER_EMB - 1) // 2
    t = jnp.linspace(0.0, 1.0, l, dtype=F32)[:, None]
    w = 2 * math.pi * jnp.arange(l, dtype=F32)[:, None] / l
    f = jnp.linspace(1e-4, bands - 1, bands, dtype=F32)[None, :]
    z = jnp.concatenate([t, jnp.cos(f * w), -jnp.sin(f * w)], axis=-1)
    return jnp.pad(z, ((0, 0), (0, FILTER_HIDDEN - FILTER_EMB)))


def _filter_mlp_kernel(z_ref, w1_ref, b1_ref, f1_ref, w2_ref, b2_ref, f2_ref,
                       w3_ref, b3_ref, f3_ref, o_ref):
    h = jnp.sin(f1_ref[...] * (jnp.dot(z_ref[...], w1_ref[...], preferred_element_type=F32)
                               + b1_ref[...]))
    h = jnp.sin(f2_ref[...] * (jnp.dot(h, w2_ref[...], preferred_element_type=F32) + b2_ref[...]))
    o_ref[...] = jnp.sin(f3_ref[...] * (jnp.dot(h, w3_ref[...], preferred_element_type=F32)
                                        + b3_ref[...]))


def _filter_mlp(z, fw1, fb1, ff1, fw2, fb2, ff2, fw3, fb3, ff3):
    l = z.shape[0]
    row = lambda a: a.reshape(1, FILTER_HIDDEN)
    w1 = jnp.pad(fw1, ((0, FILTER_HIDDEN - FILTER_EMB), (0, 0)))
    return pl.pallas_call(
        _filter_mlp_kernel,
        out_shape=jax.ShapeDtypeStruct((l, FILTER_HIDDEN), F32),
        name="filter_mlp",
        compiler_params=pltpu.CompilerParams(vmem_limit_bytes=VMEM_LIMIT),
    )(z, w1, row(fb1), row(ff1), fw2, row(fb2), row(ff2), fw3, row(fb3), row(ff3))


def _cmul_root16(z, p, inverse):
    zr, zi = z
    p = p % 16
    if inverse:
        p = (16 - p) % 16
    if p == 0:
        return zr, zi
    if p == 4:
        return zi, -zr
    if p == 8:
        return -zr, -zi
    if p == 12:
        return -zi, zr
    c = math.cos(2.0 * math.pi * p / 16)
    s = -math.sin(2.0 * math.pi * p / 16)
    return zr * c - zi * s, zr * s + zi * c


def _dft4(z, inverse):
    (ar, ai), (br, bi), (cr, ci), (dr, di) = z
    t0r, t0i = ar + cr, ai + ci
    t1r, t1i = ar - cr, ai - ci
    t2r, t2i = br + dr, bi + di
    t3r, t3i = br - dr, bi - di
    y0 = (t0r + t2r, t0i + t2i)
    y2 = (t0r - t2r, t0i - t2i)
    minus_i_t3 = (t1r + t3i, t1i - t3r)
    plus_i_t3 = (t1r - t3i, t1i + t3r)
    return [y0, plus_i_t3, y2, minus_i_t3] if inverse else [y0, minus_i_t3, y2, plus_i_t3]


def _dft_digit(z, inverse):
    if len(z) == 1:
        return z
    assert len(z) == 16
    t = [_dft4([z[4 * a + b] for a in range(4)], inverse) for b in range(4)]
    out = [None] * 16
    for c in range(4):
        y = _dft4([_cmul_root16(t[b][c], b * c, inverse) for b in range(4)], inverse)
        for d in range(4):
            out[c + 4 * d] = y[d]
    return out


def _fft_stage1(src_ref, f1_ref, a_ref):
    n2cnt = f1_ref.shape[0]
    groups = a_ref.shape[0]
    n1 = groups * FFT_KG
    half = src_ref.shape[0] // n2cnt

    def body(n2, carry):
        rows = pl.ds(n2, half, stride=n2cnt) if n2cnt > 1 else pl.ds(0, half)
        res = jnp.dot(f1_ref[n2], src_ref[rows, :].astype(BF16), preferred_element_type=F32)
        for g in range(groups):
            a_ref[g, 0, n2] = res[g * FFT_KG:(g + 1) * FFT_KG]
            a_ref[g, 1, n2] = res[n1 + g * FFT_KG:n1 + (g + 1) * FFT_KG]
        return carry

    if n2cnt == 1:
        body(0, 0)
    else:
        lax.fori_loop(0, n2cnt, body, 0, unroll=4)


def _load_digits(ref, g, rows):
    return [(ref[g, 0, d, rows, :], ref[g, 1, d, rows, :]) for d in range(ref.shape[2])]


def _filter_spec_kernel(h3_ref, wf_ref, wb_ref, df_ref, db_ref, f1_ref, kf_ref,
                        hf_ref, hb_ref, a_ref, *, l):
    groups = a_ref.shape[0]
    row = lax.broadcasted_iota(jnp.int32, (l, 1), 0)
    t = row.astype(F32) * (1.0 / (l - 1))
    h3 = h3_ref[...]
    hf = jnp.dot(h3, wf_ref[...], preferred_element_type=F32) * jnp.exp(-t * df_ref[...])
    hb = jnp.dot(h3, wb_ref[...], preferred_element_type=F32) * jnp.exp(-t * db_ref[...])
    hb = jnp.where(row > 0, hb, 0.0)
    norm = jnp.sum(jnp.abs(hf), axis=0, keepdims=True) + jnp.sum(jnp.abs(hb), axis=0, keepdims=True)
    hf_ref[...] = hf / norm
    hb_ref[...] = hb / norm

    def forward(g, carry):
        for r in range(0, FFT_KG, 8):
            rows = pl.ds(r, 8)
            for k2, (xr, xi) in enumerate(_dft_digit(_load_digits(a_ref, g, rows), False)):
                kf_ref[g, 0, k2, rows, :] = xr
                kf_ref[g, 1, k2, rows, :] = xi
        return carry

    def backward(g, carry):
        for r in range(0, FFT_KG, 8):
            rows = pl.ds(r, 8)
            for k2, (xr, xi) in enumerate(_dft_digit(_load_digits(a_ref, g, rows), False)):
                kf_ref[g, 0, k2, rows, :] = kf_ref[g, 0, k2, rows, :] + xr
                kf_ref[g, 1, k2, rows, :] = kf_ref[g, 1, k2, rows, :] - xi
        return carry

    _fft_stage1(hf_ref, f1_ref, a_ref)
    lax.fori_loop(0, groups, forward, 0)
    _fft_stage1(hb_ref, f1_ref, a_ref)
    lax.fori_loop(0, groups, backward, 0)


def _filter_spectrum(h3, fw4, decay, tables):
    l = h3.shape[0]
    f1, _ = tables
    n2 = f1.shape[0]
    groups = f1.shape[1] // 2 // FFT_KG
    nblk = HYENA_WIDTH // HY_CB
    spec_shape = (groups, 2, n2, FFT_KG)
    return pl.pallas_call(
        functools.partial(_filter_spec_kernel, l=l),
        grid=(nblk,),
        in_specs=[_single((l, FILTER_HIDDEN), lambda c: (0, 0)),
                  pl.BlockSpec((FILTER_HIDDEN, HY_CB), lambda c: (0, c)),
                  pl.BlockSpec((FILTER_HIDDEN, HY_CB), lambda c: (0, c + nblk)),
                  pl.BlockSpec((1, HY_CB), lambda c: (0, c)),
                  pl.BlockSpec((1, HY_CB), lambda c: (0, c + nblk)),
                  _single(f1.shape, lambda c: (0, 0, 0))],
        out_specs=pl.BlockSpec(spec_shape + (HY_CB,), lambda c: (0, 0, 0, 0, c)),
        out_shape=jax.ShapeDtypeStruct(spec_shape + (HYENA_WIDTH,), F32),
        scratch_shapes=[pltpu.VMEM((l, HY_CB), F32), pltpu.VMEM((l, HY_CB), F32),
                        pltpu.VMEM(spec_shape + (HY_CB,), F32)],
        name="filter_spectrum",
        compiler_params=_cparams(("arbitrary",)),
    )(h3, fw4, fw4, decay, decay, f1)


def _short_conv(src_ref, w_ref, b_ref, l, rows):
    w0, w1, w2, b = w_ref[0:1, :], w_ref[1:2, :], w_ref[2:3, :], b_ref[...]
    r = lax.broadcasted_iota(jnp.int32, (rows, 1), 0)
    for s in range(0, l, rows):
        cur = src_ref[s:s + rows, :]
        if s > 0:
            prev = src_ref[s - 1:s - 1 + rows, :]
        else:
            prev = jnp.where(r == 0, 0.0, pltpu.roll(cur, 1, axis=0))
        if s + rows < l:
            nxt = src_ref[s + 1:s + 1 + rows, :]
        else:
            nxt = jnp.where(r == rows - 1, 0.0, pltpu.roll(cur, rows - 1, axis=0))
        yield s, prev * w0 + cur * w1 + nxt * w2 + b


def _hyena_kernel(x0_ref, x1_ref, v_ref, w0_ref, w1_ref, wv_ref, b0_ref, b1_ref, bv_ref,
                  bias_ref, kf_ref, f1_ref, g1_ref, o_ref, x0c_ref, vg_ref, a_ref, *, l):
    groups = a_ref.shape[0]
    n2cnt = a_ref.shape[2]
    cb = a_ref.shape[-1]
    n1 = groups * FFT_KG
    half = l // n2cnt
    rows = min(l, 512)

    for s, u in _short_conv(x0_ref, w0_ref, b0_ref, l, rows):
        x0c_ref[s:s + rows, :] = u
    for (s, u1), (_, uv) in zip(_short_conv(x1_ref, w1_ref, b1_ref, l, rows),
                                _short_conv(v_ref, wv_ref, bv_ref, l, rows)):
        vg_ref[s:s + rows, :] = uv * u1

    _fft_stage1(vg_ref, f1_ref, a_ref)

    def spectrum(g, carry):
        for r in range(0, FFT_KG, 8):
            rws = pl.ds(r, 8)
            x = _dft_digit(_load_digits(a_ref, g, rws), False)
            k = _load_digits(kf_ref, g, rws)
            y = [(xr * kr - xi * ki, xr * ki + xi * kr) for (xr, xi), (kr, ki) in zip(x, k)]
            for d, (br, bi) in enumerate(_dft_digit(y, True)):
                a_ref[g, 0, d, rws, :] = br
                a_ref[g, 1, d, rws, :] = bi
        return carry

    lax.fori_loop(0, groups, spectrum, 0)

    def synth(n2, carry):
        b = jnp.concatenate([a_ref[:, 0, n2].reshape(n1, cb), a_ref[:, 1, n2].reshape(n1, cb)],
                            axis=0).astype(BF16)
        y = jnp.dot(g1_ref[n2], b, preferred_element_type=F32)
        idx = pl.ds(n2, half, stride=n2cnt) if n2cnt > 1 else pl.ds(0, half)
        o_ref[idx, :] = y
        return carry

    if n2cnt == 1:
        synth(0, 0)
    else:
        lax.fori_loop(0, n2cnt, synth, 0, unroll=4)

    bias = bias_ref[...]
    for s in range(0, l, rows):
        blk = slice(s, s + rows)
        o_ref[blk, :] = (o_ref[blk, :] + vg_ref[blk, :] * bias) * x0c_ref[blk, :]


def _hyena(p, col0, conv_w, conv_b, bias_d, kf, tables):
    b, l, _ = p.shape
    f1, g1 = tables
    nblk = HYENA_WIDTH // HY_CB
    c0 = col0 // HY_CB
    spec_block = kf.shape[:-1] + (HY_CB,)

    def slab(part):
        return pl.BlockSpec((None, l, HY_CB), lambda c, bi: (bi, 0, c0 + part * nblk + c))

    def cw(part):
        return pl.BlockSpec((3, HY_CB), lambda c, bi: (0, part * nblk + c))

    def cbias(part):
        return pl.BlockSpec((1, HY_CB), lambda c, bi: (0, part * nblk + c))

    return pl.pallas_call(
        functools.partial(_hyena_kernel, l=l),
        grid=(nblk, b),
        in_specs=[slab(0), slab(1), slab(2), cw(0), cw(1), cw(2), cbias(0), cbias(1), cbias(2),
                  pl.BlockSpec((1, HY_CB), lambda c, bi: (0, c)),
                  pl.BlockSpec(spec_block, lambda c, bi: (0, 0, 0, 0, c)),
                  _single(f1.shape, lambda c, bi: (0, 0, 0)),
                  _single(g1.shape, lambda c, bi: (0, 0, 0))],
        out_specs=pl.BlockSpec((None, l, HY_CB), lambda c, bi: (bi, 0, c)),
        out_shape=jax.ShapeDtypeStruct((b, l, HYENA_WIDTH), F32),
        scratch_shapes=[pltpu.VMEM((l, HY_CB), F32), pltpu.VMEM((l, HY_CB), F32),
                        pltpu.VMEM(spec_block, F32)],
        name="hyena_conv",
        compiler_params=_cparams(("arbitrary", "arbitrary")),
    )(p, p, p, conv_w, conv_w, conv_w, conv_b, conv_b, conv_b, bias_d, kf, f1, g1)


def _merge_kernel(a_ref, h_ref, wa_ref, wh_ref, *refs):
    gate_refs, o_ref = refs[:-1], refs[-1]
    nt = len(gate_refs) // 2
    tg = gate_refs[0].shape[1]
    a = a_ref[...]
    h = h_ref[...].astype(BF16)
    for k in range(nt):
        cs = slice(k * tg, (k + 1) * tg)
        ya = jnp.dot(a, wa_ref[:, cs], preferred_element_type=F32)
        yh = jnp.dot(h, wh_ref[:, cs], preferred_element_type=F32)
        o_ref[:, cs] = (jax.nn.sigmoid(gate_refs[k][...]) * ya
                        + jax.nn.sigmoid(gate_refs[nt + k][...]) * yh).astype(o_ref.dtype)


def _merge(attn, hy, p, w_ao, w_ho, layer, tm):
    m = attn.shape[0]
    nt = D_MODEL // GATE_COLS

    def gate_spec(col0, k):
        return pl.BlockSpec((tm, GATE_COLS), lambda i: (i, col0 // GATE_COLS + k))

    gate_specs = ([gate_spec(HY_END, k) for k in range(nt)]
                  + [gate_spec(GA_END, k) for k in range(nt)])
    return pl.pallas_call(
        _merge_kernel,
        grid=(m // tm,),
        in_specs=[pl.BlockSpec((tm, ATTN_WIDTH), lambda i: (i, 0)),
                  pl.BlockSpec((tm, HYENA_WIDTH), lambda i: (i, 0)),
                  _single((None, ATTN_WIDTH, D_MODEL), lambda i: (layer, 0, 0)),
                  _single((None, HYENA_WIDTH, D_MODEL), lambda i: (layer, 0, 0))] + gate_specs,
        out_specs=pl.BlockSpec((tm, D_MODEL), lambda i: (i, 0)),
        out_shape=jax.ShapeDtypeStruct((m, D_MODEL), BF16),
        name="branch_merge",
        compiler_params=_cparams(("arbitrary",)),
    )(attn, hy, w_ao, w_ho, *([p] * (2 * nt)))


def _outproj_kernel(mix_ref, w_ref, x_ref, g_ref, gate_ref, o_ref, y_ref, *, tn):
    j = pl.program_id(1)
    nj = y_ref.shape[0]
    y_ref[j] = jnp.dot(mix_ref[...], w_ref[...], preferred_element_type=F32)

    @pl.when(j == nj - 1)
    def _():
        ss = sum(jnp.sum(jnp.square(y_ref[k]), axis=-1, keepdims=True) for k in range(nj))
        r = lax.rsqrt(ss * (1.0 / D_MODEL) + EPS)
        for k in range(nj):
            cs = slice(k * tn, (k + 1) * tn)
            o_ref[:, cs] = x_ref[:, cs] + gate_ref[:, cs] * ((y_ref[k] * r) * g_ref[:, cs])


def _outproj(mix, w_o, layer, x, g, gate, tm, tn):
    m = x.shape[0]
    nmod = gate.shape[0]
    blocks_per_mod = m // nmod // tm
    return pl.pallas_call(
        functools.partial(_outproj_kernel, tn=tn),
        grid=(m // tm, D_MODEL // tn),
        in_specs=[pl.BlockSpec((tm, D_MODEL), lambda i, j: (i, 0)),
                  pl.BlockSpec((None, D_MODEL, tn), lambda i, j: (layer, 0, j),
                               pipeline_mode=pl.Buffered(1) if tn == D_MODEL else None),
                  pl.BlockSpec((tm, D_MODEL), lambda i, j: (i, 0)),
                  pl.BlockSpec((1, D_MODEL), lambda i, j: (0, 0)),
                  pl.BlockSpec((None, 1, D_MODEL), lambda i, j: (i // blocks_per_mod, 0, 0))],
        out_specs=pl.BlockSpec((tm, D_MODEL), lambda i, j: (i, 0)),
        out_shape=jax.ShapeDtypeStruct((m, D_MODEL), F32),
        scratch_shapes=[pltpu.VMEM((D_MODEL // tn, tm, tn), F32)],
        name="out_projection",
        compiler_params=_cparams(("arbitrary", "arbitrary")),
    )(mix, w_o, x, g.reshape(1, D_MODEL), gate)


def _mlp_kernel(x_ref, gin_ref, sc_ref, sh_ref, w1_ref, w2_ref, gout_ref, gate_ref, o_ref, h_ref):
    j = pl.program_id(1)
    tm = x_ref.shape[0]

    @pl.when(j == 0)
    def _():
        gin, sc1, sh = gin_ref[...], 1.0 + sc_ref[...], sh_ref[...]

        def prologue(rows):
            h_ref[rows, :] = (_rms(x_ref[rows, :]) * gin * sc1 + sh).astype(BF16)
            o_ref[rows, :] = jnp.zeros((PROLOGUE_ROWS, D_MODEL), F32)

        _row_chunks(tm, prologue)

    a = jnp.dot(h_ref[...], w1_ref[...].astype(BF16), preferred_element_type=F32)
    a = jnp.square(jnp.maximum(a, 0.0)).astype(BF16)
    for n in range(0, D_MODEL, MLP_COLS):
        o_ref[:, n:n + MLP_COLS] += jnp.dot(a, w2_ref[:, n:n + MLP_COLS].astype(BF16),
                                            preferred_element_type=F32)

    @pl.when(j == pl.num_programs(1) - 1)
    def _():
        gout, gate = gout_ref[...], gate_ref[...]

        def epilogue(rows):
            o_ref[rows, :] = x_ref[rows, :] + gate * (_rms(o_ref[rows, :]) * gout)

        _row_chunks(tm, epilogue)


def _mlp(x, g_in, scale, shift, w1, w2, layer, g_out, gate, tm, tf):
    m = x.shape[0]
    nmod = gate.shape[0]
    blocks_per_mod = m // nmod // tm
    mod_spec = pl.BlockSpec((None, 1, D_MODEL), lambda i, j: (i // blocks_per_mod, 0, 0))
    row_spec = pl.BlockSpec((1, D_MODEL), lambda i, j: (0, 0))
    return pl.pallas_call(
        _mlp_kernel,
        grid=(m // tm, D_FF // tf),
        in_specs=[pl.BlockSpec((tm, D_MODEL), lambda i, j: (i, 0), pipeline_mode=pl.Buffered(1)),
                  row_spec, mod_spec, mod_spec,
                  pl.BlockSpec((None, D_MODEL, tf), lambda i, j: (layer, 0, j)),
                  pl.BlockSpec((None, tf, D_MODEL), lambda i, j: (layer, j, 0)),
                  row_spec, mod_spec],
        out_specs=pl.BlockSpec((tm, D_MODEL), lambda i, j: (i, 0)),
        out_shape=jax.ShapeDtypeStruct((m, D_MODEL), F32),
        scratch_shapes=[pltpu.VMEM((tm, D_MODEL), BF16)],
        name="channel_mlp",
        compiler_params=_cparams(("arbitrary", "arbitrary")),
    )(x, g_in.reshape(1, D_MODEL), scale, shift, w1, w2, g_out.reshape(1, D_MODEL), gate)


def _rope_tables(l):
    pos = jnp.arange(l, dtype=jnp.int32)
    quarter = HEAD_DIM // 4
    freqs = ROPE_THETA ** (-jnp.arange(quarter, dtype=F32) / quarter)
    ang_r = (pos // GRID_W).astype(F32)[:, None] * freqs[None, :]
    ang_c = (pos % GRID_W).astype(F32)[:, None] * freqs[None, :]
    cos_t = jnp.concatenate([jnp.cos(ang_r), jnp.cos(ang_r), jnp.cos(ang_c), jnp.cos(ang_c)], axis=-1)
    sin_t = jnp.concatenate([-jnp.sin(ang_r), jnp.sin(ang_r), -jnp.sin(ang_c), jnp.sin(ang_c)], axis=-1)
    return cos_t, sin_t


def _decay_rates():
    min_decay = math.log(DECAY_TARGET) / SLOW_DECAY_PCT
    max_decay = math.log(DECAY_TARGET) / FAST_DECAY_PCT
    deltas = jnp.tile(jnp.linspace(min_decay, max_decay, HYENA_WIDTH, dtype=F32), 2)
    return jnp.abs(deltas)[None, :]


def kernel(x, c, ctx, c_ctx, w_mod, b_mod, norm_g, w_in, attn_sink, hy_conv_w, hy_conv_b,
           hy_fw1, hy_fb1, hy_ff1, hy_fw2, hy_fb2, hy_ff2, hy_fw3, hy_fb3, hy_ff3, hy_fw4,
           hy_bias, w_attn_out, w_hyena_out, w_out, w_ff1, w_ff2):
    b, l, d = x.shape
    cl = ctx.shape[1]
    assert d == D_MODEL and l % 1024 == 0 and cl % 256 == 0 and b + 1 <= MOD_ROWS

    cos_t, sin_t = _rope_tables(l)
    decay = _decay_rates()
    tables_lat = _fft_tables(l, FFT_N2)
    tables_ctx = _fft_tables(cl, 1)
    z_lat = _filter_features(l)
    z_ctx = _filter_features(cl)

    c_rows = jnp.concatenate([c, c_ctx[None, :], jnp.zeros((MOD_ROWS - b - 1, d), F32)], axis=0)
    x_lat = x.reshape(b * l, d)
    x_ctx = ctx.reshape(b * cl, d)
    tm_ctx = b * cl
    w_aob = w_attn_out.astype(BF16)
    w_hob = w_hyena_out.astype(BF16)
    w_ob = w_out.astype(BF16)

    for layer in range(DEPTH):
        last = layer == DEPTH - 1
        mod = _modulation(c_rows, w_mod, layer, b_mod[layer])
        mod_lat = [mod[:b, k * d:(k + 1) * d].reshape(b, 1, d) for k in range(N_MOD)]
        mod_ctx = [mod[b:b + 1, k * d:(k + 1) * d].reshape(1, 1, d) for k in range(N_MOD)]
        sh1, sc1, g1, sh2, sc2, g2 = mod_lat
        csh1, csc1, cg1, csh2, csc2, cg2 = mod_ctx
        g = norm_g[layer]
        fparams = (hy_fw1[layer], hy_fb1[layer], hy_ff1[layer], hy_fw2[layer], hy_fb2[layer],
                   hy_ff2[layer], hy_fw3[layer], hy_fb3[layer], hy_ff3[layer])
        bias_d = hy_bias[layer].reshape(1, HYENA_WIDTH)
        conv_b = hy_conv_b[layer].reshape(1, 3 * HYENA_WIDTH)

        p_lat = _normproj(x_lat, g[0], sc1, sh1, w_in, layer, 0, IN_WIDTH, 2048, 512)
        if last:
            kv_ctx = _normproj(x_ctx, g[0], csc1, csh1, w_in, layer, Q_END, V_END - Q_END,
                               tm_ctx, 512)
            kx, vx = kv_ctx[:, :KV_WIDTH], kv_ctx[:, KV_WIDTH:]
        else:
            p_ctx = _normproj(x_ctx, g[0], csc1, csh1, w_in, layer, 0, IN_WIDTH, tm_ctx, 512)
            kx, vx = p_ctx[:, Q_END:K_END], p_ctx[:, K_END:V_END]
        kx = kx.reshape(b, cl, KV_WIDTH)
        vx = vx.reshape(b, cl, KV_WIDTH)

        p3 = p_lat.reshape(b, l, IN_WIDTH)
        attn = _window_attention(p3, kx, vx, attn_sink[layer], cos_t, sin_t)
        kf = _filter_spectrum(_filter_mlp(z_lat, *fparams), hy_fw4[layer], decay, tables_lat)
        hy = _hyena(p3, V_END, hy_conv_w[layer], conv_b, bias_d, kf, tables_lat)
        mix = _merge(attn.reshape(b * l, ATTN_WIDTH), hy.reshape(b * l, HYENA_WIDTH), p_lat,
                     w_aob, w_hob, layer, 512)
        x_lat = _outproj(mix, w_ob, layer, x_lat, g[1], g1, 512, D_MODEL)

        if not last:
            pc3 = p_ctx.reshape(b, cl, IN_WIDTH)
            attn_c = _context_attention(pc3, attn_sink[layer])
            kf_c = _filter_spectrum(_filter_mlp(z_ctx, *fparams), hy_fw4[layer], decay, tables_ctx)
            hy_c = _hyena(pc3, V_END, hy_conv_w[layer], conv_b, bias_d, kf_c, tables_ctx)
            mix_c = _merge(attn_c.reshape(b * cl, ATTN_WIDTH), hy_c.reshape(b * cl, HYENA_WIDTH),
                           p_ctx, w_aob, w_hob, layer, tm_ctx)
            x_ctx = _outproj(mix_c, w_ob, layer, x_ctx, g[1], cg1, tm_ctx, D_MODEL)
            x_ctx = _mlp(x_ctx, g[2], csc2, csh2, w_ff1, w_ff2, layer, g[3], cg2, tm_ctx, 512)

        x_lat = _mlp(x_lat, g[2], sc2, sh2, w_ff1, w_ff2, layer, g[3], g2, 1024, 512)
    return x_lat.reshape(b, l, d)
```

```python
import functools
import math

import jax
import jax.numpy as jnp
import numpy as np
from jax import lax
from jax.experimental import pallas as pl
from jax.experimental.pallas import tpu as pltpu

F32 = jnp.float32
BF16 = jnp.bfloat16

D_MODEL = 2048
DEPTH = 2
GRID_W = 64
HEAD_DIM = 128
N_Q_HEADS = 8
N_KV_HEADS = 2
Q_GROUP = N_Q_HEADS // N_KV_HEADS
ATTN_WIDTH = N_Q_HEADS * HEAD_DIM
KV_WIDTH = N_KV_HEADS * HEAD_DIM
BLOCK = 128
ROPE_THETA = 10000.0
HYENA_WIDTH = 1024
FILTER_EMB = 33
FILTER_HIDDEN = 64
DECAY_TARGET = 1e-2
FAST_DECAY_PCT = 0.3
SLOW_DECAY_PCT = 1.5
D_FF = 4 * D_MODEL
EPS = 1e-6
N_MOD = 6
NEG_INF = -1e30
Q_END = ATTN_WIDTH
K_END = Q_END + KV_WIDTH
V_END = K_END + KV_WIDTH
HY_END = V_END + 3 * HYENA_WIDTH
GA_END = HY_END + D_MODEL
GH_END = GA_END + D_MODEL
IN_WIDTH = GH_END
LOG2E = math.log2(math.e)
LOGIT_SCALE = HEAD_DIM ** -0.5 * LOG2E

LANES = 128
VMEM_LIMIT = 56 * 1024 * 1024

FFT_N2 = 16
FFT_KG = 16
PROLOGUE_ROWS = 256
NORM_STEPS = 16
MLP_COLS = 512
GATE_COLS = 512
HY_CB = 128
MOD_ROWS = 8


def _cparams(sem):
    return pltpu.CompilerParams(dimension_semantics=sem, vmem_limit_bytes=VMEM_LIMIT)


def _single(block_shape, index_map):
    return pl.BlockSpec(block_shape, index_map, pipeline_mode=pl.Buffered(1))


def _rms(x):
    return x * lax.rsqrt(jnp.mean(x * x, axis=-1, keepdims=True) + EPS)


def _row_chunks(nrows, fn):
    def chunk(r, carry):
        fn(pl.ds(pl.multiple_of(r * PROLOGUE_ROWS, PROLOGUE_ROWS), PROLOGUE_ROWS))
        return carry

    lax.fori_loop(0, nrows // PROLOGUE_ROWS, chunk, 0)


def _mod_kernel(c_ref, w_ref, b_ref, o_ref):
    c = c_ref[...]
    s = c * jax.nn.sigmoid(c)
    o_ref[...] = jnp.dot(s.astype(BF16), w_ref[...].astype(BF16),
                         preferred_element_type=F32) + b_ref[...]


def _modulation(c_rows, w, layer, b):
    n = w.shape[2]
    tn = 1024
    return pl.pallas_call(
        _mod_kernel,
        grid=(n // tn,),
        in_specs=[pl.BlockSpec((MOD_ROWS, D_MODEL), lambda j: (0, 0)),
                  pl.BlockSpec((None, D_MODEL, tn), lambda j: (layer, 0, j)),
                  pl.BlockSpec((1, tn), lambda j: (0, j))],
        out_specs=pl.BlockSpec((MOD_ROWS, tn), lambda j: (0, j)),
        out_shape=jax.ShapeDtypeStruct((MOD_ROWS, n), F32),
        name="modulation",
        compiler_params=_cparams(("arbitrary",)),
    )(c_rows, w, b.reshape(1, n))


def _normproj_kernel(x0_ref, sc0_ref, sh0_ref, x_ref, g_ref, sc_ref, sh_ref, w_ref, o_ref,
                     ha_ref, hb_ref, *, pro_steps):
    i = pl.program_id(0)
    j = pl.program_id(1)
    chunk = x_ref.shape[0]
    g = g_ref[...]

    def norm(x, sc_r, sh_r):
        return (_rms(x) * g * (1.0 + sc_r[...]) + sh_r[...]).astype(BF16)

    @pl.when((i == 0) & (j == 0))
    def _():
        def first(rows):
            ha_ref[rows, :] = norm(x0_ref[rows, :], sc0_ref, sh0_ref)

        _row_chunks(x0_ref.shape[0], first)

    def step(cur_ref, next_ref):
        rows = pl.ds(pl.multiple_of(jnp.minimum(j, pro_steps - 1) * chunk, chunk), chunk)
        next_ref[rows, :] = norm(x_ref[...], sc_ref, sh_ref)
        o_ref[...] = jnp.dot(cur_ref[...], w_ref[...].astype(BF16), preferred_element_type=F32)

    @pl.when(i % 2 == 0)
    def _():
        step(ha_ref, hb_ref)

    @pl.when(i % 2 == 1)
    def _():
        step(hb_ref, ha_ref)


def _normproj(x, g, scale, shift, w, layer, col0, n, tm, tn):
    m = x.shape[0]
    nb = m // tm
    nj = n // tn
    j0 = col0 // tn
    nmod = scale.shape[0]
    blocks_per_mod = m // nmod // tm
    pro_steps = min(nj, NORM_STEPS)
    chunk = tm // pro_steps
    assert chunk * pro_steps == tm and chunk % 16 == 0 and tm % PROLOGUE_ROWS == 0

    def next_block(i):
        return jnp.minimum(i + 1, nb - 1)

    first_mod = pl.BlockSpec((None, 1, D_MODEL), lambda i, j: (0, 0, 0))
    next_mod = pl.BlockSpec((None, 1, D_MODEL),
                            lambda i, j: (next_block(i) // blocks_per_mod, 0, 0))
    return pl.pallas_call(
        functools.partial(_normproj_kernel, pro_steps=pro_steps),
        grid=(nb, nj),
        in_specs=[_single((tm, D_MODEL), lambda i, j: (0, 0)),
                  first_mod, first_mod,
                  pl.BlockSpec((chunk, D_MODEL),
                               lambda i, j: (next_block(i) * pro_steps
                                             + jnp.minimum(j, pro_steps - 1), 0)),
                  pl.BlockSpec((1, D_MODEL), lambda i, j: (0, 0)),
                  next_mod, next_mod,
                  pl.BlockSpec((None, D_MODEL, tn), lambda i, j: (layer, 0, j0 + j))],
        out_specs=pl.BlockSpec((tm, tn), lambda i, j: (i, j)),
        out_shape=jax.ShapeDtypeStruct((m, n), F32),
        scratch_shapes=[pltpu.VMEM((tm, D_MODEL), BF16), pltpu.VMEM((tm, D_MODEL), BF16)],
        name="normproj",
        compiler_params=_cparams(("arbitrary", "arbitrary")),
    )(x, scale, shift, x, g.reshape(1, D_MODEL), scale, shift, w)


def _rope(x, cos, sin_signed, first_half):
    rot = jnp.where(first_half, pltpu.roll(x, HEAD_DIM - 32, axis=1), pltpu.roll(x, 32, axis=1))
    return x * cos + rot * sin_signed


def _softmax_pv(s, sink, v):
    m = jnp.maximum(jnp.max(s, axis=-1, keepdims=True), sink)
    e = jnp.exp2(s - m)
    denom = jnp.sum(e, axis=-1, keepdims=True) + jnp.exp2(sink - m)
    o = jnp.dot(e.astype(BF16), v, preferred_element_type=F32)
    return o / denom


def _band_bias(nctx):
    qi = (np.arange(Q_GROUP * BLOCK) % BLOCK)[None, :, None]
    kj = np.arange(3 * BLOCK + nctx)[None, None, :]
    variant = np.arange(3)[:, None, None]
    in_prev = kj < BLOCK
    in_next = (kj >= 2 * BLOCK) & (kj < 3 * BLOCK)
    valid = np.where(in_prev, (kj >= qi) & (variant != 0),
                     np.where(in_next, (kj - 2 * BLOCK <= qi) & (variant != 2), True))
    return jnp.asarray(np.where(valid, 0.0, NEG_INF).astype(np.float32))


def _win_attn_kernel(sink_ref, q_ref, kvp_ref, kvc_ref, kvn_ref, kx_ref, vx_ref, cos_ref, sin_ref,
                     bias_ref, o_ref, *, nb):
    i = pl.program_id(1)
    lane = lax.broadcasted_iota(jnp.int32, (BLOCK, HEAD_DIM), 1)
    first_half = (lane % 64) < 32

    def table(ref, blk):
        return ref[pl.ds(pl.multiple_of(blk * BLOCK, BLOCK), BLOCK), :]

    ip = jnp.maximum(i - 1, 0)
    inx = jnp.minimum(i + 1, nb - 1)
    cos_c, sin_c = table(cos_ref, i), table(sin_ref, i)
    cos_p, sin_p = table(cos_ref, ip), table(sin_ref, ip)
    cos_n, sin_n = table(cos_ref, inx), table(sin_ref, inx)

    rows = Q_GROUP * BLOCK
    head_in_group = lax.broadcasted_iota(jnp.int32, (rows, 1), 0) // BLOCK
    bias = bias_ref[jnp.where(i == 0, 0, jnp.where(i == nb - 1, 2, 1))]

    for h in range(N_KV_HEADS):
        hs = slice(h * HEAD_DIM, (h + 1) * HEAD_DIM)
        vs = slice(KV_WIDTH + h * HEAD_DIM, KV_WIDTH + (h + 1) * HEAD_DIM)
        k = jnp.concatenate([
            _rope(kvp_ref[:, hs], cos_p, sin_p, first_half),
            _rope(kvc_ref[:, hs], cos_c, sin_c, first_half),
            _rope(kvn_ref[:, hs], cos_n, sin_n, first_half),
            kx_ref[:, hs]], axis=0).astype(BF16)
        v = jnp.concatenate([kvp_ref[:, vs], kvc_ref[:, vs], kvn_ref[:, vs], vx_ref[:, hs]],
                            axis=0).astype(BF16)
        heads = [h * Q_GROUP + g for g in range(Q_GROUP)]
        q = jnp.concatenate(
            [_rope(q_ref[:, hd * HEAD_DIM:(hd + 1) * HEAD_DIM], cos_c, sin_c, first_half)
             for hd in heads], axis=0).astype(BF16)
        sink = jnp.zeros((rows, 1), F32)
        for g, hd in enumerate(heads):
            sink = jnp.where(head_in_group == g, sink_ref[hd] * LOG2E, sink)
        s = lax.dot_general(q, k, (((1,), (1,)), ((), ())), preferred_element_type=F32)
        o = _softmax_pv(s * LOGIT_SCALE + bias, sink, v).astype(o_ref.dtype)
        for g, hd in enumerate(heads):
            o_ref[:, hd * HEAD_DIM:(hd + 1) * HEAD_DIM] = o[g * BLOCK:(g + 1) * BLOCK]


def _window_attention(p, kx, vx, sink, cos_t, sin_t):
    b, l, _ = p.shape
    c = kx.shape[1]
    nb = l // BLOCK
    def kv_spec(shift):
        return pl.BlockSpec((None, BLOCK, 2 * KV_WIDTH),
                            lambda bi, i: (bi, jnp.clip(i + shift, 0, nb - 1), Q_END // (2 * KV_WIDTH)))

    assert nb >= 2
    bias = _band_bias(c)
    ctx_spec = pl.BlockSpec((None, c, KV_WIDTH), lambda bi, i: (bi, 0, 0))
    tab_spec = _single((l, HEAD_DIM), lambda bi, i: (0, 0))
    return pl.pallas_call(
        functools.partial(_win_attn_kernel, nb=nb),
        grid=(b, nb),
        in_specs=[pl.BlockSpec(memory_space=pltpu.SMEM),
                  pl.BlockSpec((None, BLOCK, ATTN_WIDTH), lambda bi, i: (bi, i, 0)),
                  kv_spec(-1), kv_spec(0), kv_spec(1),
                  ctx_spec, ctx_spec, tab_spec, tab_spec,
                  _single(bias.shape, lambda bi, i: (0, 0, 0))],
        out_specs=pl.BlockSpec((None, BLOCK, ATTN_WIDTH), lambda bi, i: (bi, i, 0)),
        out_shape=jax.ShapeDtypeStruct((b, l, ATTN_WIDTH), BF16),
        name="window_attention",
        compiler_params=_cparams(("arbitrary", "arbitrary")),
    )(sink, p, p, p, p, kx, vx, cos_t, sin_t, bias)


def _ctx_attn_kernel(sink_ref, q_ref, k_ref, v_ref, o_ref):
    for h in range(N_KV_HEADS):
        hs = slice(h * HEAD_DIM, (h + 1) * HEAD_DIM)
        k = k_ref[:, hs].astype(BF16)
        v = v_ref[:, hs].astype(BF16)
        for g in range(Q_GROUP):
            head = h * Q_GROUP + g
            cs = slice(head * HEAD_DIM, (head + 1) * HEAD_DIM)
            q = q_ref[:, cs].astype(BF16)
            s = lax.dot_general(q, k, (((1,), (1,)), ((), ())), preferred_element_type=F32)
            o_ref[:, cs] = _softmax_pv(s * LOGIT_SCALE, sink_ref[head] * LOG2E, v).astype(o_ref.dtype)


def _context_attention(p, sink):
    b, c, _ = p.shape
    return pl.pallas_call(
        _ctx_attn_kernel,
        grid=(b,),
        in_specs=[pl.BlockSpec(memory_space=pltpu.SMEM),
                  pl.BlockSpec((None, c, ATTN_WIDTH), lambda bi: (bi, 0, 0)),
                  pl.BlockSpec((None, c, KV_WIDTH), lambda bi: (bi, 0, Q_END // KV_WIDTH)),
                  pl.BlockSpec((None, c, KV_WIDTH), lambda bi: (bi, 0, K_END // KV_WIDTH))],
        out_specs=pl.BlockSpec((None, c, ATTN_WIDTH), lambda bi: (bi, 0, 0)),
        out_shape=jax.ShapeDtypeStruct((b, c, ATTN_WIDTH), BF16),
        name="context_attention",
        compiler_params=_cparams(("arbitrary",)),
    )(sink, p, p, p)


def _fft_tables(l, n2):
    n = 2 * l
    n1 = n // n2
    k1 = np.arange(n1 // 2, dtype=np.int64)[None, :, None]
    t = (n2 * np.arange(n1 // 2, dtype=np.int64)[None, None, :]
         + np.arange(n2, dtype=np.int64)[:, None, None])
    ang = (((2 * k1 + 1) * t) % (2 * n)).astype(np.float64) * (math.pi / n)
    fwd1 = np.concatenate([np.cos(ang), -np.sin(ang)], axis=1)
    inv1 = np.swapaxes(fwd1, 1, 2) * (2.0 / n)
    as_operand = lambda a: jnp.asarray(np.ascontiguousarray(a, dtype=np.float32)).astype(BF16)
    return as_operand(fwd1), as_operand(inv1)


def _filter_features(l):
    bands = (FILTER_EMB - 1) // 2
    t = np.linspace(0.0, 1.0, l)[:, None]
    w = 2 * math.pi * np.arange(l)[:, None] / l
    f = np.linspace(1e-4, bands - 1, bands)[None, :]
    z = np.concatenate([t, np.cos(f * w), -np.sin(f * w)], axis=-1)
    z = np.pad(z, ((0, 0), (0, FILTER_HIDDEN - FILTER_EMB)))
    return jnp.asarray(z.astype(np.float32))


def _filter_mlp_kernel(z_ref, w1_ref, b1_ref, f1_ref, w2_ref, b2_ref, f2_ref,
                       w3_ref, b3_ref, f3_ref, o_ref):
    h = jnp.sin(f1_ref[...] * (jnp.dot(z_ref[...], w1_ref[...], preferred_element_type=F32)
                               + b1_ref[...]))
    h = jnp.sin(f2_ref[...] * (jnp.dot(h, w2_ref[...], preferred_element_type=F32) + b2_ref[...]))
    o_ref[...] = jnp.sin(f3_ref[...] * (jnp.dot(h, w3_ref[...], preferred_element_type=F32)
                                        + b3_ref[...]))


def _filter_mlp(z, fw1, fb1, ff1, fw2, fb2, ff2, fw3, fb3, ff3):
    l = z.shape[0]
    row = lambda a: a.reshape(1, FILTER_HIDDEN)
    w1 = jnp.pad(fw1, ((0, FILTER_HIDDEN - FILTER_EMB), (0, 0)))
    return pl.pallas_call(
        _filter_mlp_kernel,
        out_shape=jax.ShapeDtypeStruct((l, FILTER_HIDDEN), F32),
        name="filter_mlp",
        compiler_params=pltpu.CompilerParams(vmem_limit_bytes=VMEM_LIMIT),
    )(z, w1, row(fb1), row(ff1), fw2, row(fb2), row(ff2), fw3, row(fb3), row(ff3))


def _cmul_root16(z, p, inverse):
    zr, zi = z
    p = p % 16
    if inverse:
        p = (16 - p) % 16
    if p == 0:
        return zr, zi
    if p == 4:
        return zi, -zr
    if p == 8:
        return -zr, -zi
    if p == 12:
        return -zi, zr
    c = math.cos(2.0 * math.pi * p / 16)
    s = -math.sin(2.0 * math.pi * p / 16)
    return zr * c - zi * s, zr * s + zi * c


def _dft4(z, inverse):
    (ar, ai), (br, bi), (cr, ci), (dr, di) = z
    t0r, t0i = ar + cr, ai + ci
    t1r, t1i = ar - cr, ai - ci
    t2r, t2i = br + dr, bi + di
    t3r, t3i = br - dr, bi - di
    y0 = (t0r + t2r, t0i + t2i)
    y2 = (t0r - t2r, t0i - t2i)
    minus_i_t3 = (t1r + t3i, t1i - t3r)
    plus_i_t3 = (t1r - t3i, t1i + t3r)
    return [y0, plus_i_t3, y2, minus_i_t3] if inverse else [y0, minus_i_t3, y2, plus_i_t3]


def _dft_digit(z, inverse):
    if len(z) == 1:
        return z
    assert len(z) == 16
    t = [_dft4([z[4 * a + b] for a in range(4)], inverse) for b in range(4)]
    out = [None] * 16
    for c in range(4):
        y = _dft4([_cmul_root16(t[b][c], b * c, inverse) for b in range(4)], inverse)
        for d in range(4):
            out[c + 4 * d] = y[d]
    return out


def _fft_stage1(src_ref, f1_ref, a_ref):
    n2cnt = f1_ref.shape[0]
    groups = a_ref.shape[0]
    n1 = groups * FFT_KG
    half = src_ref.shape[0] // n2cnt

    def body(n2, carry):
        rows = pl.ds(n2, half, stride=n2cnt) if n2cnt > 1 else pl.ds(0, half)
        res = jnp.dot(f1_ref[n2], src_ref[rows, :].astype(BF16), preferred_element_type=F32)
        for g in range(groups):
            a_ref[g, 0, n2] = res[g * FFT_KG:(g + 1) * FFT_KG]
            a_ref[g, 1, n2] = res[n1 + g * FFT_KG:n1 + (g + 1) * FFT_KG]
        return carry

    if n2cnt == 1:
        body(0, 0)
    else:
        lax.fori_loop(0, n2cnt, body, 0, unroll=4)


def _load_digits(ref, g, rows):
    return [(ref[g, 0, d, rows, :], ref[g, 1, d, rows, :]) for d in range(ref.shape[2])]


def _filter_spec_kernel(h3_ref, wf_ref, wb_ref, df_ref, db_ref, f1_ref, kf_ref,
                        hf_ref, hb_ref, a_ref, *, l):
    groups = a_ref.shape[0]
    row = lax.broadcasted_iota(jnp.int32, (l, 1), 0)
    t = row.astype(F32) * (1.0 / (l - 1))
    h3 = h3_ref[...]
    hf = jnp.dot(h3, wf_ref[...], preferred_element_type=F32) * jnp.exp(-t * df_ref[...])
    hb = jnp.dot(h3, wb_ref[...], preferred_element_type=F32) * jnp.exp(-t * db_ref[...])
    hb = jnp.where(row > 0, hb, 0.0)
    norm = jnp.sum(jnp.abs(hf), axis=0, keepdims=True) + jnp.sum(jnp.abs(hb), axis=0, keepdims=True)
    hf_ref[...] = hf / norm
    hb_ref[...] = hb / norm

    def forward(g, carry):
        for r in range(0, FFT_KG, 8):
            rows = pl.ds(r, 8)
            for k2, (xr, xi) in enumerate(_dft_digit(_load_digits(a_ref, g, rows), False)):
                kf_ref[g, 0, k2, rows, :] = xr
                kf_ref[g, 1, k2, rows, :] = xi
        return carry

    def backward(g, carry):
        for r in range(0, FFT_KG, 8):
            rows = pl.ds(r, 8)
            for k2, (xr, xi) in enumerate(_dft_digit(_load_digits(a_ref, g, rows), False)):
                kf_ref[g, 0, k2, rows, :] = kf_ref[g, 0, k2, rows, :] + xr
                kf_ref[g, 1, k2, rows, :] = kf_ref[g, 1, k2, rows, :] - xi
        return carry

    _fft_stage1(hf_ref, f1_ref, a_ref)
    lax.fori_loop(0, groups, forward, 0)
    _fft_stage1(hb_ref, f1_ref, a_ref)
    lax.fori_loop(0, groups, backward, 0)


def _filter_spectrum(h3, fw4, decay, tables):
    l = h3.shape[0]
    f1, _ = tables
    n2 = f1.shape[0]
    groups = f1.shape[1] // 2 // FFT_KG
    nblk = HYENA_WIDTH // HY_CB
    spec_shape = (groups, 2, n2, FFT_KG)
    return pl.pallas_call(
        functools.partial(_filter_spec_kernel, l=l),
        grid=(nblk,),
        in_specs=[_single((l, FILTER_HIDDEN), lambda c: (0, 0)),
                  pl.BlockSpec((FILTER_HIDDEN, HY_CB), lambda c: (0, c)),
                  pl.BlockSpec((FILTER_HIDDEN, HY_CB), lambda c: (0, c + nblk)),
                  pl.BlockSpec((1, HY_CB), lambda c: (0, c)),
                  pl.BlockSpec((1, HY_CB), lambda c: (0, c + nblk)),
                  _single(f1.shape, lambda c: (0, 0, 0))],
        out_specs=pl.BlockSpec(spec_shape + (HY_CB,), lambda c: (0, 0, 0, 0, c)),
        out_shape=jax.ShapeDtypeStruct(spec_shape + (HYENA_WIDTH,), F32),
        scratch_shapes=[pltpu.VMEM((l, HY_CB), F32), pltpu.VMEM((l, HY_CB), F32),
                        pltpu.VMEM(spec_shape + (HY_CB,), F32)],
        name="filter_spectrum",
        compiler_params=_cparams(("arbitrary",)),
    )(h3, fw4, fw4, decay, decay, f1)


def _short_conv(src_ref, w_ref, b_ref, l, rows):
    w0, w1, w2, b = w_ref[0:1, :], w_ref[1:2, :], w_ref[2:3, :], b_ref[...]
    r = lax.broadcasted_iota(jnp.int32, (rows, 1), 0)
    for s in range(0, l, rows):
        cur = src_ref[s:s + rows, :]
        if s > 0:
            prev = src_ref[s - 1:s - 1 + rows, :]
        else:
            prev = jnp.where(r == 0, 0.0, pltpu.roll(cur, 1, axis=0))
        if s + rows < l:
            nxt = src_ref[s + 1:s + 1 + rows, :]
        else:
            nxt = jnp.where(r == rows - 1, 0.0, pltpu.roll(cur, rows - 1, axis=0))
        yield s, prev * w0 + cur * w1 + nxt * w2 + b


def _hyena_kernel(x0_ref, x1_ref, v_ref, w0_ref, w1_ref, wv_ref, b0_ref, b1_ref, bv_ref,
                  bias_ref, kf_ref, f1_ref, g1_ref, o_ref, x0c_ref, vg_ref, a_ref, *, l):
    groups = a_ref.shape[0]
    n2cnt = a_ref.shape[2]
    cb = a_ref.shape[-1]
    n1 = groups * FFT_KG
    half = l // n2cnt
    rows = min(l, 512)

    for s, u in _short_conv(x0_ref, w0_ref, b0_ref, l, rows):
        x0c_ref[s:s + rows, :] = u
    for (s, u1), (_, uv) in zip(_short_conv(x1_ref, w1_ref, b1_ref, l, rows),
                                _short_conv(v_ref, wv_ref, bv_ref, l, rows)):
        vg_ref[s:s + rows, :] = uv * u1

    _fft_stage1(vg_ref, f1_ref, a_ref)

    def spectrum(g, carry):
        for r in range(0, FFT_KG, 8):
            rws = pl.ds(r, 8)
            x = _dft_digit(_load_digits(a_ref, g, rws), False)
            k = _load_digits(kf_ref, g, rws)
            y = [(xr * kr - xi * ki, xr * ki + xi * kr) for (xr, xi), (kr, ki) in zip(x, k)]
            for d, (br, bi) in enumerate(_dft_digit(y, True)):
                a_ref[g, 0, d, rws, :] = br
                a_ref[g, 1, d, rws, :] = bi
        return carry

    lax.fori_loop(0, groups, spectrum, 0)

    def synth(n2, carry):
        b = jnp.concatenate([a_ref[:, 0, n2].reshape(n1, cb), a_ref[:, 1, n2].reshape(n1, cb)],
                            axis=0).astype(BF16)
        y = jnp.dot(g1_ref[n2], b, preferred_element_type=F32)
        idx = pl.ds(n2, half, stride=n2cnt) if n2cnt > 1 else pl.ds(0, half)
        o_ref[idx, :] = y
        return carry

    if n2cnt == 1:
        synth(0, 0)
    else:
        lax.fori_loop(0, n2cnt, synth, 0, unroll=4)

    bias = bias_ref[...]
    for s in range(0, l, rows):
        blk = slice(s, s + rows)
        o_ref[blk, :] = (o_ref[blk, :] + vg_ref[blk, :] * bias) * x0c_ref[blk, :]


def _hyena(p, col0, conv_w, conv_b, bias_d, kf, tables):
    b, l, _ = p.shape
    f1, g1 = tables
    nblk = HYENA_WIDTH // HY_CB
    c0 = col0 // HY_CB
    spec_block = kf.shape[:-1] + (HY_CB,)

    def slab(part):
        return pl.BlockSpec((None, l, HY_CB), lambda c, bi: (bi, 0, c0 + part * nblk + c))

    def cw(part):
        return pl.BlockSpec((3, HY_CB), lambda c, bi: (0, part * nblk + c))

    def cbias(part):
        return pl.BlockSpec((1, HY_CB), lambda c, bi: (0, part * nblk + c))

    return pl.pallas_call(
        functools.partial(_hyena_kernel, l=l),
        grid=(nblk, b),
        in_specs=[slab(0), slab(1), slab(2), cw(0), cw(1), cw(2), cbias(0), cbias(1), cbias(2),
                  pl.BlockSpec((1, HY_CB), lambda c, bi: (0, c)),
                  pl.BlockSpec(spec_block, lambda c, bi: (0, 0, 0, 0, c)),
                  _single(f1.shape, lambda c, bi: (0, 0, 0)),
                  _single(g1.shape, lambda c, bi: (0, 0, 0))],
        out_specs=pl.BlockSpec((None, l, HY_CB), lambda c, bi: (bi, 0, c)),
        out_shape=jax.ShapeDtypeStruct((b, l, HYENA_WIDTH), F32),
        scratch_shapes=[pltpu.VMEM((l, HY_CB), F32), pltpu.VMEM((l, HY_CB), F32),
                        pltpu.VMEM(spec_block, F32)],
        name="hyena_conv",
        compiler_params=_cparams(("arbitrary", "arbitrary")),
    )(p, p, p, conv_w, conv_w, conv_w, conv_b, conv_b, conv_b, bias_d, kf, f1, g1)


def _merge_kernel(a_ref, h_ref, wa_ref, wh_ref, *refs):
    gate_refs, o_ref = refs[:-1], refs[-1]
    nt = len(gate_refs) // 2
    tg = gate_refs[0].shape[1]
    a = a_ref[...]
    h = h_ref[...].astype(BF16)
    for k in range(nt):
        cs = slice(k * tg, (k + 1) * tg)
        ya = jnp.dot(a, wa_ref[:, cs], preferred_element_type=F32)
        yh = jnp.dot(h, wh_ref[:, cs], preferred_element_type=F32)
        o_ref[:, cs] = (jax.nn.sigmoid(gate_refs[k][...]) * ya
                        + jax.nn.sigmoid(gate_refs[nt + k][...]) * yh).astype(o_ref.dtype)


def _merge(attn, hy, p, w_ao, w_ho, layer, tm):
    m = attn.shape[0]
    nt = D_MODEL // GATE_COLS

    def gate_spec(col0, k):
        return pl.BlockSpec((tm, GATE_COLS), lambda i: (i, col0 // GATE_COLS + k))

    gate_specs = ([gate_spec(HY_END, k) for k in range(nt)]
                  + [gate_spec(GA_END, k) for k in range(nt)])
    return pl.pallas_call(
        _merge_kernel,
        grid=(m // tm,),
        in_specs=[pl.BlockSpec((tm, ATTN_WIDTH), lambda i: (i, 0)),
                  pl.BlockSpec((tm, HYENA_WIDTH), lambda i: (i, 0)),
                  _single((None, ATTN_WIDTH, D_MODEL), lambda i: (layer, 0, 0)),
                  _single((None, HYENA_WIDTH, D_MODEL), lambda i: (layer, 0, 0))] + gate_specs,
        out_specs=pl.BlockSpec((tm, D_MODEL), lambda i: (i, 0)),
        out_shape=jax.ShapeDtypeStruct((m, D_MODEL), BF16),
        name="branch_merge",
        compiler_params=_cparams(("arbitrary",)),
    )(attn, hy, w_ao, w_ho, *([p] * (2 * nt)))


def _outproj_kernel(mix_ref, w_ref, x_ref, g_ref, gate_ref, o_ref, y_ref, *, tn):
    j = pl.program_id(1)
    nj = y_ref.shape[0]
    y_ref[j] = jnp.dot(mix_ref[...], w_ref[...], preferred_element_type=F32)

    @pl.when(j == nj - 1)
    def _():
        ss = sum(jnp.sum(jnp.square(y_ref[k]), axis=-1, keepdims=True) for k in range(nj))
        r = lax.rsqrt(ss * (1.0 / D_MODEL) + EPS)
        for k in range(nj):
            cs = slice(k * tn, (k + 1) * tn)
            o_ref[:, cs] = x_ref[:, cs] + gate_ref[:, cs] * ((y_ref[k] * r) * g_ref[:, cs])


def _outproj(mix, w_o, layer, x, g, gate, tm, tn):
    m = x.shape[0]
    nmod = gate.shape[0]
    blocks_per_mod = m // nmod // tm
    return pl.pallas_call(
        functools.partial(_outproj_kernel, tn=tn),
        grid=(m // tm, D_MODEL // tn),
        in_specs=[pl.BlockSpec((tm, D_MODEL), lambda i, j: (i, 0)),
                  pl.BlockSpec((None, D_MODEL, tn), lambda i, j: (layer, 0, j),
                               pipeline_mode=pl.Buffered(1) if tn == D_MODEL else None),
                  pl.BlockSpec((tm, D_MODEL), lambda i, j: (i, 0)),
                  pl.BlockSpec((1, D_MODEL), lambda i, j: (0, 0)),
                  pl.BlockSpec((None, 1, D_MODEL), lambda i, j: (i // blocks_per_mod, 0, 0))],
        out_specs=pl.BlockSpec((tm, D_MODEL), lambda i, j: (i, 0)),
        out_shape=jax.ShapeDtypeStruct((m, D_MODEL), F32),
        scratch_shapes=[pltpu.VMEM((D_MODEL // tn, tm, tn), F32)],
        name="out_projection",
        compiler_params=_cparams(("arbitrary", "arbitrary")),
    )(mix, w_o, x, g.reshape(1, D_MODEL), gate)


def _mlp_kernel(x_ref, gin_ref, sc_ref, sh_ref, w1_ref, w2_ref, gout_ref, gate_ref, o_ref, h_ref):
    j = pl.program_id(1)
    tm = x_ref.shape[0]

    @pl.when(j == 0)
    def _():
        gin, sc1, sh = gin_ref[...], 1.0 + sc_ref[...], sh_ref[...]

        def prologue(rows):
            h_ref[rows, :] = (_rms(x_ref[rows, :]) * gin * sc1 + sh).astype(BF16)
            o_ref[rows, :] = jnp.zeros((PROLOGUE_ROWS, D_MODEL), F32)

        _row_chunks(tm, prologue)

    a = jnp.dot(h_ref[...], w1_ref[...].astype(BF16), preferred_element_type=F32)
    a = jnp.square(jnp.maximum(a, 0.0)).astype(BF16)
    for n in range(0, D_MODEL, MLP_COLS):
        o_ref[:, n:n + MLP_COLS] += jnp.dot(a, w2_ref[:, n:n + MLP_COLS].astype(BF16),
                                            preferred_element_type=F32)

    @pl.when(j == pl.num_programs(1) - 1)
    def _():
        gout, gate = gout_ref[...], gate_ref[...]

        def epilogue(rows):
            o_ref[rows, :] = x_ref[rows, :] + gate * (_rms(o_ref[rows, :]) * gout)

        _row_chunks(tm, epilogue)


def _mlp(x, g_in, scale, shift, w1, w2, layer, g_out, gate, tm, tf):
    m = x.shape[0]
    nmod = gate.shape[0]
    blocks_per_mod = m // nmod // tm
    mod_spec = pl.BlockSpec((None, 1, D_MODEL), lambda i, j: (i // blocks_per_mod, 0, 0))
    row_spec = pl.BlockSpec((1, D_MODEL), lambda i, j: (0, 0))
    return pl.pallas_call(
        _mlp_kernel,
        grid=(m // tm, D_FF // tf),
        in_specs=[pl.BlockSpec((tm, D_MODEL), lambda i, j: (i, 0), pipeline_mode=pl.Buffered(1)),
                  row_spec, mod_spec, mod_spec,
                  pl.BlockSpec((None, D_MODEL, tf), lambda i, j: (layer, 0, j)),
                  pl.BlockSpec((None, tf, D_MODEL), lambda i, j: (layer, j, 0)),
                  row_spec, mod_spec],
        out_specs=pl.BlockSpec((tm, D_MODEL), lambda i, j: (i, 0)),
        out_shape=jax.ShapeDtypeStruct((m, D_MODEL), F32),
        scratch_shapes=[pltpu.VMEM((tm, D_MODEL), BF16)],
        name="channel_mlp",
        compiler_params=_cparams(("arbitrary", "arbitrary")),
    )(x, g_in.reshape(1, D_MODEL), scale, shift, w1, w2, g_out.reshape(1, D_MODEL), gate)


def _rope_tables(l):
    pos = np.arange(l)
    quarter = HEAD_DIM // 4
    freqs = ROPE_THETA ** (-np.arange(quarter) / quarter)
    ang_r = (pos // GRID_W)[:, None] * freqs[None, :]
    ang_c = (pos % GRID_W)[:, None] * freqs[None, :]
    cos_t = np.concatenate([np.cos(ang_r), np.cos(ang_r), np.cos(ang_c), np.cos(ang_c)], axis=-1)
    sin_t = np.concatenate([-np.sin(ang_r), np.sin(ang_r), -np.sin(ang_c), np.sin(ang_c)], axis=-1)
    return jnp.asarray(cos_t.astype(np.float32)), jnp.asarray(sin_t.astype(np.float32))


def _decay_rates():
    min_decay = math.log(DECAY_TARGET) / SLOW_DECAY_PCT
    max_decay = math.log(DECAY_TARGET) / FAST_DECAY_PCT
    deltas = np.tile(np.linspace(min_decay, max_decay, HYENA_WIDTH), 2)
    return jnp.asarray(np.abs(deltas)[None, :].astype(np.float32))


def kernel(x, c, ctx, c_ctx, w_mod, b_mod, norm_g, w_in, attn_sink, hy_conv_w, hy_conv_b,
           hy_fw1, hy_fb1, hy_ff1, hy_fw2, hy_fb2, hy_ff2, hy_fw3, hy_fb3, hy_ff3, hy_fw4,
           hy_bias, w_attn_out, w_hyena_out, w_out, w_ff1, w_ff2):
    b, l, d = x.shape
    cl = ctx.shape[1]
    assert d == D_MODEL and l % 1024 == 0 and cl % 256 == 0 and b + 1 <= MOD_ROWS

    cos_t, sin_t = _rope_tables(l)
    decay = _decay_rates()
    tables_lat = _fft_tables(l, FFT_N2)
    tables_ctx = _fft_tables(cl, 1)
    z_lat = _filter_features(l)
    z_ctx = _filter_features(cl)

    c_rows = jnp.concatenate([c, c_ctx[None, :], jnp.zeros((MOD_ROWS - b - 1, d), F32)], axis=0)
    x_lat = x.reshape(b * l, d)
    x_ctx = ctx.reshape(b * cl, d)
    tm_ctx = b * cl
    w_aob = w_attn_out.astype(BF16)
    w_hob = w_hyena_out.astype(BF16)
    w_ob = w_out.astype(BF16)

    for layer in range(DEPTH):
        last = layer == DEPTH - 1
        mod = _modulation(c_rows, w_mod, layer, b_mod[layer])
        mod_lat = [mod[:b, k * d:(k + 1) * d].reshape(b, 1, d) for k in range(N_MOD)]
        mod_ctx = [mod[b:b + 1, k * d:(k + 1) * d].reshape(1, 1, d) for k in range(N_MOD)]
        sh1, sc1, g1, sh2, sc2, g2 = mod_lat
        csh1, csc1, cg1, csh2, csc2, cg2 = mod_ctx
        g = norm_g[layer]
        fparams = (hy_fw1[layer], hy_fb1[layer], hy_ff1[layer], hy_fw2[layer], hy_fb2[layer],
                   hy_ff2[layer], hy_fw3[layer], hy_fb3[layer], hy_ff3[layer])
        bias_d = hy_bias[layer].reshape(1, HYENA_WIDTH)
        conv_b = hy_conv_b[layer].reshape(1, 3 * HYENA_WIDTH)

        p_lat = _normproj(x_lat, g[0], sc1, sh1, w_in, layer, 0, IN_WIDTH, 2048, 512)
        if last:
            kv_ctx = _normproj(x_ctx, g[0], csc1, csh1, w_in, layer, Q_END, V_END - Q_END,
                               tm_ctx, 512)
            kx, vx = kv_ctx[:, :KV_WIDTH], kv_ctx[:, KV_WIDTH:]
        else:
            p_ctx = _normproj(x_ctx, g[0], csc1, csh1, w_in, layer, 0, IN_WIDTH, tm_ctx, 512)
            kx, vx = p_ctx[:, Q_END:K_END], p_ctx[:, K_END:V_END]
        kx = kx.reshape(b, cl, KV_WIDTH)
        vx = vx.reshape(b, cl, KV_WIDTH)

        p3 = p_lat.reshape(b, l, IN_WIDTH)
        attn = _window_attention(p3, kx, vx, attn_sink[layer], cos_t, sin_t)
        kf = _filter_spectrum(_filter_mlp(z_lat, *fparams), hy_fw4[layer], decay, tables_lat)
        hy = _hyena(p3, V_END, hy_conv_w[layer], conv_b, bias_d, kf, tables_lat)
        mix = _merge(attn.reshape(b * l, ATTN_WIDTH), hy.reshape(b * l, HYENA_WIDTH), p_lat,
                     w_aob, w_hob, layer, 512)
        x_lat = _outproj(mix, w_ob, layer, x_lat, g[1], g1, 512, D_MODEL)

        if not last:
            pc3 = p_ctx.reshape(b, cl, IN_WIDTH)
            attn_c = _context_attention(pc3, attn_sink[layer])
            kf_c = _filter_spectrum(_filter_mlp(z_ctx, *fparams), hy_fw4[layer], decay, tables_ctx)
            hy_c = _hyena(pc3, V_END, hy_conv_w[layer], conv_b, bias_d, kf_c, tables_ctx)
            mix_c = _merge(attn_c.reshape(b * cl, ATTN_WIDTH), hy_c.reshape(b * cl, HYENA_WIDTH),
                           p_ctx, w_aob, w_hob, layer, tm_ctx)
            x_ctx = _outproj(mix_c, w_ob, layer, x_ctx, g[1], cg1, tm_ctx, D_MODEL)
            x_ctx = _mlp(x_ctx, g[2], csc2, csh2, w_ff1, w_ff2, layer, g[3], cg2, tm_ctx, 512)

        x_lat = _mlp(x_lat, g[2], sc2, sh2, w_ff1, w_ff2, layer, g[3], g2, 1024, 512)
    return x_lat.reshape(b, l, d)
```

```python
import functools
import math

import jax
import jax.numpy as jnp
import numpy as np
from jax import lax
from jax.experimental import pallas as pl
from jax.experimental.pallas import tpu as pltpu

F32 = jnp.float32
BF16 = jnp.bfloat16

D_MODEL = 2048
DEPTH = 2
GRID_W = 64
HEAD_DIM = 128
N_Q_HEADS = 8
N_KV_HEADS = 2
Q_GROUP = N_Q_HEADS // N_KV_HEADS
ATTN_WIDTH = N_Q_HEADS * HEAD_DIM
KV_WIDTH = N_KV_HEADS * HEAD_DIM
BLOCK = 128
ROPE_THETA = 10000.0
HYENA_WIDTH = 1024
FILTER_EMB = 33
FILTER_HIDDEN = 64
DECAY_TARGET = 1e-2
FAST_DECAY_PCT = 0.3
SLOW_DECAY_PCT = 1.5
D_FF = 4 * D_MODEL
EPS = 1e-6
N_MOD = 6
NEG_INF = -1e30
Q_END = ATTN_WIDTH
K_END = Q_END + KV_WIDTH
V_END = K_END + KV_WIDTH
HY_END = V_END + 3 * HYENA_WIDTH
GA_END = HY_END + D_MODEL
GH_END = GA_END + D_MODEL
IN_WIDTH = GH_END
LOG2E = math.log2(math.e)
LOGIT_SCALE = HEAD_DIM ** -0.5 * LOG2E

LANES = 128
VMEM_LIMIT = 56 * 1024 * 1024

FFT_N2 = 16
FFT_KG = 16
PROLOGUE_ROWS = 256
NORM_STEPS = 16
MLP_COLS = 512
GATE_COLS = 512
HY_CB = 128
MOD_ROWS = 8


def _cparams(sem):
    return pltpu.CompilerParams(dimension_semantics=sem, vmem_limit_bytes=VMEM_LIMIT)


def _single(block_shape, index_map):
    return pl.BlockSpec(block_shape, index_map, pipeline_mode=pl.Buffered(1))


def _rms(x):
    return x * lax.rsqrt(jnp.mean(x * x, axis=-1, keepdims=True) + EPS)


def _row_chunks(nrows, fn):
    def chunk(r, carry):
        fn(pl.ds(pl.multiple_of(r * PROLOGUE_ROWS, PROLOGUE_ROWS), PROLOGUE_ROWS))
        return carry

    lax.fori_loop(0, nrows // PROLOGUE_ROWS, chunk, 0)


def _mod_kernel(c_ref, w_ref, b_ref, o_ref):
    c = c_ref[...]
    s = c * jax.nn.sigmoid(c)
    o_ref[...] = jnp.dot(s.astype(BF16), w_ref[...].astype(BF16),
                         preferred_element_type=F32) + b_ref[...]


def _modulation(c_rows, w, layer, b):
    n = w.shape[2]
    tn = 1024
    return pl.pallas_call(
        _mod_kernel,
        grid=(n // tn,),
        in_specs=[pl.BlockSpec((MOD_ROWS, D_MODEL), lambda j: (0, 0)),
                  pl.BlockSpec((None, D_MODEL, tn), lambda j: (layer, 0, j)),
                  pl.BlockSpec((1, tn), lambda j: (0, j))],
        out_specs=pl.BlockSpec((MOD_ROWS, tn), lambda j: (0, j)),
        out_shape=jax.ShapeDtypeStruct((MOD_ROWS, n), F32),
        name="modulation",
        compiler_params=_cparams(("arbitrary",)),
    )(c_rows, w, b.reshape(1, n))


def _normproj_kernel(x0_ref, sc0_ref, sh0_ref, x_ref, g_ref, sc_ref, sh_ref, w_ref, o_ref,
                     ha_ref, hb_ref, *, pro_steps):
    i = pl.program_id(0)
    j = pl.program_id(1)
    chunk = x_ref.shape[0]
    g = g_ref[...]

    def norm(x, sc_r, sh_r):
        return (_rms(x) * g * (1.0 + sc_r[...]) + sh_r[...]).astype(BF16)

    @pl.when((i == 0) & (j == 0))
    def _():
        def first(rows):
            ha_ref[rows, :] = norm(x0_ref[rows, :], sc0_ref, sh0_ref)

        _row_chunks(x0_ref.shape[0], first)

    def step(cur_ref, next_ref):
        rows = pl.ds(pl.multiple_of(jnp.minimum(j, pro_steps - 1) * chunk, chunk), chunk)
        next_ref[rows, :] = norm(x_ref[...], sc_ref, sh_ref)
        o_ref[...] = jnp.dot(cur_ref[...], w_ref[...].astype(BF16), preferred_element_type=F32)

    @pl.when(i % 2 == 0)
    def _():
        step(ha_ref, hb_ref)

    @pl.when(i % 2 == 1)
    def _():
        step(hb_ref, ha_ref)


def _normproj(x, g, scale, shift, w, layer, col0, n, tm, tn):
    m = x.shape[0]
    nb = m // tm
    nj = n // tn
    j0 = col0 // tn
    nmod = scale.shape[0]
    blocks_per_mod = m // nmod // tm
    pro_steps = min(nj, NORM_STEPS)
    chunk = tm // pro_steps
    assert chunk * pro_steps == tm and chunk % 16 == 0 and tm % PROLOGUE_ROWS == 0

    def next_block(i):
        return jnp.minimum(i + 1, nb - 1)

    first_mod = pl.BlockSpec((None, 1, D_MODEL), lambda i, j: (0, 0, 0))
    next_mod = pl.BlockSpec((None, 1, D_MODEL),
                            lambda i, j: (next_block(i) // blocks_per_mod, 0, 0))
    return pl.pallas_call(
        functools.partial(_normproj_kernel, pro_steps=pro_steps),
        grid=(nb, nj),
        in_specs=[_single((tm, D_MODEL), lambda i, j: (0, 0)),
                  first_mod, first_mod,
                  pl.BlockSpec((chunk, D_MODEL),
                               lambda i, j: (next_block(i) * pro_steps
                                             + jnp.minimum(j, pro_steps - 1), 0)),
                  pl.BlockSpec((1, D_MODEL), lambda i, j: (0, 0)),
                  next_mod, next_mod,
                  pl.BlockSpec((None, D_MODEL, tn), lambda i, j: (layer, 0, j0 + j))],
        out_specs=pl.BlockSpec((tm, tn), lambda i, j: (i, j)),
        out_shape=jax.ShapeDtypeStruct((m, n), F32),
        scratch_shapes=[pltpu.VMEM((tm, D_MODEL), BF16), pltpu.VMEM((tm, D_MODEL), BF16)],
        name="normproj",
        compiler_params=_cparams(("arbitrary", "arbitrary")),
    )(x, scale, shift, x, g.reshape(1, D_MODEL), scale, shift, w)


def _rope(x, cos, sin_signed, first_half):
    rot = jnp.where(first_half, pltpu.roll(x, HEAD_DIM - 32, axis=1), pltpu.roll(x, 32, axis=1))
    return x * cos + rot * sin_signed


def _softmax_pv(s, sink, v):
    m = jnp.maximum(jnp.max(s, axis=-1, keepdims=True), sink)
    e = jnp.exp2(s - m)
    denom = jnp.sum(e, axis=-1, keepdims=True) + jnp.exp2(sink - m)
    o = jnp.dot(e.astype(BF16), v, preferred_element_type=F32)
    return o / denom


def _band_bias(nctx):
    qi = (np.arange(Q_GROUP * BLOCK) % BLOCK)[None, :, None]
    kj = np.arange(3 * BLOCK + nctx)[None, None, :]
    variant = np.arange(3)[:, None, None]
    in_prev = kj < BLOCK
    in_next = (kj >= 2 * BLOCK) & (kj < 3 * BLOCK)
    valid = np.where(in_prev, (kj >= qi) & (variant != 0),
                     np.where(in_next, (kj - 2 * BLOCK <= qi) & (variant != 2), True))
    return jnp.asarray(np.where(valid, 0.0, NEG_INF).astype(np.float32))


def _win_attn_kernel(sink_ref, q_ref, kvp_ref, kvc_ref, kvn_ref, kx_ref, vx_ref, cos_ref, sin_ref,
                     bias_ref, o_ref, *, nb):
    i = pl.program_id(1)
    lane = lax.broadcasted_iota(jnp.int32, (BLOCK, HEAD_DIM), 1)
    first_half = (lane % 64) < 32

    def table(ref, blk):
        return ref[pl.ds(pl.multiple_of(blk * BLOCK, BLOCK), BLOCK), :]

    ip = jnp.maximum(i - 1, 0)
    inx = jnp.minimum(i + 1, nb - 1)
    cos_c, sin_c = table(cos_ref, i), table(sin_ref, i)
    cos_p, sin_p = table(cos_ref, ip), table(sin_ref, ip)
    cos_n, sin_n = table(cos_ref, inx), table(sin_ref, inx)

    rows = Q_GROUP * BLOCK
    head_in_group = lax.broadcasted_iota(jnp.int32, (rows, 1), 0) // BLOCK
    bias = bias_ref[jnp.where(i == 0, 0, jnp.where(i == nb - 1, 2, 1))]

    for h in range(N_KV_HEADS):
        hs = slice(h * HEAD_DIM, (h + 1) * HEAD_DIM)
        vs = slice(KV_WIDTH + h * HEAD_DIM, KV_WIDTH + (h + 1) * HEAD_DIM)
        k = jnp.concatenate([
            _rope(kvp_ref[:, hs], cos_p, sin_p, first_half),
            _rope(kvc_ref[:, hs], cos_c, sin_c, first_half),
            _rope(kvn_ref[:, hs], cos_n, sin_n, first_half),
            kx_ref[:, hs]], axis=0).astype(BF16)
        v = jnp.concatenate([kvp_ref[:, vs], kvc_ref[:, vs], kvn_ref[:, vs], vx_ref[:, hs]],
                            axis=0).astype(BF16)
        heads = [h * Q_GROUP + g for g in range(Q_GROUP)]
        q = jnp.concatenate(
            [_rope(q_ref[:, hd * HEAD_DIM:(hd + 1) * HEAD_DIM], cos_c, sin_c, first_half)
             for hd in heads], axis=0).astype(BF16)
        sink = jnp.zeros((rows, 1), F32)
        for g, hd in enumerate(heads):
            sink = jnp.where(head_in_group == g, sink_ref[hd] * LOG2E, sink)
        s = lax.dot_general(q, k, (((1,), (1,)), ((), ())), preferred_element_type=F32)
        o = _softmax_pv(s * LOGIT_SCALE + bias, sink, v).astype(o_ref.dtype)
        for g, hd in enumerate(heads):
            o_ref[:, hd * HEAD_DIM:(hd + 1) * HEAD_DIM] = o[g * BLOCK:(g + 1) * BLOCK]


def _window_attention(p, kx, vx, sink, cos_t, sin_t):
    b, l, _ = p.shape
    c = kx.shape[1]
    nb = l // BLOCK
    def kv_spec(shift):
        return pl.BlockSpec((None, BLOCK, 2 * KV_WIDTH),
                            lambda bi, i: (bi, jnp.clip(i + shift, 0, nb - 1), Q_END // (2 * KV_WIDTH)))

    assert nb >= 2
    bias = _band_bias(c)
    ctx_spec = pl.BlockSpec((None, c, KV_WIDTH), lambda bi, i: (bi, 0, 0))
    tab_spec = _single((l, HEAD_DIM), lambda bi, i: (0, 0))
    return pl.pallas_call(
        functools.partial(_win_attn_kernel, nb=nb),
        grid=(b, nb),
        in_specs=[pl.BlockSpec(memory_space=pltpu.SMEM),
                  pl.BlockSpec((None, BLOCK, ATTN_WIDTH), lambda bi, i: (bi, i, 0)),
                  kv_spec(-1), kv_spec(0), kv_spec(1),
                  ctx_spec, ctx_spec, tab_spec, tab_spec,
                  _single(bias.shape, lambda bi, i: (0, 0, 0))],
        out_specs=pl.BlockSpec((None, BLOCK, ATTN_WIDTH), lambda bi, i: (bi, i, 0)),
        out_shape=jax.ShapeDtypeStruct((b, l, ATTN_WIDTH), BF16),
        name="window_attention",
        compiler_params=_cparams(("arbitrary", "arbitrary")),
    )(sink, p, p, p, p, kx, vx, cos_t, sin_t, bias)


def _ctx_attn_kernel(sink_ref, q_ref, k_ref, v_ref, o_ref):
    for h in range(N_KV_HEADS):
        hs = slice(h * HEAD_DIM, (h + 1) * HEAD_DIM)
        k = k_ref[:, hs].astype(BF16)
        v = v_ref[:, hs].astype(BF16)
        for g in range(Q_GROUP):
            head = h * Q_GROUP + g
            cs = slice(head * HEAD_DIM, (head + 1) * HEAD_DIM)
            q = q_ref[:, cs].astype(BF16)
            s = lax.dot_general(q, k, (((1,), (1,)), ((), ())), preferred_element_type=F32)
            o_ref[:, cs] = _softmax_pv(s * LOGIT_SCALE, sink_ref[head] * LOG2E, v).astype(o_ref.dtype)


def _context_attention(p, sink):
    b, c, _ = p.shape
    return pl.pallas_call(
        _ctx_attn_kernel,
        grid=(b,),
        in_specs=[pl.BlockSpec(memory_space=pltpu.SMEM),
                  pl.BlockSpec((None, c, ATTN_WIDTH), lambda bi: (bi, 0, 0)),
                  pl.BlockSpec((None, c, KV_WIDTH), lambda bi: (bi, 0, Q_END // KV_WIDTH)),
                  pl.BlockSpec((None, c, KV_WIDTH), lambda bi: (bi, 0, K_END // KV_WIDTH))],
        out_specs=pl.BlockSpec((None, c, ATTN_WIDTH), lambda bi: (bi, 0, 0)),
        out_shape=jax.ShapeDtypeStruct((b, c, ATTN_WIDTH), BF16),
        name="context_attention",
        compiler_params=_cparams(("arbitrary",)),
    )(sink, p, p, p)


def _fft_tables(l, n2):
    n = 2 * l
    n1 = n // n2
    k1 = np.arange(n1 // 2, dtype=np.int64)[None, :, None]
    t = (n2 * np.arange(n1 // 2, dtype=np.int64)[None, None, :]
         + np.arange(n2, dtype=np.int64)[:, None, None])
    ang = (((2 * k1 + 1) * t) % (2 * n)).astype(np.float64) * (math.pi / n)
    fwd1 = np.concatenate([np.cos(ang), -np.sin(ang)], axis=1)
    inv1 = np.swapaxes(fwd1, 1, 2) * (2.0 / n)
    as_operand = lambda a: jnp.asarray(np.ascontiguousarray(a, dtype=np.float32)).astype(BF16)
    return as_operand(fwd1), as_operand(inv1)


def _filter_features(l):
    bands = (FILTER_EMB - 1) // 2
    t = np.linspace(0.0, 1.0, l)[:, None]
    w = 2 * math.pi * np.arange(l)[:, None] / l
    f = np.linspace(1e-4, bands - 1, bands)[None, :]
    z = np.concatenate([t, np.cos(f * w), -np.sin(f * w)], axis=-1)
    z = np.pad(z, ((0, 0), (0, FILTER_HIDDEN - FILTER_EMB)))
    return jnp.asarray(z.astype(np.float32))


def _filter_mlp_kernel(z_ref, w1_ref, b1_ref, f1_ref, w2_ref, b2_ref, f2_ref,
                       w3_ref, b3_ref, f3_ref, o_ref):
    h = jnp.sin(f1_ref[...] * (jnp.dot(z_ref[...], w1_ref[...], preferred_element_type=F32)
                               + b1_ref[...]))
    h = jnp.sin(f2_ref[...] * (jnp.dot(h, w2_ref[...], preferred_element_type=F32) + b2_ref[...]))
    o_ref[...] = jnp.sin(f3_ref[...] * (jnp.dot(h, w3_ref[...], preferred_element_type=F32)
                                        + b3_ref[...]))


def _filter_mlp(z, fw1, fb1, ff1, fw2, fb2, ff2, fw3, fb3, ff3):
    l = z.shape[0]
    row = lambda a: a.reshape(1, FILTER_HIDDEN)
    w1 = jnp.pad(fw1, ((0, FILTER_HIDDEN - FILTER_EMB), (0, 0)))
    return pl.pallas_call(
        _filter_mlp_kernel,
        out_shape=jax.ShapeDtypeStruct((l, FILTER_HIDDEN), F32),
        name="filter_mlp",
        compiler_params=pltpu.CompilerParams(vmem_limit_bytes=VMEM_LIMIT),
    )(z, w1, row(fb1), row(ff1), fw2, row(fb2), row(ff2), fw3, row(fb3), row(ff3))


def _cmul_root16(z, p, inverse):
    zr, zi = z
    p = p % 16
    if inverse:
        p = (16 - p) % 16
    if p == 0:
        return zr, zi
    if p == 4:
        return zi, -zr
    if p == 8:
        return -zr, -zi
    if p == 12:
        return -zi, zr
    c = math.cos(2.0 * math.pi * p / 16)
    s = -math.sin(2.0 * math.pi * p / 16)
    return zr * c - zi * s, zr * s + zi * c


def _dft4(z, inverse):
    (ar, ai), (br, bi), (cr, ci), (dr, di) = z
    t0r, t0i = ar + cr, ai + ci
    t1r, t1i = ar - cr, ai - ci
    t2r, t2i = br + dr, bi + di
    t3r, t3i = br - dr, bi - di
    y0 = (t0r + t2r, t0i + t2i)
    y2 = (t0r - t2r, t0i - t2i)
    minus_i_t3 = (t1r + t3i, t1i - t3r)
    plus_i_t3 = (t1r - t3i, t1i + t3r)
    return [y0, plus_i_t3, y2, minus_i_t3] if inverse else [y0, minus_i_t3, y2, plus_i_t3]


def _dft_digit(z, inverse):
    if len(z) == 1:
        return z
    assert len(z) == 16
    t = [_dft4([z[4 * a + b] for a in range(4)], inverse) for b in range(4)]
    out = [None] * 16
    for c in range(4):
        y = _dft4([_cmul_root16(t[b][c], b * c, inverse) for b in range(4)], inverse)
        for d in range(4):
            out[c + 4 * d] = y[d]
    return out


def _fft_stage1(src_ref, f1_ref, a_ref):
    n2cnt = f1_ref.shape[0]
    groups = a_ref.shape[0]
    n1 = groups * FFT_KG
    half = src_ref.shape[0] // n2cnt

    def body(n2, carry):
        rows = pl.ds(n2, half, stride=n2cnt) if n2cnt > 1 else pl.ds(0, half)
        res = jnp.dot(f1_ref[n2], src_ref[rows, :].astype(BF16), preferred_element_type=F32)
        for g in range(groups):
            a_ref[g, 0, n2] = res[g * FFT_KG:(g + 1) * FFT_KG]
            a_ref[g, 1, n2] = res[n1 + g * FFT_KG:n1 + (g + 1) * FFT_KG]
        return carry

    if n2cnt == 1:
        body(0, 0)
    else:
        lax.fori_loop(0, n2cnt, body, 0, unroll=4)


def _load_digits(ref, g, rows):
    return [(ref[g, 0, d, rows, :], ref[g, 1, d, rows, :]) for d in range(ref.shape[2])]


def _filter_spec_kernel(h3_ref, wf_ref, wb_ref, df_ref, db_ref, f1_ref, kf_ref,
                        hf_ref, hb_ref, a_ref, *, l):
    groups = a_ref.shape[0]
    row = lax.broadcasted_iota(jnp.int32, (l, 1), 0)
    t = row.astype(F32) * (1.0 / (l - 1))
    h3 = h3_ref[...]
    hf = jnp.dot(h3, wf_ref[...], preferred_element_type=F32) * jnp.exp(-t * df_ref[...])
    hb = jnp.dot(h3, wb_ref[...], preferred_element_type=F32) * jnp.exp(-t * db_ref[...])
    hb = jnp.where(row > 0, hb, 0.0)
    norm = jnp.sum(jnp.abs(hf), axis=0, keepdims=True) + jnp.sum(jnp.abs(hb), axis=0, keepdims=True)
    hf_ref[...] = hf / norm
    hb_ref[...] = hb / norm

    def forward(g, carry):
        for r in range(0, FFT_KG, 8):
            rows = pl.ds(r, 8)
            for k2, (xr, xi) in enumerate(_dft_digit(_load_digits(a_ref, g, rows), False)):
                kf_ref[g, 0, k2, rows, :] = xr
                kf_ref[g, 1, k2, rows, :] = xi
        return carry

    def backward(g, carry):
        for r in range(0, FFT_KG, 8):
            rows = pl.ds(r, 8)
            for k2, (xr, xi) in enumerate(_dft_digit(_load_digits(a_ref, g, rows), False)):
                kf_ref[g, 0, k2, rows, :] = kf_ref[g, 0, k2, rows, :] + xr
                kf_ref[g, 1, k2, rows, :] = kf_ref[g, 1, k2, rows, :] - xi
        return carry

    _fft_stage1(hf_ref, f1_ref, a_ref)
    lax.fori_loop(0, groups, forward, 0)
    _fft_stage1(hb_ref, f1_ref, a_ref)
    lax.fori_loop(0, groups, backward, 0)


def _filter_spectrum(h3, fw4, decay, tables):
    l = h3.shape[0]
    f1, _ = tables
    n2 = f1.shape[0]
    groups = f1.shape[1] // 2 // FFT_KG
    nblk = HYENA_WIDTH // HY_CB
    spec_shape = (groups, 2, n2, FFT_KG)
    return pl.pallas_call(
        functools.partial(_filter_spec_kernel, l=l),
        grid=(nblk,),
        in_specs=[_single((l, FILTER_HIDDEN), lambda c: (0, 0)),
                  pl.BlockSpec((FILTER_HIDDEN, HY_CB), lambda c: (0, c)),
                  pl.BlockSpec((FILTER_HIDDEN, HY_CB), lambda c: (0, c + nblk)),
                  pl.BlockSpec((1, HY_CB), lambda c: (0, c)),
                  pl.BlockSpec((1, HY_CB), lambda c: (0, c + nblk)),
                  _single(f1.shape, lambda c: (0, 0, 0))],
        out_specs=pl.BlockSpec(spec_shape + (HY_CB,), lambda c: (0, 0, 0, 0, c)),
        out_shape=jax.ShapeDtypeStruct(spec_shape + (HYENA_WIDTH,), F32),
        scratch_shapes=[pltpu.VMEM((l, HY_CB), F32), pltpu.VMEM((l, HY_CB), F32),
                        pltpu.VMEM(spec_shape + (HY_CB,), F32)],
        name="filter_spectrum",
        compiler_params=_cparams(("arbitrary",)),
    )(h3, fw4, fw4, decay, decay, f1)


def _short_conv(src_ref, w_ref, b_ref, l, rows):
    w0, w1, w2, b = w_ref[0:1, :], w_ref[1:2, :], w_ref[2:3, :], b_ref[...]
    r = lax.broadcasted_iota(jnp.int32, (rows, 1), 0)
    for s in range(0, l, rows):
        cur = src_ref[s:s + rows, :]
        if s > 0:
            prev = src_ref[s - 1:s - 1 + rows, :]
        else:
            prev = jnp.where(r == 0, 0.0, pltpu.roll(cur, 1, axis=0))
        if s + rows < l:
            nxt = src_ref[s + 1:s + 1 + rows, :]
        else:
            nxt = jnp.where(r == rows - 1, 0.0, pltpu.roll(cur, rows - 1, axis=0))
        yield s, prev * w0 + cur * w1 + nxt * w2 + b


def _hyena_kernel(x0_ref, x1_ref, v_ref, w0_ref, w1_ref, wv_ref, b0_ref, b1_ref, bv_ref,
                  bias_ref, kf_ref, f1_ref, g1_ref, o_ref, x0c_ref, vg_ref, a_ref, *, l):
    groups = a_ref.shape[0]
    n2cnt = a_ref.shape[2]
    cb = a_ref.shape[-1]
    n1 = groups * FFT_KG
    half = l // n2cnt
    rows = min(l, 512)

    for s, u in _short_conv(x0_ref, w0_ref, b0_ref, l, rows):
        x0c_ref[s:s + rows, :] = u
    for (s, u1), (_, uv) in zip(_short_conv(x1_ref, w1_ref, b1_ref, l, rows),
                                _short_conv(v_ref, wv_ref, bv_ref, l, rows)):
        vg_ref[s:s + rows, :] = uv * u1

    _fft_stage1(vg_ref, f1_ref, a_ref)

    def spectrum(g, carry):
        for r in range(0, FFT_KG, 8):
            rws = pl.ds(r, 8)
            x = _dft_digit(_load_digits(a_ref, g, rws), False)
            k = _load_digits(kf_ref, g, rws)
            y = [(xr * kr - xi * ki, xr * ki + xi * kr) for (xr, xi), (kr, ki) in zip(x, k)]
            for d, (br, bi) in enumerate(_dft_digit(y, True)):
                a_ref[g, 0, d, rws, :] = br
                a_ref[g, 1, d, rws, :] = bi
        return carry

    lax.fori_loop(0, groups, spectrum, 0)

    def synth(n2, carry):
        b = jnp.concatenate([a_ref[:, 0, n2].reshape(n1, cb), a_ref[:, 1, n2].reshape(n1, cb)],
                            axis=0).astype(BF16)
        y = jnp.dot(g1_ref[n2], b, preferred_element_type=F32)
        idx = pl.ds(n2, half, stride=n2cnt) if n2cnt > 1 else pl.ds(0, half)
        o_ref[idx, :] = y
        return carry

    if n2cnt == 1:
        synth(0, 0)
    else:
        lax.fori_loop(0, n2cnt, synth, 0, unroll=4)

    bias = bias_ref[...]
    for s in range(0, l, rows):
        blk = slice(s, s + rows)
        o_ref[blk, :] = (o_ref[blk, :] + vg_ref[blk, :] * bias) * x0c_ref[blk, :]


def _hyena(p, col0, conv_w, conv_b, bias_d, kf, tables):
    b, l, _ = p.shape
    f1, g1 = tables
    nblk = HYENA_WIDTH // HY_CB
    c0 = col0 // HY_CB
    spec_block = kf.shape[:-1] + (HY_CB,)

    def slab(part):
        return pl.BlockSpec((None, l, HY_CB), lambda c, bi: (bi, 0, c0 + part * nblk + c))

    def cw(part):
        return pl.BlockSpec((3, HY_CB), lambda c, bi: (0, part * nblk + c))

    def cbias(part):
        return pl.BlockSpec((1, HY_CB), lambda c, bi: (0, part * nblk + c))

    return pl.pallas_call(
        functools.partial(_hyena_kernel, l=l),
        grid=(nblk, b),
        in_specs=[slab(0), slab(1), slab(2), cw(0), cw(1), cw(2), cbias(0), cbias(1), cbias(2),
                  pl.BlockSpec((1, HY_CB), lambda c, bi: (0, c)),
                  pl.BlockSpec(spec_block, lambda c, bi: (0, 0, 0, 0, c)),
                  _single(f1.shape, lambda c, bi: (0, 0, 0)),
                  _single(g1.shape, lambda c, bi: (0, 0, 0))],
        out_specs=pl.BlockSpec((None, l, HY_CB), lambda c, bi: (bi, 0, c)),
        out_shape=jax.ShapeDtypeStruct((b, l, HYENA_WIDTH), F32),
        scratch_shapes=[pltpu.VMEM((l, HY_CB), F32), pltpu.VMEM((l, HY_CB), F32),
                        pltpu.VMEM(spec_block, F32)],
        name="hyena_conv",
        compiler_params=_cparams(("arbitrary", "arbitrary")),
    )(p, p, p, conv_w, conv_w, conv_w, conv_b, conv_b, conv_b, bias_d, kf, f1, g1)


def _merge_kernel(a_ref, h_ref, wa_ref, wh_ref, *refs):
    gate_refs, o_ref = refs[:-1], refs[-1]
    nt = len(gate_refs) // 2
    tg = gate_refs[0].shape[1]
    a = a_ref[...]
    h = h_ref[...].astype(BF16)
    for k in range(nt):
        cs = slice(k * tg, (k + 1) * tg)
        ya = jnp.dot(a, wa_ref[:, cs], preferred_element_type=F32)
        yh = jnp.dot(h, wh_ref[:, cs], preferred_element_type=F32)
        o_ref[:, cs] = (jax.nn.sigmoid(gate_refs[k][...]) * ya
                        + jax.nn.sigmoid(gate_refs[nt + k][...]) * yh).astype(o_ref.dtype)


def _merge(attn, hy, p, w_ao, w_ho, layer, tm):
    m = attn.shape[0]
    nt = D_MODEL // GATE_COLS

    def gate_spec(col0, k):
        return pl.BlockSpec((tm, GATE_COLS), lambda i: (i, col0 // GATE_COLS + k))

    gate_specs = ([gate_spec(HY_END, k) for k in range(nt)]
                  + [gate_spec(GA_END, k) for k in range(nt)])
    return pl.pallas_call(
        _merge_kernel,
        grid=(m // tm,),
        in_specs=[pl.BlockSpec((tm, ATTN_WIDTH), lambda i: (i, 0)),
                  pl.BlockSpec((tm, HYENA_WIDTH), lambda i: (i, 0)),
                  _single((None, ATTN_WIDTH, D_MODEL), lambda i: (layer, 0, 0)),
                  _single((None, HYENA_WIDTH, D_MODEL), lambda i: (layer, 0, 0))] + gate_specs,
        out_specs=pl.BlockSpec((tm, D_MODEL), lambda i: (i, 0)),
        out_shape=jax.ShapeDtypeStruct((m, D_MODEL), BF16),
        name="branch_merge",
        compiler_params=_cparams(("arbitrary",)),
    )(attn, hy, w_ao, w_ho, *([p] * (2 * nt)))


def _outproj_kernel(mix_ref, w_ref, x_ref, g_ref, gate_ref, o_ref, y_ref, *, tn):
    j = pl.program_id(1)
    nj = y_ref.shape[0]
    y_ref[j] = jnp.dot(mix_ref[...], w_ref[...], preferred_element_type=F32)

    @pl.when(j == nj - 1)
    def _():
        ss = sum(jnp.sum(jnp.square(y_ref[k]), axis=-1, keepdims=True) for k in range(nj))
        r = lax.rsqrt(ss * (1.0 / D_MODEL) + EPS)
        for k in range(nj):
            cs = slice(k * tn, (k + 1) * tn)
            o_ref[:, cs] = x_ref[:, cs] + gate_ref[:, cs] * ((y_ref[k] * r) * g_ref[:, cs])


def _outproj(mix, w_o, layer, x, g, gate, tm, tn):
    m = x.shape[0]
    nmod = gate.shape[0]
    blocks_per_mod = m // nmod // tm
    return pl.pallas_call(
        functools.partial(_outproj_kernel, tn=tn),
        grid=(m // tm, D_MODEL // tn),
        in_specs=[pl.BlockSpec((tm, D_MODEL), lambda i, j: (i, 0)),
                  pl.BlockSpec((None, D_MODEL, tn), lambda i, j: (layer, 0, j),
                               pipeline_mode=pl.Buffered(1) if tn == D_MODEL else None),
                  pl.BlockSpec((tm, D_MODEL), lambda i, j: (i, 0)),
                  pl.BlockSpec((1, D_MODEL), lambda i, j: (0, 0)),
                  pl.BlockSpec((None, 1, D_MODEL), lambda i, j: (i // blocks_per_mod, 0, 0))],
        out_specs=pl.BlockSpec((tm, D_MODEL), lambda i, j: (i, 0)),
        out_shape=jax.ShapeDtypeStruct((m, D_MODEL), F32),
        scratch_shapes=[pltpu.VMEM((D_MODEL // tn, tm, tn), F32)],
        name="out_projection",
        compiler_params=_cparams(("arbitrary", "arbitrary")),
    )(mix, w_o, x, g.reshape(1, D_MODEL), gate)


def _mlp_kernel(x_ref, xn_ref, gin_ref, sc_ref, sh_ref, scn_ref, shn_ref, w1_ref, w2_ref, gout_ref,
                gate_ref, o_ref, ha_ref, hb_ref, *, pro_steps):
    i = pl.program_id(0)
    j = pl.program_id(1)
    tm = x_ref.shape[0]
    chunk = xn_ref.shape[0]
    gin = gin_ref[...]

    def norm(x, sc_r, sh_r):
        return (_rms(x) * gin * (1.0 + sc_r[...]) + sh_r[...]).astype(BF16)

    @pl.when((i == 0) & (j == 0))
    def _():
        def first(rows):
            ha_ref[rows, :] = norm(x_ref[rows, :], sc_ref, sh_ref)

        _row_chunks(tm, first)

    @pl.when(j == 0)
    def _():
        def clear(rows):
            o_ref[rows, :] = jnp.zeros((PROLOGUE_ROWS, D_MODEL), F32)

        _row_chunks(tm, clear)

    def step(cur_ref, next_ref):
        rows = pl.ds(pl.multiple_of(jnp.minimum(j, pro_steps - 1) * chunk, chunk), chunk)
        next_ref[rows, :] = norm(xn_ref[...], scn_ref, shn_ref)
        a = jnp.dot(cur_ref[...], w1_ref[...].astype(BF16), preferred_element_type=F32)
        a = jnp.square(jnp.maximum(a, 0.0)).astype(BF16)
        for n in range(0, D_MODEL, MLP_COLS):
            o_ref[:, n:n + MLP_COLS] += jnp.dot(a, w2_ref[:, n:n + MLP_COLS].astype(BF16),
                                                preferred_element_type=F32)

    @pl.when(i % 2 == 0)
    def _():
        step(ha_ref, hb_ref)

    @pl.when(i % 2 == 1)
    def _():
        step(hb_ref, ha_ref)

    @pl.when(j == pl.num_programs(1) - 1)
    def _():
        gout, gate = gout_ref[...], gate_ref[...]

        def epilogue(rows):
            o_ref[rows, :] = x_ref[rows, :] + gate * (_rms(o_ref[rows, :]) * gout)

        _row_chunks(tm, epilogue)


def _mlp(x, g_in, scale, shift, w1, w2, layer, g_out, gate, tm, tf):
    m = x.shape[0]
    nb = m // tm
    nj = D_FF // tf
    nmod = gate.shape[0]
    blocks_per_mod = m // nmod // tm
    pro_steps = min(nj, NORM_STEPS)
    chunk = tm // pro_steps
    assert chunk * pro_steps == tm and chunk % 16 == 0 and tm % PROLOGUE_ROWS == 0

    def next_block(i):
        return jnp.minimum(i + 1, nb - 1)

    mod_spec = pl.BlockSpec((None, 1, D_MODEL), lambda i, j: (i // blocks_per_mod, 0, 0))
    next_mod = pl.BlockSpec((None, 1, D_MODEL),
                            lambda i, j: (next_block(i) // blocks_per_mod, 0, 0))
    row_spec = pl.BlockSpec((1, D_MODEL), lambda i, j: (0, 0))
    return pl.pallas_call(
        functools.partial(_mlp_kernel, pro_steps=pro_steps),
        grid=(nb, nj),
        in_specs=[pl.BlockSpec((tm, D_MODEL), lambda i, j: (i, 0), pipeline_mode=pl.Buffered(1)),
                  pl.BlockSpec((chunk, D_MODEL),
                               lambda i, j: (next_block(i) * pro_steps
                                             + jnp.minimum(j, pro_steps - 1), 0)),
                  row_spec, mod_spec, mod_spec, next_mod, next_mod,
                  pl.BlockSpec((None, D_MODEL, tf), lambda i, j: (layer, 0, j)),
                  pl.BlockSpec((None, tf, D_MODEL), lambda i, j: (layer, j, 0)),
                  row_spec, mod_spec],
        out_specs=pl.BlockSpec((tm, D_MODEL), lambda i, j: (i, 0)),
        out_shape=jax.ShapeDtypeStruct((m, D_MODEL), F32),
        scratch_shapes=[pltpu.VMEM((tm, D_MODEL), BF16), pltpu.VMEM((tm, D_MODEL), BF16)],
        name="channel_mlp",
        compiler_params=_cparams(("arbitrary", "arbitrary")),
    )(x, x, g_in.reshape(1, D_MODEL), scale, shift, scale, shift, w1, w2,
      g_out.reshape(1, D_MODEL), gate)


def _rope_tables(l):
    pos = np.arange(l)
    quarter = HEAD_DIM // 4
    freqs = ROPE_THETA ** (-np.arange(quarter) / quarter)
    ang_r = (pos // GRID_W)[:, None] * freqs[None, :]
    ang_c = (pos % GRID_W)[:, None] * freqs[None, :]
    cos_t = np.concatenate([np.cos(ang_r), np.cos(ang_r), np.cos(ang_c), np.cos(ang_c)], axis=-1)
    sin_t = np.concatenate([-np.sin(ang_r), np.sin(ang_r), -np.sin(ang_c), np.sin(ang_c)], axis=-1)
    return jnp.asarray(cos_t.astype(np.float32)), jnp.asarray(sin_t.astype(np.float32))


def _decay_rates():
    min_decay = math.log(DECAY_TARGET) / SLOW_DECAY_PCT
    max_decay = math.log(DECAY_TARGET) / FAST_DECAY_PCT
    deltas = np.tile(np.linspace(min_decay, max_decay, HYENA_WIDTH), 2)
    return jnp.asarray(np.abs(deltas)[None, :].astype(np.float32))


def kernel(x, c, ctx, c_ctx, w_mod, b_mod, norm_g, w_in, attn_sink, hy_conv_w, hy_conv_b,
           hy_fw1, hy_fb1, hy_ff1, hy_fw2, hy_fb2, hy_ff2, hy_fw3, hy_fb3, hy_ff3, hy_fw4,
           hy_bias, w_attn_out, w_hyena_out, w_out, w_ff1, w_ff2):
    b, l, d = x.shape
    cl = ctx.shape[1]
    assert d == D_MODEL and l % 1024 == 0 and cl % 256 == 0 and b + 1 <= MOD_ROWS

    cos_t, sin_t = _rope_tables(l)
    decay = _decay_rates()
    tables_lat = _fft_tables(l, FFT_N2)
    tables_ctx = _fft_tables(cl, 1)
    z_lat = _filter_features(l)
    z_ctx = _filter_features(cl)

    c_rows = jnp.concatenate([c, c_ctx[None, :], jnp.zeros((MOD_ROWS - b - 1, d), F32)], axis=0)
    x_lat = x.reshape(b * l, d)
    x_ctx = ctx.reshape(b * cl, d)
    tm_ctx = b * cl
    w_aob = w_attn_out.astype(BF16)
    w_hob = w_hyena_out.astype(BF16)
    w_ob = w_out.astype(BF16)

    for layer in range(DEPTH):
        last = layer == DEPTH - 1
        mod = _modulation(c_rows, w_mod, layer, b_mod[layer])
        mod_lat = [mod[:b, k * d:(k + 1) * d].reshape(b, 1, d) for k in range(N_MOD)]
        mod_ctx = [mod[b:b + 1, k * d:(k + 1) * d].reshape(1, 1, d) for k in range(N_MOD)]
        sh1, sc1, g1, sh2, sc2, g2 = mod_lat
        csh1, csc1, cg1, csh2, csc2, cg2 = mod_ctx
        g = norm_g[layer]
        fparams = (hy_fw1[layer], hy_fb1[layer], hy_ff1[layer], hy_fw2[layer], hy_fb2[layer],
                   hy_ff2[layer], hy_fw3[layer], hy_fb3[layer], hy_ff3[layer])
        bias_d = hy_bias[layer].reshape(1, HYENA_WIDTH)
        conv_b = hy_conv_b[layer].reshape(1, 3 * HYENA_WIDTH)

        p_lat = _normproj(x_lat, g[0], sc1, sh1, w_in, layer, 0, IN_WIDTH, 2048, 512)
        if last:
            kv_ctx = _normproj(x_ctx, g[0], csc1, csh1, w_in, layer, Q_END, V_END - Q_END,
                               tm_ctx, 512)
            kx, vx = kv_ctx[:, :KV_WIDTH], kv_ctx[:, KV_WIDTH:]
        else:
            p_ctx = _normproj(x_ctx, g[0], csc1, csh1, w_in, layer, 0, IN_WIDTH, tm_ctx, 512)
            kx, vx = p_ctx[:, Q_END:K_END], p_ctx[:, K_END:V_END]
        kx = kx.reshape(b, cl, KV_WIDTH)
        vx = vx.reshape(b, cl, KV_WIDTH)

        p3 = p_lat.reshape(b, l, IN_WIDTH)
        attn = _window_attention(p3, kx, vx, attn_sink[layer], cos_t, sin_t)
        kf = _filter_spectrum(_filter_mlp(z_lat, *fparams), hy_fw4[layer], decay, tables_lat)
        hy = _hyena(p3, V_END, hy_conv_w[layer], conv_b, bias_d, kf, tables_lat)
        mix = _merge(attn.reshape(b * l, ATTN_WIDTH), hy.reshape(b * l, HYENA_WIDTH), p_lat,
                     w_aob, w_hob, layer, 512)
        x_lat = _outproj(mix, w_ob, layer, x_lat, g[1], g1, 512, D_MODEL)

        if not last:
            pc3 = p_ctx.reshape(b, cl, IN_WIDTH)
            attn_c = _context_attention(pc3, attn_sink[layer])
            kf_c = _filter_spectrum(_filter_mlp(z_ctx, *fparams), hy_fw4[layer], decay, tables_ctx)
            hy_c = _hyena(pc3, V_END, hy_conv_w[layer], conv_b, bias_d, kf_c, tables_ctx)
            mix_c = _merge(attn_c.reshape(b * cl, ATTN_WIDTH), hy_c.reshape(b * cl, HYENA_WIDTH),
                           p_ctx, w_aob, w_hob, layer, tm_ctx)
            x_ctx = _outproj(mix_c, w_ob, layer, x_ctx, g[1], cg1, tm_ctx, D_MODEL)
            x_ctx = _mlp(x_ctx, g[2], csc2, csh2, w_ff1, w_ff2, layer, g[3], cg2, tm_ctx, 512)

        x_lat = _mlp(x_lat, g[2], sc2, sh2, w_ff1, w_ff2, layer, g[3], g2, 1024, 512)
    return x_lat.reshape(b, l, d)
```

```python
import functools
import math

import jax
import jax.numpy as jnp
import numpy as np
from jax import lax
from jax.experimental import pallas as pl
from jax.experimental.pallas import tpu as pltpu

F32 = jnp.float32
BF16 = jnp.bfloat16

D_MODEL = 2048
DEPTH = 2
GRID_W = 64
HEAD_DIM = 128
N_Q_HEADS = 8
N_KV_HEADS = 2
Q_GROUP = N_Q_HEADS // N_KV_HEADS
ATTN_WIDTH = N_Q_HEADS * HEAD_DIM
KV_WIDTH = N_KV_HEADS * HEAD_DIM
BLOCK = 128
ROPE_THETA = 10000.0
HYENA_WIDTH = 1024
FILTER_EMB = 33
FILTER_HIDDEN = 64
DECAY_TARGET = 1e-2
FAST_DECAY_PCT = 0.3
SLOW_DECAY_PCT = 1.5
D_FF = 4 * D_MODEL
EPS = 1e-6
N_MOD = 6
NEG_INF = -1e30
Q_END = ATTN_WIDTH
K_END = Q_END + KV_WIDTH
V_END = K_END + KV_WIDTH
HY_END = V_END + 3 * HYENA_WIDTH
GA_END = HY_END + D_MODEL
GH_END = GA_END + D_MODEL
IN_WIDTH = GH_END
LOG2E = math.log2(math.e)
LOGIT_SCALE = HEAD_DIM ** -0.5 * LOG2E

LANES = 128
VMEM_LIMIT = 56 * 1024 * 1024

FFT_N2 = 16
FFT_KG = 16
PROLOGUE_ROWS = 256
NORM_STEPS = 16
MLP_COLS = 512
GATE_COLS = 512
HY_CB = 128
MOD_ROWS = 8


def _cparams(sem):
    return pltpu.CompilerParams(dimension_semantics=sem, vmem_limit_bytes=VMEM_LIMIT)


def _single(block_shape, index_map):
    return pl.BlockSpec(block_shape, index_map, pipeline_mode=pl.Buffered(1))


def _rms(x):
    return x * lax.rsqrt(jnp.mean(x * x, axis=-1, keepdims=True) + EPS)


def _row_chunks(nrows, fn):
    def chunk(r, carry):
        fn(pl.ds(pl.multiple_of(r * PROLOGUE_ROWS, PROLOGUE_ROWS), PROLOGUE_ROWS))
        return carry

    lax.fori_loop(0, nrows // PROLOGUE_ROWS, chunk, 0)


def _mod_kernel(c_ref, w_ref, b_ref, o_ref):
    c = c_ref[...]
    s = c * jax.nn.sigmoid(c)
    o_ref[...] = jnp.dot(s.astype(BF16), w_ref[...].astype(BF16),
                         preferred_element_type=F32) + b_ref[...]


def _modulation(c_rows, w, layer, b):
    n = w.shape[2]
    tn = 1024
    return pl.pallas_call(
        _mod_kernel,
        grid=(n // tn,),
        in_specs=[pl.BlockSpec((MOD_ROWS, D_MODEL), lambda j: (0, 0)),
                  pl.BlockSpec((None, D_MODEL, tn), lambda j: (layer, 0, j)),
                  pl.BlockSpec((1, tn), lambda j: (0, j))],
        out_specs=pl.BlockSpec((MOD_ROWS, tn), lambda j: (0, j)),
        out_shape=jax.ShapeDtypeStruct((MOD_ROWS, n), F32),
        name="modulation",
        compiler_params=_cparams(("arbitrary",)),
    )(c_rows, w, b.reshape(1, n))


def _normproj_kernel(x0_ref, sc0_ref, sh0_ref, x_ref, g_ref, sc_ref, sh_ref, w_ref, o_ref,
                     ha_ref, hb_ref, *, pro_steps):
    i = pl.program_id(0)
    j = pl.program_id(1)
    chunk = x_ref.shape[0]
    g = g_ref[...]

    def norm(x, sc_r, sh_r):
        return (_rms(x) * g * (1.0 + sc_r[...]) + sh_r[...]).astype(BF16)

    @pl.when((i == 0) & (j == 0))
    def _():
        def first(rows):
            ha_ref[rows, :] = norm(x0_ref[rows, :], sc0_ref, sh0_ref)

        _row_chunks(x0_ref.shape[0], first)

    def step(cur_ref, next_ref):
        rows = pl.ds(pl.multiple_of(jnp.minimum(j, pro_steps - 1) * chunk, chunk), chunk)
        next_ref[rows, :] = norm(x_ref[...], sc_ref, sh_ref)
        o_ref[...] = jnp.dot(cur_ref[...], w_ref[...].astype(BF16), preferred_element_type=F32)

    @pl.when(i % 2 == 0)
    def _():
        step(ha_ref, hb_ref)

    @pl.when(i % 2 == 1)
    def _():
        step(hb_ref, ha_ref)


def _normproj(x, g, scale, shift, w, layer, col0, n, tm, tn):
    m = x.shape[0]
    nb = m // tm
    nj = n // tn
    j0 = col0 // tn
    nmod = scale.shape[0]
    blocks_per_mod = m // nmod // tm
    pro_steps = min(nj, NORM_STEPS)
    chunk = tm // pro_steps
    assert chunk * pro_steps == tm and chunk % 16 == 0 and tm % PROLOGUE_ROWS == 0

    def next_block(i):
        return jnp.minimum(i + 1, nb - 1)

    first_mod = pl.BlockSpec((None, 1, D_MODEL), lambda i, j: (0, 0, 0))
    next_mod = pl.BlockSpec((None, 1, D_MODEL),
                            lambda i, j: (next_block(i) // blocks_per_mod, 0, 0))
    return pl.pallas_call(
        functools.partial(_normproj_kernel, pro_steps=pro_steps),
        grid=(nb, nj),
        in_specs=[_single((tm, D_MODEL), lambda i, j: (0, 0)),
                  first_mod, first_mod,
                  pl.BlockSpec((chunk, D_MODEL),
                               lambda i, j: (next_block(i) * pro_steps
                                             + jnp.minimum(j, pro_steps - 1), 0)),
                  pl.BlockSpec((1, D_MODEL), lambda i, j: (0, 0)),
                  next_mod, next_mod,
                  pl.BlockSpec((None, D_MODEL, tn), lambda i, j: (layer, 0, j0 + j))],
        out_specs=pl.BlockSpec((tm, tn), lambda i, j: (i, j)),
        out_shape=jax.ShapeDtypeStruct((m, n), F32),
        scratch_shapes=[pltpu.VMEM((tm, D_MODEL), BF16), pltpu.VMEM((tm, D_MODEL), BF16)],
        name="normproj",
        compiler_params=_cparams(("arbitrary", "arbitrary")),
    )(x, scale, shift, x, g.reshape(1, D_MODEL), scale, shift, w)


def _rope(x, cos, sin_signed, first_half):
    rot = jnp.where(first_half, pltpu.roll(x, HEAD_DIM - 32, axis=1), pltpu.roll(x, 32, axis=1))
    return x * cos + rot * sin_signed


def _softmax_pv(s, sink, v):
    m = jnp.maximum(jnp.max(s, axis=-1, keepdims=True), sink)
    e = jnp.exp2(s - m)
    denom = jnp.sum(e, axis=-1, keepdims=True) + jnp.exp2(sink - m)
    o = jnp.dot(e.astype(BF16), v, preferred_element_type=F32)
    return o / denom


def _band_bias(nctx):
    qi = (np.arange(Q_GROUP * BLOCK) % BLOCK)[None, :, None]
    kj = np.arange(3 * BLOCK + nctx)[None, None, :]
    variant = np.arange(3)[:, None, None]
    in_prev = kj < BLOCK
    in_next = (kj >= 2 * BLOCK) & (kj < 3 * BLOCK)
    valid = np.where(in_prev, (kj >= qi) & (variant != 0),
                     np.where(in_next, (kj - 2 * BLOCK <= qi) & (variant != 2), True))
    return jnp.asarray(np.where(valid, 0.0, NEG_INF).astype(np.float32))


def _win_attn_kernel(sink_ref, q_ref, kvp_ref, kvc_ref, kvn_ref, kx_ref, vx_ref, cos_ref, sin_ref,
                     bias_ref, o_ref, *, nb):
    i = pl.program_id(1)
    lane = lax.broadcasted_iota(jnp.int32, (BLOCK, HEAD_DIM), 1)
    first_half = (lane % 64) < 32

    def table(ref, blk):
        return ref[pl.ds(pl.multiple_of(blk * BLOCK, BLOCK), BLOCK), :]

    ip = jnp.maximum(i - 1, 0)
    inx = jnp.minimum(i + 1, nb - 1)
    cos_c, sin_c = table(cos_ref, i), table(sin_ref, i)
    cos_p, sin_p = table(cos_ref, ip), table(sin_ref, ip)
    cos_n, sin_n = table(cos_ref, inx), table(sin_ref, inx)

    rows = Q_GROUP * BLOCK
    head_in_group = lax.broadcasted_iota(jnp.int32, (rows, 1), 0) // BLOCK
    bias = bias_ref[jnp.where(i == 0, 0, jnp.where(i == nb - 1, 2, 1))]

    for h in range(N_KV_HEADS):
        hs = slice(h * HEAD_DIM, (h + 1) * HEAD_DIM)
        vs = slice(KV_WIDTH + h * HEAD_DIM, KV_WIDTH + (h + 1) * HEAD_DIM)
        k = jnp.concatenate([
            _rope(kvp_ref[:, hs], cos_p, sin_p, first_half),
            _rope(kvc_ref[:, hs], cos_c, sin_c, first_half),
            _rope(kvn_ref[:, hs], cos_n, sin_n, first_half),
            kx_ref[:, hs]], axis=0).astype(BF16)
        v = jnp.concatenate([kvp_ref[:, vs], kvc_ref[:, vs], kvn_ref[:, vs], vx_ref[:, hs]],
                            axis=0).astype(BF16)
        heads = [h * Q_GROUP + g for g in range(Q_GROUP)]
        q = jnp.concatenate(
            [_rope(q_ref[:, hd * HEAD_DIM:(hd + 1) * HEAD_DIM], cos_c, sin_c, first_half)
             for hd in heads], axis=0).astype(BF16)
        sink = jnp.zeros((rows, 1), F32)
        for g, hd in enumerate(heads):
            sink = jnp.where(head_in_group == g, sink_ref[hd] * LOG2E, sink)
        s = lax.dot_general(q, k, (((1,), (1,)), ((), ())), preferred_element_type=F32)
        o = _softmax_pv(s * LOGIT_SCALE + bias, sink, v).astype(o_ref.dtype)
        for g, hd in enumerate(heads):
            o_ref[:, hd * HEAD_DIM:(hd + 1) * HEAD_DIM] = o[g * BLOCK:(g + 1) * BLOCK]


def _window_attention(p, kx, vx, sink, cos_t, sin_t):
    b, l, _ = p.shape
    c = kx.shape[1]
    nb = l // BLOCK
    def kv_spec(shift):
        return pl.BlockSpec((None, BLOCK, 2 * KV_WIDTH),
                            lambda bi, i: (bi, jnp.clip(i + shift, 0, nb - 1), Q_END // (2 * KV_WIDTH)))

    assert nb >= 2
    bias = _band_bias(c)
    ctx_spec = pl.BlockSpec((None, c, KV_WIDTH), lambda bi, i: (bi, 0, 0))
    tab_spec = _single((l, HEAD_DIM), lambda bi, i: (0, 0))
    return pl.pallas_call(
        functools.partial(_win_attn_kernel, nb=nb),
        grid=(b, nb),
        in_specs=[pl.BlockSpec(memory_space=pltpu.SMEM),
                  pl.BlockSpec((None, BLOCK, ATTN_WIDTH), lambda bi, i: (bi, i, 0)),
                  kv_spec(-1), kv_spec(0), kv_spec(1),
                  ctx_spec, ctx_spec, tab_spec, tab_spec,
                  _single(bias.shape, lambda bi, i: (0, 0, 0))],
        out_specs=pl.BlockSpec((None, BLOCK, ATTN_WIDTH), lambda bi, i: (bi, i, 0)),
        out_shape=jax.ShapeDtypeStruct((b, l, ATTN_WIDTH), BF16),
        name="window_attention",
        compiler_params=_cparams(("arbitrary", "arbitrary")),
    )(sink, p, p, p, p, kx, vx, cos_t, sin_t, bias)


def _ctx_attn_kernel(sink_ref, q_ref, k_ref, v_ref, o_ref):
    for h in range(N_KV_HEADS):
        hs = slice(h * HEAD_DIM, (h + 1) * HEAD_DIM)
        k = k_ref[:, hs].astype(BF16)
        v = v_ref[:, hs].astype(BF16)
        for g in range(Q_GROUP):
            head = h * Q_GROUP + g
            cs = slice(head * HEAD_DIM, (head + 1) * HEAD_DIM)
            q = q_ref[:, cs].astype(BF16)
            s = lax.dot_general(q, k, (((1,), (1,)), ((), ())), preferred_element_type=F32)
            o_ref[:, cs] = _softmax_pv(s * LOGIT_SCALE, sink_ref[head] * LOG2E, v).astype(o_ref.dtype)


def _context_attention(p, sink):
    b, c, _ = p.shape
    return pl.pallas_call(
        _ctx_attn_kernel,
        grid=(b,),
        in_specs=[pl.BlockSpec(memory_space=pltpu.SMEM),
                  pl.BlockSpec((None, c, ATTN_WIDTH), lambda bi: (bi, 0, 0)),
                  pl.BlockSpec((None, c, KV_WIDTH), lambda bi: (bi, 0, Q_END // KV_WIDTH)),
                  pl.BlockSpec((None, c, KV_WIDTH), lambda bi: (bi, 0, K_END // KV_WIDTH))],
        out_specs=pl.BlockSpec((None, c, ATTN_WIDTH), lambda bi: (bi, 0, 0)),
        out_shape=jax.ShapeDtypeStruct((b, c, ATTN_WIDTH), BF16),
        name="context_attention",
        compiler_params=_cparams(("arbitrary",)),
    )(sink, p, p, p)


def _fft_tables(l, n2):
    n = 2 * l
    n1 = n // n2
    k1 = np.arange(n1 // 2, dtype=np.int64)[None, :, None]
    t = (n2 * np.arange(n1 // 2, dtype=np.int64)[None, None, :]
         + np.arange(n2, dtype=np.int64)[:, None, None])
    ang = (((2 * k1 + 1) * t) % (2 * n)).astype(np.float64) * (math.pi / n)
    fwd1 = np.concatenate([np.cos(ang), -np.sin(ang)], axis=1)
    inv1 = np.swapaxes(fwd1, 1, 2) * (2.0 / n)
    as_operand = lambda a: jnp.asarray(np.ascontiguousarray(a, dtype=np.float32)).astype(BF16)
    return as_operand(fwd1), as_operand(inv1)


def _filter_features(l):
    bands = (FILTER_EMB - 1) // 2
    t = np.linspace(0.0, 1.0, l)[:, None]
    w = 2 * math.pi * np.arange(l)[:, None] / l
    f = np.linspace(1e-4, bands - 1, bands)[None, :]
    z = np.concatenate([t, np.cos(f * w), -np.sin(f * w)], axis=-1)
    z = np.pad(z, ((0, 0), (0, FILTER_HIDDEN - FILTER_EMB)))
    return jnp.asarray(z.astype(np.float32))


def _filter_mlp_kernel(z_ref, w1_ref, b1_ref, f1_ref, w2_ref, b2_ref, f2_ref,
                       w3_ref, b3_ref, f3_ref, o_ref):
    h = jnp.sin(f1_ref[...] * (jnp.dot(z_ref[...], w1_ref[...], preferred_element_type=F32)
                               + b1_ref[...]))
    h = jnp.sin(f2_ref[...] * (jnp.dot(h, w2_ref[...], preferred_element_type=F32) + b2_ref[...]))
    o_ref[...] = jnp.sin(f3_ref[...] * (jnp.dot(h, w3_ref[...], preferred_element_type=F32)
                                        + b3_ref[...]))


def _filter_mlp(z, fw1, fb1, ff1, fw2, fb2, ff2, fw3, fb3, ff3):
    l = z.shape[0]
    row = lambda a: a.reshape(1, FILTER_HIDDEN)
    w1 = jnp.pad(fw1, ((0, FILTER_HIDDEN - FILTER_EMB), (0, 0)))
    return pl.pallas_call(
        _filter_mlp_kernel,
        out_shape=jax.ShapeDtypeStruct((l, FILTER_HIDDEN), F32),
        name="filter_mlp",
        compiler_params=pltpu.CompilerParams(vmem_limit_bytes=VMEM_LIMIT),
    )(z, w1, row(fb1), row(ff1), fw2, row(fb2), row(ff2), fw3, row(fb3), row(ff3))


def _cmul_root16(z, p, inverse):
    zr, zi = z
    p = p % 16
    if inverse:
        p = (16 - p) % 16
    if p == 0:
        return zr, zi
    if p == 4:
        return zi, -zr
    if p == 8:
        return -zr, -zi
    if p == 12:
        return -zi, zr
    c = math.cos(2.0 * math.pi * p / 16)
    s = -math.sin(2.0 * math.pi * p / 16)
    return zr * c - zi * s, zr * s + zi * c


def _dft4(z, inverse):
    (ar, ai), (br, bi), (cr, ci), (dr, di) = z
    t0r, t0i = ar + cr, ai + ci
    t1r, t1i = ar - cr, ai - ci
    t2r, t2i = br + dr, bi + di
    t3r, t3i = br - dr, bi - di
    y0 = (t0r + t2r, t0i + t2i)
    y2 = (t0r - t2r, t0i - t2i)
    minus_i_t3 = (t1r + t3i, t1i - t3r)
    plus_i_t3 = (t1r - t3i, t1i + t3r)
    return [y0, plus_i_t3, y2, minus_i_t3] if inverse else [y0, minus_i_t3, y2, plus_i_t3]


def _dft_digit(z, inverse):
    if len(z) == 1:
        return z
    assert len(z) == 16
    t = [_dft4([z[4 * a + b] for a in range(4)], inverse) for b in range(4)]
    out = [None] * 16
    for c in range(4):
        y = _dft4([_cmul_root16(t[b][c], b * c, inverse) for b in range(4)], inverse)
        for d in range(4):
            out[c + 4 * d] = y[d]
    return out


def _fft_stage1(src_refs, f1_ref, a_refs):
    n2cnt = f1_ref.shape[0]
    groups = a_refs[0].shape[0]
    cb = a_refs[0].shape[-1]
    n1 = groups * FFT_KG
    half = src_refs[0].shape[0] // n2cnt

    def body(n2, carry):
        rows = pl.ds(n2, half, stride=n2cnt) if n2cnt > 1 else pl.ds(0, half)
        x = [src_ref[rows, :].astype(BF16) for src_ref in src_refs]
        x = x[0] if len(x) == 1 else jnp.concatenate(x, axis=1)
        res = jnp.dot(f1_ref[n2], x, preferred_element_type=F32)
        for s, a_ref in enumerate(a_refs):
            cols = slice(s * cb, (s + 1) * cb)
            for g in range(groups):
                a_ref[g, 0, n2] = res[g * FFT_KG:(g + 1) * FFT_KG, cols]
                a_ref[g, 1, n2] = res[n1 + g * FFT_KG:n1 + (g + 1) * FFT_KG, cols]
        return carry

    if n2cnt == 1:
        body(0, 0)
    else:
        lax.fori_loop(0, n2cnt, body, 0, unroll=4)


def _load_digits(ref, g, rows):
    return [(ref[g, 0, d, rows, :], ref[g, 1, d, rows, :]) for d in range(ref.shape[2])]


def _filter_spec_kernel(h3_ref, wf_ref, wb_ref, df_ref, db_ref, f1_ref, kf_ref,
                        hf_ref, hb_ref, af_ref, ab_ref, *, l):
    groups = af_ref.shape[0]
    row = lax.broadcasted_iota(jnp.int32, (l, 1), 0)
    t = row.astype(F32) * (1.0 / (l - 1))
    h3 = h3_ref[...]
    hf = jnp.dot(h3, wf_ref[...], preferred_element_type=F32) * jnp.exp(-t * df_ref[...])
    hb = jnp.dot(h3, wb_ref[...], preferred_element_type=F32) * jnp.exp(-t * db_ref[...])
    hb = jnp.where(row > 0, hb, 0.0)
    norm = jnp.sum(jnp.abs(hf), axis=0, keepdims=True) + jnp.sum(jnp.abs(hb), axis=0, keepdims=True)
    hf_ref[...] = hf / norm
    hb_ref[...] = hb / norm

    def combine(g, carry):
        for r in range(0, FFT_KG, 8):
            rows = pl.ds(r, 8)
            for k2, (fr, fi) in enumerate(_dft_digit(_load_digits(af_ref, g, rows), False)):
                kf_ref[g, 0, k2, rows, :] = fr
                kf_ref[g, 1, k2, rows, :] = fi
            for k2, (br, bi) in enumerate(_dft_digit(_load_digits(ab_ref, g, rows), False)):
                kf_ref[g, 0, k2, rows, :] = kf_ref[g, 0, k2, rows, :] + br
                kf_ref[g, 1, k2, rows, :] = kf_ref[g, 1, k2, rows, :] - bi
        return carry

    _fft_stage1((hf_ref, hb_ref), f1_ref, (af_ref, ab_ref))
    lax.fori_loop(0, groups, combine, 0)


def _filter_spectrum(h3, fw4, decay, tables):
    l = h3.shape[0]
    f1, _ = tables
    n2 = f1.shape[0]
    groups = f1.shape[1] // 2 // FFT_KG
    nblk = HYENA_WIDTH // HY_CB
    spec_shape = (groups, 2, n2, FFT_KG)
    return pl.pallas_call(
        functools.partial(_filter_spec_kernel, l=l),
        grid=(nblk,),
        in_specs=[_single((l, FILTER_HIDDEN), lambda c: (0, 0)),
                  pl.BlockSpec((FILTER_HIDDEN, HY_CB), lambda c: (0, c)),
                  pl.BlockSpec((FILTER_HIDDEN, HY_CB), lambda c: (0, c + nblk)),
                  pl.BlockSpec((1, HY_CB), lambda c: (0, c)),
                  pl.BlockSpec((1, HY_CB), lambda c: (0, c + nblk)),
                  _single(f1.shape, lambda c: (0, 0, 0))],
        out_specs=pl.BlockSpec(spec_shape + (HY_CB,), lambda c: (0, 0, 0, 0, c)),
        out_shape=jax.ShapeDtypeStruct(spec_shape + (HYENA_WIDTH,), F32),
        scratch_shapes=[pltpu.VMEM((l, HY_CB), F32), pltpu.VMEM((l, HY_CB), F32),
                        pltpu.VMEM(spec_shape + (HY_CB,), F32),
                        pltpu.VMEM(spec_shape + (HY_CB,), F32)],
        name="filter_spectrum",
        compiler_params=_cparams(("arbitrary",)),
    )(h3, fw4, fw4, decay, decay, f1)


def _short_conv(src_ref, w_ref, b_ref, l, rows):
    w0, w1, w2, b = w_ref[0:1, :], w_ref[1:2, :], w_ref[2:3, :], b_ref[...]
    r = lax.broadcasted_iota(jnp.int32, (rows, 1), 0)
    for s in range(0, l, rows):
        cur = src_ref[s:s + rows, :]
        if s > 0:
            prev = src_ref[s - 1:s - 1 + rows, :]
        else:
            prev = jnp.where(r == 0, 0.0, pltpu.roll(cur, 1, axis=0))
        if s + rows < l:
            nxt = src_ref[s + 1:s + 1 + rows, :]
        else:
            nxt = jnp.where(r == rows - 1, 0.0, pltpu.roll(cur, rows - 1, axis=0))
        yield s, prev * w0 + cur * w1 + nxt * w2 + b


def _hyena_kernel(x0_ref, x1_ref, v_ref, w0_ref, w1_ref, wv_ref, b0_ref, b1_ref, bv_ref,
                  bias_ref, kf_ref, f1_ref, g1_ref, o_ref, x0c_ref, vg_ref, a_ref, *, l):
    groups = a_ref.shape[0]
    n2cnt = a_ref.shape[2]
    cb = a_ref.shape[-1]
    n1 = groups * FFT_KG
    half = l // n2cnt
    rows = min(l, 512)

    for s, u in _short_conv(x0_ref, w0_ref, b0_ref, l, rows):
        x0c_ref[s:s + rows, :] = u
    for (s, u1), (_, uv) in zip(_short_conv(x1_ref, w1_ref, b1_ref, l, rows),
                                _short_conv(v_ref, wv_ref, bv_ref, l, rows)):
        vg_ref[s:s + rows, :] = uv * u1

    _fft_stage1((vg_ref,), f1_ref, (a_ref,))

    def spectrum(g, carry):
        for r in range(0, FFT_KG, 8):
            rws = pl.ds(r, 8)
            x = _dft_digit(_load_digits(a_ref, g, rws), False)
            k = _load_digits(kf_ref, g, rws)
            y = [(xr * kr - xi * ki, xr * ki + xi * kr) for (xr, xi), (kr, ki) in zip(x, k)]
            for d, (br, bi) in enumerate(_dft_digit(y, True)):
                a_ref[g, 0, d, rws, :] = br
                a_ref[g, 1, d, rws, :] = bi
        return carry

    lax.fori_loop(0, groups, spectrum, 0)

    def synth(n2, carry):
        b = jnp.concatenate([a_ref[:, 0, n2].reshape(n1, cb), a_ref[:, 1, n2].reshape(n1, cb)],
                            axis=0).astype(BF16)
        y = jnp.dot(g1_ref[n2], b, preferred_element_type=F32)
        idx = pl.ds(n2, half, stride=n2cnt) if n2cnt > 1 else pl.ds(0, half)
        o_ref[idx, :] = y
        return carry

    if n2cnt == 1:
        synth(0, 0)
    else:
        lax.fori_loop(0, n2cnt, synth, 0, unroll=4)

    bias = bias_ref[...]
    for s in range(0, l, rows):
        blk = slice(s, s + rows)
        o_ref[blk, :] = (o_ref[blk, :] + vg_ref[blk, :] * bias) * x0c_ref[blk, :]


def _hyena(p, col0, conv_w, conv_b, bias_d, kf, tables):
    b, l, _ = p.shape
    f1, g1 = tables
    nblk = HYENA_WIDTH // HY_CB
    c0 = col0 // HY_CB
    spec_block = kf.shape[:-1] + (HY_CB,)

    def slab(part):
        return pl.BlockSpec((None, l, HY_CB), lambda c, bi: (bi, 0, c0 + part * nblk + c))

    def cw(part):
        return pl.BlockSpec((3, HY_CB), lambda c, bi: (0, part * nblk + c))

    def cbias(part):
        return pl.BlockSpec((1, HY_CB), lambda c, bi: (0, part * nblk + c))

    return pl.pallas_call(
        functools.partial(_hyena_kernel, l=l),
        grid=(nblk, b),
        in_specs=[slab(0), slab(1), slab(2), cw(0), cw(1), cw(2), cbias(0), cbias(1), cbias(2),
                  pl.BlockSpec((1, HY_CB), lambda c, bi: (0, c)),
                  pl.BlockSpec(spec_block, lambda c, bi: (0, 0, 0, 0, c)),
                  _single(f1.shape, lambda c, bi: (0, 0, 0)),
                  _single(g1.shape, lambda c, bi: (0, 0, 0))],
        out_specs=pl.BlockSpec((None, l, HY_CB), lambda c, bi: (bi, 0, c)),
        out_shape=jax.ShapeDtypeStruct((b, l, HYENA_WIDTH), F32),
        scratch_shapes=[pltpu.VMEM((l, HY_CB), F32), pltpu.VMEM((l, HY_CB), F32),
                        pltpu.VMEM(spec_block, F32)],
        name="hyena_conv",
        compiler_params=_cparams(("arbitrary", "arbitrary")),
    )(p, p, p, conv_w, conv_w, conv_w, conv_b, conv_b, conv_b, bias_d, kf, f1, g1)


def _merge_kernel(a_ref, h_ref, wa_ref, wh_ref, *refs):
    gate_refs, o_ref = refs[:-1], refs[-1]
    nt = len(gate_refs) // 2
    tg = gate_refs[0].shape[1]
    a = a_ref[...]
    h = h_ref[...].astype(BF16)
    for k in range(nt):
        cs = slice(k * tg, (k + 1) * tg)
        ya = jnp.dot(a, wa_ref[:, cs], preferred_element_type=F32)
        yh = jnp.dot(h, wh_ref[:, cs], preferred_element_type=F32)
        o_ref[:, cs] = (jax.nn.sigmoid(gate_refs[k][...]) * ya
                        + jax.nn.sigmoid(gate_refs[nt + k][...]) * yh).astype(o_ref.dtype)


def _merge(attn, hy, p, w_ao, w_ho, layer, tm):
    m = attn.shape[0]
    nt = D_MODEL // GATE_COLS

    def gate_spec(col0, k):
        return pl.BlockSpec((tm, GATE_COLS), lambda i: (i, col0 // GATE_COLS + k))

    gate_specs = ([gate_spec(HY_END, k) for k in range(nt)]
                  + [gate_spec(GA_END, k) for k in range(nt)])
    return pl.pallas_call(
        _merge_kernel,
        grid=(m // tm,),
        in_specs=[pl.BlockSpec((tm, ATTN_WIDTH), lambda i: (i, 0)),
                  pl.BlockSpec((tm, HYENA_WIDTH), lambda i: (i, 0)),
                  _single((None, ATTN_WIDTH, D_MODEL), lambda i: (layer, 0, 0)),
                  _single((None, HYENA_WIDTH, D_MODEL), lambda i: (layer, 0, 0))] + gate_specs,
        out_specs=pl.BlockSpec((tm, D_MODEL), lambda i: (i, 0)),
        out_shape=jax.ShapeDtypeStruct((m, D_MODEL), BF16),
        name="branch_merge",
        compiler_params=_cparams(("arbitrary",)),
    )(attn, hy, w_ao, w_ho, *([p] * (2 * nt)))


def _outproj_kernel(mix_ref, w_ref, x_ref, g_ref, gate_ref, o_ref, y_ref, *, tn):
    j = pl.program_id(1)
    nj = y_ref.shape[0]
    y_ref[j] = jnp.dot(mix_ref[...], w_ref[...], preferred_element_type=F32)

    @pl.when(j == nj - 1)
    def _():
        ss = sum(jnp.sum(jnp.square(y_ref[k]), axis=-1, keepdims=True) for k in range(nj))
        r = lax.rsqrt(ss * (1.0 / D_MODEL) + EPS)
        for k in range(nj):
            cs = slice(k * tn, (k + 1) * tn)
            o_ref[:, cs] = x_ref[:, cs] + gate_ref[:, cs] * ((y_ref[k] * r) * g_ref[:, cs])


def _outproj(mix, w_o, layer, x, g, gate, tm, tn):
    m = x.shape[0]
    nmod = gate.shape[0]
    blocks_per_mod = m // nmod // tm
    return pl.pallas_call(
        functools.partial(_outproj_kernel, tn=tn),
        grid=(m // tm, D_MODEL // tn),
        in_specs=[pl.BlockSpec((tm, D_MODEL), lambda i, j: (i, 0)),
                  pl.BlockSpec((None, D_MODEL, tn), lambda i, j: (layer, 0, j),
                               pipeline_mode=pl.Buffered(1) if tn == D_MODEL else None),
                  pl.BlockSpec((tm, D_MODEL), lambda i, j: (i, 0)),
                  pl.BlockSpec((1, D_MODEL), lambda i, j: (0, 0)),
                  pl.BlockSpec((None, 1, D_MODEL), lambda i, j: (i // blocks_per_mod, 0, 0))],
        out_specs=pl.BlockSpec((tm, D_MODEL), lambda i, j: (i, 0)),
        out_shape=jax.ShapeDtypeStruct((m, D_MODEL), F32),
        scratch_shapes=[pltpu.VMEM((D_MODEL // tn, tm, tn), F32)],
        name="out_projection",
        compiler_params=_cparams(("arbitrary", "arbitrary")),
    )(mix, w_o, x, g.reshape(1, D_MODEL), gate)


def _mlp_kernel(x_ref, xn_ref, gin_ref, sc_ref, sh_ref, scn_ref, shn_ref, w1_ref, w2_ref, gout_ref,
                gate_ref, o_ref, ha_ref, hb_ref, *, pro_steps):
    i = pl.program_id(0)
    j = pl.program_id(1)
    tm = x_ref.shape[0]
    chunk = xn_ref.shape[0]
    gin = gin_ref[...]

    def norm(x, sc_r, sh_r):
        return (_rms(x) * gin * (1.0 + sc_r[...]) + sh_r[...]).astype(BF16)

    @pl.when((i == 0) & (j == 0))
    def _():
        def first(rows):
            ha_ref[rows, :] = norm(x_ref[rows, :], sc_ref, sh_ref)

        _row_chunks(tm, first)

    @pl.when(j == 0)
    def _():
        def clear(rows):
            o_ref[rows, :] = jnp.zeros((PROLOGUE_ROWS, D_MODEL), F32)

        _row_chunks(tm, clear)

    def step(cur_ref, next_ref):
        rows = pl.ds(pl.multiple_of(jnp.minimum(j, pro_steps - 1) * chunk, chunk), chunk)
        next_ref[rows, :] = norm(xn_ref[...], scn_ref, shn_ref)
        a = jnp.dot(cur_ref[...], w1_ref[...].astype(BF16), preferred_element_type=F32)
        a = jnp.square(jnp.maximum(a, 0.0)).astype(BF16)
        for n in range(0, D_MODEL, MLP_COLS):
            o_ref[:, n:n + MLP_COLS] += jnp.dot(a, w2_ref[:, n:n + MLP_COLS].astype(BF16),
                                                preferred_element_type=F32)

    @pl.when(i % 2 == 0)
    def _():
        step(ha_ref, hb_ref)

    @pl.when(i % 2 == 1)
    def _():
        step(hb_ref, ha_ref)

    @pl.when(j == pl.num_programs(1) - 1)
    def _():
        gout, gate = gout_ref[...], gate_ref[...]

        def epilogue(rows):
            o_ref[rows, :] = x_ref[rows, :] + gate * (_rms(o_ref[rows, :]) * gout)

        _row_chunks(tm, epilogue)


def _mlp(x, g_in, scale, shift, w1, w2, layer, g_out, gate, tm, tf):
    m = x.shape[0]
    nb = m // tm
    nj = D_FF // tf
    nmod = gate.shape[0]
    blocks_per_mod = m // nmod // tm
    pro_steps = min(nj, NORM_STEPS)
    chunk = tm // pro_steps
    assert chunk * pro_steps == tm and chunk % 16 == 0 and tm % PROLOGUE_ROWS == 0

    def next_block(i):
        return jnp.minimum(i + 1, nb - 1)

    mod_spec = pl.BlockSpec((None, 1, D_MODEL), lambda i, j: (i // blocks_per_mod, 0, 0))
    next_mod = pl.BlockSpec((None, 1, D_MODEL),
                            lambda i, j: (next_block(i) // blocks_per_mod, 0, 0))
    row_spec = pl.BlockSpec((1, D_MODEL), lambda i, j: (0, 0))
    return pl.pallas_call(
        functools.partial(_mlp_kernel, pro_steps=pro_steps),
        grid=(nb, nj),
        in_specs=[pl.BlockSpec((tm, D_MODEL), lambda i, j: (i, 0), pipeline_mode=pl.Buffered(1)),
                  pl.BlockSpec((chunk, D_MODEL),
                               lambda i, j: (next_block(i) * pro_steps
                                             + jnp.minimum(j, pro_steps - 1), 0)),
                  row_spec, mod_spec, mod_spec, next_mod, next_mod,
                  pl.BlockSpec((None, D_MODEL, tf), lambda i, j: (layer, 0, j)),
                  pl.BlockSpec((None, tf, D_MODEL), lambda i, j: (layer, j, 0)),
                  row_spec, mod_spec],
        out_specs=pl.BlockSpec((tm, D_MODEL), lambda i, j: (i, 0)),
        out_shape=jax.ShapeDtypeStruct((m, D_MODEL), F32),
        scratch_shapes=[pltpu.VMEM((tm, D_MODEL), BF16), pltpu.VMEM((tm, D_MODEL), BF16)],
        name="channel_mlp",
        compiler_params=_cparams(("arbitrary", "arbitrary")),
    )(x, x, g_in.reshape(1, D_MODEL), scale, shift, scale, shift, w1, w2,
      g_out.reshape(1, D_MODEL), gate)


def _rope_tables(l):
    pos = np.arange(l)
    quarter = HEAD_DIM // 4
    freqs = ROPE_THETA ** (-np.arange(quarter) / quarter)
    ang_r = (pos // GRID_W)[:, None] * freqs[None, :]
    ang_c = (pos % GRID_W)[:, None] * freqs[None, :]
    cos_t = np.concatenate([np.cos(ang_r), np.cos(ang_r), np.cos(ang_c), np.cos(ang_c)], axis=-1)
    sin_t = np.concatenate([-np.sin(ang_r), np.sin(ang_r), -np.sin(ang_c), np.sin(ang_c)], axis=-1)
    return jnp.asarray(cos_t.astype(np.float32)), jnp.asarray(sin_t.astype(np.float32))


def _decay_rates():
    min_decay = math.log(DECAY_TARGET) / SLOW_DECAY_PCT
    max_decay = math.log(DECAY_TARGET) / FAST_DECAY_PCT
    deltas = np.tile(np.linspace(min_decay, max_decay, HYENA_WIDTH), 2)
    return jnp.asarray(np.abs(deltas)[None, :].astype(np.float32))


def kernel(x, c, ctx, c_ctx, w_mod, b_mod, norm_g, w_in, attn_sink, hy_conv_w, hy_conv_b,
           hy_fw1, hy_fb1, hy_ff1, hy_fw2, hy_fb2, hy_ff2, hy_fw3, hy_fb3, hy_ff3, hy_fw4,
           hy_bias, w_attn_out, w_hyena_out, w_out, w_ff1, w_ff2):
    b, l, d = x.shape
    cl = ctx.shape[1]
    assert d == D_MODEL and l % 1024 == 0 and cl % 256 == 0 and b + 1 <= MOD_ROWS

    cos_t, sin_t = _rope_tables(l)
    decay = _decay_rates()
    tables_lat = _fft_tables(l, FFT_N2)
    tables_ctx = _fft_tables(cl, 1)
    z_lat = _filter_features(l)
    z_ctx = _filter_features(cl)

    c_rows = jnp.concatenate([c, c_ctx[None, :], jnp.zeros((MOD_ROWS - b - 1, d), F32)], axis=0)
    x_lat = x.reshape(b * l, d)
    x_ctx = ctx.reshape(b * cl, d)
    tm_ctx = b * cl
    w_aob = w_attn_out.astype(BF16)
    w_hob = w_hyena_out.astype(BF16)
    w_ob = w_out.astype(BF16)

    for layer in range(DEPTH):
        last = layer == DEPTH - 1
        mod = _modulation(c_rows, w_mod, layer, b_mod[layer])
        mod_lat = [mod[:b, k * d:(k + 1) * d].reshape(b, 1, d) for k in range(N_MOD)]
        mod_ctx = [mod[b:b + 1, k * d:(k + 1) * d].reshape(1, 1, d) for k in range(N_MOD)]
        sh1, sc1, g1, sh2, sc2, g2 = mod_lat
        csh1, csc1, cg1, csh2, csc2, cg2 = mod_ctx
        g = norm_g[layer]
        fparams = (hy_fw1[layer], hy_fb1[layer], hy_ff1[layer], hy_fw2[layer], hy_fb2[layer],
                   hy_ff2[layer], hy_fw3[layer], hy_fb3[layer], hy_ff3[layer])
        bias_d = hy_bias[layer].reshape(1, HYENA_WIDTH)
        conv_b = hy_conv_b[layer].reshape(1, 3 * HYENA_WIDTH)

        p_lat = _normproj(x_lat, g[0], sc1, sh1, w_in, layer, 0, IN_WIDTH, 2048, 512)
        if last:
            kv_ctx = _normproj(x_ctx, g[0], csc1, csh1, w_in, layer, Q_END, V_END - Q_END,
                               tm_ctx, 512)
            kx, vx = kv_ctx[:, :KV_WIDTH], kv_ctx[:, KV_WIDTH:]
        else:
            p_ctx = _normproj(x_ctx, g[0], csc1, csh1, w_in, layer, 0, IN_WIDTH, tm_ctx, 512)
            kx, vx = p_ctx[:, Q_END:K_END], p_ctx[:, K_END:V_END]
        kx = kx.reshape(b, cl, KV_WIDTH)
        vx = vx.reshape(b, cl, KV_WIDTH)

        p3 = p_lat.reshape(b, l, IN_WIDTH)
        attn = _window_attention(p3, kx, vx, attn_sink[layer], cos_t, sin_t)
        kf = _filter_spectrum(_filter_mlp(z_lat, *fparams), hy_fw4[layer], decay, tables_lat)
        hy = _hyena(p3, V_END, hy_conv_w[layer], conv_b, bias_d, kf, tables_lat)
        mix = _merge(attn.reshape(b * l, ATTN_WIDTH), hy.reshape(b * l, HYENA_WIDTH), p_lat,
                     w_aob, w_hob, layer, 512)
        x_lat = _outproj(mix, w_ob, layer, x_lat, g[1], g1, 512, D_MODEL)

        if not last:
            pc3 = p_ctx.reshape(b, cl, IN_WIDTH)
            attn_c = _context_attention(pc3, attn_sink[layer])
            kf_c = _filter_spectrum(_filter_mlp(z_ctx, *fparams), hy_fw4[layer], decay, tables_ctx)
            hy_c = _hyena(pc3, V_END, hy_conv_w[layer], conv_b, bias_d, kf_c, tables_ctx)
            mix_c = _merge(attn_c.reshape(b * cl, ATTN_WIDTH), hy_c.reshape(b * cl, HYENA_WIDTH),
                           p_ctx, w_aob, w_hob, layer, tm_ctx)
            x_ctx = _outproj(mix_c, w_ob, layer, x_ctx, g[1], cg1, tm_ctx, D_MODEL)
            x_ctx = _mlp(x_ctx, g[2], csc2, csh2, w_ff1, w_ff2, layer, g[3], cg2, tm_ctx, 512)

        x_lat = _mlp(x_lat, g[2], sc2, sh2, w_ff1, w_ff2, layer, g[3], g2, 1024, 512)
    return x_lat.reshape(b, l, d)
```

```python
import functools
import math

import jax
import jax.numpy as jnp
import numpy as np
from jax import lax
from jax.experimental import pallas as pl
from jax.experimental.pallas import tpu as pltpu

F32 = jnp.float32
BF16 = jnp.bfloat16

D_MODEL = 2048
DEPTH = 2
GRID_W = 64
HEAD_DIM = 128
N_Q_HEADS = 8
N_KV_HEADS = 2
Q_GROUP = N_Q_HEADS // N_KV_HEADS
ATTN_WIDTH = N_Q_HEADS * HEAD_DIM
KV_WIDTH = N_KV_HEADS * HEAD_DIM
BLOCK = 128
ROPE_THETA = 10000.0
HYENA_WIDTH = 1024
FILTER_EMB = 33
FILTER_HIDDEN = 64
DECAY_TARGET = 1e-2
FAST_DECAY_PCT = 0.3
SLOW_DECAY_PCT = 1.5
D_FF = 4 * D_MODEL
EPS = 1e-6
N_MOD = 6
NEG_INF = -1e30
Q_END = ATTN_WIDTH
K_END = Q_END + KV_WIDTH
V_END = K_END + KV_WIDTH
HY_END = V_END + 3 * HYENA_WIDTH
GA_END = HY_END + D_MODEL
GH_END = GA_END + D_MODEL
IN_WIDTH = GH_END
LOG2E = math.log2(math.e)
LOGIT_SCALE = HEAD_DIM ** -0.5 * LOG2E

VMEM_LIMIT = 56 * 1024 * 1024

PROJ_ROWS = 2048
PROJ_COLS = 512
MLP_ROWS = 1024
MLP_FF = 512
MIX_ROWS = 512
FFT_N2 = 16
FFT_KG = 16
PROLOGUE_ROWS = 256
NORM_STEPS = 16
MLP_COLS = 512
GATE_COLS = 512
ATTN_QBLOCKS = 2
HY_CB = 128
MOD_ROWS = 8


def _cparams(sem):
    return pltpu.CompilerParams(dimension_semantics=sem, vmem_limit_bytes=VMEM_LIMIT)


def _single(block_shape, index_map):
    return pl.BlockSpec(block_shape, index_map, pipeline_mode=pl.Buffered(1))


def _rms(x):
    return x * lax.rsqrt(jnp.mean(x * x, axis=-1, keepdims=True) + EPS)


def _row_chunks(nrows, fn):
    def chunk(r, carry):
        fn(pl.ds(pl.multiple_of(r * PROLOGUE_ROWS, PROLOGUE_ROWS), PROLOGUE_ROWS))
        return carry

    lax.fori_loop(0, nrows // PROLOGUE_ROWS, chunk, 0)


def _mod_kernel(c_ref, w_ref, b_ref, o_ref):
    c = c_ref[...]
    s = c * jax.nn.sigmoid(c)
    o_ref[...] = jnp.dot(s.astype(BF16), w_ref[...].astype(BF16),
                         preferred_element_type=F32) + b_ref[...]


def _modulation(c_rows, w, layer, b):
    n = w.shape[2]
    tn = 1024
    return pl.pallas_call(
        _mod_kernel,
        grid=(n // tn,),
        in_specs=[pl.BlockSpec((MOD_ROWS, D_MODEL), lambda j: (0, 0)),
                  pl.BlockSpec((None, D_MODEL, tn), lambda j: (layer, 0, j)),
                  pl.BlockSpec((1, tn), lambda j: (0, j))],
        out_specs=pl.BlockSpec((MOD_ROWS, tn), lambda j: (0, j)),
        out_shape=jax.ShapeDtypeStruct((MOD_ROWS, n), F32),
        name="modulation",
        compiler_params=_cparams(("arbitrary",)),
    )(c_rows, w, b.reshape(1, n))


def _normproj_kernel(x0_ref, sc0_ref, sh0_ref, x_ref, g_ref, sc_ref, sh_ref, w_ref, o_ref,
                     ha_ref, hb_ref, *, pro_steps):
    i = pl.program_id(0)
    j = pl.program_id(1)
    chunk = x_ref.shape[0]
    g = g_ref[...]

    def norm(x, sc_r, sh_r):
        return (_rms(x) * g * (1.0 + sc_r[...]) + sh_r[...]).astype(BF16)

    @pl.when((i == 0) & (j == 0))
    def _():
        def first(rows):
            ha_ref[rows, :] = norm(x0_ref[rows, :], sc0_ref, sh0_ref)

        _row_chunks(x0_ref.shape[0], first)

    def step(cur_ref, next_ref):
        rows = pl.ds(pl.multiple_of(jnp.minimum(j, pro_steps - 1) * chunk, chunk), chunk)
        next_ref[rows, :] = norm(x_ref[...], sc_ref, sh_ref)
        o_ref[...] = jnp.dot(cur_ref[...], w_ref[...].astype(BF16), preferred_element_type=F32)

    @pl.when(i % 2 == 0)
    def _():
        step(ha_ref, hb_ref)

    @pl.when(i % 2 == 1)
    def _():
        step(hb_ref, ha_ref)


def _normproj(x, g, scale, shift, w, layer, col0, n, tm, tn):
    m = x.shape[0]
    nb = m // tm
    nj = n // tn
    j0 = col0 // tn
    nmod = scale.shape[0]
    blocks_per_mod = m // nmod // tm
    pro_steps = min(nj, NORM_STEPS)
    chunk = tm // pro_steps
    assert chunk * pro_steps == tm and chunk % 16 == 0 and tm % PROLOGUE_ROWS == 0

    def next_block(i):
        return jnp.minimum(i + 1, nb - 1)

    first_mod = pl.BlockSpec((None, 1, D_MODEL), lambda i, j: (0, 0, 0))
    next_mod = pl.BlockSpec((None, 1, D_MODEL),
                            lambda i, j: (next_block(i) // blocks_per_mod, 0, 0))
    return pl.pallas_call(
        functools.partial(_normproj_kernel, pro_steps=pro_steps),
        grid=(nb, nj),
        in_specs=[_single((tm, D_MODEL), lambda i, j: (0, 0)),
                  first_mod, first_mod,
                  pl.BlockSpec((chunk, D_MODEL),
                               lambda i, j: (next_block(i) * pro_steps
                                             + jnp.minimum(j, pro_steps - 1), 0)),
                  pl.BlockSpec((1, D_MODEL), lambda i, j: (0, 0)),
                  next_mod, next_mod,
                  pl.BlockSpec((None, D_MODEL, tn), lambda i, j: (layer, 0, j0 + j))],
        out_specs=pl.BlockSpec((tm, tn), lambda i, j: (i, j)),
        out_shape=jax.ShapeDtypeStruct((m, n), F32),
        scratch_shapes=[pltpu.VMEM((tm, D_MODEL), BF16), pltpu.VMEM((tm, D_MODEL), BF16)],
        name="normproj",
        compiler_params=_cparams(("arbitrary", "arbitrary")),
    )(x, scale, shift, x, g.reshape(1, D_MODEL), scale, shift, w)


def _rope(x, cos, sin_signed, first_half):
    rot = jnp.where(first_half, pltpu.roll(x, HEAD_DIM - 32, axis=1), pltpu.roll(x, 32, axis=1))
    return x * cos + rot * sin_signed


def _softmax_pv(s, sink, v):
    m = jnp.maximum(jnp.max(s, axis=-1, keepdims=True), sink)
    e = jnp.exp2(s - m)
    denom = jnp.sum(e, axis=-1, keepdims=True) + jnp.exp2(sink - m)
    o = jnp.dot(e.astype(BF16), v, preferred_element_type=F32)
    return o / denom


def _band_bias(nctx):
    qi = (np.arange(Q_GROUP * BLOCK) % BLOCK)[None, :, None]
    kj = np.arange(3 * BLOCK + nctx)[None, None, :]
    variant = np.arange(3)[:, None, None]
    in_prev = kj < BLOCK
    in_next = (kj >= 2 * BLOCK) & (kj < 3 * BLOCK)
    valid = np.where(in_prev, (kj >= qi) & (variant != 0),
                     np.where(in_next, (kj - 2 * BLOCK <= qi) & (variant != 2), True))
    return jnp.asarray(np.where(valid, 0.0, NEG_INF).astype(np.float32))


def _win_attn_kernel(sink_ref, q_ref, *refs, nb):
    nkv = ATTN_QBLOCKS + 2
    kv_refs = refs[:nkv]
    kx_ref, vx_ref, cos_ref, sin_ref, bias_ref, o_ref = refs[nkv:]
    blk0 = pl.program_id(1) * ATTN_QBLOCKS
    lane = lax.broadcasted_iota(jnp.int32, (BLOCK, HEAD_DIM), 1)
    first_half = (lane % 64) < 32

    def table(ref, blk):
        return ref[pl.ds(pl.multiple_of(blk * BLOCK, BLOCK), BLOCK), :]

    kpos = [jnp.clip(blk0 - 1 + t, 0, nb - 1) for t in range(nkv)]
    cos_k = [table(cos_ref, kp) for kp in kpos]
    sin_k = [table(sin_ref, kp) for kp in kpos]

    rows = Q_GROUP * BLOCK
    head_in_group = lax.broadcasted_iota(jnp.int32, (rows, 1), 0) // BLOCK
    bias = [bias_ref[jnp.where(blk0 + u == 0, 0, jnp.where(blk0 + u == nb - 1, 2, 1))]
            for u in range(ATTN_QBLOCKS)]

    for h in range(N_KV_HEADS):
        hs = slice(h * HEAD_DIM, (h + 1) * HEAD_DIM)
        vs = slice(KV_WIDTH + h * HEAD_DIM, KV_WIDTH + (h + 1) * HEAD_DIM)
        k_blocks = [_rope(kv_refs[t][:, hs], cos_k[t], sin_k[t], first_half).astype(BF16)
                    for t in range(nkv)]
        v_blocks = [kv_refs[t][:, vs].astype(BF16) for t in range(nkv)]
        k_ctx = kx_ref[:, hs].astype(BF16)
        v_ctx = vx_ref[:, hs].astype(BF16)
        heads = [h * Q_GROUP + g for g in range(Q_GROUP)]
        sink = jnp.zeros((rows, 1), F32)
        for g, hd in enumerate(heads):
            sink = jnp.where(head_in_group == g, sink_ref[hd] * LOG2E, sink)
        for u in range(ATTN_QBLOCKS):
            qrows = slice(u * BLOCK, (u + 1) * BLOCK)
            k = jnp.concatenate(k_blocks[u:u + 3] + [k_ctx], axis=0)
            v = jnp.concatenate(v_blocks[u:u + 3] + [v_ctx], axis=0)
            q = jnp.concatenate(
                [_rope(q_ref[qrows, hd * HEAD_DIM:(hd + 1) * HEAD_DIM], cos_k[u + 1], sin_k[u + 1],
                       first_half) for hd in heads], axis=0).astype(BF16)
            s = lax.dot_general(q, k, (((1,), (1,)), ((), ())), preferred_element_type=F32)
            o = _softmax_pv(s * LOGIT_SCALE + bias[u], sink, v).astype(o_ref.dtype)
            for g, hd in enumerate(heads):
                o_ref[qrows, hd * HEAD_DIM:(hd + 1) * HEAD_DIM] = o[g * BLOCK:(g + 1) * BLOCK]


def _window_attention(p, kx, vx, sink, cos_t, sin_t):
    b, l, _ = p.shape
    c = kx.shape[1]
    nb = l // BLOCK
    nkv = ATTN_QBLOCKS + 2
    assert nb >= 2 and nb % ATTN_QBLOCKS == 0

    def kv_spec(t):
        return pl.BlockSpec(
            (None, BLOCK, 2 * KV_WIDTH),
            lambda bi, i: (bi, jnp.clip(i * ATTN_QBLOCKS - 1 + t, 0, nb - 1), Q_END // (2 * KV_WIDTH)))

    bias = _band_bias(c)
    q_spec = pl.BlockSpec((None, ATTN_QBLOCKS * BLOCK, ATTN_WIDTH), lambda bi, i: (bi, i, 0))
    ctx_spec = pl.BlockSpec((None, c, KV_WIDTH), lambda bi, i: (bi, 0, 0))
    tab_spec = _single((l, HEAD_DIM), lambda bi, i: (0, 0))
    return pl.pallas_call(
        functools.partial(_win_attn_kernel, nb=nb),
        grid=(b, nb // ATTN_QBLOCKS),
        in_specs=[pl.BlockSpec(memory_space=pltpu.SMEM), q_spec]
                 + [kv_spec(t) for t in range(nkv)]
                 + [ctx_spec, ctx_spec, tab_spec, tab_spec,
                    _single(bias.shape, lambda bi, i: (0, 0, 0))],
        out_specs=q_spec,
        out_shape=jax.ShapeDtypeStruct((b, l, ATTN_WIDTH), BF16),
        name="window_attention",
        compiler_params=_cparams(("arbitrary", "arbitrary")),
    )(sink, p, *([p] * nkv), kx, vx, cos_t, sin_t, bias)


def _ctx_attn_kernel(sink_ref, q_ref, k_ref, v_ref, o_ref):
    for h in range(N_KV_HEADS):
        hs = slice(h * HEAD_DIM, (h + 1) * HEAD_DIM)
        k = k_ref[:, hs].astype(BF16)
        v = v_ref[:, hs].astype(BF16)
        for g in range(Q_GROUP):
            head = h * Q_GROUP + g
            cs = slice(head * HEAD_DIM, (head + 1) * HEAD_DIM)
            q = q_ref[:, cs].astype(BF16)
            s = lax.dot_general(q, k, (((1,), (1,)), ((), ())), preferred_element_type=F32)
            o_ref[:, cs] = _softmax_pv(s * LOGIT_SCALE, sink_ref[head] * LOG2E, v).astype(o_ref.dtype)


def _context_attention(p, sink):
    b, c, _ = p.shape
    return pl.pallas_call(
        _ctx_attn_kernel,
        grid=(b,),
        in_specs=[pl.BlockSpec(memory_space=pltpu.SMEM),
                  pl.BlockSpec((None, c, ATTN_WIDTH), lambda bi: (bi, 0, 0)),
                  pl.BlockSpec((None, c, KV_WIDTH), lambda bi: (bi, 0, Q_END // KV_WIDTH)),
                  pl.BlockSpec((None, c, KV_WIDTH), lambda bi: (bi, 0, K_END // KV_WIDTH))],
        out_specs=pl.BlockSpec((None, c, ATTN_WIDTH), lambda bi: (bi, 0, 0)),
        out_shape=jax.ShapeDtypeStruct((b, c, ATTN_WIDTH), BF16),
        name="context_attention",
        compiler_params=_cparams(("arbitrary",)),
    )(sink, p, p, p)


def _fft_tables(l, n2):
    n = 2 * l
    n1 = n // n2
    k1 = np.arange(n1 // 2, dtype=np.int64)[None, :, None]
    t = (n2 * np.arange(n1 // 2, dtype=np.int64)[None, None, :]
         + np.arange(n2, dtype=np.int64)[:, None, None])
    ang = (((2 * k1 + 1) * t) % (2 * n)).astype(np.float64) * (math.pi / n)
    fwd1 = np.concatenate([np.cos(ang), -np.sin(ang)], axis=1)
    inv1 = np.swapaxes(fwd1, 1, 2) * (2.0 / n)
    as_operand = lambda a: jnp.asarray(np.ascontiguousarray(a, dtype=np.float32)).astype(BF16)
    return as_operand(fwd1), as_operand(inv1)


def _filter_features(l):
    bands = (FILTER_EMB - 1) // 2
    t = np.linspace(0.0, 1.0, l)[:, None]
    w = 2 * math.pi * np.arange(l)[:, None] / l
    f = np.linspace(1e-4, bands - 1, bands)[None, :]
    z = np.concatenate([t, np.cos(f * w), -np.sin(f * w)], axis=-1)
    z = np.pad(z, ((0, 0), (0, FILTER_HIDDEN - FILTER_EMB)))
    return jnp.asarray(z.astype(np.float32))


def _filter_mlp_kernel(z_ref, w1_ref, b1_ref, f1_ref, w2_ref, b2_ref, f2_ref,
                       w3_ref, b3_ref, f3_ref, o_ref):
    h = jnp.sin(f1_ref[...] * (jnp.dot(z_ref[...], w1_ref[...], preferred_element_type=F32)
                               + b1_ref[...]))
    h = jnp.sin(f2_ref[...] * (jnp.dot(h, w2_ref[...], preferred_element_type=F32) + b2_ref[...]))
    o_ref[...] = jnp.sin(f3_ref[...] * (jnp.dot(h, w3_ref[...], preferred_element_type=F32)
                                        + b3_ref[...]))


def _filter_mlp(z, fw1, fb1, ff1, fw2, fb2, ff2, fw3, fb3, ff3):
    l = z.shape[0]
    row = lambda a: a.reshape(1, FILTER_HIDDEN)
    w1 = jnp.pad(fw1, ((0, FILTER_HIDDEN - FILTER_EMB), (0, 0)))
    return pl.pallas_call(
        _filter_mlp_kernel,
        out_shape=jax.ShapeDtypeStruct((l, FILTER_HIDDEN), F32),
        name="filter_mlp",
        compiler_params=pltpu.CompilerParams(vmem_limit_bytes=VMEM_LIMIT),
    )(z, w1, row(fb1), row(ff1), fw2, row(fb2), row(ff2), fw3, row(fb3), row(ff3))


def _cmul_root16(z, p, inverse):
    zr, zi = z
    p = p % 16
    if inverse:
        p = (16 - p) % 16
    if p == 0:
        return zr, zi
    if p == 4:
        return zi, -zr
    if p == 8:
        return -zr, -zi
    if p == 12:
        return -zi, zr
    c = math.cos(2.0 * math.pi * p / 16)
    s = -math.sin(2.0 * math.pi * p / 16)
    return zr * c - zi * s, zr * s + zi * c


def _dft4(z, inverse):
    (ar, ai), (br, bi), (cr, ci), (dr, di) = z
    t0r, t0i = ar + cr, ai + ci
    t1r, t1i = ar - cr, ai - ci
    t2r, t2i = br + dr, bi + di
    t3r, t3i = br - dr, bi - di
    y0 = (t0r + t2r, t0i + t2i)
    y2 = (t0r - t2r, t0i - t2i)
    minus_i_t3 = (t1r + t3i, t1i - t3r)
    plus_i_t3 = (t1r - t3i, t1i + t3r)
    return [y0, plus_i_t3, y2, minus_i_t3] if inverse else [y0, minus_i_t3, y2, plus_i_t3]


def _dft_digit(z, inverse):
    if len(z) == 1:
        return z
    assert len(z) == 16
    t = [_dft4([z[4 * a + b] for a in range(4)], inverse) for b in range(4)]
    out = [None] * 16
    for c in range(4):
        y = _dft4([_cmul_root16(t[b][c], b * c, inverse) for b in range(4)], inverse)
        for d in range(4):
            out[c + 4 * d] = y[d]
    return out


def _fft_stage1(src_refs, f1_ref, a_refs):
    n2cnt = f1_ref.shape[0]
    groups = a_refs[0].shape[0]
    cb = a_refs[0].shape[-1]
    n1 = groups * FFT_KG
    half = src_refs[0].shape[0] // n2cnt

    def body(n2, carry):
        rows = pl.ds(n2, half, stride=n2cnt) if n2cnt > 1 else pl.ds(0, half)
        x = [src_ref[rows, :].astype(BF16) for src_ref in src_refs]
        x = x[0] if len(x) == 1 else jnp.concatenate(x, axis=1)
        res = jnp.dot(f1_ref[n2], x, preferred_element_type=F32)
        for s, a_ref in enumerate(a_refs):
            cols = slice(s * cb, (s + 1) * cb)
            for g in range(groups):
                a_ref[g, 0, n2] = res[g * FFT_KG:(g + 1) * FFT_KG, cols]
                a_ref[g, 1, n2] = res[n1 + g * FFT_KG:n1 + (g + 1) * FFT_KG, cols]
        return carry

    if n2cnt == 1:
        body(0, 0)
    else:
        lax.fori_loop(0, n2cnt, body, 0, unroll=4)


def _load_digits(ref, g, rows):
    return [(ref[g, 0, d, rows, :], ref[g, 1, d, rows, :]) for d in range(ref.shape[2])]


def _filter_spec_kernel(h3_ref, wf_ref, wb_ref, df_ref, db_ref, f1_ref, kf_ref,
                        hf_ref, hb_ref, af_ref, ab_ref, *, l):
    groups = af_ref.shape[0]
    row = lax.broadcasted_iota(jnp.int32, (l, 1), 0)
    t = row.astype(F32) * (1.0 / (l - 1))
    h3 = h3_ref[...]
    hf = jnp.dot(h3, wf_ref[...], preferred_element_type=F32) * jnp.exp(-t * df_ref[...])
    hb = jnp.dot(h3, wb_ref[...], preferred_element_type=F32) * jnp.exp(-t * db_ref[...])
    hb = jnp.where(row > 0, hb, 0.0)
    norm = jnp.sum(jnp.abs(hf), axis=0, keepdims=True) + jnp.sum(jnp.abs(hb), axis=0, keepdims=True)
    hf_ref[...] = hf / norm
    hb_ref[...] = hb / norm

    def combine(g, carry):
        for r in range(0, FFT_KG, 8):
            rows = pl.ds(r, 8)
            for k2, (fr, fi) in enumerate(_dft_digit(_load_digits(af_ref, g, rows), False)):
                kf_ref[g, 0, k2, rows, :] = fr
                kf_ref[g, 1, k2, rows, :] = fi
            for k2, (br, bi) in enumerate(_dft_digit(_load_digits(ab_ref, g, rows), False)):
                kf_ref[g, 0, k2, rows, :] = kf_ref[g, 0, k2, rows, :] + br
                kf_ref[g, 1, k2, rows, :] = kf_ref[g, 1, k2, rows, :] - bi
        return carry

    _fft_stage1((hf_ref, hb_ref), f1_ref, (af_ref, ab_ref))
    lax.fori_loop(0, groups, combine, 0)


def _filter_spectrum(h3, fw4, decay, tables):
    l = h3.shape[0]
    f1, _ = tables
    n2 = f1.shape[0]
    groups = f1.shape[1] // 2 // FFT_KG
    nblk = HYENA_WIDTH // HY_CB
    spec_shape = (groups, 2, n2, FFT_KG)
    return pl.pallas_call(
        functools.partial(_filter_spec_kernel, l=l),
        grid=(nblk,),
        in_specs=[_single((l, FILTER_HIDDEN), lambda c: (0, 0)),
                  pl.BlockSpec((FILTER_HIDDEN, HY_CB), lambda c: (0, c)),
                  pl.BlockSpec((FILTER_HIDDEN, HY_CB), lambda c: (0, c + nblk)),
                  pl.BlockSpec((1, HY_CB), lambda c: (0, c)),
                  pl.BlockSpec((1, HY_CB), lambda c: (0, c + nblk)),
                  _single(f1.shape, lambda c: (0, 0, 0))],
        out_specs=pl.BlockSpec(spec_shape + (HY_CB,), lambda c: (0, 0, 0, 0, c)),
        out_shape=jax.ShapeDtypeStruct(spec_shape + (HYENA_WIDTH,), F32),
        scratch_shapes=[pltpu.VMEM((l, HY_CB), F32), pltpu.VMEM((l, HY_CB), F32),
                        pltpu.VMEM(spec_shape + (HY_CB,), F32),
                        pltpu.VMEM(spec_shape + (HY_CB,), F32)],
        name="filter_spectrum",
        compiler_params=_cparams(("arbitrary",)),
    )(h3, fw4, fw4, decay, decay, f1)


def _short_conv(src_ref, w_ref, b_ref, l, rows):
    w0, w1, w2, b = w_ref[0:1, :], w_ref[1:2, :], w_ref[2:3, :], b_ref[...]
    r = lax.broadcasted_iota(jnp.int32, (rows, 1), 0)
    for s in range(0, l, rows):
        cur = src_ref[s:s + rows, :]
        if s > 0:
            prev = src_ref[s - 1:s - 1 + rows, :]
        else:
            prev = jnp.where(r == 0, 0.0, pltpu.roll(cur, 1, axis=0))
        if s + rows < l:
            nxt = src_ref[s + 1:s + 1 + rows, :]
        else:
            nxt = jnp.where(r == rows - 1, 0.0, pltpu.roll(cur, rows - 1, axis=0))
        yield s, prev * w0 + cur * w1 + nxt * w2 + b


def _hyena_kernel(x0_ref, x1_ref, v_ref, w0_ref, w1_ref, wv_ref, b0_ref, b1_ref, bv_ref,
                  bias_ref, kf_ref, f1_ref, g1_ref, o_ref, x0c_ref, vg_ref, a_ref, *, l):
    groups = a_ref.shape[0]
    n2cnt = a_ref.shape[2]
    cb = a_ref.shape[-1]
    n1 = groups * FFT_KG
    half = l // n2cnt
    rows = min(l, 512)

    for s, u in _short_conv(x0_ref, w0_ref, b0_ref, l, rows):
        x0c_ref[s:s + rows, :] = u
    for (s, u1), (_, uv) in zip(_short_conv(x1_ref, w1_ref, b1_ref, l, rows),
                                _short_conv(v_ref, wv_ref, bv_ref, l, rows)):
        vg_ref[s:s + rows, :] = uv * u1

    _fft_stage1((vg_ref,), f1_ref, (a_ref,))

    def spectrum(g, carry):
        for r in range(0, FFT_KG, 8):
            rws = pl.ds(r, 8)
            x = _dft_digit(_load_digits(a_ref, g, rws), False)
            k = _load_digits(kf_ref, g, rws)
            y = [(xr * kr - xi * ki, xr * ki + xi * kr) for (xr, xi), (kr, ki) in zip(x, k)]
            for d, (br, bi) in enumerate(_dft_digit(y, True)):
                a_ref[g, 0, d, rws, :] = br
                a_ref[g, 1, d, rws, :] = bi
        return carry

    lax.fori_loop(0, groups, spectrum, 0)

    def synth(n2, carry):
        b = jnp.concatenate([a_ref[:, 0, n2].reshape(n1, cb), a_ref[:, 1, n2].reshape(n1, cb)],
                            axis=0).astype(BF16)
        y = jnp.dot(g1_ref[n2], b, preferred_element_type=F32)
        idx = pl.ds(n2, half, stride=n2cnt) if n2cnt > 1 else pl.ds(0, half)
        o_ref[idx, :] = y
        return carry

    if n2cnt == 1:
        synth(0, 0)
    else:
        lax.fori_loop(0, n2cnt, synth, 0, unroll=4)

    bias = bias_ref[...]
    for s in range(0, l, rows):
        blk = slice(s, s + rows)
        o_ref[blk, :] = (o_ref[blk, :] + vg_ref[blk, :] * bias) * x0c_ref[blk, :]


def _hyena(p, col0, conv_w, conv_b, bias_d, kf, tables):
    b, l, _ = p.shape
    f1, g1 = tables
    nblk = HYENA_WIDTH // HY_CB
    c0 = col0 // HY_CB
    spec_block = kf.shape[:-1] + (HY_CB,)

    def slab(part):
        return pl.BlockSpec((None, l, HY_CB), lambda c, bi: (bi, 0, c0 + part * nblk + c))

    def cw(part):
        return pl.BlockSpec((3, HY_CB), lambda c, bi: (0, part * nblk + c))

    def cbias(part):
        return pl.BlockSpec((1, HY_CB), lambda c, bi: (0, part * nblk + c))

    return pl.pallas_call(
        functools.partial(_hyena_kernel, l=l),
        grid=(nblk, b),
        in_specs=[slab(0), slab(1), slab(2), cw(0), cw(1), cw(2), cbias(0), cbias(1), cbias(2),
                  pl.BlockSpec((1, HY_CB), lambda c, bi: (0, c)),
                  pl.BlockSpec(spec_block, lambda c, bi: (0, 0, 0, 0, c)),
                  _single(f1.shape, lambda c, bi: (0, 0, 0)),
                  _single(g1.shape, lambda c, bi: (0, 0, 0))],
        out_specs=pl.BlockSpec((None, l, HY_CB), lambda c, bi: (bi, 0, c)),
        out_shape=jax.ShapeDtypeStruct((b, l, HYENA_WIDTH), F32),
        scratch_shapes=[pltpu.VMEM((l, HY_CB), F32), pltpu.VMEM((l, HY_CB), F32),
                        pltpu.VMEM(spec_block, F32)],
        name="hyena_conv",
        compiler_params=_cparams(("arbitrary", "arbitrary")),
    )(p, p, p, conv_w, conv_w, conv_w, conv_b, conv_b, conv_b, bias_d, kf, f1, g1)


def _merge_kernel(a_ref, h_ref, wa_ref, wh_ref, *refs):
    gate_refs, o_ref = refs[:-1], refs[-1]
    nt = len(gate_refs) // 2
    tg = gate_refs[0].shape[1]
    a = a_ref[...]
    h = h_ref[...].astype(BF16)
    for k in range(nt):
        cs = slice(k * tg, (k + 1) * tg)
        ya = jnp.dot(a, wa_ref[:, cs], preferred_element_type=F32)
        yh = jnp.dot(h, wh_ref[:, cs], preferred_element_type=F32)
        o_ref[:, cs] = (jax.nn.sigmoid(gate_refs[k][...]) * ya
                        + jax.nn.sigmoid(gate_refs[nt + k][...]) * yh).astype(o_ref.dtype)


def _merge(attn, hy, p, w_ao, w_ho, layer, tm):
    m = attn.shape[0]
    nt = D_MODEL // GATE_COLS

    def gate_spec(col0, k):
        return pl.BlockSpec((tm, GATE_COLS), lambda i: (i, col0 // GATE_COLS + k))

    gate_specs = ([gate_spec(HY_END, k) for k in range(nt)]
                  + [gate_spec(GA_END, k) for k in range(nt)])
    return pl.pallas_call(
        _merge_kernel,
        grid=(m // tm,),
        in_specs=[pl.BlockSpec((tm, ATTN_WIDTH), lambda i: (i, 0)),
                  pl.BlockSpec((tm, HYENA_WIDTH), lambda i: (i, 0)),
                  _single((None, ATTN_WIDTH, D_MODEL), lambda i: (layer, 0, 0)),
                  _single((None, HYENA_WIDTH, D_MODEL), lambda i: (layer, 0, 0))] + gate_specs,
        out_specs=pl.BlockSpec((tm, D_MODEL), lambda i: (i, 0)),
        out_shape=jax.ShapeDtypeStruct((m, D_MODEL), BF16),
        name="branch_merge",
        compiler_params=_cparams(("arbitrary",)),
    )(attn, hy, w_ao, w_ho, *([p] * (2 * nt)))


def _outproj_kernel(mix_ref, w_ref, x_ref, g_ref, gate_ref, o_ref, y_ref, *, tn):
    j = pl.program_id(1)
    nj = y_ref.shape[0]
    y_ref[j] = jnp.dot(mix_ref[...], w_ref[...], preferred_element_type=F32)

    @pl.when(j == nj - 1)
    def _():
        ss = sum(jnp.sum(jnp.square(y_ref[k]), axis=-1, keepdims=True) for k in range(nj))
        r = lax.rsqrt(ss * (1.0 / D_MODEL) + EPS)
        for k in range(nj):
            cs = slice(k * tn, (k + 1) * tn)
            o_ref[:, cs] = x_ref[:, cs] + gate_ref[:, cs] * ((y_ref[k] * r) * g_ref[:, cs])


def _outproj(mix, w_o, layer, x, g, gate, tm, tn):
    m = x.shape[0]
    nmod = gate.shape[0]
    blocks_per_mod = m // nmod // tm
    return pl.pallas_call(
        functools.partial(_outproj_kernel, tn=tn),
        grid=(m // tm, D_MODEL // tn),
        in_specs=[pl.BlockSpec((tm, D_MODEL), lambda i, j: (i, 0)),
                  pl.BlockSpec((None, D_MODEL, tn), lambda i, j: (layer, 0, j),
                               pipeline_mode=pl.Buffered(1) if tn == D_MODEL else None),
                  pl.BlockSpec((tm, D_MODEL), lambda i, j: (i, 0)),
                  pl.BlockSpec((1, D_MODEL), lambda i, j: (0, 0)),
                  pl.BlockSpec((None, 1, D_MODEL), lambda i, j: (i // blocks_per_mod, 0, 0))],
        out_specs=pl.BlockSpec((tm, D_MODEL), lambda i, j: (i, 0)),
        out_shape=jax.ShapeDtypeStruct((m, D_MODEL), F32),
        scratch_shapes=[pltpu.VMEM((D_MODEL // tn, tm, tn), F32)],
        name="out_projection",
        compiler_params=_cparams(("arbitrary", "arbitrary")),
    )(mix, w_o, x, g.reshape(1, D_MODEL), gate)


def _mlp_kernel(x_ref, xn_ref, gin_ref, sc_ref, sh_ref, scn_ref, shn_ref, w1_ref, w2_ref, gout_ref,
                gate_ref, o_ref, ha_ref, hb_ref, *, pro_steps):
    i = pl.program_id(0)
    j = pl.program_id(1)
    tm = x_ref.shape[0]
    chunk = xn_ref.shape[0]
    gin = gin_ref[...]

    def norm(x, sc_r, sh_r):
        return (_rms(x) * gin * (1.0 + sc_r[...]) + sh_r[...]).astype(BF16)

    @pl.when((i == 0) & (j == 0))
    def _():
        def first(rows):
            ha_ref[rows, :] = norm(x_ref[rows, :], sc_ref, sh_ref)

        _row_chunks(tm, first)

    @pl.when(j == 0)
    def _():
        def clear(rows):
            o_ref[rows, :] = jnp.zeros((PROLOGUE_ROWS, D_MODEL), F32)

        _row_chunks(tm, clear)

    def step(cur_ref, next_ref):
        rows = pl.ds(pl.multiple_of(jnp.minimum(j, pro_steps - 1) * chunk, chunk), chunk)
        next_ref[rows, :] = norm(xn_ref[...], scn_ref, shn_ref)
        a = jnp.dot(cur_ref[...], w1_ref[...].astype(BF16), preferred_element_type=F32)
        a = jnp.square(jnp.maximum(a, 0.0)).astype(BF16)
        for n in range(0, D_MODEL, MLP_COLS):
            o_ref[:, n:n + MLP_COLS] += jnp.dot(a, w2_ref[:, n:n + MLP_COLS].astype(BF16),
                                                preferred_element_type=F32)

    @pl.when(i % 2 == 0)
    def _():
        step(ha_ref, hb_ref)

    @pl.when(i % 2 == 1)
    def _():
        step(hb_ref, ha_ref)

    @pl.when(j == pl.num_programs(1) - 1)
    def _():
        gout, gate = gout_ref[...], gate_ref[...]

        def epilogue(rows):
            o_ref[rows, :] = x_ref[rows, :] + gate * (_rms(o_ref[rows, :]) * gout)

        _row_chunks(tm, epilogue)


def _mlp(x, g_in, scale, shift, w1, w2, layer, g_out, gate, tm, tf):
    m = x.shape[0]
    nb = m // tm
    nj = D_FF // tf
    nmod = gate.shape[0]
    blocks_per_mod = m // nmod // tm
    pro_steps = min(nj, NORM_STEPS)
    chunk = tm // pro_steps
    assert chunk * pro_steps == tm and chunk % 16 == 0 and tm % PROLOGUE_ROWS == 0

    def next_block(i):
        return jnp.minimum(i + 1, nb - 1)

    mod_spec = pl.BlockSpec((None, 1, D_MODEL), lambda i, j: (i // blocks_per_mod, 0, 0))
    next_mod = pl.BlockSpec((None, 1, D_MODEL),
                            lambda i, j: (next_block(i) // blocks_per_mod, 0, 0))
    row_spec = pl.BlockSpec((1, D_MODEL), lambda i, j: (0, 0))
    return pl.pallas_call(
        functools.partial(_mlp_kernel, pro_steps=pro_steps),
        grid=(nb, nj),
        in_specs=[pl.BlockSpec((tm, D_MODEL), lambda i, j: (i, 0), pipeline_mode=pl.Buffered(1)),
                  pl.BlockSpec((chunk, D_MODEL),
                               lambda i, j: (next_block(i) * pro_steps
                                             + jnp.minimum(j, pro_steps - 1), 0)),
                  row_spec, mod_spec, mod_spec, next_mod, next_mod,
                  pl.BlockSpec((None, D_MODEL, tf), lambda i, j: (layer, 0, j)),
                  pl.BlockSpec((None, tf, D_MODEL), lambda i, j: (layer, j, 0)),
                  row_spec, mod_spec],
        out_specs=pl.BlockSpec((tm, D_MODEL), lambda i, j: (i, 0)),
        out_shape=jax.ShapeDtypeStruct((m, D_MODEL), F32),
        scratch_shapes=[pltpu.VMEM((tm, D_MODEL), BF16), pltpu.VMEM((tm, D_MODEL), BF16)],
        name="channel_mlp",
        compiler_params=_cparams(("arbitrary", "arbitrary")),
    )(x, x, g_in.reshape(1, D_MODEL), scale, shift, scale, shift, w1, w2,
      g_out.reshape(1, D_MODEL), gate)


def _rope_tables(l):
    pos = np.arange(l)
    quarter = HEAD_DIM // 4
    freqs = ROPE_THETA ** (-np.arange(quarter) / quarter)
    ang_r = (pos // GRID_W)[:, None] * freqs[None, :]
    ang_c = (pos % GRID_W)[:, None] * freqs[None, :]
    cos_t = np.concatenate([np.cos(ang_r), np.cos(ang_r), np.cos(ang_c), np.cos(ang_c)], axis=-1)
    sin_t = np.concatenate([-np.sin(ang_r), np.sin(ang_r), -np.sin(ang_c), np.sin(ang_c)], axis=-1)
    return jnp.asarray(cos_t.astype(np.float32)), jnp.asarray(sin_t.astype(np.float32))


def _decay_rates():
    min_decay = math.log(DECAY_TARGET) / SLOW_DECAY_PCT
    max_decay = math.log(DECAY_TARGET) / FAST_DECAY_PCT
    deltas = np.tile(np.linspace(min_decay, max_decay, HYENA_WIDTH), 2)
    return jnp.asarray(np.abs(deltas)[None, :].astype(np.float32))


def kernel(x, c, ctx, c_ctx, w_mod, b_mod, norm_g, w_in, attn_sink, hy_conv_w, hy_conv_b,
           hy_fw1, hy_fb1, hy_ff1, hy_fw2, hy_fb2, hy_ff2, hy_fw3, hy_fb3, hy_ff3, hy_fw4,
           hy_bias, w_attn_out, w_hyena_out, w_out, w_ff1, w_ff2):
    b, l, d = x.shape
    cl = ctx.shape[1]
    assert d == D_MODEL and l % PROJ_ROWS == 0 and cl % 256 == 0 and b + 1 <= MOD_ROWS

    cos_t, sin_t = _rope_tables(l)
    decay = _decay_rates()
    tables_lat = _fft_tables(l, FFT_N2)
    tables_ctx = _fft_tables(cl, 1)
    z_lat = _filter_features(l)
    z_ctx = _filter_features(cl)

    c_rows = jnp.concatenate([c, c_ctx[None, :], jnp.zeros((MOD_ROWS - b - 1, d), F32)], axis=0)
    x_lat = x.reshape(b * l, d)
    x_ctx = ctx.reshape(b * cl, d)
    tm_ctx = b * cl
    w_aob = w_attn_out.astype(BF16)
    w_hob = w_hyena_out.astype(BF16)
    w_ob = w_out.astype(BF16)

    for layer in range(DEPTH):
        last = layer == DEPTH - 1
        mod = _modulation(c_rows, w_mod, layer, b_mod[layer])
        mod_lat = [mod[:b, k * d:(k + 1) * d].reshape(b, 1, d) for k in range(N_MOD)]
        mod_ctx = [mod[b:b + 1, k * d:(k + 1) * d].reshape(1, 1, d) for k in range(N_MOD)]
        sh1, sc1, g1, sh2, sc2, g2 = mod_lat
        csh1, csc1, cg1, csh2, csc2, cg2 = mod_ctx
        g = norm_g[layer]
        fparams = (hy_fw1[layer], hy_fb1[layer], hy_ff1[layer], hy_fw2[layer], hy_fb2[layer],
                   hy_ff2[layer], hy_fw3[layer], hy_fb3[layer], hy_ff3[layer])
        bias_d = hy_bias[layer].reshape(1, HYENA_WIDTH)
        conv_b = hy_conv_b[layer].reshape(1, 3 * HYENA_WIDTH)

        p_lat = _normproj(x_lat, g[0], sc1, sh1, w_in, layer, 0, IN_WIDTH, PROJ_ROWS, PROJ_COLS)
        if last:
            kv_ctx = _normproj(x_ctx, g[0], csc1, csh1, w_in, layer, Q_END, V_END - Q_END,
                               tm_ctx, PROJ_COLS)
            kx, vx = kv_ctx[:, :KV_WIDTH], kv_ctx[:, KV_WIDTH:]
        else:
            p_ctx = _normproj(x_ctx, g[0], csc1, csh1, w_in, layer, 0, IN_WIDTH, tm_ctx, PROJ_COLS)
            kx, vx = p_ctx[:, Q_END:K_END], p_ctx[:, K_END:V_END]
        kx = kx.reshape(b, cl, KV_WIDTH)
        vx = vx.reshape(b, cl, KV_WIDTH)

        p3 = p_lat.reshape(b, l, IN_WIDTH)
        attn = _window_attention(p3, kx, vx, attn_sink[layer], cos_t, sin_t)
        kf = _filter_spectrum(_filter_mlp(z_lat, *fparams), hy_fw4[layer], decay, tables_lat)
        hy = _hyena(p3, V_END, hy_conv_w[layer], conv_b, bias_d, kf, tables_lat)
        mix = _merge(attn.reshape(b * l, ATTN_WIDTH), hy.reshape(b * l, HYENA_WIDTH), p_lat,
                     w_aob, w_hob, layer, MIX_ROWS)
        x_lat = _outproj(mix, w_ob, layer, x_lat, g[1], g1, MIX_ROWS, D_MODEL)

        if not last:
            pc3 = p_ctx.reshape(b, cl, IN_WIDTH)
            attn_c = _context_attention(pc3, attn_sink[layer])
            kf_c = _filter_spectrum(_filter_mlp(z_ctx, *fparams), hy_fw4[layer], decay, tables_ctx)
            hy_c = _hyena(pc3, V_END, hy_conv_w[layer], conv_b, bias_d, kf_c, tables_ctx)
            mix_c = _merge(attn_c.reshape(b * cl, ATTN_WIDTH), hy_c.reshape(b * cl, HYENA_WIDTH),
                           p_ctx, w_aob, w_hob, layer, tm_ctx)
            x_ctx = _outproj(mix_c, w_ob, layer, x_ctx, g[1], cg1, tm_ctx, D_MODEL)
            x_ctx = _mlp(x_ctx, g[2], csc2, csh2, w_ff1, w_ff2, layer, g[3], cg2, tm_ctx, MLP_FF)

        x_lat = _mlp(x_lat, g[2], sc2, sh2, w_ff1, w_ff2, layer, g[3], g2, MLP_ROWS, MLP_FF)
    return x_lat.reshape(b, l, d)
```

```python
import functools
import math

import jax
import jax.numpy as jnp
import numpy as np
from jax import lax
from jax.experimental import pallas as pl
from jax.experimental.pallas import tpu as pltpu

F32 = jnp.float32
BF16 = jnp.bfloat16

D_MODEL = 2048
DEPTH = 2
GRID_W = 64
HEAD_DIM = 128
N_Q_HEADS = 8
N_KV_HEADS = 2
Q_GROUP = N_Q_HEADS // N_KV_HEADS
ATTN_WIDTH = N_Q_HEADS * HEAD_DIM
KV_WIDTH = N_KV_HEADS * HEAD_DIM
BLOCK = 128
ROPE_THETA = 10000.0
HYENA_WIDTH = 1024
FILTER_EMB = 33
FILTER_HIDDEN = 64
DECAY_TARGET = 1e-2
FAST_DECAY_PCT = 0.3
SLOW_DECAY_PCT = 1.5
D_FF = 4 * D_MODEL
EPS = 1e-6
N_MOD = 6
NEG_INF = -1e30
Q_END = ATTN_WIDTH
K_END = Q_END + KV_WIDTH
V_END = K_END + KV_WIDTH
HY_END = V_END + 3 * HYENA_WIDTH
GA_END = HY_END + D_MODEL
GH_END = GA_END + D_MODEL
IN_WIDTH = GH_END
LOG2E = math.log2(math.e)
LOGIT_SCALE = HEAD_DIM ** -0.5 * LOG2E

VMEM_LIMIT = 56 * 1024 * 1024

PROJ_ROWS = 2048
PROJ_COLS = 512
MLP_ROWS = 1024
MLP_FF = 512
MIX_ROWS = 512
FFT_N2 = 16
FFT_KG = 16
PROLOGUE_ROWS = 256
NORM_STEPS = 16
MLP_COLS = 512
GATE_COLS = 512
ATTN_QBLOCKS = 4
HY_CB = 128
MOD_ROWS = 8


def _cparams(sem):
    return pltpu.CompilerParams(dimension_semantics=sem, vmem_limit_bytes=VMEM_LIMIT)


def _single(block_shape, index_map):
    return pl.BlockSpec(block_shape, index_map, pipeline_mode=pl.Buffered(1))


def _rms(x):
    return x * lax.rsqrt(jnp.mean(x * x, axis=-1, keepdims=True) + EPS)


def _row_chunks(nrows, fn):
    def chunk(r, carry):
        fn(pl.ds(pl.multiple_of(r * PROLOGUE_ROWS, PROLOGUE_ROWS), PROLOGUE_ROWS))
        return carry

    lax.fori_loop(0, nrows // PROLOGUE_ROWS, chunk, 0)


def _mod_kernel(c_ref, w_ref, b_ref, o_ref):
    c = c_ref[...]
    s = c * jax.nn.sigmoid(c)
    o_ref[...] = jnp.dot(s.astype(BF16), w_ref[...].astype(BF16),
                         preferred_element_type=F32) + b_ref[...]


def _modulation(c_rows, w, layer, b):
    n = w.shape[2]
    tn = 1024
    return pl.pallas_call(
        _mod_kernel,
        grid=(n // tn,),
        in_specs=[pl.BlockSpec((MOD_ROWS, D_MODEL), lambda j: (0, 0)),
                  pl.BlockSpec((None, D_MODEL, tn), lambda j: (layer, 0, j)),
                  pl.BlockSpec((1, tn), lambda j: (0, j))],
        out_specs=pl.BlockSpec((MOD_ROWS, tn), lambda j: (0, j)),
        out_shape=jax.ShapeDtypeStruct((MOD_ROWS, n), F32),
        name="modulation",
        compiler_params=_cparams(("arbitrary",)),
    )(c_rows, w, b.reshape(1, n))


def _normproj_kernel(x0_ref, sc0_ref, sh0_ref, x_ref, g_ref, sc_ref, sh_ref, w_ref, o_ref,
                     ha_ref, hb_ref, *, pro_steps):
    i = pl.program_id(0)
    j = pl.program_id(1)
    chunk = x_ref.shape[0]
    g = g_ref[...]

    def norm(x, sc_r, sh_r):
        return (_rms(x) * g * (1.0 + sc_r[...]) + sh_r[...]).astype(BF16)

    @pl.when((i == 0) & (j == 0))
    def _():
        def first(rows):
            ha_ref[rows, :] = norm(x0_ref[rows, :], sc0_ref, sh0_ref)

        _row_chunks(x0_ref.shape[0], first)

    def step(cur_ref, next_ref):
        rows = pl.ds(pl.multiple_of(jnp.minimum(j, pro_steps - 1) * chunk, chunk), chunk)
        next_ref[rows, :] = norm(x_ref[...], sc_ref, sh_ref)
        o_ref[...] = jnp.dot(cur_ref[...], w_ref[...].astype(BF16), preferred_element_type=F32)

    @pl.when(i % 2 == 0)
    def _():
        step(ha_ref, hb_ref)

    @pl.when(i % 2 == 1)
    def _():
        step(hb_ref, ha_ref)


def _normproj(x, g, scale, shift, w, layer, col0, n, tm, tn):
    m = x.shape[0]
    nb = m // tm
    nj = n // tn
    j0 = col0 // tn
    nmod = scale.shape[0]
    blocks_per_mod = m // nmod // tm
    pro_steps = min(nj, NORM_STEPS)
    chunk = tm // pro_steps
    assert chunk * pro_steps == tm and chunk % 16 == 0 and tm % PROLOGUE_ROWS == 0

    def next_block(i):
        return jnp.minimum(i + 1, nb - 1)

    first_mod = pl.BlockSpec((None, 1, D_MODEL), lambda i, j: (0, 0, 0))
    next_mod = pl.BlockSpec((None, 1, D_MODEL),
                            lambda i, j: (next_block(i) // blocks_per_mod, 0, 0))
    return pl.pallas_call(
        functools.partial(_normproj_kernel, pro_steps=pro_steps),
        grid=(nb, nj),
        in_specs=[_single((tm, D_MODEL), lambda i, j: (0, 0)),
                  first_mod, first_mod,
                  pl.BlockSpec((chunk, D_MODEL),
                               lambda i, j: (next_block(i) * pro_steps
                                             + jnp.minimum(j, pro_steps - 1), 0)),
                  pl.BlockSpec((1, D_MODEL), lambda i, j: (0, 0)),
                  next_mod, next_mod,
                  pl.BlockSpec((None, D_MODEL, tn), lambda i, j: (layer, 0, j0 + j))],
        out_specs=pl.BlockSpec((tm, tn), lambda i, j: (i, j)),
        out_shape=jax.ShapeDtypeStruct((m, n), F32),
        scratch_shapes=[pltpu.VMEM((tm, D_MODEL), BF16), pltpu.VMEM((tm, D_MODEL), BF16)],
        name="normproj",
        compiler_params=_cparams(("arbitrary", "arbitrary")),
    )(x, scale, shift, x, g.reshape(1, D_MODEL), scale, shift, w)


def _rope(x, cos, sin_signed, first_half):
    rot = jnp.where(first_half, pltpu.roll(x, HEAD_DIM - 32, axis=1), pltpu.roll(x, 32, axis=1))
    return x * cos + rot * sin_signed


def _softmax_pv(s, sink, v):
    m = jnp.maximum(jnp.max(s, axis=-1, keepdims=True), sink)
    e = jnp.exp2(s - m)
    denom = jnp.sum(e, axis=-1, keepdims=True) + jnp.exp2(sink - m)
    o = jnp.dot(e.astype(BF16), v, preferred_element_type=F32)
    return o / denom


def _band_bias(nctx):
    qi = (np.arange(Q_GROUP * BLOCK) % BLOCK)[None, :, None]
    kj = np.arange(3 * BLOCK + nctx)[None, None, :]
    variant = np.arange(3)[:, None, None]
    in_prev = kj < BLOCK
    in_next = (kj >= 2 * BLOCK) & (kj < 3 * BLOCK)
    valid = np.where(in_prev, (kj >= qi) & (variant != 0),
                     np.where(in_next, (kj - 2 * BLOCK <= qi) & (variant != 2), True))
    return jnp.asarray(np.where(valid, 0.0, NEG_INF).astype(np.float32))


def _win_attn_kernel(sink_ref, q_ref, *refs, nb):
    nkv = ATTN_QBLOCKS + 2
    kv_refs = refs[:nkv]
    kx_ref, vx_ref, cos_ref, sin_ref, bias_ref, o_ref = refs[nkv:]
    blk0 = pl.program_id(1) * ATTN_QBLOCKS
    lane = lax.broadcasted_iota(jnp.int32, (BLOCK, HEAD_DIM), 1)
    first_half = (lane % 64) < 32

    def table(ref, blk):
        return ref[pl.ds(pl.multiple_of(blk * BLOCK, BLOCK), BLOCK), :]

    kpos = [jnp.clip(blk0 - 1 + t, 0, nb - 1) for t in range(nkv)]
    cos_k = [table(cos_ref, kp) for kp in kpos]
    sin_k = [table(sin_ref, kp) for kp in kpos]

    rows = Q_GROUP * BLOCK
    head_in_group = lax.broadcasted_iota(jnp.int32, (rows, 1), 0) // BLOCK
    bias = [bias_ref[jnp.where(blk0 + u == 0, 0, jnp.where(blk0 + u == nb - 1, 2, 1))]
            for u in range(ATTN_QBLOCKS)]

    for h in range(N_KV_HEADS):
        hs = slice(h * HEAD_DIM, (h + 1) * HEAD_DIM)
        vs = slice(KV_WIDTH + h * HEAD_DIM, KV_WIDTH + (h + 1) * HEAD_DIM)
        k_blocks = [_rope(kv_refs[t][:, hs], cos_k[t], sin_k[t], first_half).astype(BF16)
                    for t in range(nkv)]
        v_blocks = [kv_refs[t][:, vs].astype(BF16) for t in range(nkv)]
        k_ctx = kx_ref[:, hs].astype(BF16)
        v_ctx = vx_ref[:, hs].astype(BF16)
        heads = [h * Q_GROUP + g for g in range(Q_GROUP)]
        sink = jnp.zeros((rows, 1), F32)
        for g, hd in enumerate(heads):
            sink = jnp.where(head_in_group == g, sink_ref[hd] * LOG2E, sink)
        for u in range(ATTN_QBLOCKS):
            qrows = slice(u * BLOCK, (u + 1) * BLOCK)
            k = jnp.concatenate(k_blocks[u:u + 3] + [k_ctx], axis=0)
            v = jnp.concatenate(v_blocks[u:u + 3] + [v_ctx], axis=0)
            q = jnp.concatenate(
                [_rope(q_ref[qrows, hd * HEAD_DIM:(hd + 1) * HEAD_DIM], cos_k[u + 1], sin_k[u + 1],
                       first_half) for hd in heads], axis=0).astype(BF16)
            s = lax.dot_general(q, k, (((1,), (1,)), ((), ())), preferred_element_type=F32)
            o = _softmax_pv(s * LOGIT_SCALE + bias[u], sink, v).astype(o_ref.dtype)
            for g, hd in enumerate(heads):
                o_ref[qrows, hd * HEAD_DIM:(hd + 1) * HEAD_DIM] = o[g * BLOCK:(g + 1) * BLOCK]


def _window_attention(p, kx, vx, sink, cos_t, sin_t):
    b, l, _ = p.shape
    c = kx.shape[1]
    nb = l // BLOCK
    nkv = ATTN_QBLOCKS + 2
    assert nb >= 2 and nb % ATTN_QBLOCKS == 0

    def kv_spec(t):
        return pl.BlockSpec(
            (None, BLOCK, 2 * KV_WIDTH),
            lambda bi, i: (bi, jnp.clip(i * ATTN_QBLOCKS - 1 + t, 0, nb - 1), Q_END // (2 * KV_WIDTH)))

    bias = _band_bias(c)
    q_spec = pl.BlockSpec((None, ATTN_QBLOCKS * BLOCK, ATTN_WIDTH), lambda bi, i: (bi, i, 0))
    ctx_spec = pl.BlockSpec((None, c, KV_WIDTH), lambda bi, i: (bi, 0, 0))
    tab_spec = _single((l, HEAD_DIM), lambda bi, i: (0, 0))
    return pl.pallas_call(
        functools.partial(_win_attn_kernel, nb=nb),
        grid=(b, nb // ATTN_QBLOCKS),
        in_specs=[pl.BlockSpec(memory_space=pltpu.SMEM), q_spec]
                 + [kv_spec(t) for t in range(nkv)]
                 + [ctx_spec, ctx_spec, tab_spec, tab_spec,
                    _single(bias.shape, lambda bi, i: (0, 0, 0))],
        out_specs=q_spec,
        out_shape=jax.ShapeDtypeStruct((b, l, ATTN_WIDTH), BF16),
        name="window_attention",
        compiler_params=_cparams(("arbitrary", "arbitrary")),
    )(sink, p, *([p] * nkv), kx, vx, cos_t, sin_t, bias)


def _ctx_attn_kernel(sink_ref, q_ref, k_ref, v_ref, o_ref):
    for h in range(N_KV_HEADS):
        hs = slice(h * HEAD_DIM, (h + 1) * HEAD_DIM)
        k = k_ref[:, hs].astype(BF16)
        v = v_ref[:, hs].astype(BF16)
        for g in range(Q_GROUP):
            head = h * Q_GROUP + g
            cs = slice(head * HEAD_DIM, (head + 1) * HEAD_DIM)
            q = q_ref[:, cs].astype(BF16)
            s = lax.dot_general(q, k, (((1,), (1,)), ((), ())), preferred_element_type=F32)
            o_ref[:, cs] = _softmax_pv(s * LOGIT_SCALE, sink_ref[head] * LOG2E, v).astype(o_ref.dtype)


def _context_attention(p, sink):
    b, c, _ = p.shape
    return pl.pallas_call(
        _ctx_attn_kernel,
        grid=(b,),
        in_specs=[pl.BlockSpec(memory_space=pltpu.SMEM),
                  pl.BlockSpec((None, c, ATTN_WIDTH), lambda bi: (bi, 0, 0)),
                  pl.BlockSpec((None, c, KV_WIDTH), lambda bi: (bi, 0, Q_END // KV_WIDTH)),
                  pl.BlockSpec((None, c, KV_WIDTH), lambda bi: (bi, 0, K_END // KV_WIDTH))],
        out_specs=pl.BlockSpec((None, c, ATTN_WIDTH), lambda bi: (bi, 0, 0)),
        out_shape=jax.ShapeDtypeStruct((b, c, ATTN_WIDTH), BF16),
        name="context_attention",
        compiler_params=_cparams(("arbitrary",)),
    )(sink, p, p, p)


def _fft_tables(l, n2):
    n = 2 * l
    n1 = n // n2
    k1 = np.arange(n1 // 2, dtype=np.int64)[None, :, None]
    t = (n2 * np.arange(n1 // 2, dtype=np.int64)[None, None, :]
         + np.arange(n2, dtype=np.int64)[:, None, None])
    ang = (((2 * k1 + 1) * t) % (2 * n)).astype(np.float64) * (math.pi / n)
    fwd1 = np.concatenate([np.cos(ang), -np.sin(ang)], axis=1)
    inv1 = np.swapaxes(fwd1, 1, 2) * (2.0 / n)
    as_operand = lambda a: jnp.asarray(np.ascontiguousarray(a, dtype=np.float32)).astype(BF16)
    return as_operand(fwd1), as_operand(inv1)


def _filter_features(l):
    bands = (FILTER_EMB - 1) // 2
    t = np.linspace(0.0, 1.0, l)[:, None]
    w = 2 * math.pi * np.arange(l)[:, None] / l
    f = np.linspace(1e-4, bands - 1, bands)[None, :]
    z = np.concatenate([t, np.cos(f * w), -np.sin(f * w)], axis=-1)
    z = np.pad(z, ((0, 0), (0, FILTER_HIDDEN - FILTER_EMB)))
    return jnp.asarray(z.astype(np.float32))


def _filter_mlp_kernel(z_ref, w1_ref, b1_ref, f1_ref, w2_ref, b2_ref, f2_ref,
                       w3_ref, b3_ref, f3_ref, o_ref):
    h = jnp.sin(f1_ref[...] * (jnp.dot(z_ref[...], w1_ref[...], preferred_element_type=F32)
                               + b1_ref[...]))
    h = jnp.sin(f2_ref[...] * (jnp.dot(h, w2_ref[...], preferred_element_type=F32) + b2_ref[...]))
    o_ref[...] = jnp.sin(f3_ref[...] * (jnp.dot(h, w3_ref[...], preferred_element_type=F32)
                                        + b3_ref[...]))


def _filter_mlp(z, fw1, fb1, ff1, fw2, fb2, ff2, fw3, fb3, ff3):
    l = z.shape[0]
    row = lambda a: a.reshape(1, FILTER_HIDDEN)
    w1 = jnp.pad(fw1, ((0, FILTER_HIDDEN - FILTER_EMB), (0, 0)))
    return pl.pallas_call(
        _filter_mlp_kernel,
        out_shape=jax.ShapeDtypeStruct((l, FILTER_HIDDEN), F32),
        name="filter_mlp",
        compiler_params=pltpu.CompilerParams(vmem_limit_bytes=VMEM_LIMIT),
    )(z, w1, row(fb1), row(ff1), fw2, row(fb2), row(ff2), fw3, row(fb3), row(ff3))


def _cmul_root16(z, p, inverse):
    zr, zi = z
    p = p % 16
    if inverse:
        p = (16 - p) % 16
    if p == 0:
        return zr, zi
    if p == 4:
        return zi, -zr
    if p == 8:
        return -zr, -zi
    if p == 12:
        return -zi, zr
    c = math.cos(2.0 * math.pi * p / 16)
    s = -math.sin(2.0 * math.pi * p / 16)
    return zr * c - zi * s, zr * s + zi * c


def _dft4(z, inverse):
    (ar, ai), (br, bi), (cr, ci), (dr, di) = z
    t0r, t0i = ar + cr, ai + ci
    t1r, t1i = ar - cr, ai - ci
    t2r, t2i = br + dr, bi + di
    t3r, t3i = br - dr, bi - di
    y0 = (t0r + t2r, t0i + t2i)
    y2 = (t0r - t2r, t0i - t2i)
    minus_i_t3 = (t1r + t3i, t1i - t3r)
    plus_i_t3 = (t1r - t3i, t1i + t3r)
    return [y0, plus_i_t3, y2, minus_i_t3] if inverse else [y0, minus_i_t3, y2, plus_i_t3]


def _dft_digit(z, inverse):
    if len(z) == 1:
        return z
    assert len(z) == 16
    t = [_dft4([z[4 * a + b] for a in range(4)], inverse) for b in range(4)]
    out = [None] * 16
    for c in range(4):
        y = _dft4([_cmul_root16(t[b][c], b * c, inverse) for b in range(4)], inverse)
        for d in range(4):
            out[c + 4 * d] = y[d]
    return out


def _fft_stage1(src_refs, f1_ref, a_refs):
    n2cnt = f1_ref.shape[0]
    groups = a_refs[0].shape[0]
    cb = a_refs[0].shape[-1]
    n1 = groups * FFT_KG
    half = src_refs[0].shape[0] // n2cnt

    def body(n2, carry):
        rows = pl.ds(n2, half, stride=n2cnt) if n2cnt > 1 else pl.ds(0, half)
        x = [src_ref[rows, :].astype(BF16) for src_ref in src_refs]
        x = x[0] if len(x) == 1 else jnp.concatenate(x, axis=1)
        res = jnp.dot(f1_ref[n2], x, preferred_element_type=F32)
        for s, a_ref in enumerate(a_refs):
            cols = slice(s * cb, (s + 1) * cb)
            for g in range(groups):
                a_ref[g, 0, n2] = res[g * FFT_KG:(g + 1) * FFT_KG, cols]
                a_ref[g, 1, n2] = res[n1 + g * FFT_KG:n1 + (g + 1) * FFT_KG, cols]
        return carry

    if n2cnt == 1:
        body(0, 0)
    else:
        lax.fori_loop(0, n2cnt, body, 0, unroll=4)


def _load_digits(ref, g, rows):
    return [(ref[g, 0, d, rows, :], ref[g, 1, d, rows, :]) for d in range(ref.shape[2])]


def _filter_spec_kernel(h3_ref, wf_ref, wb_ref, df_ref, db_ref, f1_ref, kf_ref,
                        hf_ref, hb_ref, af_ref, ab_ref, *, l):
    groups = af_ref.shape[0]
    row = lax.broadcasted_iota(jnp.int32, (l, 1), 0)
    t = row.astype(F32) * (1.0 / (l - 1))
    h3 = h3_ref[...]
    hf = jnp.dot(h3, wf_ref[...], preferred_element_type=F32) * jnp.exp(-t * df_ref[...])
    hb = jnp.dot(h3, wb_ref[...], preferred_element_type=F32) * jnp.exp(-t * db_ref[...])
    hb = jnp.where(row > 0, hb, 0.0)
    norm = jnp.sum(jnp.abs(hf), axis=0, keepdims=True) + jnp.sum(jnp.abs(hb), axis=0, keepdims=True)
    hf_ref[...] = hf / norm
    hb_ref[...] = hb / norm

    def combine(g, carry):
        for r in range(0, FFT_KG, 8):
            rows = pl.ds(r, 8)
            for k2, (fr, fi) in enumerate(_dft_digit(_load_digits(af_ref, g, rows), False)):
                kf_ref[g, 0, k2, rows, :] = fr
                kf_ref[g, 1, k2, rows, :] = fi
            for k2, (br, bi) in enumerate(_dft_digit(_load_digits(ab_ref, g, rows), False)):
                kf_ref[g, 0, k2, rows, :] = kf_ref[g, 0, k2, rows, :] + br
                kf_ref[g, 1, k2, rows, :] = kf_ref[g, 1, k2, rows, :] - bi
        return carry

    _fft_stage1((hf_ref, hb_ref), f1_ref, (af_ref, ab_ref))
    lax.fori_loop(0, groups, combine, 0)


def _filter_spectrum(h3, fw4, decay, tables):
    l = h3.shape[0]
    f1, _ = tables
    n2 = f1.shape[0]
    groups = f1.shape[1] // 2 // FFT_KG
    nblk = HYENA_WIDTH // HY_CB
    spec_shape = (groups, 2, n2, FFT_KG)
    return pl.pallas_call(
        functools.partial(_filter_spec_kernel, l=l),
        grid=(nblk,),
        in_specs=[_single((l, FILTER_HIDDEN), lambda c: (0, 0)),
                  pl.BlockSpec((FILTER_HIDDEN, HY_CB), lambda c: (0, c)),
                  pl.BlockSpec((FILTER_HIDDEN, HY_CB), lambda c: (0, c + nblk)),
                  pl.BlockSpec((1, HY_CB), lambda c: (0, c)),
                  pl.BlockSpec((1, HY_CB), lambda c: (0, c + nblk)),
                  _single(f1.shape, lambda c: (0, 0, 0))],
        out_specs=pl.BlockSpec(spec_shape + (HY_CB,), lambda c: (0, 0, 0, 0, c)),
        out_shape=jax.ShapeDtypeStruct(spec_shape + (HYENA_WIDTH,), F32),
        scratch_shapes=[pltpu.VMEM((l, HY_CB), F32), pltpu.VMEM((l, HY_CB), F32),
                        pltpu.VMEM(spec_shape + (HY_CB,), F32),
                        pltpu.VMEM(spec_shape + (HY_CB,), F32)],
        name="filter_spectrum",
        compiler_params=_cparams(("arbitrary",)),
    )(h3, fw4, fw4, decay, decay, f1)


def _short_conv(src_ref, w_ref, b_ref, l, rows):
    w0, w1, w2, b = w_ref[0:1, :], w_ref[1:2, :], w_ref[2:3, :], b_ref[...]
    r = lax.broadcasted_iota(jnp.int32, (rows, 1), 0)
    for s in range(0, l, rows):
        cur = src_ref[s:s + rows, :]
        if s > 0:
            prev = src_ref[s - 1:s - 1 + rows, :]
        else:
            prev = jnp.where(r == 0, 0.0, pltpu.roll(cur, 1, axis=0))
        if s + rows < l:
            nxt = src_ref[s + 1:s + 1 + rows, :]
        else:
            nxt = jnp.where(r == rows - 1, 0.0, pltpu.roll(cur, rows - 1, axis=0))
        yield s, prev * w0 + cur * w1 + nxt * w2 + b


def _hyena_kernel(x0_ref, x1_ref, v_ref, w0_ref, w1_ref, wv_ref, b0_ref, b1_ref, bv_ref,
                  bias_ref, kf_ref, f1_ref, g1_ref, o_ref, x0c_ref, vg_ref, a_ref, *, l):
    groups = a_ref.shape[0]
    n2cnt = a_ref.shape[2]
    cb = a_ref.shape[-1]
    n1 = groups * FFT_KG
    half = l // n2cnt
    rows = min(l, 512)

    for s, u in _short_conv(x0_ref, w0_ref, b0_ref, l, rows):
        x0c_ref[s:s + rows, :] = u
    for (s, u1), (_, uv) in zip(_short_conv(x1_ref, w1_ref, b1_ref, l, rows),
                                _short_conv(v_ref, wv_ref, bv_ref, l, rows)):
        vg_ref[s:s + rows, :] = uv * u1

    _fft_stage1((vg_ref,), f1_ref, (a_ref,))

    def spectrum(g, carry):
        for r in range(0, FFT_KG, 8):
            rws = pl.ds(r, 8)
            x = _dft_digit(_load_digits(a_ref, g, rws), False)
            k = _load_digits(kf_ref, g, rws)
            y = [(xr * kr - xi * ki, xr * ki + xi * kr) for (xr, xi), (kr, ki) in zip(x, k)]
            for d, (br, bi) in enumerate(_dft_digit(y, True)):
                a_ref[g, 0, d, rws, :] = br
                a_ref[g, 1, d, rws, :] = bi
        return carry

    lax.fori_loop(0, groups, spectrum, 0)

    def synth(n2, carry):
        b = jnp.concatenate([a_ref[:, 0, n2].reshape(n1, cb), a_ref[:, 1, n2].reshape(n1, cb)],
                            axis=0).astype(BF16)
        y = jnp.dot(g1_ref[n2], b, preferred_element_type=F32)
        idx = pl.ds(n2, half, stride=n2cnt) if n2cnt > 1 else pl.ds(0, half)
        o_ref[idx, :] = y
        return carry

    if n2cnt == 1:
        synth(0, 0)
    else:
        lax.fori_loop(0, n2cnt, synth, 0, unroll=4)

    bias = bias_ref[...]
    for s in range(0, l, rows):
        blk = slice(s, s + rows)
        o_ref[blk, :] = (o_ref[blk, :] + vg_ref[blk, :] * bias) * x0c_ref[blk, :]


def _hyena(p, col0, conv_w, conv_b, bias_d, kf, tables):
    b, l, _ = p.shape
    f1, g1 = tables
    nblk = HYENA_WIDTH // HY_CB
    c0 = col0 // HY_CB
    spec_block = kf.shape[:-1] + (HY_CB,)

    def slab(part):
        return pl.BlockSpec((None, l, HY_CB), lambda c, bi: (bi, 0, c0 + part * nblk + c))

    def cw(part):
        return pl.BlockSpec((3, HY_CB), lambda c, bi: (0, part * nblk + c))

    def cbias(part):
        return pl.BlockSpec((1, HY_CB), lambda c, bi: (0, part * nblk + c))

    return pl.pallas_call(
        functools.partial(_hyena_kernel, l=l),
        grid=(nblk, b),
        in_specs=[slab(0), slab(1), slab(2), cw(0), cw(1), cw(2), cbias(0), cbias(1), cbias(2),
                  pl.BlockSpec((1, HY_CB), lambda c, bi: (0, c)),
                  pl.BlockSpec(spec_block, lambda c, bi: (0, 0, 0, 0, c)),
                  _single(f1.shape, lambda c, bi: (0, 0, 0)),
                  _single(g1.shape, lambda c, bi: (0, 0, 0))],
        out_specs=pl.BlockSpec((None, l, HY_CB), lambda c, bi: (bi, 0, c)),
        out_shape=jax.ShapeDtypeStruct((b, l, HYENA_WIDTH), F32),
        scratch_shapes=[pltpu.VMEM((l, HY_CB), F32), pltpu.VMEM((l, HY_CB), F32),
                        pltpu.VMEM(spec_block, F32)],
        name="hyena_conv",
        compiler_params=_cparams(("arbitrary", "arbitrary")),
    )(p, p, p, conv_w, conv_w, conv_w, conv_b, conv_b, conv_b, bias_d, kf, f1, g1)


def _merge_kernel(a_ref, h_ref, wa_ref, wh_ref, *refs):
    gate_refs, o_ref = refs[:-1], refs[-1]
    nt = len(gate_refs) // 2
    tg = gate_refs[0].shape[1]
    a = a_ref[...]
    h = h_ref[...].astype(BF16)
    for k in range(nt):
        cs = slice(k * tg, (k + 1) * tg)
        ya = jnp.dot(a, wa_ref[:, cs], preferred_element_type=F32)
        yh = jnp.dot(h, wh_ref[:, cs], preferred_element_type=F32)
        o_ref[:, cs] = (jax.nn.sigmoid(gate_refs[k][...]) * ya
                        + jax.nn.sigmoid(gate_refs[nt + k][...]) * yh).astype(o_ref.dtype)


def _merge(attn, hy, p, w_ao, w_ho, layer, tm):
    m = attn.shape[0]
    nt = D_MODEL // GATE_COLS

    def gate_spec(col0, k):
        return pl.BlockSpec((tm, GATE_COLS), lambda i: (i, col0 // GATE_COLS + k))

    gate_specs = ([gate_spec(HY_END, k) for k in range(nt)]
                  + [gate_spec(GA_END, k) for k in range(nt)])
    return pl.pallas_call(
        _merge_kernel,
        grid=(m // tm,),
        in_specs=[pl.BlockSpec((tm, ATTN_WIDTH), lambda i: (i, 0)),
                  pl.BlockSpec((tm, HYENA_WIDTH), lambda i: (i, 0)),
                  _single((None, ATTN_WIDTH, D_MODEL), lambda i: (layer, 0, 0)),
                  _single((None, HYENA_WIDTH, D_MODEL), lambda i: (layer, 0, 0))] + gate_specs,
        out_specs=pl.BlockSpec((tm, D_MODEL), lambda i: (i, 0)),
        out_shape=jax.ShapeDtypeStruct((m, D_MODEL), BF16),
        name="branch_merge",
        compiler_params=_cparams(("arbitrary",)),
    )(attn, hy, w_ao, w_ho, *([p] * (2 * nt)))


def _outproj_kernel(mix_ref, w_ref, x_ref, g_ref, gate_ref, o_ref, y_ref, *, tn):
    j = pl.program_id(1)
    nj = y_ref.shape[0]
    y_ref[j] = jnp.dot(mix_ref[...], w_ref[...], preferred_element_type=F32)

    @pl.when(j == nj - 1)
    def _():
        ss = sum(jnp.sum(jnp.square(y_ref[k]), axis=-1, keepdims=True) for k in range(nj))
        r = lax.rsqrt(ss * (1.0 / D_MODEL) + EPS)
        for k in range(nj):
            cs = slice(k * tn, (k + 1) * tn)
            o_ref[:, cs] = x_ref[:, cs] + gate_ref[:, cs] * ((y_ref[k] * r) * g_ref[:, cs])


def _outproj(mix, w_o, layer, x, g, gate, tm, tn):
    m = x.shape[0]
    nmod = gate.shape[0]
    blocks_per_mod = m // nmod // tm
    return pl.pallas_call(
        functools.partial(_outproj_kernel, tn=tn),
        grid=(m // tm, D_MODEL // tn),
        in_specs=[pl.BlockSpec((tm, D_MODEL), lambda i, j: (i, 0)),
                  pl.BlockSpec((None, D_MODEL, tn), lambda i, j: (layer, 0, j),
                               pipeline_mode=pl.Buffered(1) if tn == D_MODEL else None),
                  pl.BlockSpec((tm, D_MODEL), lambda i, j: (i, 0)),
                  pl.BlockSpec((1, D_MODEL), lambda i, j: (0, 0)),
                  pl.BlockSpec((None, 1, D_MODEL), lambda i, j: (i // blocks_per_mod, 0, 0))],
        out_specs=pl.BlockSpec((tm, D_MODEL), lambda i, j: (i, 0)),
        out_shape=jax.ShapeDtypeStruct((m, D_MODEL), F32),
        scratch_shapes=[pltpu.VMEM((D_MODEL // tn, tm, tn), F32)],
        name="out_projection",
        compiler_params=_cparams(("arbitrary", "arbitrary")),
    )(mix, w_o, x, g.reshape(1, D_MODEL), gate)


def _mlp_kernel(x_ref, xn_ref, gin_ref, sc_ref, sh_ref, scn_ref, shn_ref, w1_ref, w2_ref, gout_ref,
                gate_ref, o_ref, ha_ref, hb_ref, *, pro_steps):
    i = pl.program_id(0)
    j = pl.program_id(1)
    tm = x_ref.shape[0]
    chunk = xn_ref.shape[0]
    gin = gin_ref[...]

    def norm(x, sc_r, sh_r):
        return (_rms(x) * gin * (1.0 + sc_r[...]) + sh_r[...]).astype(BF16)

    @pl.when((i == 0) & (j == 0))
    def _():
        def first(rows):
            ha_ref[rows, :] = norm(x_ref[rows, :], sc_ref, sh_ref)

        _row_chunks(tm, first)

    @pl.when(j == 0)
    def _():
        def clear(rows):
            o_ref[rows, :] = jnp.zeros((PROLOGUE_ROWS, D_MODEL), F32)

        _row_chunks(tm, clear)

    def step(cur_ref, next_ref):
        rows = pl.ds(pl.multiple_of(jnp.minimum(j, pro_steps - 1) * chunk, chunk), chunk)
        next_ref[rows, :] = norm(xn_ref[...], scn_ref, shn_ref)
        a = jnp.dot(cur_ref[...], w1_ref[...].astype(BF16), preferred_element_type=F32)
        a = jnp.square(jnp.maximum(a, 0.0)).astype(BF16)
        for n in range(0, D_MODEL, MLP_COLS):
            o_ref[:, n:n + MLP_COLS] += jnp.dot(a, w2_ref[:, n:n + MLP_COLS].astype(BF16),
                                                preferred_element_type=F32)

    @pl.when(i % 2 == 0)
    def _():
        step(ha_ref, hb_ref)

    @pl.when(i % 2 == 1)
    def _():
        step(hb_ref, ha_ref)

    @pl.when(j == pl.num_programs(1) - 1)
    def _():
        gout, gate = gout_ref[...], gate_ref[...]

        def epilogue(rows):
            o_ref[rows, :] = x_ref[rows, :] + gate * (_rms(o_ref[rows, :]) * gout)

        _row_chunks(tm, epilogue)


def _mlp(x, g_in, scale, shift, w1, w2, layer, g_out, gate, tm, tf):
    m = x.shape[0]
    nb = m // tm
    nj = D_FF // tf
    nmod = gate.shape[0]
    blocks_per_mod = m // nmod // tm
    pro_steps = min(nj, NORM_STEPS)
    chunk = tm // pro_steps
    assert chunk * pro_steps == tm and chunk % 16 == 0 and tm % PROLOGUE_ROWS == 0

    def next_block(i):
        return jnp.minimum(i + 1, nb - 1)

    mod_spec = pl.BlockSpec((None, 1, D_MODEL), lambda i, j: (i // blocks_per_mod, 0, 0))
    next_mod = pl.BlockSpec((None, 1, D_MODEL),
                            lambda i, j: (next_block(i) // blocks_per_mod, 0, 0))
    row_spec = pl.BlockSpec((1, D_MODEL), lambda i, j: (0, 0))
    return pl.pallas_call(
        functools.partial(_mlp_kernel, pro_steps=pro_steps),
        grid=(nb, nj),
        in_specs=[pl.BlockSpec((tm, D_MODEL), lambda i, j: (i, 0), pipeline_mode=pl.Buffered(1)),
                  pl.BlockSpec((chunk, D_MODEL),
                               lambda i, j: (next_block(i) * pro_steps
                                             + jnp.minimum(j, pro_steps - 1), 0)),
                  row_spec, mod_spec, mod_spec, next_mod, next_mod,
                  pl.BlockSpec((None, D_MODEL, tf), lambda i, j: (layer, 0, j)),
                  pl.BlockSpec((None, tf, D_MODEL), lambda i, j: (layer, j, 0)),
                  row_spec, mod_spec],
        out_specs=pl.BlockSpec((tm, D_MODEL), lambda i, j: (i, 0)),
        out_shape=jax.ShapeDtypeStruct((m, D_MODEL), F32),
        scratch_shapes=[pltpu.VMEM((tm, D_MODEL), BF16), pltpu.VMEM((tm, D_MODEL), BF16)],
        name="channel_mlp",
        compiler_params=_cparams(("arbitrary", "arbitrary")),
    )(x, x, g_in.reshape(1, D_MODEL), scale, shift, scale, shift, w1, w2,
      g_out.reshape(1, D_MODEL), gate)


def _rope_tables(l):
    pos = np.arange(l)
    quarter = HEAD_DIM // 4
    freqs = ROPE_THETA ** (-np.arange(quarter) / quarter)
    ang_r = (pos // GRID_W)[:, None] * freqs[None, :]
    ang_c = (pos % GRID_W)[:, None] * freqs[None, :]
    cos_t = np.concatenate([np.cos(ang_r), np.cos(ang_r), np.cos(ang_c), np.cos(ang_c)], axis=-1)
    sin_t = np.concatenate([-np.sin(ang_r), np.sin(ang_r), -np.sin(ang_c), np.sin(ang_c)], axis=-1)
    return jnp.asarray(cos_t.astype(np.float32)), jnp.asarray(sin_t.astype(np.float32))


def _decay_rates():
    min_decay = math.log(DECAY_TARGET) / SLOW_DECAY_PCT
    max_decay = math.log(DECAY_TARGET) / FAST_DECAY_PCT
    deltas = np.tile(np.linspace(min_decay, max_decay, HYENA_WIDTH), 2)
    return jnp.asarray(np.abs(deltas)[None, :].astype(np.float32))


def kernel(x, c, ctx, c_ctx, w_mod, b_mod, norm_g, w_in, attn_sink, hy_conv_w, hy_conv_b,
           hy_fw1, hy_fb1, hy_ff1, hy_fw2, hy_fb2, hy_ff2, hy_fw3, hy_fb3, hy_ff3, hy_fw4,
           hy_bias, w_attn_out, w_hyena_out, w_out, w_ff1, w_ff2):
    b, l, d = x.shape
    cl = ctx.shape[1]
    assert d == D_MODEL and l % PROJ_ROWS == 0 and cl % 256 == 0 and b + 1 <= MOD_ROWS

    cos_t, sin_t = _rope_tables(l)
    decay = _decay_rates()
    tables_lat = _fft_tables(l, FFT_N2)
    tables_ctx = _fft_tables(cl, 1)
    z_lat = _filter_features(l)
    z_ctx = _filter_features(cl)

    c_rows = jnp.concatenate([c, c_ctx[None, :], jnp.zeros((MOD_ROWS - b - 1, d), F32)], axis=0)
    x_lat = x.reshape(b * l, d)
    x_ctx = ctx.reshape(b * cl, d)
    tm_ctx = b * cl
    w_aob = w_attn_out.astype(BF16)
    w_hob = w_hyena_out.astype(BF16)
    w_ob = w_out.astype(BF16)

    for layer in range(DEPTH):
        last = layer == DEPTH - 1
        mod = _modulation(c_rows, w_mod, layer, b_mod[layer])
        mod_lat = [mod[:b, k * d:(k + 1) * d].reshape(b, 1, d) for k in range(N_MOD)]
        mod_ctx = [mod[b:b + 1, k * d:(k + 1) * d].reshape(1, 1, d) for k in range(N_MOD)]
        sh1, sc1, g1, sh2, sc2, g2 = mod_lat
        csh1, csc1, cg1, csh2, csc2, cg2 = mod_ctx
        g = norm_g[layer]
        fparams = (hy_fw1[layer], hy_fb1[layer], hy_ff1[layer], hy_fw2[layer], hy_fb2[layer],
                   hy_ff2[layer], hy_fw3[layer], hy_fb3[layer], hy_ff3[layer])
        bias_d = hy_bias[layer].reshape(1, HYENA_WIDTH)
        conv_b = hy_conv_b[layer].reshape(1, 3 * HYENA_WIDTH)

        p_lat = _normproj(x_lat, g[0], sc1, sh1, w_in, layer, 0, IN_WIDTH, PROJ_ROWS, PROJ_COLS)
        if last:
            kv_ctx = _normproj(x_ctx, g[0], csc1, csh1, w_in, layer, Q_END, V_END - Q_END,
                               tm_ctx, PROJ_COLS)
            kx, vx = kv_ctx[:, :KV_WIDTH], kv_ctx[:, KV_WIDTH:]
        else:
            p_ctx = _normproj(x_ctx, g[0], csc1, csh1, w_in, layer, 0, IN_WIDTH, tm_ctx, PROJ_COLS)
            kx, vx = p_ctx[:, Q_END:K_END], p_ctx[:, K_END:V_END]
        kx = kx.reshape(b, cl, KV_WIDTH)
        vx = vx.reshape(b, cl, KV_WIDTH)

        p3 = p_lat.reshape(b, l, IN_WIDTH)
        attn = _window_attention(p3, kx, vx, attn_sink[layer], cos_t, sin_t)
        kf = _filter_spectrum(_filter_mlp(z_lat, *fparams), hy_fw4[layer], decay, tables_lat)
        hy = _hyena(p3, V_END, hy_conv_w[layer], conv_b, bias_d, kf, tables_lat)
        mix = _merge(attn.reshape(b * l, ATTN_WIDTH), hy.reshape(b * l, HYENA_WIDTH), p_lat,
                     w_aob, w_hob, layer, MIX_ROWS)
        x_lat = _outproj(mix, w_ob, layer, x_lat, g[1], g1, MIX_ROWS, D_MODEL)

        if not last:
            pc3 = p_ctx.reshape(b, cl, IN_WIDTH)
            attn_c = _context_attention(pc3, attn_sink[layer])
            kf_c = _filter_spectrum(_filter_mlp(z_ctx, *fparams), hy_fw4[layer], decay, tables_ctx)
            hy_c = _hyena(pc3, V_END, hy_conv_w[layer], conv_b, bias_d, kf_c, tables_ctx)
            mix_c = _merge(attn_c.reshape(b * cl, ATTN_WIDTH), hy_c.reshape(b * cl, HYENA_WIDTH),
                           p_ctx, w_aob, w_hob, layer, tm_ctx)
            x_ctx = _outproj(mix_c, w_ob, layer, x_ctx, g[1], cg1, tm_ctx, D_MODEL)
            x_ctx = _mlp(x_ctx, g[2], csc2, csh2, w_ff1, w_ff2, layer, g[3], cg2, tm_ctx, MLP_FF)

        x_lat = _mlp(x_lat, g[2], sc2, sh2, w_ff1, w_ff2, layer, g[3], g2, MLP_ROWS, MLP_FF)
    return x_lat.reshape(b, l, d)
```

```python
import functools
import math

import jax
import jax.numpy as jnp
import numpy as np
from jax import lax
from jax.experimental import pallas as pl
from jax.experimental.pallas import tpu as pltpu

F32 = jnp.float32
BF16 = jnp.bfloat16

D_MODEL = 2048
DEPTH = 2
GRID_W = 64
HEAD_DIM = 128
N_Q_HEADS = 8
N_KV_HEADS = 2
Q_GROUP = N_Q_HEADS // N_KV_HEADS
ATTN_WIDTH = N_Q_HEADS * HEAD_DIM
KV_WIDTH = N_KV_HEADS * HEAD_DIM
BLOCK = 128
ROPE_THETA = 10000.0
HYENA_WIDTH = 1024
FILTER_EMB = 33
FILTER_HIDDEN = 64
DECAY_TARGET = 1e-2
FAST_DECAY_PCT = 0.3
SLOW_DECAY_PCT = 1.5
D_FF = 4 * D_MODEL
EPS = 1e-6
N_MOD = 6
NEG_INF = -1e30
Q_END = ATTN_WIDTH
K_END = Q_END + KV_WIDTH
V_END = K_END + KV_WIDTH
HY_END = V_END + 3 * HYENA_WIDTH
GA_END = HY_END + D_MODEL
GH_END = GA_END + D_MODEL
IN_WIDTH = GH_END
LOG2E = math.log2(math.e)
LOGIT_SCALE = HEAD_DIM ** -0.5 * LOG2E

VMEM_LIMIT = 56 * 1024 * 1024

PROJ_ROWS = 2048
PROJ_COLS = 512
MLP_ROWS = 1024
MLP_FF = 512
MIX_ROWS = 512
FFT_N2 = 16
FFT_KG = 16
PROLOGUE_ROWS = 256
NORM_STEPS = 16
MLP_COLS = 512
GATE_COLS = 512
ATTN_QBLOCKS = 8
HY_CB = 128
MOD_ROWS = 8


def _cparams(sem):
    return pltpu.CompilerParams(dimension_semantics=sem, vmem_limit_bytes=VMEM_LIMIT)


def _single(block_shape, index_map):
    return pl.BlockSpec(block_shape, index_map, pipeline_mode=pl.Buffered(1))


def _rms(x):
    return x * lax.rsqrt(jnp.mean(x * x, axis=-1, keepdims=True) + EPS)


def _row_chunks(nrows, fn):
    def chunk(r, carry):
        fn(pl.ds(pl.multiple_of(r * PROLOGUE_ROWS, PROLOGUE_ROWS), PROLOGUE_ROWS))
        return carry

    lax.fori_loop(0, nrows // PROLOGUE_ROWS, chunk, 0)


def _mod_kernel(c_ref, w_ref, b_ref, o_ref):
    c = c_ref[...]
    s = c * jax.nn.sigmoid(c)
    o_ref[...] = jnp.dot(s.astype(BF16), w_ref[...].astype(BF16),
                         preferred_element_type=F32) + b_ref[...]


def _modulation(c_rows, w, layer, b):
    n = w.shape[2]
    tn = 1024
    return pl.pallas_call(
        _mod_kernel,
        grid=(n // tn,),
        in_specs=[pl.BlockSpec((MOD_ROWS, D_MODEL), lambda j: (0, 0)),
                  pl.BlockSpec((None, D_MODEL, tn), lambda j: (layer, 0, j)),
                  pl.BlockSpec((1, tn), lambda j: (0, j))],
        out_specs=pl.BlockSpec((MOD_ROWS, tn), lambda j: (0, j)),
        out_shape=jax.ShapeDtypeStruct((MOD_ROWS, n), F32),
        name="modulation",
        compiler_params=_cparams(("arbitrary",)),
    )(c_rows, w, b.reshape(1, n))


def _normproj_kernel(x0_ref, sc0_ref, sh0_ref, x_ref, g_ref, sc_ref, sh_ref, w_ref, o_ref,
                     ha_ref, hb_ref, *, pro_steps):
    i = pl.program_id(0)
    j = pl.program_id(1)
    chunk = x_ref.shape[0]
    g = g_ref[...]

    def norm(x, sc_r, sh_r):
        return (_rms(x) * g * (1.0 + sc_r[...]) + sh_r[...]).astype(BF16)

    @pl.when((i == 0) & (j == 0))
    def _():
        def first(rows):
            ha_ref[rows, :] = norm(x0_ref[rows, :], sc0_ref, sh0_ref)

        _row_chunks(x0_ref.shape[0], first)

    def step(cur_ref, next_ref):
        rows = pl.ds(pl.multiple_of(jnp.minimum(j, pro_steps - 1) * chunk, chunk), chunk)
        next_ref[rows, :] = norm(x_ref[...], sc_ref, sh_ref)
        o_ref[...] = jnp.dot(cur_ref[...], w_ref[...].astype(BF16), preferred_element_type=F32)

    @pl.when(i % 2 == 0)
    def _():
        step(ha_ref, hb_ref)

    @pl.when(i % 2 == 1)
    def _():
        step(hb_ref, ha_ref)


def _normproj(x, g, scale, shift, w, layer, col0, n, tm, tn):
    m = x.shape[0]
    nb = m // tm
    nj = n // tn
    j0 = col0 // tn
    nmod = scale.shape[0]
    blocks_per_mod = m // nmod // tm
    pro_steps = min(nj, NORM_STEPS)
    chunk = tm // pro_steps
    assert chunk * pro_steps == tm and chunk % 16 == 0 and tm % PROLOGUE_ROWS == 0

    def next_block(i):
        return jnp.minimum(i + 1, nb - 1)

    first_mod = pl.BlockSpec((None, 1, D_MODEL), lambda i, j: (0, 0, 0))
    next_mod = pl.BlockSpec((None, 1, D_MODEL),
                            lambda i, j: (next_block(i) // blocks_per_mod, 0, 0))
    return pl.pallas_call(
        functools.partial(_normproj_kernel, pro_steps=pro_steps),
        grid=(nb, nj),
        in_specs=[_single((tm, D_MODEL), lambda i, j: (0, 0)),
                  first_mod, first_mod,
                  pl.BlockSpec((chunk, D_MODEL),
                               lambda i, j: (next_block(i) * pro_steps
                                             + jnp.minimum(j, pro_steps - 1), 0)),
                  pl.BlockSpec((1, D_MODEL), lambda i, j: (0, 0)),
                  next_mod, next_mod,
                  pl.BlockSpec((None, D_MODEL, tn), lambda i, j: (layer, 0, j0 + j))],
        out_specs=pl.BlockSpec((tm, tn), lambda i, j: (i, j)),
        out_shape=jax.ShapeDtypeStruct((m, n), F32),
        scratch_shapes=[pltpu.VMEM((tm, D_MODEL), BF16), pltpu.VMEM((tm, D_MODEL), BF16)],
        name="normproj",
        compiler_params=_cparams(("arbitrary", "arbitrary")),
    )(x, scale, shift, x, g.reshape(1, D_MODEL), scale, shift, w)


def _rope(x, cos, sin_signed, first_half):
    rot = jnp.where(first_half, pltpu.roll(x, HEAD_DIM - 32, axis=1), pltpu.roll(x, 32, axis=1))
    return x * cos + rot * sin_signed


def _softmax_pv(s, sink, v):
    m = jnp.maximum(jnp.max(s, axis=-1, keepdims=True), sink)
    e = jnp.exp2(s - m)
    denom = jnp.sum(e, axis=-1, keepdims=True) + jnp.exp2(sink - m)
    o = jnp.dot(e.astype(BF16), v, preferred_element_type=F32)
    return o / denom


def _band_bias(nctx):
    qi = (np.arange(Q_GROUP * BLOCK) % BLOCK)[None, :, None]
    kj = np.arange(3 * BLOCK + nctx)[None, None, :]
    variant = np.arange(3)[:, None, None]
    in_prev = kj < BLOCK
    in_next = (kj >= 2 * BLOCK) & (kj < 3 * BLOCK)
    valid = np.where(in_prev, (kj >= qi) & (variant != 0),
                     np.where(in_next, (kj - 2 * BLOCK <= qi) & (variant != 2), True))
    return jnp.asarray(np.where(valid, 0.0, NEG_INF).astype(np.float32))


def _win_attn_kernel(sink_ref, q_ref, *refs, nb):
    nkv = ATTN_QBLOCKS + 2
    kv_refs = refs[:nkv]
    kx_ref, vx_ref, cos_ref, sin_ref, bias_ref, o_ref = refs[nkv:]
    blk0 = pl.program_id(1) * ATTN_QBLOCKS
    lane = lax.broadcasted_iota(jnp.int32, (BLOCK, HEAD_DIM), 1)
    first_half = (lane % 64) < 32

    def table(ref, blk):
        return ref[pl.ds(pl.multiple_of(blk * BLOCK, BLOCK), BLOCK), :]

    kpos = [jnp.clip(blk0 - 1 + t, 0, nb - 1) for t in range(nkv)]
    cos_k = [table(cos_ref, kp) for kp in kpos]
    sin_k = [table(sin_ref, kp) for kp in kpos]

    rows = Q_GROUP * BLOCK
    head_in_group = lax.broadcasted_iota(jnp.int32, (rows, 1), 0) // BLOCK
    bias = [bias_ref[jnp.where(blk0 + u == 0, 0, jnp.where(blk0 + u == nb - 1, 2, 1))]
            for u in range(ATTN_QBLOCKS)]

    for h in range(N_KV_HEADS):
        hs = slice(h * HEAD_DIM, (h + 1) * HEAD_DIM)
        vs = slice(KV_WIDTH + h * HEAD_DIM, KV_WIDTH + (h + 1) * HEAD_DIM)
        k_blocks = [_rope(kv_refs[t][:, hs], cos_k[t], sin_k[t], first_half).astype(BF16)
                    for t in range(nkv)]
        v_blocks = [kv_refs[t][:, vs].astype(BF16) for t in range(nkv)]
        k_ctx = kx_ref[:, hs].astype(BF16)
        v_ctx = vx_ref[:, hs].astype(BF16)
        heads = [h * Q_GROUP + g for g in range(Q_GROUP)]
        sink = jnp.zeros((rows, 1), F32)
        for g, hd in enumerate(heads):
            sink = jnp.where(head_in_group == g, sink_ref[hd] * LOG2E, sink)
        for u in range(ATTN_QBLOCKS):
            qrows = slice(u * BLOCK, (u + 1) * BLOCK)
            k = jnp.concatenate(k_blocks[u:u + 3] + [k_ctx], axis=0)
            v = jnp.concatenate(v_blocks[u:u + 3] + [v_ctx], axis=0)
            q = jnp.concatenate(
                [_rope(q_ref[qrows, hd * HEAD_DIM:(hd + 1) * HEAD_DIM], cos_k[u + 1], sin_k[u + 1],
                       first_half) for hd in heads], axis=0).astype(BF16)
            s = lax.dot_general(q, k, (((1,), (1,)), ((), ())), preferred_element_type=F32)
            o = _softmax_pv(s * LOGIT_SCALE + bias[u], sink, v).astype(o_ref.dtype)
            for g, hd in enumerate(heads):
                o_ref[qrows, hd * HEAD_DIM:(hd + 1) * HEAD_DIM] = o[g * BLOCK:(g + 1) * BLOCK]


def _window_attention(p, kx, vx, sink, cos_t, sin_t):
    b, l, _ = p.shape
    c = kx.shape[1]
    nb = l // BLOCK
    nkv = ATTN_QBLOCKS + 2
    assert nb >= 2 and nb % ATTN_QBLOCKS == 0

    def kv_spec(t):
        return pl.BlockSpec(
            (None, BLOCK, 2 * KV_WIDTH),
            lambda bi, i: (bi, jnp.clip(i * ATTN_QBLOCKS - 1 + t, 0, nb - 1), Q_END // (2 * KV_WIDTH)))

    bias = _band_bias(c)
    q_spec = pl.BlockSpec((None, ATTN_QBLOCKS * BLOCK, ATTN_WIDTH), lambda bi, i: (bi, i, 0))
    ctx_spec = pl.BlockSpec((None, c, KV_WIDTH), lambda bi, i: (bi, 0, 0))
    tab_spec = _single((l, HEAD_DIM), lambda bi, i: (0, 0))
    return pl.pallas_call(
        functools.partial(_win_attn_kernel, nb=nb),
        grid=(b, nb // ATTN_QBLOCKS),
        in_specs=[pl.BlockSpec(memory_space=pltpu.SMEM), q_spec]
                 + [kv_spec(t) for t in range(nkv)]
                 + [ctx_spec, ctx_spec, tab_spec, tab_spec,
                    _single(bias.shape, lambda bi, i: (0, 0, 0))],
        out_specs=q_spec,
        out_shape=jax.ShapeDtypeStruct((b, l, ATTN_WIDTH), BF16),
        name="window_attention",
        compiler_params=_cparams(("arbitrary", "arbitrary")),
    )(sink, p, *([p] * nkv), kx, vx, cos_t, sin_t, bias)


def _ctx_attn_kernel(sink_ref, q_ref, k_ref, v_ref, o_ref):
    for h in range(N_KV_HEADS):
        hs = slice(h * HEAD_DIM, (h + 1) * HEAD_DIM)
        k = k_ref[:, hs].astype(BF16)
        v = v_ref[:, hs].astype(BF16)
        for g in range(Q_GROUP):
            head = h * Q_GROUP + g
            cs = slice(head * HEAD_DIM, (head + 1) * HEAD_DIM)
            q = q_ref[:, cs].astype(BF16)
            s = lax.dot_general(q, k, (((1,), (1,)), ((), ())), preferred_element_type=F32)
            o_ref[:, cs] = _softmax_pv(s * LOGIT_SCALE, sink_ref[head] * LOG2E, v).astype(o_ref.dtype)


def _context_attention(p, sink):
    b, c, _ = p.shape
    return pl.pallas_call(
        _ctx_attn_kernel,
        grid=(b,),
        in_specs=[pl.BlockSpec(memory_space=pltpu.SMEM),
                  pl.BlockSpec((None, c, ATTN_WIDTH), lambda bi: (bi, 0, 0)),
                  pl.BlockSpec((None, c, KV_WIDTH), lambda bi: (bi, 0, Q_END // KV_WIDTH)),
                  pl.BlockSpec((None, c, KV_WIDTH), lambda bi: (bi, 0, K_END // KV_WIDTH))],
        out_specs=pl.BlockSpec((None, c, ATTN_WIDTH), lambda bi: (bi, 0, 0)),
        out_shape=jax.ShapeDtypeStruct((b, c, ATTN_WIDTH), BF16),
        name="context_attention",
        compiler_params=_cparams(("arbitrary",)),
    )(sink, p, p, p)


def _fft_tables(l, n2):
    n = 2 * l
    n1 = n // n2
    k1 = np.arange(n1 // 2, dtype=np.int64)[None, :, None]
    t = (n2 * np.arange(n1 // 2, dtype=np.int64)[None, None, :]
         + np.arange(n2, dtype=np.int64)[:, None, None])
    ang = (((2 * k1 + 1) * t) % (2 * n)).astype(np.float64) * (math.pi / n)
    fwd1 = np.concatenate([np.cos(ang), -np.sin(ang)], axis=1)
    inv1 = np.swapaxes(fwd1, 1, 2) * (2.0 / n)
    as_operand = lambda a: jnp.asarray(np.ascontiguousarray(a, dtype=np.float32)).astype(BF16)
    return as_operand(fwd1), as_operand(inv1)


def _filter_features(l):
    bands = (FILTER_EMB - 1) // 2
    t = np.linspace(0.0, 1.0, l)[:, None]
    w = 2 * math.pi * np.arange(l)[:, None] / l
    f = np.linspace(1e-4, bands - 1, bands)[None, :]
    z = np.concatenate([t, np.cos(f * w), -np.sin(f * w)], axis=-1)
    z = np.pad(z, ((0, 0), (0, FILTER_HIDDEN - FILTER_EMB)))
    return jnp.asarray(z.astype(np.float32))


def _filter_mlp_kernel(z_ref, w1_ref, b1_ref, f1_ref, w2_ref, b2_ref, f2_ref,
                       w3_ref, b3_ref, f3_ref, o_ref):
    h = jnp.sin(f1_ref[...] * (jnp.dot(z_ref[...], w1_ref[...], preferred_element_type=F32)
                               + b1_ref[...]))
    h = jnp.sin(f2_ref[...] * (jnp.dot(h, w2_ref[...], preferred_element_type=F32) + b2_ref[...]))
    o_ref[...] = jnp.sin(f3_ref[...] * (jnp.dot(h, w3_ref[...], preferred_element_type=F32)
                                        + b3_ref[...]))


def _filter_mlp(z, fw1, fb1, ff1, fw2, fb2, ff2, fw3, fb3, ff3):
    l = z.shape[0]
    row = lambda a: a.reshape(1, FILTER_HIDDEN)
    w1 = jnp.pad(fw1, ((0, FILTER_HIDDEN - FILTER_EMB), (0, 0)))
    return pl.pallas_call(
        _filter_mlp_kernel,
        out_shape=jax.ShapeDtypeStruct((l, FILTER_HIDDEN), F32),
        name="filter_mlp",
        compiler_params=pltpu.CompilerParams(vmem_limit_bytes=VMEM_LIMIT),
    )(z, w1, row(fb1), row(ff1), fw2, row(fb2), row(ff2), fw3, row(fb3), row(ff3))


def _cmul_root16(z, p, inverse):
    zr, zi = z
    p = p % 16
    if inverse:
        p = (16 - p) % 16
    if p == 0:
        return zr, zi
    if p == 4:
        return zi, -zr
    if p == 8:
        return -zr, -zi
    if p == 12:
        return -zi, zr
    c = math.cos(2.0 * math.pi * p / 16)
    s = -math.sin(2.0 * math.pi * p / 16)
    return zr * c - zi * s, zr * s + zi * c


def _dft4(z, inverse):
    (ar, ai), (br, bi), (cr, ci), (dr, di) = z
    t0r, t0i = ar + cr, ai + ci
    t1r, t1i = ar - cr, ai - ci
    t2r, t2i = br + dr, bi + di
    t3r, t3i = br - dr, bi - di
    y0 = (t0r + t2r, t0i + t2i)
    y2 = (t0r - t2r, t0i - t2i)
    minus_i_t3 = (t1r + t3i, t1i - t3r)
    plus_i_t3 = (t1r - t3i, t1i + t3r)
    return [y0, plus_i_t3, y2, minus_i_t3] if inverse else [y0, minus_i_t3, y2, plus_i_t3]


def _dft_digit(z, inverse):
    if len(z) == 1:
        return z
    assert len(z) == 16
    t = [_dft4([z[4 * a + b] for a in range(4)], inverse) for b in range(4)]
    out = [None] * 16
    for c in range(4):
        y = _dft4([_cmul_root16(t[b][c], b * c, inverse) for b in range(4)], inverse)
        for d in range(4):
            out[c + 4 * d] = y[d]
    return out


def _fft_stage1(src_refs, f1_ref, a_refs):
    n2cnt = f1_ref.shape[0]
    groups = a_refs[0].shape[0]
    cb = a_refs[0].shape[-1]
    n1 = groups * FFT_KG
    half = src_refs[0].shape[0] // n2cnt

    def body(n2, carry):
        rows = pl.ds(n2, half, stride=n2cnt) if n2cnt > 1 else pl.ds(0, half)
        x = [src_ref[rows, :].astype(BF16) for src_ref in src_refs]
        x = x[0] if len(x) == 1 else jnp.concatenate(x, axis=1)
        res = jnp.dot(f1_ref[n2], x, preferred_element_type=F32)
        for s, a_ref in enumerate(a_refs):
            cols = slice(s * cb, (s + 1) * cb)
            for g in range(groups):
                a_ref[g, 0, n2] = res[g * FFT_KG:(g + 1) * FFT_KG, cols]
                a_ref[g, 1, n2] = res[n1 + g * FFT_KG:n1 + (g + 1) * FFT_KG, cols]
        return carry

    if n2cnt == 1:
        body(0, 0)
    else:
        lax.fori_loop(0, n2cnt, body, 0, unroll=4)


def _load_digits(ref, g, rows):
    return [(ref[g, 0, d, rows, :], ref[g, 1, d, rows, :]) for d in range(ref.shape[2])]


def _filter_spec_kernel(h3_ref, wf_ref, wb_ref, df_ref, db_ref, f1_ref, kf_ref,
                        hf_ref, hb_ref, af_ref, ab_ref, *, l):
    groups = af_ref.shape[0]
    row = lax.broadcasted_iota(jnp.int32, (l, 1), 0)
    t = row.astype(F32) * (1.0 / (l - 1))
    h3 = h3_ref[...]
    hf = jnp.dot(h3, wf_ref[...], preferred_element_type=F32) * jnp.exp(-t * df_ref[...])
    hb = jnp.dot(h3, wb_ref[...], preferred_element_type=F32) * jnp.exp(-t * db_ref[...])
    hb = jnp.where(row > 0, hb, 0.0)
    norm = jnp.sum(jnp.abs(hf), axis=0, keepdims=True) + jnp.sum(jnp.abs(hb), axis=0, keepdims=True)
    hf_ref[...] = hf / norm
    hb_ref[...] = hb / norm

    def combine(g, carry):
        for r in range(0, FFT_KG, 8):
            rows = pl.ds(r, 8)
            for k2, (fr, fi) in enumerate(_dft_digit(_load_digits(af_ref, g, rows), False)):
                kf_ref[g, 0, k2, rows, :] = fr
                kf_ref[g, 1, k2, rows, :] = fi
            for k2, (br, bi) in enumerate(_dft_digit(_load_digits(ab_ref, g, rows), False)):
                kf_ref[g, 0, k2, rows, :] = kf_ref[g, 0, k2, rows, :] + br
                kf_ref[g, 1, k2, rows, :] = kf_ref[g, 1, k2, rows, :] - bi
        return carry

    _fft_stage1((hf_ref, hb_ref), f1_ref, (af_ref, ab_ref))
    lax.fori_loop(0, groups, combine, 0)


def _filter_spectrum(h3, fw4, decay, tables):
    l = h3.shape[0]
    f1, _ = tables
    n2 = f1.shape[0]
    groups = f1.shape[1] // 2 // FFT_KG
    nblk = HYENA_WIDTH // HY_CB
    spec_shape = (groups, 2, n2, FFT_KG)
    return pl.pallas_call(
        functools.partial(_filter_spec_kernel, l=l),
        grid=(nblk,),
        in_specs=[_single((l, FILTER_HIDDEN), lambda c: (0, 0)),
                  pl.BlockSpec((FILTER_HIDDEN, HY_CB), lambda c: (0, c)),
                  pl.BlockSpec((FILTER_HIDDEN, HY_CB), lambda c: (0, c + nblk)),
                  pl.BlockSpec((1, HY_CB), lambda c: (0, c)),
                  pl.BlockSpec((1, HY_CB), lambda c: (0, c + nblk)),
                  _single(f1.shape, lambda c: (0, 0, 0))],
        out_specs=pl.BlockSpec(spec_shape + (HY_CB,), lambda c: (0, 0, 0, 0, c)),
        out_shape=jax.ShapeDtypeStruct(spec_shape + (HYENA_WIDTH,), F32),
        scratch_shapes=[pltpu.VMEM((l, HY_CB), F32), pltpu.VMEM((l, HY_CB), F32),
                        pltpu.VMEM(spec_shape + (HY_CB,), F32),
                        pltpu.VMEM(spec_shape + (HY_CB,), F32)],
        name="filter_spectrum",
        compiler_params=_cparams(("arbitrary",)),
    )(h3, fw4, fw4, decay, decay, f1)


def _short_conv(src_ref, w_ref, b_ref, l, rows):
    w0, w1, w2, b = w_ref[0:1, :], w_ref[1:2, :], w_ref[2:3, :], b_ref[...]
    r = lax.broadcasted_iota(jnp.int32, (rows, 1), 0)
    for s in range(0, l, rows):
        cur = src_ref[s:s + rows, :]
        if s > 0:
            prev = src_ref[s - 1:s - 1 + rows, :]
        else:
            prev = jnp.where(r == 0, 0.0, pltpu.roll(cur, 1, axis=0))
        if s + rows < l:
            nxt = src_ref[s + 1:s + 1 + rows, :]
        else:
            nxt = jnp.where(r == rows - 1, 0.0, pltpu.roll(cur, rows - 1, axis=0))
        yield s, prev * w0 + cur * w1 + nxt * w2 + b


def _hyena_kernel(x0_ref, x1_ref, v_ref, w0_ref, w1_ref, wv_ref, b0_ref, b1_ref, bv_ref,
                  bias_ref, kf_ref, f1_ref, g1_ref, o_ref, x0c_ref, vg_ref, a_ref, *, l):
    groups = a_ref.shape[0]
    n2cnt = a_ref.shape[2]
    cb = a_ref.shape[-1]
    n1 = groups * FFT_KG
    half = l // n2cnt
    rows = min(l, 512)

    for s, u in _short_conv(x0_ref, w0_ref, b0_ref, l, rows):
        x0c_ref[s:s + rows, :] = u
    for (s, u1), (_, uv) in zip(_short_conv(x1_ref, w1_ref, b1_ref, l, rows),
                                _short_conv(v_ref, wv_ref, bv_ref, l, rows)):
        vg_ref[s:s + rows, :] = uv * u1

    _fft_stage1((vg_ref,), f1_ref, (a_ref,))

    def spectrum(g, carry):
        for r in range(0, FFT_KG, 8):
            rws = pl.ds(r, 8)
            x = _dft_digit(_load_digits(a_ref, g, rws), False)
            k = _load_digits(kf_ref, g, rws)
            y = [(xr * kr - xi * ki, xr * ki + xi * kr) for (xr, xi), (kr, ki) in zip(x, k)]
            for d, (br, bi) in enumerate(_dft_digit(y, True)):
                a_ref[g, 0, d, rws, :] = br
                a_ref[g, 1, d, rws, :] = bi
        return carry

    lax.fori_loop(0, groups, spectrum, 0)

    def synth(n2, carry):
        b = jnp.concatenate([a_ref[:, 0, n2].reshape(n1, cb), a_ref[:, 1, n2].reshape(n1, cb)],
                            axis=0).astype(BF16)
        y = jnp.dot(g1_ref[n2], b, preferred_element_type=F32)
        idx = pl.ds(n2, half, stride=n2cnt) if n2cnt > 1 else pl.ds(0, half)
        o_ref[idx, :] = y
        return carry

    if n2cnt == 1:
        synth(0, 0)
    else:
        lax.fori_loop(0, n2cnt, synth, 0, unroll=4)

    bias = bias_ref[...]
    for s in range(0, l, rows):
        blk = slice(s, s + rows)
        o_ref[blk, :] = (o_ref[blk, :] + vg_ref[blk, :] * bias) * x0c_ref[blk, :]


def _hyena(p, col0, conv_w, conv_b, bias_d, kf, tables):
    b, l, _ = p.shape
    f1, g1 = tables
    nblk = HYENA_WIDTH // HY_CB
    c0 = col0 // HY_CB
    spec_block = kf.shape[:-1] + (HY_CB,)

    def slab(part):
        return pl.BlockSpec((None, l, HY_CB), lambda c, bi: (bi, 0, c0 + part * nblk + c))

    def cw(part):
        return pl.BlockSpec((3, HY_CB), lambda c, bi: (0, part * nblk + c))

    def cbias(part):
        return pl.BlockSpec((1, HY_CB), lambda c, bi: (0, part * nblk + c))

    return pl.pallas_call(
        functools.partial(_hyena_kernel, l=l),
        grid=(nblk, b),
        in_specs=[slab(0), slab(1), slab(2), cw(0), cw(1), cw(2), cbias(0), cbias(1), cbias(2),
                  pl.BlockSpec((1, HY_CB), lambda c, bi: (0, c)),
                  pl.BlockSpec(spec_block, lambda c, bi: (0, 0, 0, 0, c)),
                  _single(f1.shape, lambda c, bi: (0, 0, 0)),
                  _single(g1.shape, lambda c, bi: (0, 0, 0))],
        out_specs=pl.BlockSpec((None, l, HY_CB), lambda c, bi: (bi, 0, c)),
        out_shape=jax.ShapeDtypeStruct((b, l, HYENA_WIDTH), F32),
        scratch_shapes=[pltpu.VMEM((l, HY_CB), F32), pltpu.VMEM((l, HY_CB), F32),
                        pltpu.VMEM(spec_block, F32)],
        name="hyena_conv",
        compiler_params=_cparams(("arbitrary", "arbitrary")),
    )(p, p, p, conv_w, conv_w, conv_w, conv_b, conv_b, conv_b, bias_d, kf, f1, g1)


def _merge_kernel(a_ref, h_ref, wa_ref, wh_ref, *refs):
    gate_refs, o_ref = refs[:-1], refs[-1]
    nt = len(gate_refs) // 2
    tg = gate_refs[0].shape[1]
    a = a_ref[...]
    h = h_ref[...].astype(BF16)
    for k in range(nt):
        cs = slice(k * tg, (k + 1) * tg)
        ya = jnp.dot(a, wa_ref[:, cs], preferred_element_type=F32)
        yh = jnp.dot(h, wh_ref[:, cs], preferred_element_type=F32)
        o_ref[:, cs] = (jax.nn.sigmoid(gate_refs[k][...]) * ya
                        + jax.nn.sigmoid(gate_refs[nt + k][...]) * yh).astype(o_ref.dtype)


def _merge(attn, hy, p, w_ao, w_ho, layer, tm):
    m = attn.shape[0]
    nt = D_MODEL // GATE_COLS

    def gate_spec(col0, k):
        return pl.BlockSpec((tm, GATE_COLS), lambda i: (i, col0 // GATE_COLS + k))

    gate_specs = ([gate_spec(HY_END, k) for k in range(nt)]
                  + [gate_spec(GA_END, k) for k in range(nt)])
    return pl.pallas_call(
        _merge_kernel,
        grid=(m // tm,),
        in_specs=[pl.BlockSpec((tm, ATTN_WIDTH), lambda i: (i, 0)),
                  pl.BlockSpec((tm, HYENA_WIDTH), lambda i: (i, 0)),
                  _single((None, ATTN_WIDTH, D_MODEL), lambda i: (layer, 0, 0)),
                  _single((None, HYENA_WIDTH, D_MODEL), lambda i: (layer, 0, 0))] + gate_specs,
        out_specs=pl.BlockSpec((tm, D_MODEL), lambda i: (i, 0)),
        out_shape=jax.ShapeDtypeStruct((m, D_MODEL), BF16),
        name="branch_merge",
        compiler_params=_cparams(("arbitrary",)),
    )(attn, hy, w_ao, w_ho, *([p] * (2 * nt)))


def _outproj_kernel(mix_ref, w_ref, x_ref, g_ref, gate_ref, o_ref, y_ref, *, tn):
    j = pl.program_id(1)
    nj = y_ref.shape[0]
    y_ref[j] = jnp.dot(mix_ref[...], w_ref[...], preferred_element_type=F32)

    @pl.when(j == nj - 1)
    def _():
        ss = sum(jnp.sum(jnp.square(y_ref[k]), axis=-1, keepdims=True) for k in range(nj))
        r = lax.rsqrt(ss * (1.0 / D_MODEL) + EPS)
        for k in range(nj):
            cs = slice(k * tn, (k + 1) * tn)
            o_ref[:, cs] = x_ref[:, cs] + gate_ref[:, cs] * ((y_ref[k] * r) * g_ref[:, cs])


def _outproj(mix, w_o, layer, x, g, gate, tm, tn):
    m = x.shape[0]
    nmod = gate.shape[0]
    blocks_per_mod = m // nmod // tm
    return pl.pallas_call(
        functools.partial(_outproj_kernel, tn=tn),
        grid=(m // tm, D_MODEL // tn),
        in_specs=[pl.BlockSpec((tm, D_MODEL), lambda i, j: (i, 0)),
                  pl.BlockSpec((None, D_MODEL, tn), lambda i, j: (layer, 0, j),
                               pipeline_mode=pl.Buffered(1) if tn == D_MODEL else None),
                  pl.BlockSpec((tm, D_MODEL), lambda i, j: (i, 0)),
                  pl.BlockSpec((1, D_MODEL), lambda i, j: (0, 0)),
                  pl.BlockSpec((None, 1, D_MODEL), lambda i, j: (i // blocks_per_mod, 0, 0))],
        out_specs=pl.BlockSpec((tm, D_MODEL), lambda i, j: (i, 0)),
        out_shape=jax.ShapeDtypeStruct((m, D_MODEL), F32),
        scratch_shapes=[pltpu.VMEM((D_MODEL // tn, tm, tn), F32)],
        name="out_projection",
        compiler_params=_cparams(("arbitrary", "arbitrary")),
    )(mix, w_o, x, g.reshape(1, D_MODEL), gate)


def _mlp_kernel(x_ref, xn_ref, gin_ref, sc_ref, sh_ref, scn_ref, shn_ref, w1_ref, w2_ref, gout_ref,
                gate_ref, o_ref, ha_ref, hb_ref, *, pro_steps):
    i = pl.program_id(0)
    j = pl.program_id(1)
    tm = x_ref.shape[0]
    chunk = xn_ref.shape[0]
    gin = gin_ref[...]

    def norm(x, sc_r, sh_r):
        return (_rms(x) * gin * (1.0 + sc_r[...]) + sh_r[...]).astype(BF16)

    @pl.when((i == 0) & (j == 0))
    def _():
        def first(rows):
            ha_ref[rows, :] = norm(x_ref[rows, :], sc_ref, sh_ref)

        _row_chunks(tm, first)

    @pl.when(j == 0)
    def _():
        def clear(rows):
            o_ref[rows, :] = jnp.zeros((PROLOGUE_ROWS, D_MODEL), F32)

        _row_chunks(tm, clear)

    def step(cur_ref, next_ref):
        rows = pl.ds(pl.multiple_of(jnp.minimum(j, pro_steps - 1) * chunk, chunk), chunk)
        next_ref[rows, :] = norm(xn_ref[...], scn_ref, shn_ref)
        a = jnp.dot(cur_ref[...], w1_ref[...].astype(BF16), preferred_element_type=F32)
        a = jnp.square(jnp.maximum(a, 0.0)).astype(BF16)
        for n in range(0, D_MODEL, MLP_COLS):
            o_ref[:, n:n + MLP_COLS] += jnp.dot(a, w2_ref[:, n:n + MLP_COLS].astype(BF16),
                                                preferred_element_type=F32)

    @pl.when(i % 2 == 0)
    def _():
        step(ha_ref, hb_ref)

    @pl.when(i % 2 == 1)
    def _():
        step(hb_ref, ha_ref)

    @pl.when(j == pl.num_programs(1) - 1)
    def _():
        gout, gate = gout_ref[...], gate_ref[...]

        def epilogue(rows):
            o_ref[rows, :] = x_ref[rows, :] + gate * (_rms(o_ref[rows, :]) * gout)

        _row_chunks(tm, epilogue)


def _mlp(x, g_in, scale, shift, w1, w2, layer, g_out, gate, tm, tf):
    m = x.shape[0]
    nb = m // tm
    nj = D_FF // tf
    nmod = gate.shape[0]
    blocks_per_mod = m // nmod // tm
    pro_steps = min(nj, NORM_STEPS)
    chunk = tm // pro_steps
    assert chunk * pro_steps == tm and chunk % 16 == 0 and tm % PROLOGUE_ROWS == 0

    def next_block(i):
        return jnp.minimum(i + 1, nb - 1)

    mod_spec = pl.BlockSpec((None, 1, D_MODEL), lambda i, j: (i // blocks_per_mod, 0, 0))
    next_mod = pl.BlockSpec((None, 1, D_MODEL),
                            lambda i, j: (next_block(i) // blocks_per_mod, 0, 0))
    row_spec = pl.BlockSpec((1, D_MODEL), lambda i, j: (0, 0))
    return pl.pallas_call(
        functools.partial(_mlp_kernel, pro_steps=pro_steps),
        grid=(nb, nj),
        in_specs=[pl.BlockSpec((tm, D_MODEL), lambda i, j: (i, 0), pipeline_mode=pl.Buffered(1)),
                  pl.BlockSpec((chunk, D_MODEL),
                               lambda i, j: (next_block(i) * pro_steps
                                             + jnp.minimum(j, pro_steps - 1), 0)),
                  row_spec, mod_spec, mod_spec, next_mod, next_mod,
                  pl.BlockSpec((None, D_MODEL, tf), lambda i, j: (layer, 0, j)),
                  pl.BlockSpec((None, tf, D_MODEL), lambda i, j: (layer, j, 0)),
                  row_spec, mod_spec],
        out_specs=pl.BlockSpec((tm, D_MODEL), lambda i, j: (i, 0)),
        out_shape=jax.ShapeDtypeStruct((m, D_MODEL), F32),
        scratch_shapes=[pltpu.VMEM((tm, D_MODEL), BF16), pltpu.VMEM((tm, D_MODEL), BF16)],
        name="channel_mlp",
        compiler_params=_cparams(("arbitrary", "arbitrary")),
    )(x, x, g_in.reshape(1, D_MODEL), scale, shift, scale, shift, w1, w2,
      g_out.reshape(1, D_MODEL), gate)


def _rope_tables(l):
    pos = np.arange(l)
    quarter = HEAD_DIM // 4
    freqs = ROPE_THETA ** (-np.arange(quarter) / quarter)
    ang_r = (pos // GRID_W)[:, None] * freqs[None, :]
    ang_c = (pos % GRID_W)[:, None] * freqs[None, :]
    cos_t = np.concatenate([np.cos(ang_r), np.cos(ang_r), np.cos(ang_c), np.cos(ang_c)], axis=-1)
    sin_t = np.concatenate([-np.sin(ang_r), np.sin(ang_r), -np.sin(ang_c), np.sin(ang_c)], axis=-1)
    return jnp.asarray(cos_t.astype(np.float32)), jnp.asarray(sin_t.astype(np.float32))


def _decay_rates():
    min_decay = math.log(DECAY_TARGET) / SLOW_DECAY_PCT
    max_decay = math.log(DECAY_TARGET) / FAST_DECAY_PCT
    deltas = np.tile(np.linspace(min_decay, max_decay, HYENA_WIDTH), 2)
    return jnp.asarray(np.abs(deltas)[None, :].astype(np.float32))


def kernel(x, c, ctx, c_ctx, w_mod, b_mod, norm_g, w_in, attn_sink, hy_conv_w, hy_conv_b,
           hy_fw1, hy_fb1, hy_ff1, hy_fw2, hy_fb2, hy_ff2, hy_fw3, hy_fb3, hy_ff3, hy_fw4,
           hy_bias, w_attn_out, w_hyena_out, w_out, w_ff1, w_ff2):
    b, l, d = x.shape
    cl = ctx.shape[1]
    assert d == D_MODEL and l % PROJ_ROWS == 0 and cl % 256 == 0 and b + 1 <= MOD_ROWS

    cos_t, sin_t = _rope_tables(l)
    decay = _decay_rates()
    tables_lat = _fft_tables(l, FFT_N2)
    tables_ctx = _fft_tables(cl, 1)
    z_lat = _filter_features(l)
    z_ctx = _filter_features(cl)

    c_rows = jnp.concatenate([c, c_ctx[None, :], jnp.zeros((MOD_ROWS - b - 1, d), F32)], axis=0)
    x_lat = x.reshape(b * l, d)
    x_ctx = ctx.reshape(b * cl, d)
    tm_ctx = b * cl
    w_aob = w_attn_out.astype(BF16)
    w_hob = w_hyena_out.astype(BF16)
    w_ob = w_out.astype(BF16)

    for layer in range(DEPTH):
        last = layer == DEPTH - 1
        mod = _modulation(c_rows, w_mod, layer, b_mod[layer])
        mod_lat = [mod[:b, k * d:(k + 1) * d].reshape(b, 1, d) for k in range(N_MOD)]
        mod_ctx = [mod[b:b + 1, k * d:(k + 1) * d].reshape(1, 1, d) for k in range(N_MOD)]
        sh1, sc1, g1, sh2, sc2, g2 = mod_lat
        csh1, csc1, cg1, csh2, csc2, cg2 = mod_ctx
        g = norm_g[layer]
        fparams = (hy_fw1[layer], hy_fb1[layer], hy_ff1[layer], hy_fw2[layer], hy_fb2[layer],
                   hy_ff2[layer], hy_fw3[layer], hy_fb3[layer], hy_ff3[layer])
        bias_d = hy_bias[layer].reshape(1, HYENA_WIDTH)
        conv_b = hy_conv_b[layer].reshape(1, 3 * HYENA_WIDTH)

        p_lat = _normproj(x_lat, g[0], sc1, sh1, w_in, layer, 0, IN_WIDTH, PROJ_ROWS, PROJ_COLS)
        if last:
            kv_ctx = _normproj(x_ctx, g[0], csc1, csh1, w_in, layer, Q_END, V_END - Q_END,
                               tm_ctx, PROJ_COLS)
            kx, vx = kv_ctx[:, :KV_WIDTH], kv_ctx[:, KV_WIDTH:]
        else:
            p_ctx = _normproj(x_ctx, g[0], csc1, csh1, w_in, layer, 0, IN_WIDTH, tm_ctx, PROJ_COLS)
            kx, vx = p_ctx[:, Q_END:K_END], p_ctx[:, K_END:V_END]
        kx = kx.reshape(b, cl, KV_WIDTH)
        vx = vx.reshape(b, cl, KV_WIDTH)

        p3 = p_lat.reshape(b, l, IN_WIDTH)
        attn = _window_attention(p3, kx, vx, attn_sink[layer], cos_t, sin_t)
        kf = _filter_spectrum(_filter_mlp(z_lat, *fparams), hy_fw4[layer], decay, tables_lat)
        hy = _hyena(p3, V_END, hy_conv_w[layer], conv_b, bias_d, kf, tables_lat)
        mix = _merge(attn.reshape(b * l, ATTN_WIDTH), hy.reshape(b * l, HYENA_WIDTH), p_lat,
                     w_aob, w_hob, layer, MIX_ROWS)
        x_lat = _outproj(mix, w_ob, layer, x_lat, g[1], g1, MIX_ROWS, D_MODEL)

        if not last:
            pc3 = p_ctx.reshape(b, cl, IN_WIDTH)
            attn_c = _context_attention(pc3, attn_sink[layer])
            kf_c = _filter_spectrum(_filter_mlp(z_ctx, *fparams), hy_fw4[layer], decay, tables_ctx)
            hy_c = _hyena(pc3, V_END, hy_conv_w[layer], conv_b, bias_d, kf_c, tables_ctx)
            mix_c = _merge(attn_c.reshape(b * cl, ATTN_WIDTH), hy_c.reshape(b * cl, HYENA_WIDTH),
                           p_ctx, w_aob, w_hob, layer, tm_ctx)
            x_ctx = _outproj(mix_c, w_ob, layer, x_ctx, g[1], cg1, tm_ctx, D_MODEL)
            x_ctx = _mlp(x_ctx, g[2], csc2, csh2, w_ff1, w_ff2, layer, g[3], cg2, tm_ctx, MLP_FF)

        x_lat = _mlp(x_lat, g[2], sc2, sh2, w_ff1, w_ff2, layer, g[3], g2, MLP_ROWS, MLP_FF)
    return x_lat.reshape(b, l, d)
```
